```python
import jax, jax.numpy as jnp
from jax import lax
import numpy as np

D_MODEL = 1024
BATCH = 16
SEQ = 256
DEPTH = 2
DEC_BATCH = 8
DEC_SEQ = 1024
PAST_LEN = 256

GRID_W = 64
HEAD_DIM = 64
N_HEADS_A = 8
N_KV_A = 2
N_HEADS_C = 8
N_KV_C = 2
WINDOW = 128
Q_BLOCK = 128
N_FOURIER_GROUPS = 4
FOURIER_GROUP_DIM = 128
FOURIER_DIM = N_FOURIER_GROUPS * FOURIER_GROUP_DIM
N_BRANCHES = 3
Q_A_W = N_HEADS_A * HEAD_DIM
KV_A_W = N_KV_A * HEAD_DIM
Q_C_W = N_HEADS_C * HEAD_DIM
KV_C_W = N_KV_C * HEAD_DIM
IN_WIDTHS = (Q_A_W, KV_A_W, KV_A_W, Q_C_W, KV_C_W, KV_C_W, FOURIER_DIM, N_BRANCHES * D_MODEL)
IN_WIDTH = sum(IN_WIDTHS)
ROPE_THETA = 10000.0
AXIS_FREQS = HEAD_DIM // 4
N_EXPERT_GROUPS = 4
EXPERTS_PER_GROUP = 8
N_EXPERTS = N_EXPERT_GROUPS * EXPERTS_PER_GROUP
TOP_K_INNER = 2
EXPERT_FF = 128
NORM_EPS = 1e-6
NEG_INF = -1e30

kernel_name = "hybrid_diffusion_prefix_trunk_step"

f32 = jnp.float32


def _rmsnorm(x, g):
    x32 = x.astype(f32)
    y = x32 * lax.rsqrt(jnp.mean(x32 * x32, axis=-1, keepdims=True) + NORM_EPS)
    return (y * g.astype(f32)).astype(x.dtype)


def _axial_rope(n):
    rows = n // GRID_W
    row = jnp.repeat(jnp.arange(rows, dtype=f32), GRID_W)
    col = jnp.tile(jnp.arange(GRID_W, dtype=f32), rows)
    inv = ROPE_THETA ** (-jnp.arange(AXIS_FREQS, dtype=f32) / AXIS_FREQS)
    ang = jnp.concatenate([row[:, None] * inv, col[:, None] * inv], axis=-1)
    return jnp.cos(ang)[:, None, :], jnp.sin(ang)[:, None, :]


def _apply_rope(x, cos, sin):
    half = HEAD_DIM // 2
    x32 = x.astype(f32)
    x1, x2 = x32[..., :half], x32[..., half:]
    return jnp.concatenate([x1 * cos - x2 * sin, x2 * cos + x1 * sin], axis=-1).astype(x.dtype)


def _dense_attention(q, k, v, sink=None):
    b, n, h, d = q.shape
    kvh = k.shape[2]
    g = h // kvh
    nb = n // Q_BLOCK
    scale = d ** -0.5
    qb = q.reshape(b, nb, Q_BLOCK, kvh, g, d).swapaxes(0, 1)

    def block(qblk):
        s = jnp.einsum('bqkgd,bskd->bkgqs', qblk, k).astype(f32) * scale
        if sink is not None:
            sk = jnp.broadcast_to(sink.astype(f32).reshape(1, kvh, g, 1, 1), s.shape[:-1] + (1,))
            s = jnp.concatenate([s, sk], axis=-1)
        p = jax.nn.softmax(s, axis=-1)
        if sink is not None:
            p = p[..., :-1]
        return jnp.einsum('bkgqs,bskd->bqkgd', p.astype(v.dtype), v)

    o = lax.map(block, qb)
    return o.swapaxes(0, 1).reshape(b, n, h * d)


def _window_attention(q, k, v, k_ctx, v_ctx, sink):
    b, n, h, d = q.shape
    kvh = k.shape[2]
    g = h // kvh
    nb = n // Q_BLOCK
    scale = d ** -0.5
    span = Q_BLOCK + 2 * WINDOW
    pad = ((0, 0), (WINDOW, WINDOW), (0, 0), (0, 0))
    kp = jnp.pad(k, pad)
    vp = jnp.pad(v, pad)
    sk = sink.astype(f32).reshape(1, kvh, g, 1, 1)

    def block(i):
        start = i * Q_BLOCK
        qblk = lax.dynamic_slice_in_dim(q, start, Q_BLOCK, axis=1).reshape(b, Q_BLOCK, kvh, g, d)
        kw = lax.dynamic_slice_in_dim(kp, start, span, axis=1)
        vw = lax.dynamic_slice_in_dim(vp, start, span, axis=1)
        qpos = start + jnp.arange(Q_BLOCK)
        kpos = start - WINDOW + jnp.arange(span)
        ok = (jnp.abs(qpos[:, None] - kpos[None, :]) <= WINDOW) & (kpos >= 0)[None, :] & (kpos < n)[None, :]
        s_w = jnp.einsum('bqkgd,bskd->bkgqs', qblk, kw).astype(f32) * scale
        s_w = jnp.where(ok, s_w, NEG_INF)
        s_c = jnp.einsum('bqkgd,bskd->bkgqs', qblk, k_ctx).astype(f32) * scale
        s = jnp.concatenate([s_w, s_c, jnp.broadcast_to(sk, s_w.shape[:-1] + (1,))], axis=-1)
        p = jax.nn.softmax(s, axis=-1).astype(v.dtype)
        o = jnp.einsum('bkgqs,bskd->bqkgd', p[..., :span], vw)
        return o + jnp.einsum('bkgqs,bskd->bqkgd', p[..., span:-1], v_ctx)

    o = lax.map(block, jnp.arange(nb))
    return o.swapaxes(0, 1).reshape(b, n, h * d)


def _fourier_mix(u):
    b, n, _ = u.shape
    ug = u.astype(f32).reshape(b, n, N_FOURIER_GROUPS, FOURIER_GROUP_DIM)
    fr = jnp.fft.fft2(ug, axes=(1, 3), norm='ortho').real
    return fr.reshape(b, n, FOURIER_DIM).astype(u.dtype)


def _mixer(h, p, rope, ctx_kv):
    b, n, _ = h.shape
    z = h @ p['w_in']
    split_at = np.cumsum(IN_WIDTHS)[:-1].tolist()
    qa, ka, va, qc, kc, vc, ub, gl = jnp.split(z, split_at, axis=-1)
    qa = _rmsnorm(qa.reshape(b, n, N_HEADS_A, HEAD_DIM), p['q_norm'])
    ka = _rmsnorm(ka.reshape(b, n, N_KV_A, HEAD_DIM), p['k_norm'])
    va = va.reshape(b, n, N_KV_A, HEAD_DIM)
    qc = qc.reshape(b, n, N_HEADS_C, HEAD_DIM)
    kc = kc.reshape(b, n, N_KV_C, HEAD_DIM)
    vc = vc.reshape(b, n, N_KV_C, HEAD_DIM)
    if ctx_kv is None:
        oa = _dense_attention(qa, ka, va)
        oc = _dense_attention(qc, kc, vc, p['sink'])
        new_kv = (ka, va, kc, vc)
    else:
        cos, sin = rope
        ka_ctx, va_ctx, kc_ctx, vc_ctx = ctx_kv
        qa = _apply_rope(qa, cos, sin)
        ka = _apply_rope(ka, cos, sin)
        qc = _apply_rope(qc, cos, sin)
        kc = _apply_rope(kc, cos, sin)
        oa = _dense_attention(qa, jnp.concatenate([ka, ka_ctx], axis=1), jnp.concatenate([va, va_ctx], axis=1))
        oc = _window_attention(qc, kc, vc, kc_ctx, vc_ctx, p['sink'])
        new_kv = None
    ob = _fourier_mix(ub)
    ga, gb, gc = jnp.split(jax.nn.sigmoid(gl), N_BRANCHES, axis=-1)
    merged = ga * (oa @ p['w_ba']) + gb * (ob @ p['w_bb']) + gc * (oc @ p['w_bc'])
    return merged @ p['w_out'], new_kv


def _moe(h, p):
    b, n, d = h.shape
    t = h.reshape(b * n, d)
    lg = (t @ p['w_rg']).astype(f32)
    gidx = jnp.argmax(lg, axis=-1)
    gw = jnp.max(jax.nn.softmax(lg, axis=-1), axis=-1)
    le = (t @ p['w_re']).astype(f32).reshape(-1, N_EXPERT_GROUPS, EXPERTS_PER_GROUP)
    le = jnp.take_along_axis(le, gidx[:, None, None], axis=1)[:, 0]
    tw, ti = lax.top_k(jax.nn.softmax(le, axis=-1), TOP_K_INNER)
    tw = tw / jnp.sum(tw, axis=-1, keepdims=True)
    inner = jnp.einsum('tk,tke->te', tw, jax.nn.one_hot(ti, EXPERTS_PER_GROUP, dtype=f32))
    outer = jax.nn.one_hot(gidx, N_EXPERT_GROUPS, dtype=f32) * gw[:, None]
    w = (outer[:, :, None] * inner[:, None, :]).reshape(-1, N_EXPERTS, 1).astype(h.dtype)
    a = (jax.nn.silu(t @ p['w_eg']) * (t @ p['w_eu'])).reshape(-1, N_EXPERTS, EXPERT_FF)
    out = (a * w).reshape(-1, N_EXPERTS * EXPERT_FF) @ p['w_ed']
    return out.reshape(b, n, d)


def _layer(x, cond, p, rope, ctx_kv):
    mod = jax.nn.silu(cond) @ p['w_mod'] + p['b_mod']
    sh1, sc1, g1, sh2, sc2, g2 = [m[:, None, :] for m in jnp.split(mod, 6, axis=-1)]
    h = _rmsnorm(x, p['norm1']) * (1 + sc1) + sh1
    mix, new_kv = _mixer(h, p, rope, ctx_kv)
    x = x + g1 * mix
    h = _rmsnorm(x, p['norm2']) * (1 + sc2) + sh2
    x = x + g2 * _moe(h, p)
    return x, new_kv


def setup_inputs(seed: int = 0) -> dict:
    key = jax.random.key(seed)
    ks = jax.random.split(key, 26)
    D = D_MODEL

    def nrm(k, shape, scale):
        return jax.random.normal(k, shape, f32) * scale

    return {
        'x_prompt': nrm(ks[0], (BATCH, SEQ, D), 1.0),
        'x_sample': nrm(ks[1], (DEC_BATCH, DEC_SEQ, D), 1.0),
        'cache_global_kv': nrm(ks[2], (DEC_BATCH, DEPTH, 2, PAST_LEN, N_KV_A, HEAD_DIM), 1.0),
        'cache_window_kv': nrm(ks[3], (DEC_BATCH, DEPTH, 2, PAST_LEN, N_KV_C, HEAD_DIM), 1.0),
        'c': nrm(ks[4], (DEC_BATCH, D), 1.0),
        'c_ctx': nrm(ks[5], (D,), 1.0),
        'w_mod': nrm(ks[6], (DEPTH, D, 6 * D), 0.5 * D ** -0.5),
        'b_mod': nrm(ks[7], (DEPTH, 6 * D), 0.02),
        'norm1_g': 1.0 + nrm(ks[8], (DEPTH, D), 0.05),
        'w_in': nrm(ks[9], (DEPTH, D, IN_WIDTH), D ** -0.5),
        'q_norm_g': 1.0 + nrm(ks[10], (DEPTH, HEAD_DIM), 0.05),
        'k_norm_g': 1.0 + nrm(ks[11], (DEPTH, HEAD_DIM), 0.05),
        'sink_logit': nrm(ks[12], (DEPTH, N_HEADS_C), 0.5),
        'w_branch_a': nrm(ks[13], (DEPTH, Q_A_W, D), Q_A_W ** -0.5),
        'w_branch_b': nrm(ks[14], (DEPTH, FOURIER_DIM, D), FOURIER_DIM ** -0.5),
        'w_branch_c': nrm(ks[15], (DEPTH, Q_C_W, D), Q_C_W ** -0.5),
        'w_out': nrm(ks[16], (DEPTH, D, D), D ** -0.5),
        'norm2_g': 1.0 + nrm(ks[17], (DEPTH, D), 0.05),
        'w_route_group': nrm(ks[18], (DEPTH, D, N_EXPERT_GROUPS), D ** -0.5),
        'w_route_expert': nrm(ks[19], (DEPTH, D, N_EXPERTS), D ** -0.5),
        'w_exp_gate': nrm(ks[20], (DEPTH, D, N_EXPERTS * EXPERT_FF), D ** -0.5),
        'w_exp_up': nrm(ks[21], (DEPTH, D, N_EXPERTS * EXPERT_FF), D ** -0.5),
        'w_exp_down': nrm(ks[22], (DEPTH, N_EXPERTS * EXPERT_FF, D), EXPERT_FF ** -0.5),
        'final_norm_g': 1.0 + nrm(ks[23], (D,), 0.05),
    }


def reference(x_prompt, x_sample, cache_global_kv, cache_window_kv, c, c_ctx, w_mod, b_mod, norm1_g, w_in,
              q_norm_g, k_norm_g, sink_logit, w_branch_a, w_branch_b, w_branch_c, w_out, norm2_g,
              w_route_group, w_route_expert, w_exp_gate, w_exp_up, w_exp_down, final_norm_g):
    rope = _axial_rope(x_sample.shape[1])
    ctx_cond = c_ctx[None, :]
    xp, xs = x_prompt, x_sample
    gkv_layers, wkv_layers = [], []
    for l in range(DEPTH):
        p = {
            'w_mod': w_mod[l], 'b_mod': b_mod[l], 'norm1': norm1_g[l], 'w_in': w_in[l],
            'q_norm': q_norm_g[l], 'k_norm': k_norm_g[l], 'sink': sink_logit[l],
            'w_ba': w_branch_a[l], 'w_bb': w_branch_b[l], 'w_bc': w_branch_c[l], 'w_out': w_out[l],
            'norm2': norm2_g[l], 'w_rg': w_route_group[l], 'w_re': w_route_expert[l],
            'w_eg': w_exp_gate[l], 'w_eu': w_exp_up[l], 'w_ed': w_exp_down[l],
        }
        xp, (ka, va, kc, vc) = _layer(xp, ctx_cond, p, None, None)
        gkv_layers.append(jnp.stack([ka, va], axis=1))
        wkv_layers.append(jnp.stack([kc, vc], axis=1))
        cached = (cache_global_kv[:, l, 0], cache_global_kv[:, l, 1],
                  cache_window_kv[:, l, 0], cache_window_kv[:, l, 1])
        xs, _ = _layer(xs, c, p, rope, cached)
    y_prompt = _rmsnorm(xp, final_norm_g)
    y_sample = _rmsnorm(xs, final_norm_g)
    new_global_kv = jnp.stack(gkv_layers, axis=1)
    new_window_kv = jnp.stack(wkv_layers, axis=1)
    return (y_prompt, y_sample, new_global_kv, new_window_kv)
```

```python
import functools

import jax
import jax.numpy as jnp
import numpy as np
from jax import lax
from jax.experimental import pallas as pl
from jax.experimental.pallas import tpu as pltpu

f32 = jnp.float32
bf16 = jnp.bfloat16

D_MODEL = 1024
DEPTH = 2
GRID_W = 64
HEAD_DIM = 64
HALF = HEAD_DIM // 2
N_HEADS = 8
N_KV = 2
WINDOW = 128
N_FOURIER_GROUPS = 4
FOURIER_GROUP_DIM = 128
FOURIER_DIM = N_FOURIER_GROUPS * FOURIER_GROUP_DIM
ROPE_THETA = 10000.0
AXIS_FREQS = HEAD_DIM // 4
N_EXPERT_GROUPS = 4
EXPERTS_PER_GROUP = 8
N_EXPERTS = N_EXPERT_GROUPS * EXPERTS_PER_GROUP
EXPERT_FF = 128
NORM_EPS = 1e-6
NEG_INF = -1e30

LANES = 128
Q_W = N_HEADS * HEAD_DIM
KV_DUP_W = 2 * N_KV * HEAD_DIM
C_QA, C_KA, C_VA, C_QC, C_KC, C_VC, C_UB, C_END = 0, 512, 768, 1024, 1536, 1792, 2048, 2560
N_COND_ROWS = 16
VMEM_LIMIT = 56 * 1024 * 1024


def _cparams(sem):
    return pltpu.CompilerParams(dimension_semantics=sem, vmem_limit_bytes=VMEM_LIMIT)


def _resident(shape):
    nd = len(shape)
    return pl.BlockSpec(shape, lambda *_: (0,) * nd, pipeline_mode=pl.Buffered(1))


def _sigmoid(x):
    return 1.0 / (1.0 + jnp.exp(-x))


def _rms_mod(x, g, scale, shift):
    y = x * lax.rsqrt(jnp.mean(x * x, axis=-1, keepdims=True) + NORM_EPS)
    return (y * g) * (1.0 + scale) + shift


def _dot(a, b):
    return jnp.dot(a, b, preferred_element_type=f32)


def _dot_t(a, b):
    return lax.dot_general(a, b, (((1,), (1,)), ((), ())), preferred_element_type=f32)


def _split_bf16(x):
    hi = x.astype(bf16)
    lo = (x - hi.astype(f32)).astype(bf16)
    return hi, lo


def _mod_kernel(cond_ref, w_ref, b_ref, o_ref):
    c = cond_ref[...]
    s = (c * _sigmoid(c)).astype(bf16)
    o_ref[0] = _dot(s, w_ref[0].astype(bf16)) + b_ref[0]


def _mod_call(cond, w_mod, b_mod):
    n_tiles = 6 * D_MODEL // 1024
    return pl.pallas_call(
        _mod_kernel,
        grid=(DEPTH, n_tiles),
        in_specs=[
            pl.BlockSpec((N_COND_ROWS, D_MODEL), lambda l, j: (0, 0)),
            pl.BlockSpec((1, D_MODEL, 1024), lambda l, j: (l, 0, j)),
            pl.BlockSpec((1, 1, 1024), lambda l, j: (l, 0, j)),
        ],
        out_specs=pl.BlockSpec((1, N_COND_ROWS, 1024), lambda l, j: (l, 0, j)),
        out_shape=jax.ShapeDtypeStruct((DEPTH, N_COND_ROWS, 6 * D_MODEL), f32),
        compiler_params=_cparams(("arbitrary", "arbitrary")),
        name="mod",
    )(cond, w_mod, b_mod.reshape(DEPTH, 1, 6 * D_MODEL))


def _head_norm(z, g, ones_blk):
    hi, lo = _split_bf16(z * z)
    ss = _dot(hi, ones_blk) + _dot(lo, ones_blk)
    return z * lax.rsqrt(ss * (1.0 / HEAD_DIM) + NORM_EPS) * g


def _rope(z, cos, sin_signed, first_half):
    outs = []
    for j in range(z.shape[1] // LANES):
        zj = z[:, j * LANES:(j + 1) * LANES]
        swapped = jnp.where(first_half, pltpu.roll(zj, LANES - HALF, 1), pltpu.roll(zj, HALF, 1))
        outs.append(zj * cos + swapped * sin_signed)
    return jnp.concatenate(outs, axis=1)


def _undup(d, low_lanes):
    return jnp.where(low_lanes, d[:, 0:LANES], d[:, LANES:2 * LANES])


def _s1_kernel(*refs, rope, emit_kv):
    x_ref, mod_ref, g1_ref, w_ref, ones_ref, qg_ref, kg_ref = refs[:7]
    pos = 7
    if rope:
        cos_ref, sin_ref = refs[pos:pos + 2]
        pos += 2
    qa_ref, ka_ref, va_ref, qc_ref, kc_ref, vc_ref, ub_ref = refs[pos:pos + 7]
    pos += 7
    if emit_kv:
        gk_ref, gv_ref, wk_ref, wv_ref = refs[pos:pos + 4]

    tm = x_ref.shape[0]
    h = _rms_mod(x_ref[...], g1_ref[...], mod_ref[0, 1:2, :], mod_ref[0, 0:1, :]).astype(bf16)

    def proj(lo, hi):
        return _dot(h, w_ref[:, lo:hi])

    lane = lax.broadcasted_iota(jnp.int32, (tm, LANES), 1)
    low_lanes = lane < HEAD_DIM
    if rope:
        cos = cos_ref[...]
        sin_signed = sin_ref[...]
        first_half = (lane & (HEAD_DIM - 1)) < HALF
        rot = lambda z: _rope(z, cos, sin_signed, first_half)
    else:
        rot = lambda z: z
    q_scale = HEAD_DIM ** -0.5

    qa = rot(_head_norm(proj(C_QA, C_KA), qg_ref[...], ones_ref[...]))
    qa_ref[...] = (qa * q_scale).astype(bf16)
    ka_n = _head_norm(proj(C_KA, C_VA), kg_ref[...], ones_ref[0:KV_DUP_W, 0:KV_DUP_W])
    ka_ref[...] = rot(ka_n).astype(bf16)
    va = proj(C_VA, C_QC)
    va_ref[...] = va.astype(bf16)
    qc_ref[...] = (rot(proj(C_QC, C_KC)) * q_scale).astype(bf16)
    kc = proj(C_KC, C_VC)
    kc_ref[...] = rot(kc).astype(bf16)
    vc = proj(C_VC, C_UB)
    vc_ref[...] = vc.astype(bf16)
    ub_ref[...] = proj(C_UB, C_END).astype(bf16)
    if emit_kv:
        gk_ref[...] = _undup(ka_n, low_lanes)
        gv_ref[...] = _undup(va, low_lanes)
        wk_ref[...] = _undup(kc, low_lanes)
        wv_ref[...] = _undup(vc, low_lanes)


def _s1_call(x, mod, g1, w_ext, ones_blk, qg, kg, rope_tabs, *, tokens_per_batch, cond_row0, per_batch_cond, emit_kv):
    n_tok = x.shape[0]
    tm = 512 if tokens_per_batch % 512 == 0 or not per_batch_cond else 256
    rope = rope_tabs is not None
    if rope:
        assert tokens_per_batch % tm == 0
    tiles_per_batch = max(tokens_per_batch // tm, 1)

    def cond_idx(t):
        if per_batch_cond:
            return (cond_row0 + t // tiles_per_batch, 0, 0)
        return (cond_row0, 0, 0)

    tok = lambda w: pl.BlockSpec((tm, w), lambda t: (t, 0))
    in_specs = [
        tok(D_MODEL),
        pl.BlockSpec((1, 6, D_MODEL), cond_idx),
        _resident((1, D_MODEL)),
        _resident((D_MODEL, C_END)),
        _resident((Q_W, Q_W)),
        _resident((1, Q_W)),
        _resident((1, KV_DUP_W)),
    ]
    args = [x, mod, g1, w_ext, ones_blk, qg, kg]
    if rope:
        in_specs += [pl.BlockSpec((tm, LANES), lambda t: (t % tiles_per_batch, 0))] * 2
        args += list(rope_tabs)
    widths = [Q_W, KV_DUP_W, KV_DUP_W, Q_W, KV_DUP_W, KV_DUP_W, FOURIER_DIM]
    out_specs = [tok(w) for w in widths]
    out_shape = [jax.ShapeDtypeStruct((n_tok, w), bf16) for w in widths]
    if emit_kv:
        out_specs += [tok(LANES)] * 4
        out_shape += [jax.ShapeDtypeStruct((n_tok, LANES), f32)] * 4
    return pl.pallas_call(
        functools.partial(_s1_kernel, rope=rope, emit_kv=emit_kv),
        grid=(n_tok // tm,),
        in_specs=in_specs,
        out_specs=out_specs,
        out_shape=out_shape,
        compiler_params=_cparams(("arbitrary",)),
        name="s1_rope" if rope else "s1_ctx",
    )(*args)


def _gqa_group(q2, segs, sink_pair_lo, sink_pair_hi):
    qb = q2.shape[0]
    qs = jnp.concatenate([q2[:, 0:LANES], q2[:, LANES:2 * LANES]], axis=0)
    lane = lax.broadcasted_iota(jnp.int32, qs.shape, 1)
    low = lane < HEAD_DIM
    zero = jnp.zeros_like(qs)
    row = lax.broadcasted_iota(jnp.int32, (2 * qb, 1), 0)

    def half(qh, sink_pair):
        scores = []
        for k, _, ok in segs:
            s = _dot_t(qh, k)
            if ok is not None:
                s = jnp.where(ok, s, NEG_INF)
            scores.append(s)
        m = scores[0].max(axis=-1, keepdims=True)
        for s in scores[1:]:
            m = jnp.maximum(m, s.max(axis=-1, keepdims=True))
        if sink_pair is not None:
            sink = jnp.where(row < qb, sink_pair[0], sink_pair[1])
            m = jnp.maximum(m, sink)
            denom = jnp.exp(sink - m)
        else:
            denom = jnp.zeros_like(m)
        acc = None
        for s, (_, v, _) in zip(scores, segs):
            p = jnp.exp(s - m)
            denom = denom + p.sum(axis=-1, keepdims=True)
            pv = _dot(p.astype(bf16), v)
            acc = pv if acc is None else acc + pv
        return acc / denom

    o_lo = half(jnp.where(low, qs, zero), sink_pair_lo)
    o_hi = half(jnp.where(low, zero, qs), sink_pair_hi)
    o = jnp.where(low, o_lo, o_hi)
    return jnp.concatenate([o[0:qb], o[qb:2 * qb]], axis=1)


def _sink_pairs(sink_ref, kv):
    h = 4 * kv
    return (sink_ref[h], sink_ref[h + 2]), (sink_ref[h + 1], sink_ref[h + 3])


def _attn_ctx_kernel(sink_ref, qa_ref, ka_ref, va_ref, qc_ref, kc_ref, vc_ref, oa_ref, oc_ref):
    for kv in range(N_KV):
        cq = slice(2 * LANES * kv, 2 * LANES * (kv + 1))
        ck = slice(LANES * kv, LANES * (kv + 1))
        oa = _gqa_group(qa_ref[:, cq], [(ka_ref[:, ck], va_ref[:, ck], None)], None, None)
        oa_ref[:, cq] = oa.astype(bf16)
        lo, hi = _sink_pairs(sink_ref, kv)
        oc = _gqa_group(qc_ref[:, cq], [(kc_ref[:, ck], vc_ref[:, ck], None)], lo, hi)
        oc_ref[:, cq] = oc.astype(bf16)


def _attn_ctx_call(sink, qa, ka, va, qc, kc, vc, seq):
    n_tok = qa.shape[0]
    tok = lambda w: pl.BlockSpec((seq, w), lambda b: (b, 0))
    return pl.pallas_call(
        _attn_ctx_kernel,
        grid=(n_tok // seq,),
        in_specs=[pl.BlockSpec(memory_space=pltpu.SMEM),
                  tok(Q_W), tok(KV_DUP_W), tok(KV_DUP_W), tok(Q_W), tok(KV_DUP_W), tok(KV_DUP_W)],
        out_specs=[tok(Q_W), tok(Q_W)],
        out_shape=[jax.ShapeDtypeStruct((n_tok, Q_W), bf16)] * 2,
        compiler_params=_cparams(("arbitrary",)),
        name="attn_ctx",
    )(sink, qa, ka, va, qc, kc, vc)


QB = 256
WIN_QB = 128
WIN_SPAN = WIN_QB + 2 * WINDOW


def _attn_lat_kernel(sink_ref, qa_ref, ka_ref, va_ref, kxa_ref, vxa_ref,
                     qc_ref, kc_ref, vc_ref, kxc_ref, vxc_ref, oa_ref, oc_ref, *, seq):
    i = pl.program_id(1)
    for kv in range(N_KV):
        cq = slice(2 * LANES * kv, 2 * LANES * (kv + 1))
        ck = slice(LANES * kv, LANES * (kv + 1))
        segs = [(ka_ref[:, ck], va_ref[:, ck], None), (kxa_ref[0, :, ck], vxa_ref[0, :, ck], None)]
        oa_ref[:, cq] = _gqa_group(qa_ref[:, cq], segs, None, None).astype(bf16)

        lo, hi = _sink_pairs(sink_ref, kv)
        for sub in range(QB // WIN_QB):
            blk = i * (QB // WIN_QB) + sub
            start = jnp.clip((blk - 1) * WIN_QB, 0, seq - WIN_SPAN)
            start = pl.multiple_of(start, WIN_QB)
            r = lax.broadcasted_iota(jnp.int32, (2 * WIN_QB, WIN_SPAN), 0)
            c = lax.broadcasted_iota(jnp.int32, (2 * WIN_QB, WIN_SPAN), 1)
            qpos = blk * WIN_QB + (r & (WIN_QB - 1))
            ok = jnp.abs(qpos - (start + c)) <= WINDOW
            rows = slice(sub * WIN_QB, (sub + 1) * WIN_QB)
            segs = [(kc_ref[pl.ds(start, WIN_SPAN), ck], vc_ref[pl.ds(start, WIN_SPAN), ck], ok),
                    (kxc_ref[0, :, ck], vxc_ref[0, :, ck], None)]
            oc_ref[rows, cq] = _gqa_group(qc_ref[rows, cq], segs, lo, hi).astype(bf16)


def _attn_lat_call(sink, qa, ka, va, kxa, vxa, qc, kc, vc, kxc, vxc, seq):
    n_tok = qa.shape[0]
    nq = seq // QB
    past = kxa.shape[1]
    qspec = pl.BlockSpec((QB, Q_W), lambda b, i: (b * nq + i, 0))
    kspec = pl.BlockSpec((seq, KV_DUP_W), lambda b, i: (b, 0))
    xspec = pl.BlockSpec((1, past, KV_DUP_W), lambda b, i: (b, 0, 0))
    return pl.pallas_call(
        functools.partial(_attn_lat_kernel, seq=seq),
        grid=(n_tok // seq, nq),
        in_specs=[pl.BlockSpec(memory_space=pltpu.SMEM),
                  qspec, kspec, kspec, xspec, xspec, qspec, kspec, kspec, xspec, xspec],
        out_specs=[qspec, qspec],
        out_shape=[jax.ShapeDtypeStruct((n_tok, Q_W), bf16)] * 2,
        compiler_params=_cparams(("arbitrary", "arbitrary")),
        name="attn_lat",
    )(sink, qa, ka, va, kxa, vxa, qc, kc, vc, kxc, vxc)


def _fourier_kernel(u_ref, bdc_ref, bds_ref, cn_ref, sn_ref, o_ref, *, scale):
    u = u_ref[...]
    uc = _dot(u, bdc_ref[...]).astype(bf16)
    us = _dot(u, bds_ref[...]).astype(bf16)
    o = _dot(cn_ref[...], uc) - _dot(sn_ref[...], us)
    o_ref[...] = (o * scale).astype(bf16)


def _fourier_call(ub, bdc, bds, cn, sn, seq):
    n_tok = ub.shape[0]
    tok = pl.BlockSpec((seq, FOURIER_DIM), lambda b: (b, 0))
    scale = float(1.0 / np.sqrt(seq * FOURIER_GROUP_DIM))
    return pl.pallas_call(
        functools.partial(_fourier_kernel, scale=scale),
        grid=(n_tok // seq,),
        in_specs=[tok, _resident((FOURIER_DIM, FOURIER_DIM)), _resident((FOURIER_DIM, FOURIER_DIM)),
                  _resident((seq, seq)), _resident((seq, seq))],
        out_specs=tok,
        out_shape=jax.ShapeDtypeStruct((n_tok, FOURIER_DIM), bf16),
        compiler_params=_cparams(("arbitrary",)),
        name=f"fourier_{seq}",
    )(ub, bdc, bds, cn, sn)


ROUTE_W = LANES


def _route_weights(logits):
    tm = logits.shape[0]
    lane = lax.broadcasted_iota(jnp.int32, (tm, ROUTE_W), 1)
    g = [logits[:, N_EXPERTS + i:N_EXPERTS + i + 1] for i in range(N_EXPERT_GROUPS)]
    gmax = jnp.maximum(jnp.maximum(g[0], g[1]), jnp.maximum(g[2], g[3]))
    gsum = sum(jnp.exp(gi - gmax) for gi in g)
    gw = 1.0 / gsum
    gidx = jnp.where(g[0] == gmax, 0, jnp.where(g[1] == gmax, 1, jnp.where(g[2] == gmax, 2, 3)))
    in_group = (lane >> 3) == gidx
    neg = jnp.float32(-jnp.inf)
    lanef = lane.astype(f32)
    big = jnp.float32(ROUTE_W)
    l1 = jnp.where(in_group, logits, neg)
    m1 = l1.max(axis=-1, keepdims=True)
    i1 = jnp.where(l1 == m1, lanef, big).min(axis=-1, keepdims=True)
    l2 = jnp.where(lanef == i1, neg, l1)
    m2 = l2.max(axis=-1, keepdims=True)
    i2 = jnp.where(l2 == m2, lanef, big).min(axis=-1, keepdims=True)
    e = jnp.exp(m2 - m1)
    tw1 = 1.0 / (1.0 + e)
    tw2 = e / (1.0 + e)
    return jnp.where(lanef == i1, gw * tw1, jnp.where(lanef == i2, gw * tw2, 0.0))


def _s3_kernel(x_ref, oa_ref, ob_ref, oc_ref, mod_ref, g1_ref, g2_ref, wg_ref, wba_ref, wbb_ref, wbc_ref,
               wout_ref, wr_ref, x1_ref, h2_ref, wts_ref):
    x = x_ref[...]
    h = _rms_mod(x, g1_ref[...], mod_ref[0, 1:2, :], mod_ref[0, 0:1, :]).astype(bf16)
    merged = None
    for j, (o_ref, wb_ref) in enumerate(((oa_ref, wba_ref), (ob_ref, wbb_ref), (oc_ref, wbc_ref))):
        gate = _sigmoid(_dot(h, wg_ref[:, j * D_MODEL:(j + 1) * D_MODEL]))
        term = gate * _dot(o_ref[...], wb_ref[...])
        merged = term if merged is None else merged + term
    mix = _dot(merged.astype(bf16), wout_ref[...])
    x1 = x + mod_ref[0, 2:3, :] * mix
    x1_ref[...] = x1
    h2 = _rms_mod(x1, g2_ref[...], mod_ref[0, 4:5, :], mod_ref[0, 3:4, :])
    h2_ref[...] = h2.astype(bf16)
    th, tl = _split_bf16(h2)
    wh, wl = _split_bf16(wr_ref[...])
    logits = _dot(th, wh) + (_dot(th, wl) + _dot(tl, wh))
    wts_ref[...] = _route_weights(logits)


def _s3_call(x, oa, ob, oc, mod, g1, g2, wg, wba, wbb, wbc, wout, wr, *, tokens_per_batch, cond_row0, per_batch_cond):
    n_tok = x.shape[0]
    tm = 512
    tiles_per_batch = max(tokens_per_batch // tm, 1)

    def cond_idx(t):
        if per_batch_cond:
            return (cond_row0 + t // tiles_per_batch, 0, 0)
        return (cond_row0, 0, 0)

    tok = lambda w: pl.BlockSpec((tm, w), lambda t: (t, 0))
    return pl.pallas_call(
        _s3_kernel,
        grid=(n_tok // tm,),
        in_specs=[tok(D_MODEL), tok(Q_W), tok(FOURIER_DIM), tok(Q_W),
                  pl.BlockSpec((1, 6, D_MODEL), cond_idx),
                  _resident((1, D_MODEL)), _resident((1, D_MODEL)),
                  _resident((D_MODEL, 3 * D_MODEL)),
                  _resident((Q_W, D_MODEL)), _resident((FOURIER_DIM, D_MODEL)), _resident((Q_W, D_MODEL)),
                  _resident((D_MODEL, D_MODEL)), _resident((D_MODEL, ROUTE_W))],
        out_specs=[tok(D_MODEL), tok(D_MODEL), tok(ROUTE_W)],
        out_shape=[jax.ShapeDtypeStruct((n_tok, D_MODEL), f32),
                   jax.ShapeDtypeStruct((n_tok, D_MODEL), bf16),
                   jax.ShapeDtypeStruct((n_tok, ROUTE_W), f32)],
        compiler_params=_cparams(("arbitrary",)),
        name="s3",
    )(x, oa, ob, oc, mod, g1, g2, wg, wba, wbb, wbc, wout, wr)


FF_CHUNK = 512
EXPERTS_PER_CHUNK = FF_CHUNK // EXPERT_FF


def _s4_kernel(x1_ref, h2_ref, wts_ref, mod_ref, gf_ref, weg_ref, weu_ref, wed_ref, o_ref, *, final):
    h2 = h2_ref[...]
    wts = wts_ref[...]
    acc = None
    for c in range(N_EXPERTS * EXPERT_FF // FF_CHUNK):
        cols = slice(c * FF_CHUNK, (c + 1) * FF_CHUNK)
        gate = _dot(h2, weg_ref[:, cols])
        a = gate * _sigmoid(gate) * _dot(h2, weu_ref[:, cols])
        pieces = []
        for j in range(EXPERTS_PER_CHUNK):
            e = c * EXPERTS_PER_CHUNK + j
            pieces.append(a[:, j * EXPERT_FF:(j + 1) * EXPERT_FF] * wts[:, e:e + 1])
        aw = jnp.concatenate(pieces, axis=1).astype(bf16)
        part = _dot(aw, wed_ref[cols, :])
        acc = part if acc is None else acc + part
    x2 = x1_ref[...] + mod_ref[0, 5:6, :] * acc
    if final:
        y = x2 * lax.rsqrt(jnp.mean(x2 * x2, axis=-1, keepdims=True) + NORM_EPS)
        o_ref[...] = y * gf_ref[...]
    else:
        o_ref[...] = x2


def _s4_call(x1, h2, wts, mod, gf, weg, weu, wed, *, tokens_per_batch, cond_row0, per_batch_cond, final):
    n_tok = x1.shape[0]
    tm = 512
    tiles_per_batch = max(tokens_per_batch // tm, 1)
    ff = N_EXPERTS * EXPERT_FF

    def cond_idx(t):
        if per_batch_cond:
            return (cond_row0 + t // tiles_per_batch, 0, 0)
        return (cond_row0, 0, 0)

    tok = lambda w: pl.BlockSpec((tm, w), lambda t: (t, 0))
    return pl.pallas_call(
        functools.partial(_s4_kernel, final=final),
        grid=(n_tok // tm,),
        in_specs=[tok(D_MODEL), tok(D_MODEL), tok(ROUTE_W),
                  pl.BlockSpec((1, 6, D_MODEL), cond_idx),
                  _resident((1, D_MODEL)),
                  _resident((D_MODEL, ff)), _resident((D_MODEL, ff)), _resident((ff, D_MODEL))],
        out_specs=tok(D_MODEL),
        out_shape=jax.ShapeDtypeStruct((n_tok, D_MODEL), f32),
        compiler_params=_cparams(("arbitrary",)),
        name="s4_final" if final else "s4",
    )(x1, h2, wts, mod, gf, weg, weu, wed)


def _dup_heads(w):
    a, b = w[..., :HEAD_DIM], w[..., HEAD_DIM:]
    return jnp.concatenate([a, a, b, b], axis=-1)


def _relayout_w_in(w):
    qa, ka, va, qc, kc, vc, ub = (w[:, 0:512], w[:, 512:640], w[:, 640:768], w[:, 768:1280],
                                  w[:, 1280:1408], w[:, 1408:1536], w[:, 1536:2048])
    ext = jnp.concatenate([qa, _dup_heads(ka), _dup_heads(va), qc, _dup_heads(kc), _dup_heads(vc), ub], axis=1)
    return ext.astype(bf16), w[:, 2048:].astype(bf16)


def _rope_tables(n):
    rows = n // GRID_W
    row = jnp.repeat(jnp.arange(rows, dtype=f32), GRID_W)
    col = jnp.tile(jnp.arange(GRID_W, dtype=f32), rows)
    inv = ROPE_THETA ** (-jnp.arange(AXIS_FREQS, dtype=f32) / AXIS_FREQS)
    ang = jnp.concatenate([row[:, None] * inv, col[:, None] * inv], axis=-1)
    cos, sin = jnp.cos(ang), jnp.sin(ang)
    cos_t = jnp.concatenate([cos, cos, cos, cos], axis=-1)
    sin_t = jnp.concatenate([-sin, sin, -sin, sin], axis=-1)
    return cos_t, sin_t


def _dft_tables(n):
    k = np.arange(n)
    m = (k[:, None] * k[None, :]) % n
    ang = 2.0 * np.pi * m / n
    return np.cos(ang), np.sin(ang)


def _fourier_tables(seq):
    cn, sn = _dft_tables(seq)
    cc, sc = _dft_tables(FOURIER_GROUP_DIM)
    eye = np.eye(N_FOURIER_GROUPS)
    bdc, bds = np.kron(eye, cc), np.kron(eye, sc)
    return tuple(jnp.asarray(t, dtype=f32).astype(bf16) for t in (bdc, bds, cn, sn))


def _dup_cache(cache, l):
    b, _, _, p, _, _ = cache.shape
    k = _dup_heads(cache[:, l, 0].reshape(b, p, N_KV * HEAD_DIM)).astype(bf16)
    v = _dup_heads(cache[:, l, 1].reshape(b, p, N_KV * HEAD_DIM)).astype(bf16)
    return k, v


def kernel(x_prompt, x_sample, cache_global_kv, cache_window_kv, c, c_ctx, w_mod, b_mod, norm1_g, w_in,
           q_norm_g, k_norm_g, sink_logit, w_branch_a, w_branch_b, w_branch_c, w_out, norm2_g,
           w_route_group, w_route_expert, w_exp_gate, w_exp_up, w_exp_down, final_norm_g):
    batch, seq, _ = x_prompt.shape
    dec_batch, dec_seq, _ = x_sample.shape

    cond = jnp.zeros((N_COND_ROWS, D_MODEL), f32).at[0].set(c_ctx).at[1:1 + dec_batch].set(c)
    mod = _mod_call(cond, w_mod, b_mod).reshape(DEPTH, N_COND_ROWS, 6, D_MODEL)

    ones_blk = jnp.asarray(np.kron(np.eye(N_HEADS), np.ones((HEAD_DIM, HEAD_DIM))), dtype=bf16)
    rope_tabs = _rope_tables(dec_seq)
    ftab_ctx = _fourier_tables(seq)
    ftab_lat = _fourier_tables(dec_seq)
    gf = final_norm_g.reshape(1, D_MODEL)

    xp = x_prompt.reshape(batch * seq, D_MODEL)
    xs = x_sample.reshape(dec_batch * dec_seq, D_MODEL)
    gkv, wkv = [], []
    for l in range(DEPTH):
        w_ext, w_gate = _relayout_w_in(w_in[l])
        g1 = norm1_g[l].reshape(1, D_MODEL)
        g2 = norm2_g[l].reshape(1, D_MODEL)
        qg = jnp.tile(q_norm_g[l], N_HEADS).reshape(1, Q_W)
        kg = jnp.tile(k_norm_g[l], 2 * N_KV).reshape(1, KV_DUP_W)
        wba, wbb, wbc = (w.astype(bf16) for w in (w_branch_a[l], w_branch_b[l], w_branch_c[l]))
        wout = w_out[l].astype(bf16)
        wr = jnp.concatenate([w_route_expert[l], w_route_group[l],
                              jnp.zeros((D_MODEL, ROUTE_W - N_EXPERTS - N_EXPERT_GROUPS), f32)], axis=1)
        weg, weu, wed = (w.astype(bf16) for w in (w_exp_gate[l], w_exp_up[l], w_exp_down[l]))
        sink = sink_logit[l]
        final = l == DEPTH - 1

        ctx = dict(tokens_per_batch=seq, cond_row0=0, per_batch_cond=False)
        qa, ka, va, qc, kc, vc, ub, gk, gv, wk, wv = _s1_call(
            xp, mod[l], g1, w_ext, ones_blk, qg, kg, None, emit_kv=True, **ctx)
        oa, oc = _attn_ctx_call(sink, qa, ka, va, qc, kc, vc, seq)
        ob = _fourier_call(ub, *ftab_ctx, seq)
        x1, h2, wts = _s3_call(xp, oa, ob, oc, mod[l], g1, g2, w_gate, wba, wbb, wbc, wout, wr, **ctx)
        xp = _s4_call(x1, h2, wts, mod[l], gf, weg, weu, wed, final=final, **ctx)
        kv_shape = (batch, seq, N_KV, HEAD_DIM)
        gkv.append(jnp.stack([gk.reshape(kv_shape), gv.reshape(kv_shape)], axis=1))
        wkv.append(jnp.stack([wk.reshape(kv_shape), wv.reshape(kv_shape)], axis=1))

        lat = dict(tokens_per_batch=dec_seq, cond_row0=1, per_batch_cond=True)
        qa, ka, va, qc, kc, vc, ub = _s1_call(
            xs, mod[l], g1, w_ext, ones_blk, qg, kg, rope_tabs, emit_kv=False, **lat)
        kxa, vxa = _dup_cache(cache_global_kv, l)
        kxc, vxc = _dup_cache(cache_window_kv, l)
        oa, oc = _attn_lat_call(sink, qa, ka, va, kxa, vxa, qc, kc, vc, kxc, vxc, dec_seq)
        ob = _fourier_call(ub, *ftab_lat, dec_seq)
        x1, h2, wts = _s3_call(xs, oa, ob, oc, mod[l], g1, g2, w_gate, wba, wbb, wbc, wout, wr, **lat)
        xs = _s4_call(x1, h2, wts, mod[l], gf, weg, weu, wed, final=final, **lat)

    y_prompt = xp.reshape(batch, seq, D_MODEL)
    y_sample = xs.reshape(dec_batch, dec_seq, D_MODEL)
    return (y_prompt, y_sample, jnp.stack(gkv, axis=1), jnp.stack(wkv, axis=1))
```

```python
import functools

import jax
import jax.numpy as jnp
import numpy as np
from jax import lax
from jax.experimental import pallas as pl
from jax.experimental.pallas import tpu as pltpu

f32 = jnp.float32
bf16 = jnp.bfloat16

D_MODEL = 1024
DEPTH = 2
GRID_W = 64
HEAD_DIM = 64
HALF = HEAD_DIM // 2
N_HEADS = 8
N_KV = 2
WINDOW = 128
N_FOURIER_GROUPS = 4
FOURIER_GROUP_DIM = 128
FOURIER_DIM = N_FOURIER_GROUPS * FOURIER_GROUP_DIM
ROPE_THETA = 10000.0
AXIS_FREQS = HEAD_DIM // 4
N_EXPERT_GROUPS = 4
EXPERTS_PER_GROUP = 8
N_EXPERTS = N_EXPERT_GROUPS * EXPERTS_PER_GROUP
EXPERT_FF = 128
NORM_EPS = 1e-6
NEG_INF = -1e30

LANES = 128
Q_W = N_HEADS * HEAD_DIM
KV_DUP_W = 2 * N_KV * HEAD_DIM
C_QA, C_KA, C_VA, C_QC, C_KC, C_VC, C_UB, C_END = 0, 512, 768, 1024, 1536, 1792, 2048, 2560
N_COND_ROWS = 16
VMEM_LIMIT = 56 * 1024 * 1024


def _cparams(sem):
    return pltpu.CompilerParams(dimension_semantics=sem, vmem_limit_bytes=VMEM_LIMIT)


def _resident(shape):
    nd = len(shape)
    return pl.BlockSpec(shape, lambda *_: (0,) * nd, pipeline_mode=pl.Buffered(1))


def _sigmoid(x):
    return 1.0 / (1.0 + jnp.exp(-x))


def _rms_mod(x, g, scale, shift):
    y = x * lax.rsqrt(jnp.mean(x * x, axis=-1, keepdims=True) + NORM_EPS)
    return (y * g) * (1.0 + scale) + shift


def _dot(a, b):
    return jnp.dot(a, b, preferred_element_type=f32)


def _dot_t(a, b):
    return lax.dot_general(a, b, (((1,), (1,)), ((), ())), preferred_element_type=f32)


def _split_bf16(x):
    hi = x.astype(bf16)
    lo = (x - hi.astype(f32)).astype(bf16)
    return hi, lo


def _mod_kernel(cond_ref, w_ref, b_ref, o_ref):
    c = cond_ref[...]
    s = (c * _sigmoid(c)).astype(bf16)
    o_ref[0] = _dot(s, w_ref[0].astype(bf16)) + b_ref[0]


def _mod_call(cond, w_mod, b_mod):
    n_tiles = 6 * D_MODEL // 1024
    return pl.pallas_call(
        _mod_kernel,
        grid=(DEPTH, n_tiles),
        in_specs=[
            pl.BlockSpec((N_COND_ROWS, D_MODEL), lambda l, j: (0, 0)),
            pl.BlockSpec((1, D_MODEL, 1024), lambda l, j: (l, 0, j)),
            pl.BlockSpec((1, 1, 1024), lambda l, j: (l, 0, j)),
        ],
        out_specs=pl.BlockSpec((1, N_COND_ROWS, 1024), lambda l, j: (l, 0, j)),
        out_shape=jax.ShapeDtypeStruct((DEPTH, N_COND_ROWS, 6 * D_MODEL), f32),
        compiler_params=_cparams(("arbitrary", "arbitrary")),
        name="mod",
    )(cond, w_mod, b_mod.reshape(DEPTH, 1, 6 * D_MODEL))


def _head_norm(z, g, ones_blk):
    hi, lo = _split_bf16(z * z)
    ss = _dot(hi, ones_blk) + _dot(lo, ones_blk)
    return z * lax.rsqrt(ss * (1.0 / HEAD_DIM) + NORM_EPS) * g


def _rope(z, cos, sin_signed, first_half):
    outs = []
    for j in range(z.shape[1] // LANES):
        zj = z[:, j * LANES:(j + 1) * LANES]
        swapped = jnp.where(first_half, pltpu.roll(zj, LANES - HALF, 1), pltpu.roll(zj, HALF, 1))
        outs.append(zj * cos + swapped * sin_signed)
    return jnp.concatenate(outs, axis=1)


def _undup(d, low_lanes):
    return jnp.where(low_lanes, d[:, 0:LANES], d[:, LANES:2 * LANES])


def _layer_input(x_refs):
    if len(x_refs) == 1:
        return x_refs[0][...]
    x1_ref, moe_ref, modp_ref = x_refs
    return x1_ref[...] + modp_ref[0, 5:6, :] * moe_ref[...]


def _s1_kernel(*refs, rope, emit_kv, n_x):
    x_refs, refs = refs[:n_x], refs[n_x:]
    mod_ref, g1_ref, w_ref, ones_ref, qg_ref, kg_ref = refs[:6]
    pos = 6
    if rope:
        cos_ref, sin_ref = refs[pos:pos + 2]
        pos += 2
    qa_ref, ka_ref, va_ref, qc_ref, kc_ref, vc_ref, ub_ref = refs[pos:pos + 7]
    pos += 7
    if emit_kv:
        gk_ref, gv_ref, wk_ref, wv_ref = refs[pos:pos + 4]

    tm = x_refs[0].shape[0]
    h = _rms_mod(_layer_input(x_refs), g1_ref[...], mod_ref[0, 1:2, :], mod_ref[0, 0:1, :]).astype(bf16)

    def proj(lo, hi):
        return _dot(h, w_ref[:, lo:hi])

    lane = lax.broadcasted_iota(jnp.int32, (tm, LANES), 1)
    low_lanes = lane < HEAD_DIM
    if rope:
        cos = cos_ref[...]
        sin_signed = sin_ref[...]
        first_half = (lane & (HEAD_DIM - 1)) < HALF
        rot = lambda z: _rope(z, cos, sin_signed, first_half)
    else:
        rot = lambda z: z
    q_scale = HEAD_DIM ** -0.5

    qa = rot(_head_norm(proj(C_QA, C_KA), qg_ref[...], ones_ref[...]))
    qa_ref[...] = (qa * q_scale).astype(bf16)
    ka_n = _head_norm(proj(C_KA, C_VA), kg_ref[...], ones_ref[0:KV_DUP_W, 0:KV_DUP_W])
    ka_ref[...] = rot(ka_n).astype(bf16)
    va = proj(C_VA, C_QC)
    va_ref[...] = va.astype(bf16)
    qc_ref[...] = (rot(proj(C_QC, C_KC)) * q_scale).astype(bf16)
    kc = proj(C_KC, C_VC)
    kc_ref[...] = rot(kc).astype(bf16)
    vc = proj(C_VC, C_UB)
    vc_ref[...] = vc.astype(bf16)
    ub_ref[...] = proj(C_UB, C_END).astype(bf16)
    if emit_kv:
        gk_ref[...] = _undup(ka_n, low_lanes)
        gv_ref[...] = _undup(va, low_lanes)
        wk_ref[...] = _undup(kc, low_lanes)
        wv_ref[...] = _undup(vc, low_lanes)


TOKEN_TILE = 512


def _cond_index_map(tokens_per_batch, cond_row0, per_batch_cond):
    tiles_per_batch = max(tokens_per_batch // TOKEN_TILE, 1)
    if per_batch_cond:
        assert tokens_per_batch % TOKEN_TILE == 0
        return lambda t: (cond_row0 + t // tiles_per_batch, 0, 0)
    return lambda t: (cond_row0, 0, 0)


def _layer_input_specs(xin, mod_prev, cond_idx):
    tok = pl.BlockSpec((TOKEN_TILE, D_MODEL), lambda t: (t, 0))
    if len(xin) == 1:
        return [tok], list(xin)
    return [tok, tok, pl.BlockSpec((1, 6, D_MODEL), cond_idx)], [xin[0], xin[1], mod_prev]


def _s1_call(xin, mod_prev, mod, g1, w_ext, ones_blk, qg, kg, rope_tabs, *, tokens_per_batch, cond_row0,
             per_batch_cond, emit_kv):
    n_tok = xin[0].shape[0]
    tm = TOKEN_TILE
    rope = rope_tabs is not None
    tiles_per_batch = max(tokens_per_batch // tm, 1)
    cond_idx = _cond_index_map(tokens_per_batch, cond_row0, per_batch_cond)

    tok = lambda w: pl.BlockSpec((tm, w), lambda t: (t, 0))
    x_specs, x_args = _layer_input_specs(xin, mod_prev, cond_idx)
    in_specs = x_specs + [
        pl.BlockSpec((1, 6, D_MODEL), cond_idx),
        _resident((1, D_MODEL)),
        _resident((D_MODEL, C_END)),
        _resident((Q_W, Q_W)),
        _resident((1, Q_W)),
        _resident((1, KV_DUP_W)),
    ]
    args = x_args + [mod, g1, w_ext, ones_blk, qg, kg]
    if rope:
        in_specs += [pl.BlockSpec((tm, LANES), lambda t: (t % tiles_per_batch, 0))] * 2
        args += list(rope_tabs)
    widths = [Q_W, KV_DUP_W, KV_DUP_W, Q_W, KV_DUP_W, KV_DUP_W, FOURIER_DIM]
    out_specs = [tok(w) for w in widths]
    out_shape = [jax.ShapeDtypeStruct((n_tok, w), bf16) for w in widths]
    if emit_kv:
        out_specs += [tok(LANES)] * 4
        out_shape += [jax.ShapeDtypeStruct((n_tok, LANES), f32)] * 4
    return pl.pallas_call(
        functools.partial(_s1_kernel, rope=rope, emit_kv=emit_kv, n_x=len(x_args)),
        grid=(n_tok // tm,),
        in_specs=in_specs,
        out_specs=out_specs,
        out_shape=out_shape,
        compiler_params=_cparams(("arbitrary",)),
        name="s1_rope" if rope else "s1_ctx",
    )(*args)


def _gqa_group(q2, segs, sink_pair_lo, sink_pair_hi):
    qb = q2.shape[0]
    qs = jnp.concatenate([q2[:, 0:LANES], q2[:, LANES:2 * LANES]], axis=0)
    lane = lax.broadcasted_iota(jnp.int32, qs.shape, 1)
    low = lane < HEAD_DIM
    zero = jnp.zeros_like(qs)
    row = lax.broadcasted_iota(jnp.int32, (2 * qb, 1), 0)

    def half(qh, sink_pair):
        scores = []
        for k, _, ok in segs:
            s = _dot_t(qh, k)
            if ok is not None:
                s = jnp.where(ok, s, NEG_INF)
            scores.append(s)
        m = scores[0].max(axis=-1, keepdims=True)
        for s in scores[1:]:
            m = jnp.maximum(m, s.max(axis=-1, keepdims=True))
        if sink_pair is not None:
            sink = jnp.where(row < qb, sink_pair[0], sink_pair[1])
            m = jnp.maximum(m, sink)
            denom = jnp.exp(sink - m)
        else:
            denom = jnp.zeros_like(m)
        acc = None
        for s, (_, v, _) in zip(scores, segs):
            p = jnp.exp(s - m)
            denom = denom + p.sum(axis=-1, keepdims=True)
            pv = _dot(p.astype(bf16), v)
            acc = pv if acc is None else acc + pv
        return acc / denom

    o_lo = half(jnp.where(low, qs, zero), sink_pair_lo)
    o_hi = half(jnp.where(low, zero, qs), sink_pair_hi)
    o = jnp.where(low, o_lo, o_hi)
    return jnp.concatenate([o[0:qb], o[qb:2 * qb]], axis=1)


def _sink_pairs(sink_ref, kv):
    h = 4 * kv
    return (sink_ref[h], sink_ref[h + 2]), (sink_ref[h + 1], sink_ref[h + 3])


def _attn_ctx_kernel(sink_ref, qa_ref, ka_ref, va_ref, qc_ref, kc_ref, vc_ref, oa_ref, oc_ref):
    for kv in range(N_KV):
        cq = slice(2 * LANES * kv, 2 * LANES * (kv + 1))
        ck = slice(LANES * kv, LANES * (kv + 1))
        oa = _gqa_group(qa_ref[:, cq], [(ka_ref[:, ck], va_ref[:, ck], None)], None, None)
        oa_ref[:, cq] = oa.astype(bf16)
        lo, hi = _sink_pairs(sink_ref, kv)
        oc = _gqa_group(qc_ref[:, cq], [(kc_ref[:, ck], vc_ref[:, ck], None)], lo, hi)
        oc_ref[:, cq] = oc.astype(bf16)


def _attn_ctx_call(sink, qa, ka, va, qc, kc, vc, seq):
    n_tok = qa.shape[0]
    tok = lambda w: pl.BlockSpec((seq, w), lambda b: (b, 0))
    return pl.pallas_call(
        _attn_ctx_kernel,
        grid=(n_tok // seq,),
        in_specs=[pl.BlockSpec(memory_space=pltpu.SMEM),
                  tok(Q_W), tok(KV_DUP_W), tok(KV_DUP_W), tok(Q_W), tok(KV_DUP_W), tok(KV_DUP_W)],
        out_specs=[tok(Q_W), tok(Q_W)],
        out_shape=[jax.ShapeDtypeStruct((n_tok, Q_W), bf16)] * 2,
        compiler_params=_cparams(("arbitrary",)),
        name="attn_ctx",
    )(sink, qa, ka, va, qc, kc, vc)


QB = 256
WIN_QB = 128
WIN_SPAN = WIN_QB + 2 * WINDOW


def _attn_lat_kernel(sink_ref, qa_ref, ka_ref, va_ref, kxa_ref, vxa_ref,
                     qc_ref, kc_ref, vc_ref, kxc_ref, vxc_ref, oa_ref, oc_ref, *, seq):
    i = pl.program_id(1)
    for kv in range(N_KV):
        cq = slice(2 * LANES * kv, 2 * LANES * (kv + 1))
        ck = slice(LANES * kv, LANES * (kv + 1))
        segs = [(ka_ref[:, ck], va_ref[:, ck], None), (kxa_ref[0, :, ck], vxa_ref[0, :, ck], None)]
        oa_ref[:, cq] = _gqa_group(qa_ref[:, cq], segs, None, None).astype(bf16)

        lo, hi = _sink_pairs(sink_ref, kv)
        for sub in range(QB // WIN_QB):
            blk = i * (QB // WIN_QB) + sub
            start = jnp.clip((blk - 1) * WIN_QB, 0, seq - WIN_SPAN)
            start = pl.multiple_of(start, WIN_QB)
            r = lax.broadcasted_iota(jnp.int32, (2 * WIN_QB, WIN_SPAN), 0)
            c = lax.broadcasted_iota(jnp.int32, (2 * WIN_QB, WIN_SPAN), 1)
            qpos = blk * WIN_QB + (r & (WIN_QB - 1))
            ok = jnp.abs(qpos - (start + c)) <= WINDOW
            rows = slice(sub * WIN_QB, (sub + 1) * WIN_QB)
            segs = [(kc_ref[pl.ds(start, WIN_SPAN), ck], vc_ref[pl.ds(start, WIN_SPAN), ck], ok),
                    (kxc_ref[0, :, ck], vxc_ref[0, :, ck], None)]
            oc_ref[rows, cq] = _gqa_group(qc_ref[rows, cq], segs, lo, hi).astype(bf16)


def _attn_lat_call(sink, qa, ka, va, kxa, vxa, qc, kc, vc, kxc, vxc, seq):
    n_tok = qa.shape[0]
    nq = seq // QB
    past = kxa.shape[1]
    qspec = pl.BlockSpec((QB, Q_W), lambda b, i: (b * nq + i, 0))
    kspec = pl.BlockSpec((seq, KV_DUP_W), lambda b, i: (b, 0))
    xspec = pl.BlockSpec((1, past, KV_DUP_W), lambda b, i: (b, 0, 0))
    return pl.pallas_call(
        functools.partial(_attn_lat_kernel, seq=seq),
        grid=(n_tok // seq, nq),
        in_specs=[pl.BlockSpec(memory_space=pltpu.SMEM),
                  qspec, kspec, kspec, xspec, xspec, qspec, kspec, kspec, xspec, xspec],
        out_specs=[qspec, qspec],
        out_shape=[jax.ShapeDtypeStruct((n_tok, Q_W), bf16)] * 2,
        compiler_params=_cparams(("arbitrary", "arbitrary")),
        name="attn_lat",
    )(sink, qa, ka, va, kxa, vxa, qc, kc, vc, kxc, vxc)


def _fourier_kernel(u_ref, bdc_ref, bds_ref, cn_ref, sn_ref, o_ref, *, scale):
    u = u_ref[...]
    uc = _dot(u, bdc_ref[...]).astype(bf16)
    us = _dot(u, bds_ref[...]).astype(bf16)
    o = _dot(cn_ref[...], uc) - _dot(sn_ref[...], us)
    o_ref[...] = (o * scale).astype(bf16)


def _fourier_call(ub, bdc, bds, cn, sn, seq):
    n_tok = ub.shape[0]
    tok = pl.BlockSpec((seq, FOURIER_DIM), lambda b: (b, 0))
    scale = float(1.0 / np.sqrt(seq * FOURIER_GROUP_DIM))
    return pl.pallas_call(
        functools.partial(_fourier_kernel, scale=scale),
        grid=(n_tok // seq,),
        in_specs=[tok, _resident((FOURIER_DIM, FOURIER_DIM)), _resident((FOURIER_DIM, FOURIER_DIM)),
                  _resident((seq, seq)), _resident((seq, seq))],
        out_specs=tok,
        out_shape=jax.ShapeDtypeStruct((n_tok, FOURIER_DIM), bf16),
        compiler_params=_cparams(("arbitrary",)),
        name=f"fourier_{seq}",
    )(ub, bdc, bds, cn, sn)


ROUTE_ROWS = LANES
H2X_W = D_MODEL + LANES
RANK_BITS = 16


def _route(lt, tri_ref, carry_ref):
    tm = lt.shape[1]
    row = lax.broadcasted_iota(jnp.int32, (EXPERTS_PER_GROUP, tm), 0).astype(f32)
    neg = jnp.float32(-jnp.inf)
    none = jnp.float32(EXPERTS_PER_GROUP)
    gl = jnp.where(row < N_EXPERT_GROUPS, lt[N_EXPERTS:N_EXPERTS + EXPERTS_PER_GROUP], neg)
    gmax = gl.max(axis=0, keepdims=True)
    gidx = jnp.where(gl == gmax, row, none).min(axis=0, keepdims=True)
    gw = 1.0 / jnp.exp(gl - gmax).sum(axis=0, keepdims=True)
    lg = lt[0:EXPERTS_PER_GROUP]
    for g in range(1, N_EXPERT_GROUPS):
        lg = jnp.where(gidx == g, lt[g * EXPERTS_PER_GROUP:(g + 1) * EXPERTS_PER_GROUP], lg)
    m1 = lg.max(axis=0, keepdims=True)
    i1 = jnp.where(lg == m1, row, none).min(axis=0, keepdims=True)
    l2 = jnp.where(row == i1, neg, lg)
    m2 = l2.max(axis=0, keepdims=True)
    i2 = jnp.where(l2 == m2, row, none).min(axis=0, keepdims=True)
    e = jnp.exp(m2 - m1)
    w8 = jnp.where(row == i1, gw / (1.0 + e), jnp.where(row == i2, gw * e / (1.0 + e), 0.0))

    onehot = (row == gidx).astype(f32)
    prefix = _dot(onehot.astype(bf16), tri_ref[...])
    carry = carry_ref[...]
    rank = (onehot * (prefix - 1.0 + carry[:, 0:1])).sum(axis=0, keepdims=True)
    carry_ref[...] = carry + onehot.sum(axis=1, keepdims=True)
    code = gidx.astype(jnp.int32) * (1 << RANK_BITS) + rank.astype(jnp.int32)
    return w8, code


def _s3_kernel(*refs, n_x):
    x_refs, refs = refs[:n_x], refs[n_x:]
    (oa_ref, ob_ref, oc_ref, mod_ref, g1_ref, g2_ref, wg_ref, wba_ref, wbb_ref, wbc_ref,
     wout_ref, wrt_ref, tri_ref, x1_ref, h2x_ref, code_ref, cnt_ref, carry_ref) = refs

    @pl.when(pl.program_id(0) == 0)
    def _():
        carry_ref[...] = jnp.zeros_like(carry_ref)

    x = _layer_input(x_refs)
    h = _rms_mod(x, g1_ref[...], mod_ref[0, 1:2, :], mod_ref[0, 0:1, :]).astype(bf16)
    merged = None
    for j, (o_ref, wb_ref) in enumerate(((oa_ref, wba_ref), (ob_ref, wbb_ref), (oc_ref, wbc_ref))):
        gate = _sigmoid(_dot(h, wg_ref[:, j * D_MODEL:(j + 1) * D_MODEL]))
        term = gate * _dot(o_ref[...], wb_ref[...])
        merged = term if merged is None else merged + term
    mix = _dot(merged.astype(bf16), wout_ref[...])
    x1 = x + mod_ref[0, 2:3, :] * mix
    x1_ref[...] = x1
    h2 = _rms_mod(x1, g2_ref[...], mod_ref[0, 4:5, :], mod_ref[0, 3:4, :])
    h2x_ref[:, 0:D_MODEL] = h2
    th, tl = _split_bf16(h2)
    wh, wl = _split_bf16(wrt_ref[...])
    lt = _dot_t(wh, th) + (_dot_t(wh, tl) + _dot_t(wl, th))
    w8, code = _route(lt, tri_ref, carry_ref)
    code_ref[...] = code
    cnt_ref[...] = carry_ref[...]
    tm = x.shape[0]
    w_rows = jnp.concatenate([w8, jnp.zeros((LANES - EXPERTS_PER_GROUP, tm), f32)], axis=0)
    h2x_ref[:, D_MODEL:H2X_W] = w_rows.T


def _s3_call(xin, mod_prev, oa, ob, oc, mod, g1, g2, wg, wba, wbb, wbc, wout, wrt, tri, *, tokens_per_batch,
             cond_row0, per_batch_cond):
    n_tok = xin[0].shape[0]
    tm = TOKEN_TILE
    cond_idx = _cond_index_map(tokens_per_batch, cond_row0, per_batch_cond)
    tok = lambda w: pl.BlockSpec((tm, w), lambda t: (t, 0))
    x_specs, x_args = _layer_input_specs(xin, mod_prev, cond_idx)
    return pl.pallas_call(
        functools.partial(_s3_kernel, n_x=len(x_args)),
        grid=(n_tok // tm,),
        in_specs=x_specs + [
            tok(Q_W), tok(FOURIER_DIM), tok(Q_W),
            pl.BlockSpec((1, 6, D_MODEL), cond_idx),
            _resident((1, D_MODEL)), _resident((1, D_MODEL)),
            _resident((D_MODEL, 3 * D_MODEL)),
            _resident((Q_W, D_MODEL)), _resident((FOURIER_DIM, D_MODEL)), _resident((Q_W, D_MODEL)),
            _resident((D_MODEL, D_MODEL)), _resident((ROUTE_ROWS, D_MODEL)), _resident((tm, tm))],
        out_specs=[tok(D_MODEL), tok(H2X_W), pl.BlockSpec((1, tm), lambda t: (0, t)),
                   pl.BlockSpec((EXPERTS_PER_GROUP, LANES), lambda t: (0, 0))],
        out_shape=[jax.ShapeDtypeStruct((n_tok, D_MODEL), f32),
                   jax.ShapeDtypeStruct((n_tok, H2X_W), f32),
                   jax.ShapeDtypeStruct((1, n_tok), jnp.int32),
                   jax.ShapeDtypeStruct((EXPERTS_PER_GROUP, LANES), f32)],
        scratch_shapes=[pltpu.VMEM((EXPERTS_PER_GROUP, LANES), f32)],
        compiler_params=_cparams(("arbitrary",)),
        name="s3",
    )(*x_args, oa, ob, oc, mod, g1, g2, wg, wba, wbb, wbc, wout, wrt, tri)


MOE_TILE = 256
GROUP_FF = EXPERTS_PER_GROUP * EXPERT_FF
PLAN_BASE, PLAN_FIRST, PLAN_ITEMS, PLAN_LEN = 0, N_EXPERT_GROUPS, 2 * N_EXPERT_GROUPS, 2 * N_EXPERT_GROUPS + 1


def _moe_kernel(code_ref, cnt_ref, h2x_hbm, weg_ref, weu_ref, wed_ref, out_hbm,
                inv_ref, plan_ref, gbuf, obuf, dump, gsem, ssem, *, n_tok):
    tm = MOE_TILE
    k = pl.program_id(0)
    n_steps = pl.num_programs(0)

    @pl.when(k == 0)
    def _plan():
        base = jnp.int32(0)
        first = jnp.int32(0)
        for g in range(N_EXPERT_GROUPS):
            c = cnt_ref[g]
            n_tiles = (c + (tm - 1)) // tm
            plan_ref[PLAN_BASE + g] = base
            plan_ref[PLAN_FIRST + g] = first

            def pad(p, carry):
                inv_ref[p] = 0
                return carry

            lax.fori_loop(base + c, base + n_tiles * tm, pad, 0)
            base = base + n_tiles * tm
            first = first + n_tiles
        plan_ref[PLAN_ITEMS] = first

        def place(t, carry):
            cd = code_ref[t]
            inv_ref[plan_ref[PLAN_BASE + (cd >> RANK_BITS)] + (cd & ((1 << RANK_BITS) - 1))] = t
            return carry

        lax.fori_loop(0, n_tok, place, 0)

    n_items = plan_ref[PLAN_ITEMS]

    def gather_row(tok, slot, i):
        return pltpu.make_async_copy(h2x_hbm.at[pl.ds(tok, 1)], gbuf.at[slot, pl.ds(i, 1)], gsem.at[slot])

    def start_gather(item, slot):
        def body(i, carry):
            gather_row(inv_ref[item * tm + i], slot, i).start()
            return carry
        lax.fori_loop(0, tm, body, 0)

    def wait_gather(slot):
        pltpu.make_async_copy(h2x_hbm.at[pl.ds(0, tm)], gbuf.at[slot], gsem.at[slot]).wait()

    def start_scatter(item, n_valid, slot):
        def to_token(i, carry):
            dst = inv_ref[item * tm + i]
            pltpu.make_async_copy(obuf.at[slot, pl.ds(i, 1)], out_hbm.at[pl.ds(dst, 1)], ssem.at[slot]).start()
            return carry

        def to_dump(i, carry):
            pltpu.make_async_copy(obuf.at[slot, pl.ds(i, 1)], dump.at[slot, pl.ds(i, 1)], ssem.at[slot]).start()
            return carry

        lax.fori_loop(0, n_valid, to_token, 0)
        lax.fori_loop(n_valid, tm, to_dump, 0)

    def wait_scatter(slot):
        pltpu.make_async_copy(obuf.at[slot], out_hbm.at[pl.ds(0, tm)], ssem.at[slot]).wait()

    slot = k % 2

    @pl.when(k == 0)
    def _():
        start_gather(0, 0)

    @pl.when(k + 1 < n_items)
    def _():
        start_gather(k + 1, 1 - slot)

    @pl.when(k < n_items)
    def _():
        group = ((k >= plan_ref[PLAN_FIRST + 1]).astype(jnp.int32) + (k >= plan_ref[PLAN_FIRST + 2]).astype(jnp.int32)
                 + (k >= plan_ref[PLAN_FIRST + 3]).astype(jnp.int32))
        wait_gather(slot)
        rows = gbuf[slot]
        h = rows[:, 0:D_MODEL].astype(bf16)
        w8 = rows[:, D_MODEL:H2X_W]
        gate = _dot(h, weg_ref[group])
        a = gate * _sigmoid(gate) * _dot(h, weu_ref[group])
        pieces = [a[:, j * EXPERT_FF:(j + 1) * EXPERT_FF] * w8[:, j:j + 1] for j in range(EXPERTS_PER_GROUP)]
        out = _dot(jnp.concatenate(pieces, axis=1).astype(bf16), wed_ref[group])

        @pl.when(k >= 2)
        def _():
            wait_scatter(slot)

        obuf[slot] = out
        n_valid = jnp.minimum(tm, cnt_ref[group] - (k - plan_ref[PLAN_FIRST + group]) * tm)
        start_scatter(k, n_valid, slot)

    @pl.when(k == n_steps - 1)
    def _drain():
        wait_scatter((n_items - 1) % 2)

        @pl.when(n_items >= 2)
        def _():
            wait_scatter(n_items % 2)


def _moe_call(code, cnt, h2x, weg, weu, wed):
    n_tok = h2x.shape[0]
    tm = MOE_TILE
    n_steps = n_tok // tm + N_EXPERT_GROUPS - 1
    wspec = pl.BlockSpec((N_EXPERT_GROUPS, D_MODEL, GROUP_FF), lambda k, *_: (0, 0, 0), pipeline_mode=pl.Buffered(1))
    grid_spec = pltpu.PrefetchScalarGridSpec(
        num_scalar_prefetch=2,
        grid=(n_steps,),
        in_specs=[pl.BlockSpec(memory_space=pl.ANY), wspec, wspec, wspec],
        out_specs=pl.BlockSpec(memory_space=pl.ANY),
        scratch_shapes=[
            pltpu.SMEM((n_tok + N_EXPERT_GROUPS * tm,), jnp.int32),
            pltpu.SMEM((PLAN_LEN,), jnp.int32),
            pltpu.VMEM((2, tm, H2X_W), f32),
            pltpu.VMEM((2, tm, D_MODEL), f32),
            pltpu.VMEM((2, tm, D_MODEL), f32),
            pltpu.SemaphoreType.DMA((2,)),
            pltpu.SemaphoreType.DMA((2,)),
        ],
    )
    return pl.pallas_call(
        functools.partial(_moe_kernel, n_tok=n_tok),
        grid_spec=grid_spec,
        out_shape=jax.ShapeDtypeStruct((n_tok, D_MODEL), f32),
        compiler_params=_cparams(("arbitrary",)),
        name="moe",
    )(code, cnt, h2x, weg, weu, wed)


def _final_kernel(x1_ref, moe_ref, mod_ref, gf_ref, o_ref):
    x2 = x1_ref[...] + mod_ref[0, 5:6, :] * moe_ref[...]
    y = x2 * lax.rsqrt(jnp.mean(x2 * x2, axis=-1, keepdims=True) + NORM_EPS)
    o_ref[...] = y * gf_ref[...]


def _final_call(x1, moe, mod, gf, *, tokens_per_batch, cond_row0, per_batch_cond):
    n_tok = x1.shape[0]
    tok = pl.BlockSpec((TOKEN_TILE, D_MODEL), lambda t: (t, 0))
    cond_idx = _cond_index_map(tokens_per_batch, cond_row0, per_batch_cond)
    return pl.pallas_call(
        _final_kernel,
        grid=(n_tok // TOKEN_TILE,),
        in_specs=[tok, tok, pl.BlockSpec((1, 6, D_MODEL), cond_idx), _resident((1, D_MODEL))],
        out_specs=tok,
        out_shape=jax.ShapeDtypeStruct((n_tok, D_MODEL), f32),
        compiler_params=_cparams(("arbitrary",)),
        name="final_norm",
    )(x1, moe, mod, gf)


def _dup_heads(w):
    a, b = w[..., :HEAD_DIM], w[..., HEAD_DIM:]
    return jnp.concatenate([a, a, b, b], axis=-1)


def _relayout_w_in(w):
    qa, ka, va, qc, kc, vc, ub = (w[:, 0:512], w[:, 512:640], w[:, 640:768], w[:, 768:1280],
                                  w[:, 1280:1408], w[:, 1408:1536], w[:, 1536:2048])
    ext = jnp.concatenate([qa, _dup_heads(ka), _dup_heads(va), qc, _dup_heads(kc), _dup_heads(vc), ub], axis=1)
    return ext.astype(bf16), w[:, 2048:].astype(bf16)


def _rope_tables(n):
    rows = n // GRID_W
    row = jnp.repeat(jnp.arange(rows, dtype=f32), GRID_W)
    col = jnp.tile(jnp.arange(GRID_W, dtype=f32), rows)
    inv = ROPE_THETA ** (-jnp.arange(AXIS_FREQS, dtype=f32) / AXIS_FREQS)
    ang = jnp.concatenate([row[:, None] * inv, col[:, None] * inv], axis=-1)
    cos, sin = jnp.cos(ang), jnp.sin(ang)
    cos_t = jnp.concatenate([cos, cos, cos, cos], axis=-1)
    sin_t = jnp.concatenate([-sin, sin, -sin, sin], axis=-1)
    return cos_t, sin_t


def _dft_tables(n):
    k = np.arange(n)
    m = (k[:, None] * k[None, :]) % n
    ang = 2.0 * np.pi * m / n
    return np.cos(ang), np.sin(ang)


def _fourier_tables(seq):
    cn, sn = _dft_tables(seq)
    cc, sc = _dft_tables(FOURIER_GROUP_DIM)
    eye = np.eye(N_FOURIER_GROUPS)
    bdc, bds = np.kron(eye, cc), np.kron(eye, sc)
    return tuple(jnp.asarray(t, dtype=f32).astype(bf16) for t in (bdc, bds, cn, sn))


def _dup_cache(cache, l):
    b, _, _, p, _, _ = cache.shape
    k = _dup_heads(cache[:, l, 0].reshape(b, p, N_KV * HEAD_DIM)).astype(bf16)
    v = _dup_heads(cache[:, l, 1].reshape(b, p, N_KV * HEAD_DIM)).astype(bf16)
    return k, v


def kernel(x_prompt, x_sample, cache_global_kv, cache_window_kv, c, c_ctx, w_mod, b_mod, norm1_g, w_in,
           q_norm_g, k_norm_g, sink_logit, w_branch_a, w_branch_b, w_branch_c, w_out, norm2_g,
           w_route_group, w_route_expert, w_exp_gate, w_exp_up, w_exp_down, final_norm_g):
    batch, seq, _ = x_prompt.shape
    dec_batch, dec_seq, _ = x_sample.shape

    cond = jnp.zeros((N_COND_ROWS, D_MODEL), f32).at[0].set(c_ctx).at[1:1 + dec_batch].set(c)
    mod = _mod_call(cond, w_mod, b_mod).reshape(DEPTH, N_COND_ROWS, 6, D_MODEL)

    ones_blk = jnp.asarray(np.kron(np.eye(N_HEADS), np.ones((HEAD_DIM, HEAD_DIM))), dtype=bf16)
    tri = jnp.asarray(np.triu(np.ones((TOKEN_TILE, TOKEN_TILE))), dtype=bf16)
    rope_tabs = _rope_tables(dec_seq)
    ftab_ctx = _fourier_tables(seq)
    ftab_lat = _fourier_tables(dec_seq)
    gf = final_norm_g.reshape(1, D_MODEL)

    ctx = dict(tokens_per_batch=seq, cond_row0=0, per_batch_cond=False)
    lat = dict(tokens_per_batch=dec_seq, cond_row0=1, per_batch_cond=True)
    xp = (x_prompt.reshape(batch * seq, D_MODEL),)
    xs = (x_sample.reshape(dec_batch * dec_seq, D_MODEL),)
    mod_prev = None
    gkv, wkv = [], []
    for l in range(DEPTH):
        w_ext, w_gate = _relayout_w_in(w_in[l])
        g1 = norm1_g[l].reshape(1, D_MODEL)
        g2 = norm2_g[l].reshape(1, D_MODEL)
        qg = jnp.tile(q_norm_g[l], N_HEADS).reshape(1, Q_W)
        kg = jnp.tile(k_norm_g[l], 2 * N_KV).reshape(1, KV_DUP_W)
        wba, wbb, wbc = (w.astype(bf16) for w in (w_branch_a[l], w_branch_b[l], w_branch_c[l]))
        wout = w_out[l].astype(bf16)
        wrt = jnp.concatenate([w_route_expert[l].T, w_route_group[l].T,
                               jnp.zeros((ROUTE_ROWS - N_EXPERTS - N_EXPERT_GROUPS, D_MODEL), f32)], axis=0)
        by_group = lambda w: w.reshape(D_MODEL, N_EXPERT_GROUPS, GROUP_FF).transpose(1, 0, 2).astype(bf16)
        weg, weu = by_group(w_exp_gate[l]), by_group(w_exp_up[l])
        wed = w_exp_down[l].reshape(N_EXPERT_GROUPS, GROUP_FF, D_MODEL).astype(bf16)
        sink = sink_logit[l]

        def mix_and_moe(xin, qa, ka, va, qc, kc, vc, ub, attn, ftab, seq_len, stream):
            oa, oc = attn(qa, ka, va, qc, kc, vc)
            ob = _fourier_call(ub, *ftab, seq_len)
            x1, h2x, code, cnt = _s3_call(xin, mod_prev, oa, ob, oc, mod[l], g1, g2, w_gate, wba, wbb, wbc, wout,
                                          wrt, tri, **stream)
            counts = cnt[:, 0].astype(jnp.int32)
            moe = _moe_call(code.reshape(-1), counts, h2x, weg, weu, wed)
            return (x1, moe)

        qa, ka, va, qc, kc, vc, ub, gk, gv, wk, wv = _s1_call(
            xp, mod_prev, mod[l], g1, w_ext, ones_blk, qg, kg, None, emit_kv=True, **ctx)
        attn_ctx = lambda *qkv: _attn_ctx_call(sink, *qkv, seq)
        xp = mix_and_moe(xp, qa, ka, va, qc, kc, vc, ub, attn_ctx, ftab_ctx, seq, ctx)
        kv_shape = (batch, seq, N_KV, HEAD_DIM)
        gkv.append(jnp.stack([gk.reshape(kv_shape), gv.reshape(kv_shape)], axis=1))
        wkv.append(jnp.stack([wk.reshape(kv_shape), wv.reshape(kv_shape)], axis=1))

        qa, ka, va, qc, kc, vc, ub = _s1_call(
            xs, mod_prev, mod[l], g1, w_ext, ones_blk, qg, kg, rope_tabs, emit_kv=False, **lat)
        kxa, vxa = _dup_cache(cache_global_kv, l)
        kxc, vxc = _dup_cache(cache_window_kv, l)
        attn_lat = lambda qa, ka, va, qc, kc, vc: _attn_lat_call(
            sink, qa, ka, va, kxa, vxa, qc, kc, vc, kxc, vxc, dec_seq)
        xs = mix_and_moe(xs, qa, ka, va, qc, kc, vc, ub, attn_lat, ftab_lat, dec_seq, lat)
        mod_prev = mod[l]

    y_prompt = _final_call(*xp, mod_prev, gf, **ctx).reshape(batch, seq, D_MODEL)
    y_sample = _final_call(*xs, mod_prev, gf, **lat).reshape(dec_batch, dec_seq, D_MODEL)
    return (y_prompt, y_sample, jnp.stack(gkv, axis=1), jnp.stack(wkv, axis=1))
```

```python
import functools

import jax
import jax.numpy as jnp
import numpy as np
from jax import lax
from jax.experimental import pallas as pl
from jax.experimental.pallas import tpu as pltpu

f32 = jnp.float32
bf16 = jnp.bfloat16

D_MODEL = 1024
DEPTH = 2
GRID_W = 64
HEAD_DIM = 64
HALF = HEAD_DIM // 2
N_HEADS = 8
N_KV = 2
WINDOW = 128
N_FOURIER_GROUPS = 4
FOURIER_GROUP_DIM = 128
FOURIER_DIM = N_FOURIER_GROUPS * FOURIER_GROUP_DIM
ROPE_THETA = 10000.0
AXIS_FREQS = HEAD_DIM // 4
N_EXPERT_GROUPS = 4
EXPERTS_PER_GROUP = 8
N_EXPERTS = N_EXPERT_GROUPS * EXPERTS_PER_GROUP
EXPERT_FF = 128
NORM_EPS = 1e-6
NEG_INF = -1e30

LANES = 128
Q_W = N_HEADS * HEAD_DIM
KV_DUP_W = 2 * N_KV * HEAD_DIM
C_QA, C_KA, C_VA, C_QC, C_KC, C_VC, C_UB, C_END = 0, 512, 768, 1024, 1536, 1792, 2048, 2560
N_COND_ROWS = 16
VMEM_LIMIT = 56 * 1024 * 1024


def _cparams(sem):
    return pltpu.CompilerParams(dimension_semantics=sem, vmem_limit_bytes=VMEM_LIMIT)


def _resident(shape):
    nd = len(shape)
    return pl.BlockSpec(shape, lambda *_: (0,) * nd, pipeline_mode=pl.Buffered(1))


def _sigmoid(x):
    return 1.0 / (1.0 + jnp.exp(-x))


def _rms_mod(x, g, scale, shift):
    y = x * lax.rsqrt(jnp.mean(x * x, axis=-1, keepdims=True) + NORM_EPS)
    return (y * g) * (1.0 + scale) + shift


def _dot(a, b):
    return jnp.dot(a, b, preferred_element_type=f32)


def _dot_t(a, b):
    return lax.dot_general(a, b, (((1,), (1,)), ((), ())), preferred_element_type=f32)


def _split_bf16(x):
    hi = x.astype(bf16)
    lo = (x - hi.astype(f32)).astype(bf16)
    return hi, lo


def _mod_kernel(cond_ref, w_ref, b_ref, o_ref):
    c = cond_ref[...]
    s = (c * _sigmoid(c)).astype(bf16)
    o_ref[0] = _dot(s, w_ref[0].astype(bf16)) + b_ref[0]


def _mod_call(cond, w_mod, b_mod):
    n_tiles = 6 * D_MODEL // 1024
    return pl.pallas_call(
        _mod_kernel,
        grid=(DEPTH, n_tiles),
        in_specs=[
            pl.BlockSpec((N_COND_ROWS, D_MODEL), lambda l, j: (0, 0)),
            pl.BlockSpec((1, D_MODEL, 1024), lambda l, j: (l, 0, j)),
            pl.BlockSpec((1, 1, 1024), lambda l, j: (l, 0, j)),
        ],
        out_specs=pl.BlockSpec((1, N_COND_ROWS, 1024), lambda l, j: (l, 0, j)),
        out_shape=jax.ShapeDtypeStruct((DEPTH, N_COND_ROWS, 6 * D_MODEL), f32),
        compiler_params=_cparams(("arbitrary", "arbitrary")),
        name="mod",
    )(cond, w_mod, b_mod.reshape(DEPTH, 1, 6 * D_MODEL))


def _head_norm(z, g, ones_blk):
    hi, lo = _split_bf16(z * z)
    ss = _dot(hi, ones_blk) + _dot(lo, ones_blk)
    return z * lax.rsqrt(ss * (1.0 / HEAD_DIM) + NORM_EPS) * g


def _rope(z, cos, sin_signed, first_half):
    outs = []
    for j in range(z.shape[1] // LANES):
        zj = z[:, j * LANES:(j + 1) * LANES]
        swapped = jnp.where(first_half, pltpu.roll(zj, LANES - HALF, 1), pltpu.roll(zj, HALF, 1))
        outs.append(zj * cos + swapped * sin_signed)
    return jnp.concatenate(outs, axis=1)


def _undup(d, low_lanes):
    return jnp.where(low_lanes, d[:, 0:LANES], d[:, LANES:2 * LANES])


def _layer_input(x_refs):
    if len(x_refs) == 1:
        return x_refs[0][...]
    x1_ref, moe_ref, modp_ref = x_refs
    n = x1_ref.shape[0]
    return x1_ref[...] + modp_ref[0, 5:6, :] * _rows_from_slabs(moe_ref, 0, n, SLAB_ROWS)


SLAB_ROWS = D_MODEL // LANES


def _rows_from_slabs(ref, start, n_tok, pitch):
    return jnp.concatenate([ref[pl.ds(start + c, n_tok, stride=pitch), :] for c in range(SLAB_ROWS)], axis=1)


def _rows_to_slabs(ref, start, x, pitch):
    for c in range(SLAB_ROWS):
        ref[pl.ds(start + c, x.shape[0], stride=pitch), :] = x[:, c * LANES:(c + 1) * LANES]


def _s1_kernel(*refs, rope, emit_kv, n_x):
    x_refs, refs = refs[:n_x], refs[n_x:]
    mod_ref, g1_ref, w_ref, ones_ref, qg_ref, kg_ref = refs[:6]
    pos = 6
    if rope:
        cos_ref, sin_ref = refs[pos:pos + 2]
        pos += 2
    qa_ref, ka_ref, va_ref, qc_ref, kc_ref, vc_ref, ub_ref = refs[pos:pos + 7]
    pos += 7
    if emit_kv:
        gk_ref, gv_ref, wk_ref, wv_ref = refs[pos:pos + 4]

    tm = x_refs[0].shape[0]
    h = _rms_mod(_layer_input(x_refs), g1_ref[...], mod_ref[0, 1:2, :], mod_ref[0, 0:1, :]).astype(bf16)

    def proj(lo, hi):
        return _dot(h, w_ref[:, lo:hi])

    lane = lax.broadcasted_iota(jnp.int32, (tm, LANES), 1)
    low_lanes = lane < HEAD_DIM
    if rope:
        cos = cos_ref[...]
        sin_signed = sin_ref[...]
        first_half = (lane & (HEAD_DIM - 1)) < HALF
        rot = lambda z: _rope(z, cos, sin_signed, first_half)
    else:
        rot = lambda z: z
    q_scale = HEAD_DIM ** -0.5

    qa = rot(_head_norm(proj(C_QA, C_KA), qg_ref[...], ones_ref[...]))
    qa_ref[...] = (qa * q_scale).astype(bf16)
    ka_n = _head_norm(proj(C_KA, C_VA), kg_ref[...], ones_ref[0:KV_DUP_W, 0:KV_DUP_W])
    ka_ref[...] = rot(ka_n).astype(bf16)
    va = proj(C_VA, C_QC)
    va_ref[...] = va.astype(bf16)
    qc_ref[...] = (rot(proj(C_QC, C_KC)) * q_scale).astype(bf16)
    kc = proj(C_KC, C_VC)
    kc_ref[...] = rot(kc).astype(bf16)
    vc = proj(C_VC, C_UB)
    vc_ref[...] = vc.astype(bf16)
    ub_ref[...] = proj(C_UB, C_END).astype(bf16)
    if emit_kv:
        gk_ref[...] = _undup(ka_n, low_lanes)
        gv_ref[...] = _undup(va, low_lanes)
        wk_ref[...] = _undup(kc, low_lanes)
        wv_ref[...] = _undup(vc, low_lanes)


TOKEN_TILE = 512


def _cond_index_map(tokens_per_batch, cond_row0, per_batch_cond):
    tiles_per_batch = max(tokens_per_batch // TOKEN_TILE, 1)
    if per_batch_cond:
        assert tokens_per_batch % TOKEN_TILE == 0
        return lambda t: (cond_row0 + t // tiles_per_batch, 0, 0)
    return lambda t: (cond_row0, 0, 0)


def _layer_input_specs(xin, mod_prev, cond_idx):
    tok = pl.BlockSpec((TOKEN_TILE, D_MODEL), lambda t: (t, 0))
    if len(xin) == 1:
        return [tok], list(xin)
    slab = pl.BlockSpec((TOKEN_TILE * SLAB_ROWS, LANES), lambda t: (t, 0))
    return [tok, slab, pl.BlockSpec((1, 6, D_MODEL), cond_idx)], [xin[0], xin[1], mod_prev]


def _s1_call(xin, mod_prev, mod, g1, w_ext, ones_blk, qg, kg, rope_tabs, *, tokens_per_batch, cond_row0,
             per_batch_cond, emit_kv):
    n_tok = xin[0].shape[0]
    tm = TOKEN_TILE
    rope = rope_tabs is not None
    tiles_per_batch = max(tokens_per_batch // tm, 1)
    cond_idx = _cond_index_map(tokens_per_batch, cond_row0, per_batch_cond)

    tok = lambda w: pl.BlockSpec((tm, w), lambda t: (t, 0))
    x_specs, x_args = _layer_input_specs(xin, mod_prev, cond_idx)
    in_specs = x_specs + [
        pl.BlockSpec((1, 6, D_MODEL), cond_idx),
        _resident((1, D_MODEL)),
        _resident((D_MODEL, C_END)),
        _resident((Q_W, Q_W)),
        _resident((1, Q_W)),
        _resident((1, KV_DUP_W)),
    ]
    args = x_args + [mod, g1, w_ext, ones_blk, qg, kg]
    if rope:
        in_specs += [pl.BlockSpec((tm, LANES), lambda t: (t % tiles_per_batch, 0))] * 2
        args += list(rope_tabs)
    widths = [Q_W, KV_DUP_W, KV_DUP_W, Q_W, KV_DUP_W, KV_DUP_W, FOURIER_DIM]
    out_specs = [tok(w) for w in widths]
    out_shape = [jax.ShapeDtypeStruct((n_tok, w), bf16) for w in widths]
    if emit_kv:
        out_specs += [tok(LANES)] * 4
        out_shape += [jax.ShapeDtypeStruct((n_tok, LANES), f32)] * 4
    return pl.pallas_call(
        functools.partial(_s1_kernel, rope=rope, emit_kv=emit_kv, n_x=len(x_args)),
        grid=(n_tok // tm,),
        in_specs=in_specs,
        out_specs=out_specs,
        out_shape=out_shape,
        compiler_params=_cparams(("arbitrary",)),
        name="s1_rope" if rope else "s1_ctx",
    )(*args)


def _gqa_group(q2, segs, sink_pair_lo, sink_pair_hi):
    qb = q2.shape[0]
    qs = jnp.concatenate([q2[:, 0:LANES], q2[:, LANES:2 * LANES]], axis=0)
    lane = lax.broadcasted_iota(jnp.int32, qs.shape, 1)
    low = lane < HEAD_DIM
    zero = jnp.zeros_like(qs)
    row = lax.broadcasted_iota(jnp.int32, (2 * qb, 1), 0)

    def half(qh, sink_pair):
        scores = []
        for k, _, ok in segs:
            s = _dot_t(qh, k)
            if ok is not None:
                s = jnp.where(ok, s, NEG_INF)
            scores.append(s)
        m = scores[0].max(axis=-1, keepdims=True)
        for s in scores[1:]:
            m = jnp.maximum(m, s.max(axis=-1, keepdims=True))
        if sink_pair is not None:
            sink = jnp.where(row < qb, sink_pair[0], sink_pair[1])
            m = jnp.maximum(m, sink)
            denom = jnp.exp(sink - m)
        else:
            denom = jnp.zeros_like(m)
        acc = None
        for s, (_, v, _) in zip(scores, segs):
            p = jnp.exp(s - m)
            denom = denom + p.sum(axis=-1, keepdims=True)
            pv = _dot(p.astype(bf16), v)
            acc = pv if acc is None else acc + pv
        return acc / denom

    o_lo = half(jnp.where(low, qs, zero), sink_pair_lo)
    o_hi = half(jnp.where(low, zero, qs), sink_pair_hi)
    o = jnp.where(low, o_lo, o_hi)
    return jnp.concatenate([o[0:qb], o[qb:2 * qb]], axis=1)


def _sink_pairs(sink_ref, kv):
    h = 4 * kv
    return (sink_ref[h], sink_ref[h + 2]), (sink_ref[h + 1], sink_ref[h + 3])


def _attn_ctx_kernel(sink_ref, qa_ref, ka_ref, va_ref, qc_ref, kc_ref, vc_ref, oa_ref, oc_ref):
    for kv in range(N_KV):
        cq = slice(2 * LANES * kv, 2 * LANES * (kv + 1))
        ck = slice(LANES * kv, LANES * (kv + 1))
        oa = _gqa_group(qa_ref[:, cq], [(ka_ref[:, ck], va_ref[:, ck], None)], None, None)
        oa_ref[:, cq] = oa.astype(bf16)
        lo, hi = _sink_pairs(sink_ref, kv)
        oc = _gqa_group(qc_ref[:, cq], [(kc_ref[:, ck], vc_ref[:, ck], None)], lo, hi)
        oc_ref[:, cq] = oc.astype(bf16)


def _attn_ctx_call(sink, qa, ka, va, qc, kc, vc, seq):
    n_tok = qa.shape[0]
    tok = lambda w: pl.BlockSpec((seq, w), lambda b: (b, 0))
    return pl.pallas_call(
        _attn_ctx_kernel,
        grid=(n_tok // seq,),
        in_specs=[pl.BlockSpec(memory_space=pltpu.SMEM),
                  tok(Q_W), tok(KV_DUP_W), tok(KV_DUP_W), tok(Q_W), tok(KV_DUP_W), tok(KV_DUP_W)],
        out_specs=[tok(Q_W), tok(Q_W)],
        out_shape=[jax.ShapeDtypeStruct((n_tok, Q_W), bf16)] * 2,
        compiler_params=_cparams(("arbitrary",)),
        name="attn_ctx",
    )(sink, qa, ka, va, qc, kc, vc)


QB = 256
WIN_QB = 128
WIN_SPAN = WIN_QB + 2 * WINDOW


def _attn_lat_kernel(sink_ref, qa_ref, ka_ref, va_ref, kxa_ref, vxa_ref,
                     qc_ref, kc_ref, vc_ref, kxc_ref, vxc_ref, oa_ref, oc_ref, *, seq):
    i = pl.program_id(1)
    for kv in range(N_KV):
        cq = slice(2 * LANES * kv, 2 * LANES * (kv + 1))
        ck = slice(LANES * kv, LANES * (kv + 1))
        segs = [(ka_ref[:, ck], va_ref[:, ck], None), (kxa_ref[0, :, ck], vxa_ref[0, :, ck], None)]
        oa_ref[:, cq] = _gqa_group(qa_ref[:, cq], segs, None, None).astype(bf16)

        lo, hi = _sink_pairs(sink_ref, kv)
        for sub in range(QB // WIN_QB):
            blk = i * (QB // WIN_QB) + sub
            start = jnp.clip((blk - 1) * WIN_QB, 0, seq - WIN_SPAN)
            start = pl.multiple_of(start, WIN_QB)
            r = lax.broadcasted_iota(jnp.int32, (2 * WIN_QB, WIN_SPAN), 0)
            c = lax.broadcasted_iota(jnp.int32, (2 * WIN_QB, WIN_SPAN), 1)
            qpos = blk * WIN_QB + (r & (WIN_QB - 1))
            ok = jnp.abs(qpos - (start + c)) <= WINDOW
            rows = slice(sub * WIN_QB, (sub + 1) * WIN_QB)
            segs = [(kc_ref[pl.ds(start, WIN_SPAN), ck], vc_ref[pl.ds(start, WIN_SPAN), ck], ok),
                    (kxc_ref[0, :, ck], vxc_ref[0, :, ck], None)]
            oc_ref[rows, cq] = _gqa_group(qc_ref[rows, cq], segs, lo, hi).astype(bf16)


def _attn_lat_call(sink, qa, ka, va, kxa, vxa, qc, kc, vc, kxc, vxc, seq):
    n_tok = qa.shape[0]
    nq = seq // QB
    past = kxa.shape[1]
    qspec = pl.BlockSpec((QB, Q_W), lambda b, i: (b * nq + i, 0))
    kspec = pl.BlockSpec((seq, KV_DUP_W), lambda b, i: (b, 0))
    xspec = pl.BlockSpec((1, past, KV_DUP_W), lambda b, i: (b, 0, 0))
    return pl.pallas_call(
        functools.partial(_attn_lat_kernel, seq=seq),
        grid=(n_tok // seq, nq),
        in_specs=[pl.BlockSpec(memory_space=pltpu.SMEM),
                  qspec, kspec, kspec, xspec, xspec, qspec, kspec, kspec, xspec, xspec],
        out_specs=[qspec, qspec],
        out_shape=[jax.ShapeDtypeStruct((n_tok, Q_W), bf16)] * 2,
        compiler_params=_cparams(("arbitrary", "arbitrary")),
        name="attn_lat",
    )(sink, qa, ka, va, kxa, vxa, qc, kc, vc, kxc, vxc)


def _fourier_kernel(u_ref, bdc_ref, bds_ref, cn_ref, sn_ref, o_ref, *, scale):
    u = u_ref[...]
    uc = _dot(u, bdc_ref[...]).astype(bf16)
    us = _dot(u, bds_ref[...]).astype(bf16)
    o = _dot(cn_ref[...], uc) - _dot(sn_ref[...], us)
    o_ref[...] = (o * scale).astype(bf16)


def _fourier_call(ub, bdc, bds, cn, sn, seq):
    n_tok = ub.shape[0]
    tok = pl.BlockSpec((seq, FOURIER_DIM), lambda b: (b, 0))
    scale = float(1.0 / np.sqrt(seq * FOURIER_GROUP_DIM))
    return pl.pallas_call(
        functools.partial(_fourier_kernel, scale=scale),
        grid=(n_tok // seq,),
        in_specs=[tok, _resident((FOURIER_DIM, FOURIER_DIM)), _resident((FOURIER_DIM, FOURIER_DIM)),
                  _resident((seq, seq)), _resident((seq, seq))],
        out_specs=tok,
        out_shape=jax.ShapeDtypeStruct((n_tok, FOURIER_DIM), bf16),
        compiler_params=_cparams(("arbitrary",)),
        name=f"fourier_{seq}",
    )(ub, bdc, bds, cn, sn)


ROUTE_ROWS = LANES
H2X_PITCH = SLAB_ROWS + 1
MOE_PITCH = SLAB_ROWS
RANK_BITS = 16


def _route(lt, tri_ref, carry_ref):
    tm = lt.shape[1]
    row = lax.broadcasted_iota(jnp.int32, (EXPERTS_PER_GROUP, tm), 0).astype(f32)
    neg = jnp.float32(-jnp.inf)
    none = jnp.float32(EXPERTS_PER_GROUP)
    gl = jnp.where(row < N_EXPERT_GROUPS, lt[N_EXPERTS:N_EXPERTS + EXPERTS_PER_GROUP], neg)
    gmax = gl.max(axis=0, keepdims=True)
    gidx = jnp.where(gl == gmax, row, none).min(axis=0, keepdims=True)
    gw = 1.0 / jnp.exp(gl - gmax).sum(axis=0, keepdims=True)
    lg = lt[0:EXPERTS_PER_GROUP]
    for g in range(1, N_EXPERT_GROUPS):
        lg = jnp.where(gidx == g, lt[g * EXPERTS_PER_GROUP:(g + 1) * EXPERTS_PER_GROUP], lg)
    m1 = lg.max(axis=0, keepdims=True)
    i1 = jnp.where(lg == m1, row, none).min(axis=0, keepdims=True)
    l2 = jnp.where(row == i1, neg, lg)
    m2 = l2.max(axis=0, keepdims=True)
    i2 = jnp.where(l2 == m2, row, none).min(axis=0, keepdims=True)
    e = jnp.exp(m2 - m1)
    w8 = jnp.where(row == i1, gw / (1.0 + e), jnp.where(row == i2, gw * e / (1.0 + e), 0.0))

    onehot = (row == gidx).astype(f32)
    prefix = _dot(onehot.astype(bf16), tri_ref[...])
    carry = carry_ref[...]
    rank = (onehot * (prefix - 1.0 + carry[:, 0:1])).sum(axis=0, keepdims=True)
    carry_ref[...] = carry + onehot.sum(axis=1, keepdims=True)
    code = gidx.astype(jnp.int32) * (1 << RANK_BITS) + rank.astype(jnp.int32)
    return w8, code


def _s3_kernel(*refs, n_x):
    x_refs, refs = refs[:n_x], refs[n_x:]
    (oa_ref, ob_ref, oc_ref, mod_ref, g1_ref, g2_ref, wg_ref, wba_ref, wbb_ref, wbc_ref,
     wout_ref, wrt_ref, tri_ref, x1_ref, h2x_ref, code_ref, cnt_ref, carry_ref) = refs

    @pl.when(pl.program_id(0) == 0)
    def _():
        carry_ref[...] = jnp.zeros_like(carry_ref)

    x = _layer_input(x_refs)
    h = _rms_mod(x, g1_ref[...], mod_ref[0, 1:2, :], mod_ref[0, 0:1, :]).astype(bf16)
    merged = None
    for j, (o_ref, wb_ref) in enumerate(((oa_ref, wba_ref), (ob_ref, wbb_ref), (oc_ref, wbc_ref))):
        gate = _sigmoid(_dot(h, wg_ref[:, j * D_MODEL:(j + 1) * D_MODEL]))
        term = gate * _dot(o_ref[...], wb_ref[...])
        merged = term if merged is None else merged + term
    mix = _dot(merged.astype(bf16), wout_ref[...])
    x1 = x + mod_ref[0, 2:3, :] * mix
    x1_ref[...] = x1
    h2 = _rms_mod(x1, g2_ref[...], mod_ref[0, 4:5, :], mod_ref[0, 3:4, :])
    _rows_to_slabs(h2x_ref, 0, h2, H2X_PITCH)
    th, tl = _split_bf16(h2)
    wh, wl = _split_bf16(wrt_ref[...])
    lt = _dot_t(wh, th) + (_dot_t(wh, tl) + _dot_t(wl, th))
    w8, code = _route(lt, tri_ref, carry_ref)
    code_ref[...] = code
    cnt_ref[...] = carry_ref[...]
    tm = x.shape[0]
    w_rows = jnp.concatenate([w8, jnp.zeros((LANES - EXPERTS_PER_GROUP, tm), f32)], axis=0)
    h2x_ref[pl.ds(SLAB_ROWS, tm, stride=H2X_PITCH), :] = w_rows.T


def _s3_call(xin, mod_prev, oa, ob, oc, mod, g1, g2, wg, wba, wbb, wbc, wout, wrt, tri, *, tokens_per_batch,
             cond_row0, per_batch_cond):
    n_tok = xin[0].shape[0]
    tm = TOKEN_TILE
    cond_idx = _cond_index_map(tokens_per_batch, cond_row0, per_batch_cond)
    tok = lambda w: pl.BlockSpec((tm, w), lambda t: (t, 0))
    x_specs, x_args = _layer_input_specs(xin, mod_prev, cond_idx)
    return pl.pallas_call(
        functools.partial(_s3_kernel, n_x=len(x_args)),
        grid=(n_tok // tm,),
        in_specs=x_specs + [
            tok(Q_W), tok(FOURIER_DIM), tok(Q_W),
            pl.BlockSpec((1, 6, D_MODEL), cond_idx),
            _resident((1, D_MODEL)), _resident((1, D_MODEL)),
            _resident((D_MODEL, 3 * D_MODEL)),
            _resident((Q_W, D_MODEL)), _resident((FOURIER_DIM, D_MODEL)), _resident((Q_W, D_MODEL)),
            _resident((D_MODEL, D_MODEL)), _resident((ROUTE_ROWS, D_MODEL)), _resident((tm, tm))],
        out_specs=[tok(D_MODEL), pl.BlockSpec((tm * H2X_PITCH, LANES), lambda t: (t, 0)),
                   pl.BlockSpec((1, tm), lambda t: (0, t)),
                   pl.BlockSpec((EXPERTS_PER_GROUP, LANES), lambda t: (0, 0))],
        out_shape=[jax.ShapeDtypeStruct((n_tok, D_MODEL), f32),
                   jax.ShapeDtypeStruct((n_tok * H2X_PITCH, LANES), f32),
                   jax.ShapeDtypeStruct((1, n_tok), jnp.int32),
                   jax.ShapeDtypeStruct((EXPERTS_PER_GROUP, LANES), f32)],
        scratch_shapes=[pltpu.VMEM((EXPERTS_PER_GROUP, LANES), f32)],
        compiler_params=_cparams(("arbitrary",)),
        name="s3",
    )(*x_args, oa, ob, oc, mod, g1, g2, wg, wba, wbb, wbc, wout, wrt, tri)


MOE_TILE_LOG2 = 8
MOE_TILE = 1 << MOE_TILE_LOG2
ROW_UNROLL_LOG2 = 3
ROW_UNROLL = 1 << ROW_UNROLL_LOG2
GROUP_FF = EXPERTS_PER_GROUP * EXPERT_FF
PLAN_BASE, PLAN_FIRST, PLAN_ITEMS, PLAN_LEN = 0, N_EXPERT_GROUPS, 2 * N_EXPERT_GROUPS, 2 * N_EXPERT_GROUPS + 1


def _moe_kernel(code_ref, cnt_ref, h2x_hbm, weg_ref, weu_ref, wed_ref, out_hbm,
                pos_ref, plan_ref, gbuf, obuf, dump, gsem, ssem, *, n_tok):
    tm = MOE_TILE
    k = pl.program_id(0)
    n_steps = pl.num_programs(0)
    g_rows = tm * H2X_PITCH
    o_rows = tm * MOE_PITCH

    def out_slab(buf, slot, i):
        return buf.at[pl.ds(pl.multiple_of(slot * o_rows + i * MOE_PITCH, MOE_PITCH), MOE_PITCH)]

    def token_slab(tok):
        return out_hbm.at[pl.ds(pl.multiple_of(tok * MOE_PITCH, MOE_PITCH), MOE_PITCH)]

    def for_row_chunks(n_chunks, body):
        def chunk(c, carry):
            body(c * ROW_UNROLL)
            return carry
        lax.fori_loop(0, n_chunks, chunk, 0)

    @pl.when(k == 0)
    def _plan():
        base = jnp.int32(0)
        first = jnp.int32(0)
        for g in range(N_EXPERT_GROUPS):
            c = cnt_ref[g]
            n_tiles = (c + (tm - 1)) >> MOE_TILE_LOG2
            plan_ref[PLAN_BASE + g] = base
            plan_ref[PLAN_FIRST + g] = first
            end = base + n_tiles * tm

            def pad(p, carry):
                pos_ref[p] = 0
                return carry

            lax.fori_loop(base + c, end, pad, 0)
            base = end
            first = first + n_tiles
        plan_ref[PLAN_ITEMS] = first

        def place(t0):
            codes = [code_ref[t0 + j] for j in range(ROW_UNROLL)]
            for j, cd in enumerate(codes):
                pos_ref[plan_ref[PLAN_BASE + (cd >> RANK_BITS)] + (cd & ((1 << RANK_BITS) - 1))] = t0 + j

        for_row_chunks(n_tok >> ROW_UNROLL_LOG2, place)

    n_items = plan_ref[PLAN_ITEMS]

    def start_gather(item, slot):
        def chunk(i0):
            toks = [pos_ref[item * tm + i0 + j] for j in range(ROW_UNROLL)]
            for j, tok in enumerate(toks):
                pltpu.make_async_copy(h2x_hbm.at[pl.ds(tok * H2X_PITCH, H2X_PITCH)],
                                      gbuf.at[pl.ds(slot * g_rows + (i0 + j) * H2X_PITCH, H2X_PITCH)],
                                      gsem.at[slot]).start()
        for_row_chunks(tm >> ROW_UNROLL_LOG2, chunk)

    def wait_gather(slot):
        pltpu.make_async_copy(h2x_hbm.at[pl.ds(0, g_rows)], gbuf.at[pl.ds(slot * g_rows, g_rows)],
                              gsem.at[slot]).wait()

    def start_scatter(item, n_valid, slot):
        def to_token(i0):
            dsts = [pos_ref[item * tm + i0 + j] for j in range(ROW_UNROLL)]
            for j, dst in enumerate(dsts):
                pltpu.make_async_copy(out_slab(obuf, slot, i0 + j), token_slab(dst), ssem.at[slot]).start()

        def row_to_token(i, carry):
            pltpu.make_async_copy(out_slab(obuf, slot, i), token_slab(pos_ref[item * tm + i]), ssem.at[slot]).start()
            return carry

        def row_to_dump(i, carry):
            pltpu.make_async_copy(out_slab(obuf, slot, i), out_slab(dump, slot, i), ssem.at[slot]).start()
            return carry

        full = n_valid >> ROW_UNROLL_LOG2
        for_row_chunks(full, to_token)
        lax.fori_loop(full * ROW_UNROLL, n_valid, row_to_token, 0)
        lax.fori_loop(n_valid, tm, row_to_dump, 0)

    def wait_scatter(slot):
        pltpu.make_async_copy(obuf.at[pl.ds(slot * o_rows, o_rows)], out_hbm.at[pl.ds(0, o_rows)],
                              ssem.at[slot]).wait()

    slot = k % 2

    @pl.when(k == 0)
    def _():
        start_gather(0, 0)

    @pl.when(k + 1 < n_items)
    def _():
        start_gather(k + 1, 1 - slot)

    @pl.when(k < n_items)
    def _():
        group = ((k >= plan_ref[PLAN_FIRST + 1]).astype(jnp.int32) + (k >= plan_ref[PLAN_FIRST + 2]).astype(jnp.int32)
                 + (k >= plan_ref[PLAN_FIRST + 3]).astype(jnp.int32))
        wait_gather(slot)
        h = _rows_from_slabs(gbuf, slot * g_rows, tm, H2X_PITCH).astype(bf16)
        w8 = gbuf[pl.ds(slot * g_rows + SLAB_ROWS, tm, stride=H2X_PITCH), :]
        gate = _dot(h, weg_ref[group])
        a = gate * _sigmoid(gate) * _dot(h, weu_ref[group])
        pieces = [a[:, j * EXPERT_FF:(j + 1) * EXPERT_FF] * w8[:, j:j + 1] for j in range(EXPERTS_PER_GROUP)]
        out = _dot(jnp.concatenate(pieces, axis=1).astype(bf16), wed_ref[group])

        @pl.when(k >= 2)
        def _():
            wait_scatter(slot)

        _rows_to_slabs(obuf, slot * o_rows, out, MOE_PITCH)
        n_valid = jnp.minimum(tm, cnt_ref[group] - (k - plan_ref[PLAN_FIRST + group]) * tm)
        start_scatter(k, n_valid, slot)

    @pl.when(k == n_steps - 1)
    def _drain():
        wait_scatter((n_items - 1) % 2)

        @pl.when(n_items >= 2)
        def _():
            wait_scatter(n_items % 2)


def _moe_call(code, cnt, h2x, weg, weu, wed):
    n_tok = h2x.shape[0] // H2X_PITCH
    tm = MOE_TILE
    n_steps = n_tok // tm + N_EXPERT_GROUPS - 1
    wspec = pl.BlockSpec((N_EXPERT_GROUPS, D_MODEL, GROUP_FF), lambda k, *_: (0, 0, 0), pipeline_mode=pl.Buffered(1))
    grid_spec = pltpu.PrefetchScalarGridSpec(
        num_scalar_prefetch=2,
        grid=(n_steps,),
        in_specs=[pl.BlockSpec(memory_space=pl.ANY), wspec, wspec, wspec],
        out_specs=pl.BlockSpec(memory_space=pl.ANY),
        scratch_shapes=[
            pltpu.SMEM((n_tok + N_EXPERT_GROUPS * tm,), jnp.int32),
            pltpu.SMEM((PLAN_LEN,), jnp.int32),
            pltpu.VMEM((2 * tm * H2X_PITCH, LANES), f32),
            pltpu.VMEM((2 * tm * MOE_PITCH, LANES), f32),
            pltpu.VMEM((2 * tm * MOE_PITCH, LANES), f32),
            pltpu.SemaphoreType.DMA((2,)),
            pltpu.SemaphoreType.DMA((2,)),
        ],
    )
    return pl.pallas_call(
        functools.partial(_moe_kernel, n_tok=n_tok),
        grid_spec=grid_spec,
        out_shape=jax.ShapeDtypeStruct((n_tok * MOE_PITCH, LANES), f32),
        compiler_params=_cparams(("arbitrary",)),
        name="moe",
    )(code, cnt, h2x, weg, weu, wed)


def _final_kernel(x1_ref, moe_ref, mod_ref, gf_ref, o_ref):
    x2 = x1_ref[...] + mod_ref[0, 5:6, :] * _rows_from_slabs(moe_ref, 0, x1_ref.shape[0], MOE_PITCH)
    y = x2 * lax.rsqrt(jnp.mean(x2 * x2, axis=-1, keepdims=True) + NORM_EPS)
    o_ref[...] = y * gf_ref[...]


def _final_call(x1, moe, mod, gf, *, tokens_per_batch, cond_row0, per_batch_cond):
    n_tok = x1.shape[0]
    tok = pl.BlockSpec((TOKEN_TILE, D_MODEL), lambda t: (t, 0))
    slab = pl.BlockSpec((TOKEN_TILE * MOE_PITCH, LANES), lambda t: (t, 0))
    cond_idx = _cond_index_map(tokens_per_batch, cond_row0, per_batch_cond)
    return pl.pallas_call(
        _final_kernel,
        grid=(n_tok // TOKEN_TILE,),
        in_specs=[tok, slab, pl.BlockSpec((1, 6, D_MODEL), cond_idx), _resident((1, D_MODEL))],
        out_specs=tok,
        out_shape=jax.ShapeDtypeStruct((n_tok, D_MODEL), f32),
        compiler_params=_cparams(("arbitrary",)),
        name="final_norm",
    )(x1, moe, mod, gf)


def _dup_heads(w):
    a, b = w[..., :HEAD_DIM], w[..., HEAD_DIM:]
    return jnp.concatenate([a, a, b, b], axis=-1)


def _relayout_w_in(w):
    qa, ka, va, qc, kc, vc, ub = (w[:, 0:512], w[:, 512:640], w[:, 640:768], w[:, 768:1280],
                                  w[:, 1280:1408], w[:, 1408:1536], w[:, 1536:2048])
    ext = jnp.concatenate([qa, _dup_heads(ka), _dup_heads(va), qc, _dup_heads(kc), _dup_heads(vc), ub], axis=1)
    return ext.astype(bf16), w[:, 2048:].astype(bf16)


def _rope_tables(n):
    rows = n // GRID_W
    row = jnp.repeat(jnp.arange(rows, dtype=f32), GRID_W)
    col = jnp.tile(jnp.arange(GRID_W, dtype=f32), rows)
    inv = ROPE_THETA ** (-jnp.arange(AXIS_FREQS, dtype=f32) / AXIS_FREQS)
    ang = jnp.concatenate([row[:, None] * inv, col[:, None] * inv], axis=-1)
    cos, sin = jnp.cos(ang), jnp.sin(ang)
    cos_t = jnp.concatenate([cos, cos, cos, cos], axis=-1)
    sin_t = jnp.concatenate([-sin, sin, -sin, sin], axis=-1)
    return cos_t, sin_t


def _dft_tables(n):
    k = np.arange(n)
    m = (k[:, None] * k[None, :]) % n
    ang = 2.0 * np.pi * m / n
    return np.cos(ang), np.sin(ang)


def _fourier_tables(seq):
    cn, sn = _dft_tables(seq)
    cc, sc = _dft_tables(FOURIER_GROUP_DIM)
    eye = np.eye(N_FOURIER_GROUPS)
    bdc, bds = np.kron(eye, cc), np.kron(eye, sc)
    return tuple(jnp.asarray(t, dtype=f32).astype(bf16) for t in (bdc, bds, cn, sn))


def _dup_cache(cache, l):
    b, _, _, p, _, _ = cache.shape
    k = _dup_heads(cache[:, l, 0].reshape(b, p, N_KV * HEAD_DIM)).astype(bf16)
    v = _dup_heads(cache[:, l, 1].reshape(b, p, N_KV * HEAD_DIM)).astype(bf16)
    return k, v


def kernel(x_prompt, x_sample, cache_global_kv, cache_window_kv, c, c_ctx, w_mod, b_mod, norm1_g, w_in,
           q_norm_g, k_norm_g, sink_logit, w_branch_a, w_branch_b, w_branch_c, w_out, norm2_g,
           w_route_group, w_route_expert, w_exp_gate, w_exp_up, w_exp_down, final_norm_g):
    batch, seq, _ = x_prompt.shape
    dec_batch, dec_seq, _ = x_sample.shape

    cond = jnp.zeros((N_COND_ROWS, D_MODEL), f32).at[0].set(c_ctx).at[1:1 + dec_batch].set(c)
    mod = _mod_call(cond, w_mod, b_mod).reshape(DEPTH, N_COND_ROWS, 6, D_MODEL)

    ones_blk = jnp.asarray(np.kron(np.eye(N_HEADS), np.ones((HEAD_DIM, HEAD_DIM))), dtype=bf16)
    tri = jnp.asarray(np.triu(np.ones((TOKEN_TILE, TOKEN_TILE))), dtype=bf16)
    rope_tabs = _rope_tables(dec_seq)
    ftab_ctx = _fourier_tables(seq)
    ftab_lat = _fourier_tables(dec_seq)
    gf = final_norm_g.reshape(1, D_MODEL)

    ctx = dict(tokens_per_batch=seq, cond_row0=0, per_batch_cond=False)
    lat = dict(tokens_per_batch=dec_seq, cond_row0=1, per_batch_cond=True)
    xp = (x_prompt.reshape(batch * seq, D_MODEL),)
    xs = (x_sample.reshape(dec_batch * dec_seq, D_MODEL),)
    mod_prev = None
    gkv, wkv = [], []
    for l in range(DEPTH):
        w_ext, w_gate = _relayout_w_in(w_in[l])
        g1 = norm1_g[l].reshape(1, D_MODEL)
        g2 = norm2_g[l].reshape(1, D_MODEL)
        qg = jnp.tile(q_norm_g[l], N_HEADS).reshape(1, Q_W)
        kg = jnp.tile(k_norm_g[l], 2 * N_KV).reshape(1, KV_DUP_W)
        wba, wbb, wbc = (w.astype(bf16) for w in (w_branch_a[l], w_branch_b[l], w_branch_c[l]))
        wout = w_out[l].astype(bf16)
        wrt = jnp.concatenate([w_route_expert[l].T, w_route_group[l].T,
                               jnp.zeros((ROUTE_ROWS - N_EXPERTS - N_EXPERT_GROUPS, D_MODEL), f32)], axis=0)
        by_group = lambda w: w.reshape(D_MODEL, N_EXPERT_GROUPS, GROUP_FF).transpose(1, 0, 2).astype(bf16)
        weg, weu = by_group(w_exp_gate[l]), by_group(w_exp_up[l])
        wed = w_exp_down[l].reshape(N_EXPERT_GROUPS, GROUP_FF, D_MODEL).astype(bf16)
        sink = sink_logit[l]

        def mix_and_moe(xin, qa, ka, va, qc, kc, vc, ub, attn, ftab, seq_len, stream):
            oa, oc = attn(qa, ka, va, qc, kc, vc)
            ob = _fourier_call(ub, *ftab, seq_len)
            x1, h2x, code, cnt = _s3_call(xin, mod_prev, oa, ob, oc, mod[l], g1, g2, w_gate, wba, wbb, wbc, wout,
                                          wrt, tri, **stream)
            counts = cnt[:, 0].astype(jnp.int32)
            moe = _moe_call(code.reshape(-1), counts, h2x, weg, weu, wed)
            return (x1, moe)

        qa, ka, va, qc, kc, vc, ub, gk, gv, wk, wv = _s1_call(
            xp, mod_prev, mod[l], g1, w_ext, ones_blk, qg, kg, None, emit_kv=True, **ctx)
        attn_ctx = lambda *qkv: _attn_ctx_call(sink, *qkv, seq)
        xp = mix_and_moe(xp, qa, ka, va, qc, kc, vc, ub, attn_ctx, ftab_ctx, seq, ctx)
        kv_shape = (batch, seq, N_KV, HEAD_DIM)
        gkv.append(jnp.stack([gk.reshape(kv_shape), gv.reshape(kv_shape)], axis=1))
        wkv.append(jnp.stack([wk.reshape(kv_shape), wv.reshape(kv_shape)], axis=1))

        qa, ka, va, qc, kc, vc, ub = _s1_call(
            xs, mod_prev, mod[l], g1, w_ext, ones_blk, qg, kg, rope_tabs, emit_kv=False, **lat)
        kxa, vxa = _dup_cache(cache_global_kv, l)
        kxc, vxc = _dup_cache(cache_window_kv, l)
        attn_lat = lambda qa, ka, va, qc, kc, vc: _attn_lat_call(
            sink, qa, ka, va, kxa, vxa, qc, kc, vc, kxc, vxc, dec_seq)
        xs = mix_and_moe(xs, qa, ka, va, qc, kc, vc, ub, attn_lat, ftab_lat, dec_seq, lat)
        mod_prev = mod[l]

    y_prompt = _final_call(*xp, mod_prev, gf, **ctx).reshape(batch, seq, D_MODEL)
    y_sample = _final_call(*xs, mod_prev, gf, **lat).reshape(dec_batch, dec_seq, D_MODEL)
    return (y_prompt, y_sample, jnp.stack(gkv, axis=1), jnp.stack(wkv, axis=1))
```

```python
import functools

import jax
import jax.numpy as jnp
import numpy as np
from jax import lax
from jax.experimental import pallas as pl
from jax.experimental.pallas import tpu as pltpu

f32 = jnp.float32
bf16 = jnp.bfloat16

D_MODEL = 1024
DEPTH = 2
GRID_W = 64
HEAD_DIM = 64
HALF = HEAD_DIM // 2
N_HEADS = 8
N_KV = 2
WINDOW = 128
N_FOURIER_GROUPS = 4
FOURIER_GROUP_DIM = 128
FOURIER_DIM = N_FOURIER_GROUPS * FOURIER_GROUP_DIM
ROPE_THETA = 10000.0
AXIS_FREQS = HEAD_DIM // 4
N_EXPERT_GROUPS = 4
EXPERTS_PER_GROUP = 8
N_EXPERTS = N_EXPERT_GROUPS * EXPERTS_PER_GROUP
EXPERT_FF = 128
NORM_EPS = 1e-6
NEG_INF = -1e30

LANES = 128
Q_W = N_HEADS * HEAD_DIM
KV_DUP_W = 2 * N_KV * HEAD_DIM
C_QA, C_KA, C_VA, C_QC, C_KC, C_VC, C_UB, C_GATES, C_END = 0, 512, 640, 768, 1280, 1408, 1536, 2048, 5120
N_COND_ROWS = 16
VMEM_LIMIT = 56 * 1024 * 1024


def _cparams(sem):
    return pltpu.CompilerParams(dimension_semantics=sem, vmem_limit_bytes=VMEM_LIMIT)


def _resident(shape):
    nd = len(shape)
    return pl.BlockSpec(shape, lambda *_: (0,) * nd, pipeline_mode=pl.Buffered(1))


def _sigmoid(x):
    return 1.0 / (1.0 + jnp.exp(-x))


def _rms_mod(x, g, scale, shift):
    y = x * lax.rsqrt(jnp.mean(x * x, axis=-1, keepdims=True) + NORM_EPS)
    return (y * g) * (1.0 + scale) + shift


def _dot(a, b):
    return jnp.dot(a, b, preferred_element_type=f32)


def _dot_t(a, b):
    return lax.dot_general(a, b, (((1,), (1,)), ((), ())), preferred_element_type=f32)


def _split_bf16(x):
    hi = x.astype(bf16)
    lo = (x - hi.astype(f32)).astype(bf16)
    return hi, lo


def _mod_kernel(cond_ref, w_ref, b_ref, o_ref):
    c = cond_ref[...]
    s = (c * _sigmoid(c)).astype(bf16)
    o_ref[0] = _dot(s, w_ref[0].astype(bf16)) + b_ref[0]


def _mod_call(cond, w_mod, b_mod):
    n_tiles = 6 * D_MODEL // 1024
    return pl.pallas_call(
        _mod_kernel,
        grid=(DEPTH, n_tiles),
        in_specs=[
            pl.BlockSpec((N_COND_ROWS, D_MODEL), lambda l, j: (0, 0)),
            pl.BlockSpec((1, D_MODEL, 1024), lambda l, j: (l, 0, j)),
            pl.BlockSpec((1, 1, 1024), lambda l, j: (l, 0, j)),
        ],
        out_specs=pl.BlockSpec((1, N_COND_ROWS, 1024), lambda l, j: (l, 0, j)),
        out_shape=jax.ShapeDtypeStruct((DEPTH, N_COND_ROWS, 6 * D_MODEL), f32),
        compiler_params=_cparams(("arbitrary", "arbitrary")),
        name="mod",
    )(cond, w_mod, b_mod.reshape(DEPTH, 1, 6 * D_MODEL))


def _head_norm(z, g, ones_blk):
    hi, lo = _split_bf16(z * z)
    ss = _dot(hi, ones_blk) + _dot(lo, ones_blk)
    return z * lax.rsqrt(ss * (1.0 / HEAD_DIM) + NORM_EPS) * g


def _rope(z, cos, sin_signed, first_half):
    outs = []
    for j in range(z.shape[1] // LANES):
        zj = z[:, j * LANES:(j + 1) * LANES]
        swapped = jnp.where(first_half, pltpu.roll(zj, LANES - HALF, 1), pltpu.roll(zj, HALF, 1))
        outs.append(zj * cos + swapped * sin_signed)
    return jnp.concatenate(outs, axis=1)


def _layer_input(x_refs):
    if len(x_refs) == 1:
        return x_refs[0][...]
    x1_ref, moe_ref, modp_ref = x_refs
    n = x1_ref.shape[0]
    return x1_ref[...] + modp_ref[0, 5:6, :] * _rows_from_slabs(moe_ref, 0, n, SLAB_ROWS)


SLAB_ROWS = D_MODEL // LANES


def _rows_from_slabs(ref, start, n_tok, pitch):
    return jnp.concatenate([ref[pl.ds(start + c, n_tok, stride=pitch), :] for c in range(SLAB_ROWS)], axis=1)


def _rows_to_slabs(ref, start, x, pitch):
    for c in range(SLAB_ROWS):
        ref[pl.ds(start + c, x.shape[0], stride=pitch), :] = x[:, c * LANES:(c + 1) * LANES]


def _s1_kernel(*refs, rope, emit_kv, n_x):
    x_refs, refs = refs[:n_x], refs[n_x:]
    mod_ref, g1_ref, w_ref, ones_ref, qg_ref, kg_ref = refs[:6]
    pos = 6
    if rope:
        cos_ref, sin_ref = refs[pos:pos + 2]
        pos += 2
    qa_ref, ka_ref, va_ref, qc_ref, kc_ref, vc_ref, ub_ref = refs[pos:pos + 7]
    pos += 7
    if emit_kv:
        gk_ref, gv_ref, wk_ref, wv_ref = refs[pos:pos + 4]

    tm = x_refs[0].shape[0]
    h = _rms_mod(_layer_input(x_refs), g1_ref[...], mod_ref[0, 1:2, :], mod_ref[0, 0:1, :]).astype(bf16)

    def proj(lo, hi):
        return _dot(h, w_ref[:, lo:hi])

    lane = lax.broadcasted_iota(jnp.int32, (tm, LANES), 1)
    low_lanes = lane < HEAD_DIM
    if rope:
        cos = cos_ref[...]
        sin_signed = sin_ref[...]
        first_half = (lane & (HEAD_DIM - 1)) < HALF
        rot = lambda z: _rope(z, cos, sin_signed, first_half)
    else:
        rot = lambda z: z
    q_scale = HEAD_DIM ** -0.5

    def dup(z):
        r = pltpu.roll(z, HEAD_DIM, 1)
        return jnp.concatenate([jnp.where(low_lanes, z, r), jnp.where(low_lanes, r, z)], axis=1).astype(bf16)

    qa = rot(_head_norm(proj(C_QA, C_KA), qg_ref[...], ones_ref[...]))
    qa_ref[...] = (qa * q_scale).astype(bf16)
    ka_n = _head_norm(proj(C_KA, C_VA), kg_ref[...], ones_ref[0:LANES, 0:LANES])
    ka_ref[...] = dup(rot(ka_n))
    va = proj(C_VA, C_QC)
    va_ref[...] = dup(va)
    qc_ref[...] = (rot(proj(C_QC, C_KC)) * q_scale).astype(bf16)
    kc = proj(C_KC, C_VC)
    kc_ref[...] = dup(rot(kc))
    vc = proj(C_VC, C_UB)
    vc_ref[...] = dup(vc)
    ub_ref[...] = proj(C_UB, C_GATES).astype(bf16)
    if emit_kv:
        gk_ref[...] = ka_n
        gv_ref[...] = va
        wk_ref[...] = kc
        wv_ref[...] = vc


TOKEN_TILE = 512


def _cond_index_map(tokens_per_batch, cond_row0, per_batch_cond):
    tiles_per_batch = max(tokens_per_batch // TOKEN_TILE, 1)
    if per_batch_cond:
        assert tokens_per_batch % TOKEN_TILE == 0
        return lambda t: (cond_row0 + t // tiles_per_batch, 0, 0)
    return lambda t: (cond_row0, 0, 0)


def _layer_input_specs(xin, mod_prev, cond_idx):
    tok = pl.BlockSpec((TOKEN_TILE, D_MODEL), lambda t: (t, 0))
    if len(xin) == 1:
        return [tok], list(xin)
    slab = pl.BlockSpec((TOKEN_TILE * SLAB_ROWS, LANES), lambda t: (t, 0))
    return [tok, slab, pl.BlockSpec((1, 6, D_MODEL), cond_idx)], [xin[0], xin[1], mod_prev]


def _s1_call(xin, mod_prev, mod, g1, w_ext, ones_blk, qg, kg, rope_tabs, *, tokens_per_batch, cond_row0,
             per_batch_cond, emit_kv):
    n_tok = xin[0].shape[0]
    tm = TOKEN_TILE
    rope = rope_tabs is not None
    tiles_per_batch = max(tokens_per_batch // tm, 1)
    cond_idx = _cond_index_map(tokens_per_batch, cond_row0, per_batch_cond)

    tok = lambda w: pl.BlockSpec((tm, w), lambda t: (t, 0))
    x_specs, x_args = _layer_input_specs(xin, mod_prev, cond_idx)
    in_specs = x_specs + [
        pl.BlockSpec((1, 6, D_MODEL), cond_idx),
        _resident((1, D_MODEL)),
        _resident((D_MODEL, C_GATES)),
        _resident((Q_W, Q_W)),
        _resident((1, Q_W)),
        _resident((1, LANES)),
    ]
    args = x_args + [mod, g1, w_ext, ones_blk, qg, kg]
    if rope:
        in_specs += [pl.BlockSpec((tm, LANES), lambda t: (t % tiles_per_batch, 0))] * 2
        args += list(rope_tabs)
    widths = [Q_W, KV_DUP_W, KV_DUP_W, Q_W, KV_DUP_W, KV_DUP_W, FOURIER_DIM]
    out_specs = [tok(w) for w in widths]
    out_shape = [jax.ShapeDtypeStruct((n_tok, w), bf16) for w in widths]
    if emit_kv:
        out_specs += [tok(LANES)] * 4
        out_shape += [jax.ShapeDtypeStruct((n_tok, LANES), f32)] * 4
    return pl.pallas_call(
        functools.partial(_s1_kernel, rope=rope, emit_kv=emit_kv, n_x=len(x_args)),
        grid=(n_tok // tm,),
        in_specs=in_specs,
        out_specs=out_specs,
        out_shape=out_shape,
        compiler_params=_cparams(("arbitrary",)),
        name="s1_rope" if rope else "s1_ctx",
    )(*args)


def _gqa_group(q2, segs, sink_pair_lo, sink_pair_hi):
    qb = q2.shape[0]
    qs = jnp.concatenate([q2[:, 0:LANES], q2[:, LANES:2 * LANES]], axis=0)
    lane = lax.broadcasted_iota(jnp.int32, qs.shape, 1)
    low = lane < HEAD_DIM
    zero = jnp.zeros_like(qs)
    row = lax.broadcasted_iota(jnp.int32, (2 * qb, 1), 0)

    def half(qh, sink_pair):
        scores = []
        for k, _, ok in segs:
            s = _dot_t(qh, k)
            if ok is not None:
                s = jnp.where(ok, s, NEG_INF)
            scores.append(s)
        m = scores[0].max(axis=-1, keepdims=True)
        for s in scores[1:]:
            m = jnp.maximum(m, s.max(axis=-1, keepdims=True))
        if sink_pair is not None:
            sink = jnp.where(row < qb, sink_pair[0], sink_pair[1])
            m = jnp.maximum(m, sink)
            denom = jnp.exp(sink - m)
        else:
            denom = jnp.zeros_like(m)
        acc = None
        for s, (_, v, _) in zip(scores, segs):
            p = jnp.exp(s - m)
            denom = denom + p.sum(axis=-1, keepdims=True)
            pv = _dot(p.astype(bf16), v)
            acc = pv if acc is None else acc + pv
        return acc / denom

    o_lo = half(jnp.where(low, qs, zero), sink_pair_lo)
    o_hi = half(jnp.where(low, zero, qs), sink_pair_hi)
    o = jnp.where(low, o_lo, o_hi)
    return jnp.concatenate([o[0:qb], o[qb:2 * qb]], axis=1)


def _sink_pairs(sink_ref, kv):
    h = 4 * kv
    return (sink_ref[h], sink_ref[h + 2]), (sink_ref[h + 1], sink_ref[h + 3])


def _attn_ctx_kernel(sink_ref, qa_ref, ka_ref, va_ref, qc_ref, kc_ref, vc_ref, oa_ref, oc_ref):
    for kv in range(N_KV):
        cq = slice(2 * LANES * kv, 2 * LANES * (kv + 1))
        ck = slice(LANES * kv, LANES * (kv + 1))
        oa = _gqa_group(qa_ref[:, cq], [(ka_ref[:, ck], va_ref[:, ck], None)], None, None)
        oa_ref[:, cq] = oa.astype(bf16)
        lo, hi = _sink_pairs(sink_ref, kv)
        oc = _gqa_group(qc_ref[:, cq], [(kc_ref[:, ck], vc_ref[:, ck], None)], lo, hi)
        oc_ref[:, cq] = oc.astype(bf16)


def _attn_ctx_call(sink, qa, ka, va, qc, kc, vc, seq):
    n_tok = qa.shape[0]
    tok = lambda w: pl.BlockSpec((seq, w), lambda b: (b, 0))
    return pl.pallas_call(
        _attn_ctx_kernel,
        grid=(n_tok // seq,),
        in_specs=[pl.BlockSpec(memory_space=pltpu.SMEM),
                  tok(Q_W), tok(KV_DUP_W), tok(KV_DUP_W), tok(Q_W), tok(KV_DUP_W), tok(KV_DUP_W)],
        out_specs=[tok(Q_W), tok(Q_W)],
        out_shape=[jax.ShapeDtypeStruct((n_tok, Q_W), bf16)] * 2,
        compiler_params=_cparams(("arbitrary",)),
        name="attn_ctx",
    )(sink, qa, ka, va, qc, kc, vc)


QB = 256
WIN_QB = 128
WIN_SPAN = WIN_QB + 2 * WINDOW


def _attn_lat_kernel(sink_ref, qa_ref, ka_ref, va_ref, kxa_ref, vxa_ref,
                     qc_ref, kc_ref, vc_ref, kxc_ref, vxc_ref, oa_ref, oc_ref, *, seq):
    i = pl.program_id(1)
    for kv in range(N_KV):
        cq = slice(2 * LANES * kv, 2 * LANES * (kv + 1))
        ck = slice(LANES * kv, LANES * (kv + 1))
        segs = [(ka_ref[:, ck], va_ref[:, ck], None), (kxa_ref[0, :, ck], vxa_ref[0, :, ck], None)]
        oa_ref[:, cq] = _gqa_group(qa_ref[:, cq], segs, None, None).astype(bf16)

        lo, hi = _sink_pairs(sink_ref, kv)
        for sub in range(QB // WIN_QB):
            blk = i * (QB // WIN_QB) + sub
            start = jnp.clip((blk - 1) * WIN_QB, 0, seq - WIN_SPAN)
            start = pl.multiple_of(start, WIN_QB)
            r = lax.broadcasted_iota(jnp.int32, (2 * WIN_QB, WIN_SPAN), 0)
            c = lax.broadcasted_iota(jnp.int32, (2 * WIN_QB, WIN_SPAN), 1)
            qpos = blk * WIN_QB + (r & (WIN_QB - 1))
            ok = jnp.abs(qpos - (start + c)) <= WINDOW
            rows = slice(sub * WIN_QB, (sub + 1) * WIN_QB)
            segs = [(kc_ref[pl.ds(start, WIN_SPAN), ck], vc_ref[pl.ds(start, WIN_SPAN), ck], ok),
                    (kxc_ref[0, :, ck], vxc_ref[0, :, ck], None)]
            oc_ref[rows, cq] = _gqa_group(qc_ref[rows, cq], segs, lo, hi).astype(bf16)


def _attn_lat_call(sink, qa, ka, va, kxa, vxa, qc, kc, vc, kxc, vxc, seq):
    n_tok = qa.shape[0]
    nq = seq // QB
    past = kxa.shape[1]
    qspec = pl.BlockSpec((QB, Q_W), lambda b, i: (b * nq + i, 0))
    kspec = pl.BlockSpec((seq, KV_DUP_W), lambda b, i: (b, 0))
    xspec = pl.BlockSpec((1, past, KV_DUP_W), lambda b, i: (b, 0, 0))
    return pl.pallas_call(
        functools.partial(_attn_lat_kernel, seq=seq),
        grid=(n_tok // seq, nq),
        in_specs=[pl.BlockSpec(memory_space=pltpu.SMEM),
                  qspec, kspec, kspec, xspec, xspec, qspec, kspec, kspec, xspec, xspec],
        out_specs=[qspec, qspec],
        out_shape=[jax.ShapeDtypeStruct((n_tok, Q_W), bf16)] * 2,
        compiler_params=_cparams(("arbitrary", "arbitrary")),
        name="attn_lat",
    )(sink, qa, ka, va, kxa, vxa, qc, kc, vc, kxc, vxc)


def _fourier_kernel(u_ref, bdc_ref, bds_ref, cn_ref, sn_ref, o_ref, *, scale):
    u = u_ref[...]
    uc = _dot(u, bdc_ref[...]).astype(bf16)
    us = _dot(u, bds_ref[...]).astype(bf16)
    o = _dot(cn_ref[...], uc) - _dot(sn_ref[...], us)
    o_ref[...] = (o * scale).astype(bf16)


def _fourier_call(ub, bdc, bds, cn, sn, seq):
    n_tok = ub.shape[0]
    tok = pl.BlockSpec((seq, FOURIER_DIM), lambda b: (b, 0))
    scale = float(1.0 / np.sqrt(seq * FOURIER_GROUP_DIM))
    return pl.pallas_call(
        functools.partial(_fourier_kernel, scale=scale),
        grid=(n_tok // seq,),
        in_specs=[tok, _resident((FOURIER_DIM, FOURIER_DIM)), _resident((FOURIER_DIM, FOURIER_DIM)),
                  _resident((seq, seq)), _resident((seq, seq))],
        out_specs=tok,
        out_shape=jax.ShapeDtypeStruct((n_tok, FOURIER_DIM), bf16),
        compiler_params=_cparams(("arbitrary",)),
        name=f"fourier_{seq}",
    )(ub, bdc, bds, cn, sn)


ROUTE_ROWS = LANES
H2X_PITCH = SLAB_ROWS + 1
MOE_PITCH = SLAB_ROWS
RANK_BITS = 16


def _route(lt, tri_ref, carry_ref):
    tm = lt.shape[1]
    row = lax.broadcasted_iota(jnp.int32, (EXPERTS_PER_GROUP, tm), 0).astype(f32)
    neg = jnp.float32(-jnp.inf)
    none = jnp.float32(EXPERTS_PER_GROUP)
    gl = jnp.where(row < N_EXPERT_GROUPS, lt[N_EXPERTS:N_EXPERTS + EXPERTS_PER_GROUP], neg)
    gmax = gl.max(axis=0, keepdims=True)
    gidx = jnp.where(gl == gmax, row, none).min(axis=0, keepdims=True)
    gw = 1.0 / jnp.exp(gl - gmax).sum(axis=0, keepdims=True)
    lg = lt[0:EXPERTS_PER_GROUP]
    for g in range(1, N_EXPERT_GROUPS):
        lg = jnp.where(gidx == g, lt[g * EXPERTS_PER_GROUP:(g + 1) * EXPERTS_PER_GROUP], lg)
    m1 = lg.max(axis=0, keepdims=True)
    i1 = jnp.where(lg == m1, row, none).min(axis=0, keepdims=True)
    l2 = jnp.where(row == i1, neg, lg)
    m2 = l2.max(axis=0, keepdims=True)
    i2 = jnp.where(l2 == m2, row, none).min(axis=0, keepdims=True)
    e = jnp.exp(m2 - m1)
    w8 = jnp.where(row == i1, gw / (1.0 + e), jnp.where(row == i2, gw * e / (1.0 + e), 0.0))

    onehot = (row == gidx).astype(f32)
    prefix = _dot(onehot.astype(bf16), tri_ref[...])
    carry = carry_ref[...]
    rank = (onehot * (prefix - 1.0 + carry[:, 0:1])).sum(axis=0, keepdims=True)
    carry_ref[...] = carry + onehot.sum(axis=1, keepdims=True)
    code = gidx.astype(jnp.int32) * (1 << RANK_BITS) + rank.astype(jnp.int32)
    return w8, code


def _s3_kernel(*refs, n_x):
    x_refs, refs = refs[:n_x], refs[n_x:]
    (oa_ref, ob_ref, oc_ref, mod_ref, g1_ref, g2_ref, wga_ref, wgb_ref, wgc_ref, wba_ref, wbb_ref, wbc_ref,
     wout_ref, wrt_ref, tri_ref, x1_ref, h2x_ref, code_ref, cnt_ref, carry_ref) = refs
    wg_refs = (wga_ref, wgb_ref, wgc_ref)

    @pl.when(pl.program_id(0) == 0)
    def _():
        carry_ref[...] = jnp.zeros_like(carry_ref)

    x = _layer_input(x_refs)
    h = _rms_mod(x, g1_ref[...], mod_ref[0, 1:2, :], mod_ref[0, 0:1, :]).astype(bf16)
    merged = None
    for j, (o_ref, wb_ref) in enumerate(((oa_ref, wba_ref), (ob_ref, wbb_ref), (oc_ref, wbc_ref))):
        gate = _sigmoid(_dot(h, wg_refs[j][...]))
        term = gate * _dot(o_ref[...], wb_ref[...])
        merged = term if merged is None else merged + term
    mix = _dot(merged.astype(bf16), wout_ref[...])
    x1 = x + mod_ref[0, 2:3, :] * mix
    x1_ref[...] = x1
    h2 = _rms_mod(x1, g2_ref[...], mod_ref[0, 4:5, :], mod_ref[0, 3:4, :])
    _rows_to_slabs(h2x_ref, 0, h2, H2X_PITCH)
    th, tl = _split_bf16(h2)
    wh, wl = _split_bf16(wrt_ref[...])
    lt = _dot_t(wh, th) + (_dot_t(wh, tl) + _dot_t(wl, th))
    w8, code = _route(lt, tri_ref, carry_ref)
    code_ref[...] = code
    cnt_ref[...] = carry_ref[...]
    tm = x.shape[0]
    w_rows = jnp.concatenate([w8, jnp.zeros((LANES - EXPERTS_PER_GROUP, tm), f32)], axis=0)
    h2x_ref[pl.ds(SLAB_ROWS, tm, stride=H2X_PITCH), :] = w_rows.T


def _s3_call(xin, mod_prev, oa, ob, oc, mod, g1, g2, wg, wba, wbb, wbc, wout, wrt, tri, *, tokens_per_batch,
             cond_row0, per_batch_cond):
    n_tok = xin[0].shape[0]
    tm = TOKEN_TILE
    cond_idx = _cond_index_map(tokens_per_batch, cond_row0, per_batch_cond)
    tok = lambda w: pl.BlockSpec((tm, w), lambda t: (t, 0))
    x_specs, x_args = _layer_input_specs(xin, mod_prev, cond_idx)
    gate_cols = lambda j: pl.BlockSpec((D_MODEL, D_MODEL), lambda t: (0, C_GATES // D_MODEL + j),
                                       pipeline_mode=pl.Buffered(1))
    return pl.pallas_call(
        functools.partial(_s3_kernel, n_x=len(x_args)),
        grid=(n_tok // tm,),
        in_specs=x_specs + [
            tok(Q_W), tok(FOURIER_DIM), tok(Q_W),
            pl.BlockSpec((1, 6, D_MODEL), cond_idx),
            _resident((1, D_MODEL)), _resident((1, D_MODEL)),
            gate_cols(0), gate_cols(1), gate_cols(2),
            _resident((Q_W, D_MODEL)), _resident((FOURIER_DIM, D_MODEL)), _resident((Q_W, D_MODEL)),
            _resident((D_MODEL, D_MODEL)), _resident((ROUTE_ROWS, D_MODEL)), _resident((tm, tm))],
        out_specs=[tok(D_MODEL), pl.BlockSpec((tm * H2X_PITCH, LANES), lambda t: (t, 0)),
                   pl.BlockSpec((1, tm), lambda t: (0, t)),
                   pl.BlockSpec((EXPERTS_PER_GROUP, LANES), lambda t: (0, 0))],
        out_shape=[jax.ShapeDtypeStruct((n_tok, D_MODEL), f32),
                   jax.ShapeDtypeStruct((n_tok * H2X_PITCH, LANES), f32),
                   jax.ShapeDtypeStruct((1, n_tok), jnp.int32),
                   jax.ShapeDtypeStruct((EXPERTS_PER_GROUP, LANES), f32)],
        scratch_shapes=[pltpu.VMEM((EXPERTS_PER_GROUP, LANES), f32)],
        compiler_params=_cparams(("arbitrary",)),
        name="s3",
    )(*x_args, oa, ob, oc, mod, g1, g2, wg, wg, wg, wba, wbb, wbc, wout, wrt, tri)


MOE_TILE_LOG2 = 8
MOE_TILE = 1 << MOE_TILE_LOG2
ROW_UNROLL_LOG2 = 3
ROW_UNROLL = 1 << ROW_UNROLL_LOG2
GROUP_FF = EXPERTS_PER_GROUP * EXPERT_FF
PLAN_BASE, PLAN_FIRST, PLAN_ITEMS, PLAN_LEN = 0, N_EXPERT_GROUPS, 2 * N_EXPERT_GROUPS, 2 * N_EXPERT_GROUPS + 1


def _moe_kernel(code_ref, cnt_ref, h2x_hbm, weg_ref, weu_ref, wed_ref, out_hbm,
                pos_ref, plan_ref, gbuf, obuf, dump, gsem, ssem, *, n_tok):
    tm = MOE_TILE
    k = pl.program_id(0)
    n_steps = pl.num_programs(0)
    g_rows = tm * H2X_PITCH
    o_rows = tm * MOE_PITCH

    def out_slab(buf, slot, i):
        return buf.at[pl.ds(pl.multiple_of(slot * o_rows + i * MOE_PITCH, MOE_PITCH), MOE_PITCH)]

    def token_slab(tok):
        return out_hbm.at[pl.ds(pl.multiple_of(tok * MOE_PITCH, MOE_PITCH), MOE_PITCH)]

    def for_row_chunks(n_chunks, body):
        def chunk(c, carry):
            body(c * ROW_UNROLL)
            return carry
        lax.fori_loop(0, n_chunks, chunk, 0)

    @pl.when(k == 0)
    def _plan():
        base = jnp.int32(0)
        first = jnp.int32(0)
        for g in range(N_EXPERT_GROUPS):
            c = cnt_ref[g]
            n_tiles = (c + (tm - 1)) >> MOE_TILE_LOG2
            plan_ref[PLAN_BASE + g] = base
            plan_ref[PLAN_FIRST + g] = first
            end = base + n_tiles * tm

            def pad(p, carry):
                pos_ref[p] = 0
                return carry

            lax.fori_loop(base + c, end, pad, 0)
            base = end
            first = first + n_tiles
        plan_ref[PLAN_ITEMS] = first

        def place(t0):
            codes = [code_ref[t0 + j] for j in range(ROW_UNROLL)]
            for j, cd in enumerate(codes):
                pos_ref[plan_ref[PLAN_BASE + (cd >> RANK_BITS)] + (cd & ((1 << RANK_BITS) - 1))] = t0 + j

        for_row_chunks(n_tok >> ROW_UNROLL_LOG2, place)

    n_items = plan_ref[PLAN_ITEMS]

    def start_gather(item, slot):
        def chunk(i0):
            toks = [pos_ref[item * tm + i0 + j] for j in range(ROW_UNROLL)]
            for j, tok in enumerate(toks):
                pltpu.make_async_copy(h2x_hbm.at[pl.ds(tok * H2X_PITCH, H2X_PITCH)],
                                      gbuf.at[pl.ds(slot * g_rows + (i0 + j) * H2X_PITCH, H2X_PITCH)],
                                      gsem.at[slot]).start()
        for_row_chunks(tm >> ROW_UNROLL_LOG2, chunk)

    def wait_gather(slot):
        pltpu.make_async_copy(h2x_hbm.at[pl.ds(0, g_rows)], gbuf.at[pl.ds(slot * g_rows, g_rows)],
                              gsem.at[slot]).wait()

    def start_scatter(item, n_valid, slot):
        def to_token(i0):
            dsts = [pos_ref[item * tm + i0 + j] for j in range(ROW_UNROLL)]
            for j, dst in enumerate(dsts):
                pltpu.make_async_copy(out_slab(obuf, slot, i0 + j), token_slab(dst), ssem.at[slot]).start()

        def row_to_token(i, carry):
            pltpu.make_async_copy(out_slab(obuf, slot, i), token_slab(pos_ref[item * tm + i]), ssem.at[slot]).start()
            return carry

        def row_to_dump(i, carry):
            pltpu.make_async_copy(out_slab(obuf, slot, i), out_slab(dump, slot, i), ssem.at[slot]).start()
            return carry

        full = n_valid >> ROW_UNROLL_LOG2
        for_row_chunks(full, to_token)
        lax.fori_loop(full * ROW_UNROLL, n_valid, row_to_token, 0)
        lax.fori_loop(n_valid, tm, row_to_dump, 0)

    def wait_scatter(slot):
        pltpu.make_async_copy(obuf.at[pl.ds(slot * o_rows, o_rows)], out_hbm.at[pl.ds(0, o_rows)],
                              ssem.at[slot]).wait()

    slot = k % 2

    @pl.when(k == 0)
    def _():
        start_gather(0, 0)

    @pl.when(k + 1 < n_items)
    def _():
        start_gather(k + 1, 1 - slot)

    @pl.when(k < n_items)
    def _():
        group = ((k >= plan_ref[PLAN_FIRST + 1]).astype(jnp.int32) + (k >= plan_ref[PLAN_FIRST + 2]).astype(jnp.int32)
                 + (k >= plan_ref[PLAN_FIRST + 3]).astype(jnp.int32))
        wait_gather(slot)
        h = _rows_from_slabs(gbuf, slot * g_rows, tm, H2X_PITCH).astype(bf16)
        w8 = gbuf[pl.ds(slot * g_rows + SLAB_ROWS, tm, stride=H2X_PITCH), :]
        cols = pl.ds(pl.multiple_of(group * GROUP_FF, GROUP_FF), GROUP_FF)
        gate = _dot(h, weg_ref[:, cols])
        a = gate * _sigmoid(gate) * _dot(h, weu_ref[:, cols])
        pieces = [a[:, j * EXPERT_FF:(j + 1) * EXPERT_FF] * w8[:, j:j + 1] for j in range(EXPERTS_PER_GROUP)]
        out = _dot(jnp.concatenate(pieces, axis=1).astype(bf16), wed_ref[group])

        @pl.when(k >= 2)
        def _():
            wait_scatter(slot)

        _rows_to_slabs(obuf, slot * o_rows, out, MOE_PITCH)
        n_valid = jnp.minimum(tm, cnt_ref[group] - (k - plan_ref[PLAN_FIRST + group]) * tm)
        start_scatter(k, n_valid, slot)

    @pl.when(k == n_steps - 1)
    def _drain():
        wait_scatter((n_items - 1) % 2)

        @pl.when(n_items >= 2)
        def _():
            wait_scatter(n_items % 2)


def _moe_call(code, cnt, h2x, weg, weu, wed):
    n_tok = h2x.shape[0] // H2X_PITCH
    tm = MOE_TILE
    n_steps = n_tok // tm + N_EXPERT_GROUPS - 1
    ff = N_EXPERT_GROUPS * GROUP_FF
    up_spec = pl.BlockSpec((D_MODEL, ff), lambda k, *_: (0, 0), pipeline_mode=pl.Buffered(1))
    down_spec = pl.BlockSpec((N_EXPERT_GROUPS, GROUP_FF, D_MODEL), lambda k, *_: (0, 0, 0),
                             pipeline_mode=pl.Buffered(1))
    grid_spec = pltpu.PrefetchScalarGridSpec(
        num_scalar_prefetch=2,
        grid=(n_steps,),
        in_specs=[pl.BlockSpec(memory_space=pl.ANY), up_spec, up_spec, down_spec],
        out_specs=pl.BlockSpec(memory_space=pl.ANY),
        scratch_shapes=[
            pltpu.SMEM((n_tok + N_EXPERT_GROUPS * tm,), jnp.int32),
            pltpu.SMEM((PLAN_LEN,), jnp.int32),
            pltpu.VMEM((2 * tm * H2X_PITCH, LANES), f32),
            pltpu.VMEM((2 * tm * MOE_PITCH, LANES), f32),
            pltpu.VMEM((2 * tm * MOE_PITCH, LANES), f32),
            pltpu.SemaphoreType.DMA((2,)),
            pltpu.SemaphoreType.DMA((2,)),
        ],
    )
    return pl.pallas_call(
        functools.partial(_moe_kernel, n_tok=n_tok),
        grid_spec=grid_spec,
        out_shape=jax.ShapeDtypeStruct((n_tok * MOE_PITCH, LANES), f32),
        compiler_params=_cparams(("arbitrary",)),
        name="moe",
    )(code, cnt, h2x, weg, weu, wed)


def _final_kernel(x1_ref, moe_ref, mod_ref, gf_ref, o_ref):
    x2 = x1_ref[...] + mod_ref[0, 5:6, :] * _rows_from_slabs(moe_ref, 0, x1_ref.shape[0], MOE_PITCH)
    y = x2 * lax.rsqrt(jnp.mean(x2 * x2, axis=-1, keepdims=True) + NORM_EPS)
    o_ref[...] = y * gf_ref[...]


def _final_call(x1, moe, mod, gf, *, tokens_per_batch, cond_row0, per_batch_cond):
    n_tok = x1.shape[0]
    tok = pl.BlockSpec((TOKEN_TILE, D_MODEL), lambda t: (t, 0))
    slab = pl.BlockSpec((TOKEN_TILE * MOE_PITCH, LANES), lambda t: (t, 0))
    cond_idx = _cond_index_map(tokens_per_batch, cond_row0, per_batch_cond)
    return pl.pallas_call(
        _final_kernel,
        grid=(n_tok // TOKEN_TILE,),
        in_specs=[tok, slab, pl.BlockSpec((1, 6, D_MODEL), cond_idx), _resident((1, D_MODEL))],
        out_specs=tok,
        out_shape=jax.ShapeDtypeStruct((n_tok, D_MODEL), f32),
        compiler_params=_cparams(("arbitrary",)),
        name="final_norm",
    )(x1, moe, mod, gf)


def _dup_heads(w):
    a, b = w[..., :HEAD_DIM], w[..., HEAD_DIM:]
    return jnp.concatenate([a, a, b, b], axis=-1)


def _rope_tables(n):
    rows = n // GRID_W
    row = jnp.repeat(jnp.arange(rows, dtype=f32), GRID_W)
    col = jnp.tile(jnp.arange(GRID_W, dtype=f32), rows)
    inv = ROPE_THETA ** (-jnp.arange(AXIS_FREQS, dtype=f32) / AXIS_FREQS)
    ang = jnp.concatenate([row[:, None] * inv, col[:, None] * inv], axis=-1)
    cos, sin = jnp.cos(ang), jnp.sin(ang)
    cos_t = jnp.concatenate([cos, cos, cos, cos], axis=-1)
    sin_t = jnp.concatenate([-sin, sin, -sin, sin], axis=-1)
    return cos_t, sin_t


def _dft_tables(n):
    k = np.arange(n)
    m = (k[:, None] * k[None, :]) % n
    ang = 2.0 * np.pi * m / n
    return np.cos(ang), np.sin(ang)


def _fourier_tables(seq):
    cn, sn = _dft_tables(seq)
    cc, sc = _dft_tables(FOURIER_GROUP_DIM)
    eye = np.eye(N_FOURIER_GROUPS)
    bdc, bds = np.kron(eye, cc), np.kron(eye, sc)
    return tuple(jnp.asarray(t, dtype=f32).astype(bf16) for t in (bdc, bds, cn, sn))


def _dup_cache(cache, l):
    b, _, _, p, _, _ = cache.shape
    k = _dup_heads(cache[:, l, 0].reshape(b, p, N_KV * HEAD_DIM)).astype(bf16)
    v = _dup_heads(cache[:, l, 1].reshape(b, p, N_KV * HEAD_DIM)).astype(bf16)
    return k, v


def kernel(x_prompt, x_sample, cache_global_kv, cache_window_kv, c, c_ctx, w_mod, b_mod, norm1_g, w_in,
           q_norm_g, k_norm_g, sink_logit, w_branch_a, w_branch_b, w_branch_c, w_out, norm2_g,
           w_route_group, w_route_expert, w_exp_gate, w_exp_up, w_exp_down, final_norm_g):
    batch, seq, _ = x_prompt.shape
    dec_batch, dec_seq, _ = x_sample.shape

    cond = jnp.zeros((N_COND_ROWS, D_MODEL), f32).at[0].set(c_ctx).at[1:1 + dec_batch].set(c)
    mod = _mod_call(cond, w_mod, b_mod).reshape(DEPTH, N_COND_ROWS, 6, D_MODEL)

    ones_blk = jnp.asarray(np.kron(np.eye(N_HEADS), np.ones((HEAD_DIM, HEAD_DIM))), dtype=bf16)
    tri = jnp.asarray(np.triu(np.ones((TOKEN_TILE, TOKEN_TILE))), dtype=bf16)
    rope_tabs = _rope_tables(dec_seq)
    ftab_ctx = _fourier_tables(seq)
    ftab_lat = _fourier_tables(dec_seq)
    gf = final_norm_g.reshape(1, D_MODEL)

    ctx = dict(tokens_per_batch=seq, cond_row0=0, per_batch_cond=False)
    lat = dict(tokens_per_batch=dec_seq, cond_row0=1, per_batch_cond=True)
    xp = (x_prompt.reshape(batch * seq, D_MODEL),)
    xs = (x_sample.reshape(dec_batch * dec_seq, D_MODEL),)
    mod_prev = None
    gkv, wkv = [], []
    for l in range(DEPTH):
        w_ext = w_gate = w_in[l].astype(bf16)
        g1 = norm1_g[l].reshape(1, D_MODEL)
        g2 = norm2_g[l].reshape(1, D_MODEL)
        qg = jnp.tile(q_norm_g[l], N_HEADS).reshape(1, Q_W)
        kg = jnp.tile(k_norm_g[l], N_KV).reshape(1, LANES)
        wba, wbb, wbc = (w.astype(bf16) for w in (w_branch_a[l], w_branch_b[l], w_branch_c[l]))
        wout = w_out[l].astype(bf16)
        wrt = jnp.concatenate([w_route_expert[l].T, w_route_group[l].T,
                               jnp.zeros((ROUTE_ROWS - N_EXPERTS - N_EXPERT_GROUPS, D_MODEL), f32)], axis=0)
        weg, weu = w_exp_gate[l].astype(bf16), w_exp_up[l].astype(bf16)
        wed = w_exp_down[l].reshape(N_EXPERT_GROUPS, GROUP_FF, D_MODEL).astype(bf16)
        sink = sink_logit[l]

        def mix_and_moe(xin, qa, ka, va, qc, kc, vc, ub, attn, ftab, seq_len, stream):
            oa, oc = attn(qa, ka, va, qc, kc, vc)
            ob = _fourier_call(ub, *ftab, seq_len)
            x1, h2x, code, cnt = _s3_call(xin, mod_prev, oa, ob, oc, mod[l], g1, g2, w_gate, wba, wbb, wbc, wout,
                                          wrt, tri, **stream)
            counts = cnt[:, 0].astype(jnp.int32)
            moe = _moe_call(code.reshape(-1), counts, h2x, weg, weu, wed)
            return (x1, moe)

        qa, ka, va, qc, kc, vc, ub, gk, gv, wk, wv = _s1_call(
            xp, mod_prev, mod[l], g1, w_ext, ones_blk, qg, kg, None, emit_kv=True, **ctx)
        attn_ctx = lambda *qkv: _attn_ctx_call(sink, *qkv, seq)
        xp = mix_and_moe(xp, qa, ka, va, qc, kc, vc, ub, attn_ctx, ftab_ctx, seq, ctx)
        kv_shape = (batch, seq, N_KV, HEAD_DIM)
        gkv.append(jnp.stack([gk.reshape(kv_shape), gv.reshape(kv_shape)], axis=1))
        wkv.append(jnp.stack([wk.reshape(kv_shape), wv.reshape(kv_shape)], axis=1))

        qa, ka, va, qc, kc, vc, ub = _s1_call(
            xs, mod_prev, mod[l], g1, w_ext, ones_blk, qg, kg, rope_tabs, emit_kv=False, **lat)
        kxa, vxa = _dup_cache(cache_global_kv, l)
        kxc, vxc = _dup_cache(cache_window_kv, l)
        attn_lat = lambda qa, ka, va, qc, kc, vc: _attn_lat_call(
            sink, qa, ka, va, kxa, vxa, qc, kc, vc, kxc, vxc, dec_seq)
        xs = mix_and_moe(xs, qa, ka, va, qc, kc, vc, ub, attn_lat, ftab_lat, dec_seq, lat)
        mod_prev = mod[l]

    y_prompt = _final_call(*xp, mod_prev, gf, **ctx).reshape(batch, seq, D_MODEL)
    y_sample = _final_call(*xs, mod_prev, gf, **lat).reshape(dec_batch, dec_seq, D_MODEL)
    return (y_prompt, y_sample, jnp.stack(gkv, axis=1), jnp.stack(wkv, axis=1))
```

```python
import functools

import jax
import jax.numpy as jnp
import numpy as np
from jax import lax
from jax.experimental import pallas as pl
from jax.experimental.pallas import tpu as pltpu

f32 = jnp.float32
bf16 = jnp.bfloat16

D_MODEL = 1024
DEPTH = 2
GRID_W = 64
HEAD_DIM = 64
HALF = HEAD_DIM // 2
N_HEADS = 8
N_KV = 2
WINDOW = 128
N_FOURIER_GROUPS = 4
FOURIER_GROUP_DIM = 128
FOURIER_DIM = N_FOURIER_GROUPS * FOURIER_GROUP_DIM
ROPE_THETA = 10000.0
AXIS_FREQS = HEAD_DIM // 4
N_EXPERT_GROUPS = 4
EXPERTS_PER_GROUP = 8
N_EXPERTS = N_EXPERT_GROUPS * EXPERTS_PER_GROUP
EXPERT_FF = 128
NORM_EPS = 1e-6
NEG_INF = -1e30

LANES = 128
Q_W = N_HEADS * HEAD_DIM
KV_DUP_W = 2 * N_KV * HEAD_DIM
C_QA, C_KA, C_VA, C_QC, C_KC, C_VC, C_UB, C_GATES, C_END = 0, 512, 640, 768, 1280, 1408, 1536, 2048, 5120
N_COND_ROWS = 16
VMEM_LIMIT = 56 * 1024 * 1024


def _cparams(sem):
    return pltpu.CompilerParams(dimension_semantics=sem, vmem_limit_bytes=VMEM_LIMIT)


def _resident(shape):
    nd = len(shape)
    return pl.BlockSpec(shape, lambda *_: (0,) * nd, pipeline_mode=pl.Buffered(1))


def _sigmoid(x):
    return 0.5 * jnp.tanh(0.5 * x) + 0.5


def _rms_mod(x, g, scale, shift):
    y = x * lax.rsqrt(jnp.mean(x * x, axis=-1, keepdims=True) + NORM_EPS)
    return (y * g) * (1.0 + scale) + shift


def _dot(a, b):
    return jnp.dot(a, b, preferred_element_type=f32)


def _dot_t(a, b):
    return lax.dot_general(a, b, (((1,), (1,)), ((), ())), preferred_element_type=f32)


def _split_bf16(x):
    hi = x.astype(bf16)
    lo = (x - hi.astype(f32)).astype(bf16)
    return hi, lo


def _mod_kernel(cond_ref, w_ref, b_ref, o_ref):
    c = cond_ref[...]
    s = (c * _sigmoid(c)).astype(bf16)
    o_ref[0] = _dot(s, w_ref[0].astype(bf16)) + b_ref[0]


def _mod_call(cond, w_mod, b_mod):
    n_tiles = 6 * D_MODEL // 1024
    return pl.pallas_call(
        _mod_kernel,
        grid=(DEPTH, n_tiles),
        in_specs=[
            pl.BlockSpec((N_COND_ROWS, D_MODEL), lambda l, j: (0, 0)),
            pl.BlockSpec((1, D_MODEL, 1024), lambda l, j: (l, 0, j)),
            pl.BlockSpec((1, 1, 1024), lambda l, j: (l, 0, j)),
        ],
        out_specs=pl.BlockSpec((1, N_COND_ROWS, 1024), lambda l, j: (l, 0, j)),
        out_shape=jax.ShapeDtypeStruct((DEPTH, N_COND_ROWS, 6 * D_MODEL), f32),
        compiler_params=_cparams(("arbitrary", "arbitrary")),
        name="mod",
    )(cond, w_mod, b_mod.reshape(DEPTH, 1, 6 * D_MODEL))


def _head_norm(z, g, ones_blk):
    hi, lo = _split_bf16(z * z)
    ss = _dot(hi, ones_blk) + _dot(lo, ones_blk)
    return z * lax.rsqrt(ss * (1.0 / HEAD_DIM) + NORM_EPS) * g


def _rope(z, cos, sin_signed, first_half):
    outs = []
    for j in range(z.shape[1] // LANES):
        zj = z[:, j * LANES:(j + 1) * LANES]
        swapped = jnp.where(first_half, pltpu.roll(zj, LANES - HALF, 1), pltpu.roll(zj, HALF, 1))
        outs.append(zj * cos + swapped * sin_signed)
    return jnp.concatenate(outs, axis=1)


def _layer_input(x_refs, r0, n):
    rows = pl.ds(r0, n)
    if len(x_refs) == 1:
        return x_refs[0][rows, :]
    x1_ref, moe_ref, modp_ref = x_refs
    return x1_ref[rows, :] + modp_ref[0, 5:6, :] * _rows_from_slabs(moe_ref, r0 * SLAB_ROWS, n, SLAB_ROWS)


S1_SUB_TILES = 2
S3_SUB_TILES = 1


SLAB_ROWS = D_MODEL // LANES


def _rows_from_slabs(ref, start, n_tok, pitch):
    return jnp.concatenate([ref[pl.ds(start + c, n_tok, stride=pitch), :] for c in range(SLAB_ROWS)], axis=1)


def _rows_to_slabs(ref, start, x, pitch):
    for c in range(SLAB_ROWS):
        ref[pl.ds(start + c, x.shape[0], stride=pitch), :] = x[:, c * LANES:(c + 1) * LANES]


def _s1_kernel(*refs, rope, emit_kv, n_x):
    x_refs, refs = refs[:n_x], refs[n_x:]
    mod_ref, g1_ref, w_ref, ones_ref, qg_ref, kg_ref = refs[:6]
    pos = 6
    if rope:
        cos_ref, sin_ref = refs[pos:pos + 2]
        pos += 2
    qa_ref, ka_ref, va_ref, qc_ref, kc_ref, vc_ref, ub_ref = refs[pos:pos + 7]
    pos += 7
    if emit_kv:
        gk_ref, gv_ref, wk_ref, wv_ref = refs[pos:pos + 4]

    tm = x_refs[0].shape[0] // S1_SUB_TILES
    lane = lax.broadcasted_iota(jnp.int32, (tm, LANES), 1)
    low_lanes = lane < HEAD_DIM
    first_half = (lane & (HEAD_DIM - 1)) < HALF
    q_scale = HEAD_DIM ** -0.5

    def dup(z):
        r = pltpu.roll(z, HEAD_DIM, 1)
        return jnp.concatenate([jnp.where(low_lanes, z, r), jnp.where(low_lanes, r, z)], axis=1).astype(bf16)

    for sub in range(S1_SUB_TILES):
        rows = pl.ds(sub * tm, tm)
        x = _layer_input(x_refs, sub * tm, tm)
        h = _rms_mod(x, g1_ref[...], mod_ref[0, 1:2, :], mod_ref[0, 0:1, :]).astype(bf16)
        proj = lambda lo, hi: _dot(h, w_ref[:, lo:hi])
        if rope:
            cos = cos_ref[rows, :]
            sin_signed = sin_ref[rows, :]
            rot = lambda z: _rope(z, cos, sin_signed, first_half)
        else:
            rot = lambda z: z

        qa = rot(_head_norm(proj(C_QA, C_KA), qg_ref[...], ones_ref[...]))
        qa_ref[rows, :] = (qa * q_scale).astype(bf16)
        ka_n = _head_norm(proj(C_KA, C_VA), kg_ref[...], ones_ref[0:LANES, 0:LANES])
        ka_ref[rows, :] = dup(rot(ka_n))
        va = proj(C_VA, C_QC)
        va_ref[rows, :] = dup(va)
        qc_ref[rows, :] = (rot(proj(C_QC, C_KC)) * q_scale).astype(bf16)
        kc = proj(C_KC, C_VC)
        kc_ref[rows, :] = dup(rot(kc))
        vc = proj(C_VC, C_UB)
        vc_ref[rows, :] = dup(vc)
        ub_ref[rows, :] = proj(C_UB, C_GATES).astype(bf16)
        if emit_kv:
            gk_ref[rows, :] = ka_n
            gv_ref[rows, :] = va
            wk_ref[rows, :] = kc
            wv_ref[rows, :] = vc


TOKEN_TILE = 512


def _cond_index_map(tokens_per_batch, cond_row0, per_batch_cond):
    tiles_per_batch = max(tokens_per_batch // TOKEN_TILE, 1)
    if per_batch_cond:
        assert tokens_per_batch % TOKEN_TILE == 0
        return lambda t: (cond_row0 + t // tiles_per_batch, 0, 0)
    return lambda t: (cond_row0, 0, 0)


def _layer_input_specs(xin, mod_prev, cond_idx):
    tok = pl.BlockSpec((TOKEN_TILE, D_MODEL), lambda t: (t, 0))
    if len(xin) == 1:
        return [tok], list(xin)
    slab = pl.BlockSpec((TOKEN_TILE * SLAB_ROWS, LANES), lambda t: (t, 0))
    return [tok, slab, pl.BlockSpec((1, 6, D_MODEL), cond_idx)], [xin[0], xin[1], mod_prev]


def _s1_call(xin, mod_prev, mod, g1, w_ext, ones_blk, qg, kg, rope_tabs, *, tokens_per_batch, cond_row0,
             per_batch_cond, emit_kv):
    n_tok = xin[0].shape[0]
    tm = TOKEN_TILE
    rope = rope_tabs is not None
    tiles_per_batch = max(tokens_per_batch // tm, 1)
    cond_idx = _cond_index_map(tokens_per_batch, cond_row0, per_batch_cond)

    tok = lambda w: pl.BlockSpec((tm, w), lambda t: (t, 0))
    x_specs, x_args = _layer_input_specs(xin, mod_prev, cond_idx)
    in_specs = x_specs + [
        pl.BlockSpec((1, 6, D_MODEL), cond_idx),
        _resident((1, D_MODEL)),
        _resident((D_MODEL, C_GATES)),
        _resident((Q_W, Q_W)),
        _resident((1, Q_W)),
        _resident((1, LANES)),
    ]
    args = x_args + [mod, g1, w_ext, ones_blk, qg, kg]
    if rope:
        in_specs += [pl.BlockSpec((tm, LANES), lambda t: (t % tiles_per_batch, 0))] * 2
        args += list(rope_tabs)
    widths = [Q_W, KV_DUP_W, KV_DUP_W, Q_W, KV_DUP_W, KV_DUP_W, FOURIER_DIM]
    out_specs = [tok(w) for w in widths]
    out_shape = [jax.ShapeDtypeStruct((n_tok, w), bf16) for w in widths]
    if emit_kv:
        out_specs += [tok(LANES)] * 4
        out_shape += [jax.ShapeDtypeStruct((n_tok, LANES), f32)] * 4
    return pl.pallas_call(
        functools.partial(_s1_kernel, rope=rope, emit_kv=emit_kv, n_x=len(x_args)),
        grid=(n_tok // tm,),
        in_specs=in_specs,
        out_specs=out_specs,
        out_shape=out_shape,
        compiler_params=_cparams(("arbitrary",)),
        name="s1_rope" if rope else "s1_ctx",
    )(*args)


def _gqa_group(q2, segs, sink_pair_lo, sink_pair_hi):
    qb = q2.shape[0]
    qs = jnp.concatenate([q2[:, 0:LANES], q2[:, LANES:2 * LANES]], axis=0)
    lane = lax.broadcasted_iota(jnp.int32, qs.shape, 1)
    low = lane < HEAD_DIM
    zero = jnp.zeros_like(qs)
    row = lax.broadcasted_iota(jnp.int32, (2 * qb, 1), 0)

    def half(qh, sink_pair):
        scores = []
        for k, _, ok in segs:
            s = _dot_t(qh, k)
            if ok is not None:
                s = jnp.where(ok, s, NEG_INF)
            scores.append(s)
        m = scores[0].max(axis=-1, keepdims=True)
        for s in scores[1:]:
            m = jnp.maximum(m, s.max(axis=-1, keepdims=True))
        if sink_pair is not None:
            sink = jnp.where(row < qb, sink_pair[0], sink_pair[1])
            m = jnp.maximum(m, sink)
            denom = jnp.exp(sink - m)
        else:
            denom = jnp.zeros_like(m)
        acc = None
        for s, (_, v, _) in zip(scores, segs):
            p = jnp.exp(s - m)
            denom = denom + p.sum(axis=-1, keepdims=True)
            pv = _dot(p.astype(bf16), v)
            acc = pv if acc is None else acc + pv
        return acc / denom

    o_lo = half(jnp.where(low, qs, zero), sink_pair_lo)
    o_hi = half(jnp.where(low, zero, qs), sink_pair_hi)
    o = jnp.where(low, o_lo, o_hi)
    return jnp.concatenate([o[0:qb], o[qb:2 * qb]], axis=1)


def _sink_pairs(sink_ref, kv):
    h = 4 * kv
    return (sink_ref[h], sink_ref[h + 2]), (sink_ref[h + 1], sink_ref[h + 3])


def _attn_ctx_kernel(sink_ref, qa_ref, ka_ref, va_ref, qc_ref, kc_ref, vc_ref, oa_ref, oc_ref):
    for kv in range(N_KV):
        cq = slice(2 * LANES * kv, 2 * LANES * (kv + 1))
        ck = slice(LANES * kv, LANES * (kv + 1))
        oa = _gqa_group(qa_ref[:, cq], [(ka_ref[:, ck], va_ref[:, ck], None)], None, None)
        oa_ref[:, cq] = oa.astype(bf16)
        lo, hi = _sink_pairs(sink_ref, kv)
        oc = _gqa_group(qc_ref[:, cq], [(kc_ref[:, ck], vc_ref[:, ck], None)], lo, hi)
        oc_ref[:, cq] = oc.astype(bf16)


def _attn_ctx_call(sink, qa, ka, va, qc, kc, vc, seq):
    n_tok = qa.shape[0]
    tok = lambda w: pl.BlockSpec((seq, w), lambda b: (b, 0))
    return pl.pallas_call(
        _attn_ctx_kernel,
        grid=(n_tok // seq,),
        in_specs=[pl.BlockSpec(memory_space=pltpu.SMEM),
                  tok(Q_W), tok(KV_DUP_W), tok(KV_DUP_W), tok(Q_W), tok(KV_DUP_W), tok(KV_DUP_W)],
        out_specs=[tok(Q_W), tok(Q_W)],
        out_shape=[jax.ShapeDtypeStruct((n_tok, Q_W), bf16)] * 2,
        compiler_params=_cparams(("arbitrary",)),
        name="attn_ctx",
    )(sink, qa, ka, va, qc, kc, vc)


QB = 256
WIN_QB = 128
WIN_SPAN = WIN_QB + 2 * WINDOW


def _attn_lat_kernel(sink_ref, qa_ref, ka_ref, va_ref, kxa_ref, vxa_ref,
                     qc_ref, kc_ref, vc_ref, kxc_ref, vxc_ref, oa_ref, oc_ref, *, seq):
    i = pl.program_id(1)
    for kv in range(N_KV):
        cq = slice(2 * LANES * kv, 2 * LANES * (kv + 1))
        ck = slice(LANES * kv, LANES * (kv + 1))
        segs = [(ka_ref[:, ck], va_ref[:, ck], None), (kxa_ref[0, :, ck], vxa_ref[0, :, ck], None)]
        oa_ref[:, cq] = _gqa_group(qa_ref[:, cq], segs, None, None).astype(bf16)

        lo, hi = _sink_pairs(sink_ref, kv)
        for sub in range(QB // WIN_QB):
            blk = i * (QB // WIN_QB) + sub
            start = jnp.clip((blk - 1) * WIN_QB, 0, seq - WIN_SPAN)
            start = pl.multiple_of(start, WIN_QB)
            r = lax.broadcasted_iota(jnp.int32, (2 * WIN_QB, WIN_SPAN), 0)
            c = lax.broadcasted_iota(jnp.int32, (2 * WIN_QB, WIN_SPAN), 1)
            qpos = blk * WIN_QB + (r & (WIN_QB - 1))
            ok = jnp.abs(qpos - (start + c)) <= WINDOW
            rows = slice(sub * WIN_QB, (sub + 1) * WIN_QB)
            segs = [(kc_ref[pl.ds(start, WIN_SPAN), ck], vc_ref[pl.ds(start, WIN_SPAN), ck], ok),
                    (kxc_ref[0, :, ck], vxc_ref[0, :, ck], None)]
            oc_ref[rows, cq] = _gqa_group(qc_ref[rows, cq], segs, lo, hi).astype(bf16)


def _attn_lat_call(sink, qa, ka, va, kxa, vxa, qc, kc, vc, kxc, vxc, seq):
    n_tok = qa.shape[0]
    nq = seq // QB
    past = kxa.shape[1]
    qspec = pl.BlockSpec((QB, Q_W), lambda b, i: (b * nq + i, 0))
    kspec = pl.BlockSpec((seq, KV_DUP_W), lambda b, i: (b, 0))
    xspec = pl.BlockSpec((1, past, KV_DUP_W), lambda b, i: (b, 0, 0))
    return pl.pallas_call(
        functools.partial(_attn_lat_kernel, seq=seq),
        grid=(n_tok // seq, nq),
        in_specs=[pl.BlockSpec(memory_space=pltpu.SMEM),
                  qspec, kspec, kspec, xspec, xspec, qspec, kspec, kspec, xspec, xspec],
        out_specs=[qspec, qspec],
        out_shape=[jax.ShapeDtypeStruct((n_tok, Q_W), bf16)] * 2,
        compiler_params=_cparams(("arbitrary", "arbitrary")),
        name="attn_lat",
    )(sink, qa, ka, va, kxa, vxa, qc, kc, vc, kxc, vxc)


def _fourier_kernel(u_ref, bdc_ref, bds_ref, cn_ref, sn_ref, o_ref, *, scale):
    u = u_ref[...]
    uc = _dot(u, bdc_ref[...]).astype(bf16)
    us = _dot(u, bds_ref[...]).astype(bf16)
    o = _dot(cn_ref[...], uc) - _dot(sn_ref[...], us)
    o_ref[...] = (o * scale).astype(bf16)


def _fourier_call(ub, bdc, bds, cn, sn, seq):
    n_tok = ub.shape[0]
    tok = pl.BlockSpec((seq, FOURIER_DIM), lambda b: (b, 0))
    scale = float(1.0 / np.sqrt(seq * FOURIER_GROUP_DIM))
    return pl.pallas_call(
        functools.partial(_fourier_kernel, scale=scale),
        grid=(n_tok // seq,),
        in_specs=[tok, _resident((FOURIER_DIM, FOURIER_DIM)), _resident((FOURIER_DIM, FOURIER_DIM)),
                  _resident((seq, seq)), _resident((seq, seq))],
        out_specs=tok,
        out_shape=jax.ShapeDtypeStruct((n_tok, FOURIER_DIM), bf16),
        compiler_params=_cparams(("arbitrary",)),
        name=f"fourier_{seq}",
    )(ub, bdc, bds, cn, sn)


ROUTE_ROWS = LANES
H2X_PITCH = SLAB_ROWS + 1
MOE_PITCH = SLAB_ROWS


def _route(lt, tri_ref, carry_ref, n_tok):
    tm = lt.shape[1]
    row = lax.broadcasted_iota(jnp.int32, (EXPERTS_PER_GROUP, tm), 0).astype(f32)
    neg = jnp.float32(-jnp.inf)
    none = jnp.float32(EXPERTS_PER_GROUP)
    gl = jnp.where(row < N_EXPERT_GROUPS, lt[N_EXPERTS:N_EXPERTS + EXPERTS_PER_GROUP], neg)
    gmax = gl.max(axis=0, keepdims=True)
    gidx = jnp.where(gl == gmax, row, none).min(axis=0, keepdims=True)
    gw = 1.0 / jnp.exp(gl - gmax).sum(axis=0, keepdims=True)
    lg = lt[0:EXPERTS_PER_GROUP]
    for g in range(1, N_EXPERT_GROUPS):
        lg = jnp.where(gidx == g, lt[g * EXPERTS_PER_GROUP:(g + 1) * EXPERTS_PER_GROUP], lg)
    m1 = lg.max(axis=0, keepdims=True)
    i1 = jnp.where(lg == m1, row, none).min(axis=0, keepdims=True)
    l2 = jnp.where(row == i1, neg, lg)
    m2 = l2.max(axis=0, keepdims=True)
    i2 = jnp.where(l2 == m2, row, none).min(axis=0, keepdims=True)
    e = jnp.exp(m2 - m1)
    w8 = jnp.where(row == i1, gw / (1.0 + e), jnp.where(row == i2, gw * e / (1.0 + e), 0.0))

    onehot = (row == gidx).astype(f32)
    prefix = _dot(onehot.astype(bf16), tri_ref[0:tm, 0:tm])
    carry = carry_ref[...]
    rank = (onehot * (prefix - 1.0 + carry[:, 0:1])).sum(axis=0, keepdims=True)
    carry_ref[...] = carry + onehot.sum(axis=1, keepdims=True)
    dest = gidx.astype(jnp.int32) * n_tok + rank.astype(jnp.int32)
    return w8, dest


def _s3_kernel(*refs, n_x, n_tok):
    x_refs, refs = refs[:n_x], refs[n_x:]
    (oa_ref, ob_ref, oc_ref, mod_ref, g1_ref, g2_ref, wga_ref, wgb_ref, wgc_ref, wba_ref, wbb_ref, wbc_ref,
     wout_ref, wrt_ref, tri_ref, x1_ref, h2x_ref, code_ref, cnt_ref, carry_ref) = refs
    wg_refs = (wga_ref, wgb_ref, wgc_ref)

    @pl.when(pl.program_id(0) == 0)
    def _():
        carry_ref[...] = jnp.zeros_like(carry_ref)

    tm = x1_ref.shape[0] // S3_SUB_TILES
    wh, wl = _split_bf16(wrt_ref[...])
    for sub in range(S3_SUB_TILES):
        r0 = sub * tm
        rows = pl.ds(r0, tm)
        x = _layer_input(x_refs, r0, tm)
        h = _rms_mod(x, g1_ref[...], mod_ref[0, 1:2, :], mod_ref[0, 0:1, :]).astype(bf16)
        merged = None
        for j, (o_ref, wb_ref) in enumerate(((oa_ref, wba_ref), (ob_ref, wbb_ref), (oc_ref, wbc_ref))):
            gate = _sigmoid(_dot(h, wg_refs[j][...]))
            term = gate * _dot(o_ref[rows, :], wb_ref[...])
            merged = term if merged is None else merged + term
        mix = _dot(merged.astype(bf16), wout_ref[...])
        x1 = x + mod_ref[0, 2:3, :] * mix
        x1_ref[rows, :] = x1
        h2 = _rms_mod(x1, g2_ref[...], mod_ref[0, 4:5, :], mod_ref[0, 3:4, :])
        _rows_to_slabs(h2x_ref, r0 * H2X_PITCH, h2, H2X_PITCH)
        th, tl = _split_bf16(h2)
        lt = _dot_t(wh, th) + (_dot_t(wh, tl) + _dot_t(wl, th))
        w8, code = _route(lt, tri_ref, carry_ref, n_tok)
        code_ref[:, rows] = code
        w_rows = jnp.concatenate([w8, jnp.zeros((LANES - EXPERTS_PER_GROUP, tm), f32)], axis=0)
        h2x_ref[pl.ds(r0 * H2X_PITCH + SLAB_ROWS, tm, stride=H2X_PITCH), :] = w_rows.T
    cnt_ref[...] = carry_ref[...]


def _s3_call(xin, mod_prev, oa, ob, oc, mod, g1, g2, wg, wba, wbb, wbc, wout, wrt, tri, *, tokens_per_batch,
             cond_row0, per_batch_cond):
    n_tok = xin[0].shape[0]
    tm = TOKEN_TILE
    cond_idx = _cond_index_map(tokens_per_batch, cond_row0, per_batch_cond)
    tok = lambda w: pl.BlockSpec((tm, w), lambda t: (t, 0))
    x_specs, x_args = _layer_input_specs(xin, mod_prev, cond_idx)
    gate_cols = lambda j: pl.BlockSpec((D_MODEL, D_MODEL), lambda t: (0, C_GATES // D_MODEL + j),
                                       pipeline_mode=pl.Buffered(1))
    return pl.pallas_call(
        functools.partial(_s3_kernel, n_x=len(x_args), n_tok=n_tok),
        grid=(n_tok // tm,),
        in_specs=x_specs + [
            tok(Q_W), tok(FOURIER_DIM), tok(Q_W),
            pl.BlockSpec((1, 6, D_MODEL), cond_idx),
            _resident((1, D_MODEL)), _resident((1, D_MODEL)),
            gate_cols(0), gate_cols(1), gate_cols(2),
            _resident((Q_W, D_MODEL)), _resident((FOURIER_DIM, D_MODEL)), _resident((Q_W, D_MODEL)),
            _resident((D_MODEL, D_MODEL)), _resident((ROUTE_ROWS, D_MODEL)), _resident((tm, tm))],
        out_specs=[tok(D_MODEL), pl.BlockSpec((tm * H2X_PITCH, LANES), lambda t: (t, 0)),
                   pl.BlockSpec((1, tm), lambda t: (0, t)),
                   pl.BlockSpec((EXPERTS_PER_GROUP, LANES), lambda t: (0, 0))],
        out_shape=[jax.ShapeDtypeStruct((n_tok, D_MODEL), f32),
                   jax.ShapeDtypeStruct((n_tok * H2X_PITCH, LANES), f32),
                   jax.ShapeDtypeStruct((1, n_tok), jnp.int32),
                   jax.ShapeDtypeStruct((EXPERTS_PER_GROUP, LANES), f32)],
        scratch_shapes=[pltpu.VMEM((EXPERTS_PER_GROUP, LANES), f32)],
        compiler_params=_cparams(("arbitrary",)),
        name="s3",
    )(*x_args, oa, ob, oc, mod, g1, g2, wg, wg, wg, wba, wbb, wbc, wout, wrt, tri)


MOE_TILE_LOG2 = 8
MOE_TILE = 1 << MOE_TILE_LOG2
ROW_UNROLL_LOG2 = 3
ROW_UNROLL = 1 << ROW_UNROLL_LOG2
GROUP_FF = EXPERTS_PER_GROUP * EXPERT_FF
PLAN_FIRST, PLAN_ITEMS, PLAN_LEN = 0, N_EXPERT_GROUPS, N_EXPERT_GROUPS + 1


def _moe_kernel(code_ref, cnt_ref, h2x_hbm, weg_ref, weu_ref, wed_ref, out_hbm,
                pos_ref, plan_ref, gbuf, obuf, dump, gsem, ssem, *, n_tok):
    tm = MOE_TILE
    k = pl.program_id(0)
    n_steps = pl.num_programs(0)
    g_rows = tm * H2X_PITCH
    o_rows = tm * MOE_PITCH

    def out_slab(buf, slot, i):
        return buf.at[pl.ds(pl.multiple_of(slot * o_rows + i * MOE_PITCH, MOE_PITCH), MOE_PITCH)]

    def token_slab(tok):
        return out_hbm.at[pl.ds(pl.multiple_of(tok * MOE_PITCH, MOE_PITCH), MOE_PITCH)]

    def for_row_chunks(n_chunks, body):
        def chunk(c, carry):
            body(c * ROW_UNROLL)
            return carry
        lax.fori_loop(0, n_chunks, chunk, 0)

    @pl.when(k == 0)
    def _plan():
        first = jnp.int32(0)
        for g in range(N_EXPERT_GROUPS):
            c = cnt_ref[g]
            n_tiles = (c + (tm - 1)) >> MOE_TILE_LOG2
            plan_ref[PLAN_FIRST + g] = first

            def pad(p, carry):
                pos_ref[p] = 0
                return carry

            lax.fori_loop(g * n_tok + c, g * n_tok + n_tiles * tm, pad, 0)
            first = first + n_tiles
        plan_ref[PLAN_ITEMS] = first

        def place(t0):
            dests = [code_ref[t0 + j] for j in range(ROW_UNROLL)]
            for j, dest in enumerate(dests):
                pos_ref[dest] = t0 + j

        for_row_chunks(n_tok >> ROW_UNROLL_LOG2, place)

    n_items = plan_ref[PLAN_ITEMS]

    def item_group(item):
        return ((item >= plan_ref[PLAN_FIRST + 1]).astype(jnp.int32)
                + (item >= plan_ref[PLAN_FIRST + 2]).astype(jnp.int32)
                + (item >= plan_ref[PLAN_FIRST + 3]).astype(jnp.int32))

    def item_pos0(item):
        g = item_group(item)
        return g * n_tok + (item - plan_ref[PLAN_FIRST + g]) * tm

    def start_gather(item, slot):
        pos0 = item_pos0(item)

        def chunk(i0):
            toks = [pos_ref[pos0 + i0 + j] for j in range(ROW_UNROLL)]
            for j, tok in enumerate(toks):
                pltpu.make_async_copy(h2x_hbm.at[pl.ds(tok * H2X_PITCH, H2X_PITCH)],
                                      gbuf.at[pl.ds(slot * g_rows + (i0 + j) * H2X_PITCH, H2X_PITCH)],
                                      gsem.at[slot]).start()
        for_row_chunks(tm >> ROW_UNROLL_LOG2, chunk)

    def wait_gather(slot):
        pltpu.make_async_copy(h2x_hbm.at[pl.ds(0, g_rows)], gbuf.at[pl.ds(slot * g_rows, g_rows)],
                              gsem.at[slot]).wait()

    def start_scatter(item, n_valid, slot):
        pos0 = item_pos0(item)

        def to_token(i0):
            dsts = [pos_ref[pos0 + i0 + j] for j in range(ROW_UNROLL)]
            for j, dst in enumerate(dsts):
                pltpu.make_async_copy(out_slab(obuf, slot, i0 + j), token_slab(dst), ssem.at[slot]).start()

        def row_to_token(i, carry):
            pltpu.make_async_copy(out_slab(obuf, slot, i), token_slab(pos_ref[pos0 + i]), ssem.at[slot]).start()
            return carry

        def row_to_dump(i, carry):
            pltpu.make_async_copy(out_slab(obuf, slot, i), out_slab(dump, slot, i), ssem.at[slot]).start()
            return carry

        full = n_valid >> ROW_UNROLL_LOG2
        for_row_chunks(full, to_token)
        lax.fori_loop(full * ROW_UNROLL, n_valid, row_to_token, 0)
        lax.fori_loop(n_valid, tm, row_to_dump, 0)

    def wait_scatter(slot):
        pltpu.make_async_copy(obuf.at[pl.ds(slot * o_rows, o_rows)], out_hbm.at[pl.ds(0, o_rows)],
                              ssem.at[slot]).wait()

    slot = k % 2

    @pl.when(k == 0)
    def _():
        start_gather(0, 0)

    @pl.when(k + 1 < n_items)
    def _():
        start_gather(k + 1, 1 - slot)

    @pl.when(k < n_items)
    def _():
        group = item_group(k)
        wait_gather(slot)
        h = _rows_from_slabs(gbuf, slot * g_rows, tm, H2X_PITCH).astype(bf16)
        w8 = gbuf[pl.ds(slot * g_rows + SLAB_ROWS, tm, stride=H2X_PITCH), :]
        cols = pl.ds(pl.multiple_of(group * GROUP_FF, GROUP_FF), GROUP_FF)
        gate = _dot(h, weg_ref[:, cols])
        a = gate * _sigmoid(gate) * _dot(h, weu_ref[:, cols])
        pieces = [a[:, j * EXPERT_FF:(j + 1) * EXPERT_FF] * w8[:, j:j + 1] for j in range(EXPERTS_PER_GROUP)]
        out = _dot(jnp.concatenate(pieces, axis=1).astype(bf16), wed_ref[group])

        @pl.when(k >= 2)
        def _():
            wait_scatter(slot)

        _rows_to_slabs(obuf, slot * o_rows, out, MOE_PITCH)
        n_valid = jnp.minimum(tm, cnt_ref[group] - (k - plan_ref[PLAN_FIRST + group]) * tm)
        start_scatter(k, n_valid, slot)

    @pl.when(k == n_steps - 1)
    def _drain():
        wait_scatter((n_items - 1) % 2)

        @pl.when(n_items >= 2)
        def _():
            wait_scatter(n_items % 2)


def _moe_call(code, cnt, h2x, weg, weu, wed):
    n_tok = h2x.shape[0] // H2X_PITCH
    tm = MOE_TILE
    n_steps = n_tok // tm + N_EXPERT_GROUPS - 1
    ff = N_EXPERT_GROUPS * GROUP_FF
    up_spec = pl.BlockSpec((D_MODEL, ff), lambda k, *_: (0, 0), pipeline_mode=pl.Buffered(1))
    down_spec = pl.BlockSpec((N_EXPERT_GROUPS, GROUP_FF, D_MODEL), lambda k, *_: (0, 0, 0),
                             pipeline_mode=pl.Buffered(1))
    grid_spec = pltpu.PrefetchScalarGridSpec(
        num_scalar_prefetch=2,
        grid=(n_steps,),
        in_specs=[pl.BlockSpec(memory_space=pl.ANY), up_spec, up_spec, down_spec],
        out_specs=pl.BlockSpec(memory_space=pl.ANY),
        scratch_shapes=[
            pltpu.SMEM((N_EXPERT_GROUPS * n_tok,), jnp.int32),
            pltpu.SMEM((PLAN_LEN,), jnp.int32),
            pltpu.VMEM((2 * tm * H2X_PITCH, LANES), f32),
            pltpu.VMEM((2 * tm * MOE_PITCH, LANES), f32),
            pltpu.VMEM((2 * tm * MOE_PITCH, LANES), f32),
            pltpu.SemaphoreType.DMA((2,)),
            pltpu.SemaphoreType.DMA((2,)),
        ],
    )
    return pl.pallas_call(
        functools.partial(_moe_kernel, n_tok=n_tok),
        grid_spec=grid_spec,
        out_shape=jax.ShapeDtypeStruct((n_tok * MOE_PITCH, LANES), f32),
        compiler_params=_cparams(("arbitrary",)),
        name="moe",
    )(code, cnt, h2x, weg, weu, wed)


def _final_kernel(x1_ref, moe_ref, mod_ref, gf_ref, o_ref):
    x2 = x1_ref[...] + mod_ref[0, 5:6, :] * _rows_from_slabs(moe_ref, 0, x1_ref.shape[0], MOE_PITCH)
    y = x2 * lax.rsqrt(jnp.mean(x2 * x2, axis=-1, keepdims=True) + NORM_EPS)
    o_ref[...] = y * gf_ref[...]


def _final_call(x1, moe, mod, gf, *, tokens_per_batch, cond_row0, per_batch_cond):
    n_tok = x1.shape[0]
    tok = pl.BlockSpec((TOKEN_TILE, D_MODEL), lambda t: (t, 0))
    slab = pl.BlockSpec((TOKEN_TILE * MOE_PITCH, LANES), lambda t: (t, 0))
    cond_idx = _cond_index_map(tokens_per_batch, cond_row0, per_batch_cond)
    return pl.pallas_call(
        _final_kernel,
        grid=(n_tok // TOKEN_TILE,),
        in_specs=[tok, slab, pl.BlockSpec((1, 6, D_MODEL), cond_idx), _resident((1, D_MODEL))],
        out_specs=tok,
        out_shape=jax.ShapeDtypeStruct((n_tok, D_MODEL), f32),
        compiler_params=_cparams(("arbitrary",)),
        name="final_norm",
    )(x1, moe, mod, gf)


def _dup_heads(w):
    a, b = w[..., :HEAD_DIM], w[..., HEAD_DIM:]
    return jnp.concatenate([a, a, b, b], axis=-1)


def _rope_tables(n):
    rows = n // GRID_W
    row = jnp.repeat(jnp.arange(rows, dtype=f32), GRID_W)
    col = jnp.tile(jnp.arange(GRID_W, dtype=f32), rows)
    inv = ROPE_THETA ** (-jnp.arange(AXIS_FREQS, dtype=f32) / AXIS_FREQS)
    ang = jnp.concatenate([row[:, None] * inv, col[:, None] * inv], axis=-1)
    cos, sin = jnp.cos(ang), jnp.sin(ang)
    cos_t = jnp.concatenate([cos, cos, cos, cos], axis=-1)
    sin_t = jnp.concatenate([-sin, sin, -sin, sin], axis=-1)
    return cos_t, sin_t


def _dft_tables(n):
    k = np.arange(n)
    m = (k[:, None] * k[None, :]) % n
    ang = 2.0 * np.pi * m / n
    return np.cos(ang), np.sin(ang)


def _fourier_tables(seq):
    cn, sn = _dft_tables(seq)
    cc, sc = _dft_tables(FOURIER_GROUP_DIM)
    eye = np.eye(N_FOURIER_GROUPS)
    bdc, bds = np.kron(eye, cc), np.kron(eye, sc)
    return tuple(jnp.asarray(t, dtype=f32).astype(bf16) for t in (bdc, bds, cn, sn))


def _dup_cache(cache, l):
    b, _, _, p, _, _ = cache.shape
    k = _dup_heads(cache[:, l, 0].reshape(b, p, N_KV * HEAD_DIM)).astype(bf16)
    v = _dup_heads(cache[:, l, 1].reshape(b, p, N_KV * HEAD_DIM)).astype(bf16)
    return k, v


def kernel(x_prompt, x_sample, cache_global_kv, cache_window_kv, c, c_ctx, w_mod, b_mod, norm1_g, w_in,
           q_norm_g, k_norm_g, sink_logit, w_branch_a, w_branch_b, w_branch_c, w_out, norm2_g,
           w_route_group, w_route_expert, w_exp_gate, w_exp_up, w_exp_down, final_norm_g):
    batch, seq, _ = x_prompt.shape
    dec_batch, dec_seq, _ = x_sample.shape

    cond = jnp.zeros((N_COND_ROWS, D_MODEL), f32).at[0].set(c_ctx).at[1:1 + dec_batch].set(c)
    mod = _mod_call(cond, w_mod, b_mod).reshape(DEPTH, N_COND_ROWS, 6, D_MODEL)

    ones_blk = jnp.asarray(np.kron(np.eye(N_HEADS), np.ones((HEAD_DIM, HEAD_DIM))), dtype=bf16)
    tri = jnp.asarray(np.triu(np.ones((TOKEN_TILE, TOKEN_TILE))), dtype=bf16)
    rope_tabs = _rope_tables(dec_seq)
    ftab_ctx = _fourier_tables(seq)
    ftab_lat = _fourier_tables(dec_seq)
    gf = final_norm_g.reshape(1, D_MODEL)

    ctx = dict(tokens_per_batch=seq, cond_row0=0, per_batch_cond=False)
    lat = dict(tokens_per_batch=dec_seq, cond_row0=1, per_batch_cond=True)
    xp = (x_prompt.reshape(batch * seq, D_MODEL),)
    xs = (x_sample.reshape(dec_batch * dec_seq, D_MODEL),)
    mod_prev = None
    gkv, wkv = [], []
    for l in range(DEPTH):
        w_ext = w_gate = w_in[l].astype(bf16)
        g1 = norm1_g[l].reshape(1, D_MODEL)
        g2 = norm2_g[l].reshape(1, D_MODEL)
        qg = jnp.tile(q_norm_g[l], N_HEADS).reshape(1, Q_W)
        kg = jnp.tile(k_norm_g[l], N_KV).reshape(1, LANES)
        wba, wbb, wbc = (w.astype(bf16) for w in (w_branch_a[l], w_branch_b[l], w_branch_c[l]))
        wout = w_out[l].astype(bf16)
        wrt = jnp.concatenate([w_route_expert[l].T, w_route_group[l].T,
                               jnp.zeros((ROUTE_ROWS - N_EXPERTS - N_EXPERT_GROUPS, D_MODEL), f32)], axis=0)
        weg, weu = w_exp_gate[l].astype(bf16), w_exp_up[l].astype(bf16)
        wed = w_exp_down[l].reshape(N_EXPERT_GROUPS, GROUP_FF, D_MODEL).astype(bf16)
        sink = sink_logit[l]

        def mix_and_moe(xin, qa, ka, va, qc, kc, vc, ub, attn, ftab, seq_len, stream):
            oa, oc = attn(qa, ka, va, qc, kc, vc)
            ob = _fourier_call(ub, *ftab, seq_len)
            x1, h2x, code, cnt = _s3_call(xin, mod_prev, oa, ob, oc, mod[l], g1, g2, w_gate, wba, wbb, wbc, wout,
                                          wrt, tri, **stream)
            counts = cnt[:, 0].astype(jnp.int32)
            moe = _moe_call(code.reshape(-1), counts, h2x, weg, weu, wed)
            return (x1, moe)

        qa, ka, va, qc, kc, vc, ub, gk, gv, wk, wv = _s1_call(
            xp, mod_prev, mod[l], g1, w_ext, ones_blk, qg, kg, None, emit_kv=True, **ctx)
        attn_ctx = lambda *qkv: _attn_ctx_call(sink, *qkv, seq)
        xp = mix_and_moe(xp, qa, ka, va, qc, kc, vc, ub, attn_ctx, ftab_ctx, seq, ctx)
        kv_shape = (batch, seq, N_KV, HEAD_DIM)
        gkv.append(jnp.stack([gk.reshape(kv_shape), gv.reshape(kv_shape)], axis=1))
        wkv.append(jnp.stack([wk.reshape(kv_shape), wv.reshape(kv_shape)], axis=1))

        qa, ka, va, qc, kc, vc, ub = _s1_call(
            xs, mod_prev, mod[l], g1, w_ext, ones_blk, qg, kg, rope_tabs, emit_kv=False, **lat)
        kxa, vxa = _dup_cache(cache_global_kv, l)
        kxc, vxc = _dup_cache(cache_window_kv, l)
        attn_lat = lambda qa, ka, va, qc, kc, vc: _attn_lat_call(
            sink, qa, ka, va, kxa, vxa, qc, kc, vc, kxc, vxc, dec_seq)
        xs = mix_and_moe(xs, qa, ka, va, qc, kc, vc, ub, attn_lat, ftab_lat, dec_seq, lat)
        mod_prev = mod[l]

    y_prompt = _final_call(*xp, mod_prev, gf, **ctx).reshape(batch, seq, D_MODEL)
    y_sample = _final_call(*xs, mod_prev, gf, **lat).reshape(dec_batch, dec_seq, D_MODEL)
    return (y_prompt, y_sample, jnp.stack(gkv, axis=1), jnp.stack(wkv, axis=1))
```

```python
import functools

import jax
import jax.numpy as jnp
import numpy as np
from jax import lax
from jax.experimental import pallas as pl
from jax.experimental.pallas import tpu as pltpu

f32 = jnp.float32
bf16 = jnp.bfloat16

D_MODEL = 1024
DEPTH = 2
GRID_W = 64
HEAD_DIM = 64
HALF = HEAD_DIM // 2
N_HEADS = 8
N_KV = 2
WINDOW = 128
N_FOURIER_GROUPS = 4
FOURIER_GROUP_DIM = 128
FOURIER_DIM = N_FOURIER_GROUPS * FOURIER_GROUP_DIM
ROPE_THETA = 10000.0
AXIS_FREQS = HEAD_DIM // 4
N_EXPERT_GROUPS = 4
EXPERTS_PER_GROUP = 8
N_EXPERTS = N_EXPERT_GROUPS * EXPERTS_PER_GROUP
EXPERT_FF = 128
NORM_EPS = 1e-6
NEG_INF = -1e30

LANES = 128
Q_W = N_HEADS * HEAD_DIM
KV_DUP_W = 2 * N_KV * HEAD_DIM
C_QA, C_KA, C_VA, C_QC, C_KC, C_VC, C_UB, C_GATES, C_END = 0, 512, 640, 768, 1280, 1408, 1536, 2048, 5120
N_COND_ROWS = 16
VMEM_LIMIT = 56 * 1024 * 1024


def _cparams(sem):
    return pltpu.CompilerParams(dimension_semantics=sem, vmem_limit_bytes=VMEM_LIMIT)


def _resident(shape):
    nd = len(shape)
    return pl.BlockSpec(shape, lambda *_: (0,) * nd, pipeline_mode=pl.Buffered(1))


def _sigmoid(x):
    return 0.5 * jnp.tanh(0.5 * x) + 0.5


def _rms_mod(x, g, scale, shift):
    y = x * lax.rsqrt(jnp.mean(x * x, axis=-1, keepdims=True) + NORM_EPS)
    return (y * g) * (1.0 + scale) + shift


def _dot(a, b):
    return jnp.dot(a, b, preferred_element_type=f32)


def _dot_t(a, b):
    return lax.dot_general(a, b, (((1,), (1,)), ((), ())), preferred_element_type=f32)


def _split_bf16(x):
    hi = x.astype(bf16)
    lo = (x - hi.astype(f32)).astype(bf16)
    return hi, lo


def _mod_kernel(cond_ref, w_ref, b_ref, o_ref):
    c = cond_ref[...]
    s = (c * _sigmoid(c)).astype(bf16)
    o_ref[0] = _dot(s, w_ref[0].astype(bf16)) + b_ref[0]


def _mod_call(cond, w_mod, b_mod):
    n_tiles = 6 * D_MODEL // 1024
    return pl.pallas_call(
        _mod_kernel,
        grid=(DEPTH, n_tiles),
        in_specs=[
            pl.BlockSpec((N_COND_ROWS, D_MODEL), lambda l, j: (0, 0)),
            pl.BlockSpec((1, D_MODEL, 1024), lambda l, j: (l, 0, j)),
            pl.BlockSpec((1, 1, 1024), lambda l, j: (l, 0, j)),
        ],
        out_specs=pl.BlockSpec((1, N_COND_ROWS, 1024), lambda l, j: (l, 0, j)),
        out_shape=jax.ShapeDtypeStruct((DEPTH, N_COND_ROWS, 6 * D_MODEL), f32),
        compiler_params=_cparams(("arbitrary", "arbitrary")),
        name="mod",
    )(cond, w_mod, b_mod.reshape(DEPTH, 1, 6 * D_MODEL))


def _head_norm(z, g, ones_blk):
    hi, lo = _split_bf16(z * z)
    ss = _dot(hi, ones_blk) + _dot(lo, ones_blk)
    return z * lax.rsqrt(ss * (1.0 / HEAD_DIM) + NORM_EPS) * g


def _rope(z, cos, sin_signed, first_half):
    outs = []
    for j in range(z.shape[1] // LANES):
        zj = z[:, j * LANES:(j + 1) * LANES]
        swapped = jnp.where(first_half, pltpu.roll(zj, LANES - HALF, 1), pltpu.roll(zj, HALF, 1))
        outs.append(zj * cos + swapped * sin_signed)
    return jnp.concatenate(outs, axis=1)


def _layer_input(x_refs, r0, n):
    rows = pl.ds(r0, n)
    if len(x_refs) == 1:
        return x_refs[0][rows, :]
    x1_ref, moe_ref, modp_ref = x_refs
    return x1_ref[rows, :] + modp_ref[0, 5:6, :] * _rows_from_slabs(moe_ref, r0 * SLAB_ROWS, n, SLAB_ROWS)


S1_SUB_TILES = 2
S3_SUB_TILES = 1


SLAB_ROWS = D_MODEL // LANES


def _rows_from_slabs(ref, start, n_tok, pitch):
    return jnp.concatenate([ref[pl.ds(start + c, n_tok, stride=pitch), :] for c in range(SLAB_ROWS)], axis=1)


def _rows_to_slabs(ref, start, x, pitch):
    for c in range(SLAB_ROWS):
        ref[pl.ds(start + c, x.shape[0], stride=pitch), :] = x[:, c * LANES:(c + 1) * LANES]


def _s1_kernel(*refs, rope, emit_kv, n_x):
    x_refs, refs = refs[:n_x], refs[n_x:]
    mod_ref, g1_ref, w_ref, ones_ref, qg_ref, kg_ref = refs[:6]
    pos = 6
    if rope:
        cos_ref, sin_ref = refs[pos:pos + 2]
        pos += 2
    qa_ref, ka_ref, va_ref, qc_ref, kc_ref, vc_ref, ub_ref = refs[pos:pos + 7]
    pos += 7
    if emit_kv:
        gk_ref, gv_ref, wk_ref, wv_ref = refs[pos:pos + 4]

    tm = x_refs[0].shape[0] // S1_SUB_TILES
    lane = lax.broadcasted_iota(jnp.int32, (tm, LANES), 1)
    low_lanes = lane < HEAD_DIM
    first_half = (lane & (HEAD_DIM - 1)) < HALF
    q_scale = HEAD_DIM ** -0.5

    def dup(z):
        r = pltpu.roll(z, HEAD_DIM, 1)
        return jnp.concatenate([jnp.where(low_lanes, z, r), jnp.where(low_lanes, r, z)], axis=1).astype(bf16)

    for sub in range(S1_SUB_TILES):
        rows = pl.ds(sub * tm, tm)
        x = _layer_input(x_refs, sub * tm, tm)
        h = _rms_mod(x, g1_ref[...], mod_ref[0, 1:2, :], mod_ref[0, 0:1, :]).astype(bf16)
        proj = lambda lo, hi: _dot(h, w_ref[:, lo:hi])
        if rope:
            cos = cos_ref[rows, :]
            sin_signed = sin_ref[rows, :]
            rot = lambda z: _rope(z, cos, sin_signed, first_half)
        else:
            rot = lambda z: z

        qa = rot(_head_norm(proj(C_QA, C_KA), qg_ref[...], ones_ref[...]))
        qa_ref[rows, :] = (qa * q_scale).astype(bf16)
        ka_n = _head_norm(proj(C_KA, C_VA), kg_ref[...], ones_ref[0:LANES, 0:LANES])
        ka_ref[rows, :] = dup(rot(ka_n))
        va = proj(C_VA, C_QC)
        va_ref[rows, :] = dup(va)
        qc_ref[rows, :] = (rot(proj(C_QC, C_KC)) * q_scale).astype(bf16)
        kc = proj(C_KC, C_VC)
        kc_ref[rows, :] = dup(rot(kc))
        vc = proj(C_VC, C_UB)
        vc_ref[rows, :] = dup(vc)
        ub_ref[rows, :] = proj(C_UB, C_GATES).astype(bf16)
        if emit_kv:
            gk_ref[rows, :] = ka_n
            gv_ref[rows, :] = va
            wk_ref[rows, :] = kc
            wv_ref[rows, :] = vc


TOKEN_TILE = 512


def _cond_index_map(tokens_per_batch, cond_row0, per_batch_cond):
    tiles_per_batch = max(tokens_per_batch // TOKEN_TILE, 1)
    if per_batch_cond:
        assert tokens_per_batch % TOKEN_TILE == 0
        return lambda t: (cond_row0 + t // tiles_per_batch, 0, 0)
    return lambda t: (cond_row0, 0, 0)


def _layer_input_specs(xin, mod_prev, cond_idx):
    tok = pl.BlockSpec((TOKEN_TILE, D_MODEL), lambda t: (t, 0))
    if len(xin) == 1:
        return [tok], list(xin)
    slab = pl.BlockSpec((TOKEN_TILE * SLAB_ROWS, LANES), lambda t: (t, 0))
    return [tok, slab, pl.BlockSpec((1, 6, D_MODEL), cond_idx)], [xin[0], xin[1], mod_prev]


def _s1_call(xin, mod_prev, mod, g1, w_ext, ones_blk, qg, kg, rope_tabs, *, tokens_per_batch, cond_row0,
             per_batch_cond, emit_kv):
    n_tok = xin[0].shape[0]
    tm = TOKEN_TILE
    rope = rope_tabs is not None
    tiles_per_batch = max(tokens_per_batch // tm, 1)
    cond_idx = _cond_index_map(tokens_per_batch, cond_row0, per_batch_cond)

    tok = lambda w: pl.BlockSpec((tm, w), lambda t: (t, 0))
    x_specs, x_args = _layer_input_specs(xin, mod_prev, cond_idx)
    in_specs = x_specs + [
        pl.BlockSpec((1, 6, D_MODEL), cond_idx),
        _resident((1, D_MODEL)),
        _resident((D_MODEL, C_GATES)),
        _resident((Q_W, Q_W)),
        _resident((1, Q_W)),
        _resident((1, LANES)),
    ]
    args = x_args + [mod, g1, w_ext, ones_blk, qg, kg]
    if rope:
        in_specs += [pl.BlockSpec((tm, LANES), lambda t: (t % tiles_per_batch, 0))] * 2
        args += list(rope_tabs)
    widths = [Q_W, KV_DUP_W, KV_DUP_W, Q_W, KV_DUP_W, KV_DUP_W, FOURIER_DIM]
    out_specs = [tok(w) for w in widths]
    out_shape = [jax.ShapeDtypeStruct((n_tok, w), bf16) for w in widths]
    if emit_kv:
        out_specs += [tok(LANES)] * 4
        out_shape += [jax.ShapeDtypeStruct((n_tok, LANES), f32)] * 4
    return pl.pallas_call(
        functools.partial(_s1_kernel, rope=rope, emit_kv=emit_kv, n_x=len(x_args)),
        grid=(n_tok // tm,),
        in_specs=in_specs,
        out_specs=out_specs,
        out_shape=out_shape,
        compiler_params=_cparams(("arbitrary",)),
        name="s1_rope" if rope else "s1_ctx",
    )(*args)


def _gqa_group(q2, segs, sink_pair_lo, sink_pair_hi):
    qb = q2.shape[0]
    qs = jnp.concatenate([q2[:, 0:LANES], q2[:, LANES:2 * LANES]], axis=0)
    lane = lax.broadcasted_iota(jnp.int32, qs.shape, 1)
    low = lane < HEAD_DIM
    zero = jnp.zeros_like(qs)
    row = lax.broadcasted_iota(jnp.int32, (2 * qb, 1), 0)

    def half(qh, sink_pair):
        scores = []
        for k, _, ok in segs:
            s = _dot_t(qh, k)
            if ok is not None:
                s = jnp.where(ok, s, NEG_INF)
            scores.append(s)
        m = scores[0].max(axis=-1, keepdims=True)
        for s in scores[1:]:
            m = jnp.maximum(m, s.max(axis=-1, keepdims=True))
        if sink_pair is not None:
            sink = jnp.where(row < qb, sink_pair[0], sink_pair[1])
            m = jnp.maximum(m, sink)
            denom = jnp.exp(sink - m)
        else:
            denom = jnp.zeros_like(m)
        acc = None
        for s, (_, v, _) in zip(scores, segs):
            p = jnp.exp(s - m)
            denom = denom + p.sum(axis=-1, keepdims=True)
            pv = _dot(p.astype(bf16), v)
            acc = pv if acc is None else acc + pv
        return acc / denom

    o_lo = half(jnp.where(low, qs, zero), sink_pair_lo)
    o_hi = half(jnp.where(low, zero, qs), sink_pair_hi)
    o = jnp.where(low, o_lo, o_hi)
    return jnp.concatenate([o[0:qb], o[qb:2 * qb]], axis=1)


def _sink_pairs(sink_ref, kv):
    h = 4 * kv
    return (sink_ref[h], sink_ref[h + 2]), (sink_ref[h + 1], sink_ref[h + 3])


def _attn_ctx_kernel(sink_ref, qa_ref, ka_ref, va_ref, qc_ref, kc_ref, vc_ref, oa_ref, oc_ref):
    for kv in range(N_KV):
        cq = slice(2 * LANES * kv, 2 * LANES * (kv + 1))
        ck = slice(LANES * kv, LANES * (kv + 1))
        oa = _gqa_group(qa_ref[:, cq], [(ka_ref[:, ck], va_ref[:, ck], None)], None, None)
        oa_ref[:, cq] = oa.astype(bf16)
        lo, hi = _sink_pairs(sink_ref, kv)
        oc = _gqa_group(qc_ref[:, cq], [(kc_ref[:, ck], vc_ref[:, ck], None)], lo, hi)
        oc_ref[:, cq] = oc.astype(bf16)


def _attn_ctx_call(sink, qa, ka, va, qc, kc, vc, seq):
    n_tok = qa.shape[0]
    tok = lambda w: pl.BlockSpec((seq, w), lambda b: (b, 0))
    return pl.pallas_call(
        _attn_ctx_kernel,
        grid=(n_tok // seq,),
        in_specs=[pl.BlockSpec(memory_space=pltpu.SMEM),
                  tok(Q_W), tok(KV_DUP_W), tok(KV_DUP_W), tok(Q_W), tok(KV_DUP_W), tok(KV_DUP_W)],
        out_specs=[tok(Q_W), tok(Q_W)],
        out_shape=[jax.ShapeDtypeStruct((n_tok, Q_W), bf16)] * 2,
        compiler_params=_cparams(("arbitrary",)),
        name="attn_ctx",
    )(sink, qa, ka, va, qc, kc, vc)


QB = 256
WIN_QB = 128
WIN_SPAN = WIN_QB + 2 * WINDOW


def _attn_lat_kernel(sink_ref, qa_ref, ka_ref, va_ref, kxa_ref, vxa_ref,
                     qc_ref, kc_ref, vc_ref, kxc_ref, vxc_ref, oa_ref, oc_ref, *, seq):
    i = pl.program_id(1)
    for kv in range(N_KV):
        cq = slice(2 * LANES * kv, 2 * LANES * (kv + 1))
        ck = slice(LANES * kv, LANES * (kv + 1))
        segs = [(ka_ref[:, ck], va_ref[:, ck], None), (kxa_ref[0, :, ck], vxa_ref[0, :, ck], None)]
        oa_ref[:, cq] = _gqa_group(qa_ref[:, cq], segs, None, None).astype(bf16)

        lo, hi = _sink_pairs(sink_ref, kv)
        for sub in range(QB // WIN_QB):
            blk = i * (QB // WIN_QB) + sub
            start = jnp.clip((blk - 1) * WIN_QB, 0, seq - WIN_SPAN)
            start = pl.multiple_of(start, WIN_QB)
            r = lax.broadcasted_iota(jnp.int32, (2 * WIN_QB, WIN_SPAN), 0)
            c = lax.broadcasted_iota(jnp.int32, (2 * WIN_QB, WIN_SPAN), 1)
            qpos = blk * WIN_QB + (r & (WIN_QB - 1))
            ok = jnp.abs(qpos - (start + c)) <= WINDOW
            rows = slice(sub * WIN_QB, (sub + 1) * WIN_QB)
            segs = [(kc_ref[pl.ds(start, WIN_SPAN), ck], vc_ref[pl.ds(start, WIN_SPAN), ck], ok),
                    (kxc_ref[0, :, ck], vxc_ref[0, :, ck], None)]
            oc_ref[rows, cq] = _gqa_group(qc_ref[rows, cq], segs, lo, hi).astype(bf16)


def _attn_lat_call(sink, qa, ka, va, kxa, vxa, qc, kc, vc, kxc, vxc, seq):
    n_tok = qa.shape[0]
    nq = seq // QB
    past = kxa.shape[1]
    qspec = pl.BlockSpec((QB, Q_W), lambda b, i: (b * nq + i, 0))
    kspec = pl.BlockSpec((seq, KV_DUP_W), lambda b, i: (b, 0))
    xspec = pl.BlockSpec((1, past, KV_DUP_W), lambda b, i: (b, 0, 0))
    return pl.pallas_call(
        functools.partial(_attn_lat_kernel, seq=seq),
        grid=(n_tok // seq, nq),
        in_specs=[pl.BlockSpec(memory_space=pltpu.SMEM),
                  qspec, kspec, kspec, xspec, xspec, qspec, kspec, kspec, xspec, xspec],
        out_specs=[qspec, qspec],
        out_shape=[jax.ShapeDtypeStruct((n_tok, Q_W), bf16)] * 2,
        compiler_params=_cparams(("arbitrary", "arbitrary")),
        name="attn_lat",
    )(sink, qa, ka, va, kxa, vxa, qc, kc, vc, kxc, vxc)


def _fourier_kernel(u_ref, bdc_ref, bds_ref, cn_ref, sn_ref, o_ref, *, scale):
    u = u_ref[...]
    uc = _dot(u, bdc_ref[...]).astype(bf16)
    us = _dot(u, bds_ref[...]).astype(bf16)
    o = _dot(cn_ref[...], uc) - _dot(sn_ref[...], us)
    o_ref[...] = (o * scale).astype(bf16)


def _fourier_call(ub, bdc, bds, cn, sn, seq):
    n_tok = ub.shape[0]
    tok = pl.BlockSpec((seq, FOURIER_DIM), lambda b: (b, 0))
    scale = float(1.0 / np.sqrt(seq * FOURIER_GROUP_DIM))
    return pl.pallas_call(
        functools.partial(_fourier_kernel, scale=scale),
        grid=(n_tok // seq,),
        in_specs=[tok, _resident((FOURIER_DIM, FOURIER_DIM)), _resident((FOURIER_DIM, FOURIER_DIM)),
                  _resident((seq, seq)), _resident((seq, seq))],
        out_specs=tok,
        out_shape=jax.ShapeDtypeStruct((n_tok, FOURIER_DIM), bf16),
        compiler_params=_cparams(("arbitrary",)),
        name=f"fourier_{seq}",
    )(ub, bdc, bds, cn, sn)


ROUTE_ROWS = LANES
H2X_PITCH = SLAB_ROWS + 1
MOE_PITCH = SLAB_ROWS


def _route(lt, tri_ref, carry_ref, n_tok):
    tm = lt.shape[1]
    row = lax.broadcasted_iota(jnp.int32, (EXPERTS_PER_GROUP, tm), 0).astype(f32)
    neg = jnp.float32(-jnp.inf)
    none = jnp.float32(EXPERTS_PER_GROUP)
    gl = jnp.where(row < N_EXPERT_GROUPS, lt[N_EXPERTS:N_EXPERTS + EXPERTS_PER_GROUP], neg)
    gmax = gl.max(axis=0, keepdims=True)
    gidx = jnp.where(gl == gmax, row, none).min(axis=0, keepdims=True)
    gw = 1.0 / jnp.exp(gl - gmax).sum(axis=0, keepdims=True)
    lg = lt[0:EXPERTS_PER_GROUP]
    for g in range(1, N_EXPERT_GROUPS):
        lg = jnp.where(gidx == g, lt[g * EXPERTS_PER_GROUP:(g + 1) * EXPERTS_PER_GROUP], lg)
    m1 = lg.max(axis=0, keepdims=True)
    i1 = jnp.where(lg == m1, row, none).min(axis=0, keepdims=True)
    l2 = jnp.where(row == i1, neg, lg)
    m2 = l2.max(axis=0, keepdims=True)
    i2 = jnp.where(l2 == m2, row, none).min(axis=0, keepdims=True)
    e = jnp.exp(m2 - m1)
    w8 = jnp.where(row == i1, gw / (1.0 + e), jnp.where(row == i2, gw * e / (1.0 + e), 0.0))

    onehot = (row == gidx).astype(f32)
    prefix = _dot(onehot.astype(bf16), tri_ref[0:tm, 0:tm])
    carry = carry_ref[...]
    rank = (onehot * (prefix - 1.0 + carry[:, 0:1])).sum(axis=0, keepdims=True)
    carry_ref[...] = carry + onehot.sum(axis=1, keepdims=True)
    dest = gidx.astype(jnp.int32) * n_tok + rank.astype(jnp.int32)
    return w8, dest


def _s3_kernel(*refs, n_x, n_tok):
    x_refs, refs = refs[:n_x], refs[n_x:]
    (oa_ref, ob_ref, oc_ref, mod_ref, g1_ref, g2_ref, wga_ref, wgb_ref, wgc_ref, wba_ref, wbb_ref, wbc_ref,
     wout_ref, wrt_ref, tri_ref, x1_ref, h2x_ref, code_ref, cnt_ref, carry_ref) = refs
    wg_refs = (wga_ref, wgb_ref, wgc_ref)

    @pl.when(pl.program_id(0) == 0)
    def _():
        carry_ref[...] = jnp.zeros_like(carry_ref)

    tm = x1_ref.shape[0] // S3_SUB_TILES
    wh, wl = _split_bf16(wrt_ref[...])
    for sub in range(S3_SUB_TILES):
        r0 = sub * tm
        rows = pl.ds(r0, tm)
        x = _layer_input(x_refs, r0, tm)
        h = _rms_mod(x, g1_ref[...], mod_ref[0, 1:2, :], mod_ref[0, 0:1, :]).astype(bf16)
        merged = None
        for j, (o_ref, wb_ref) in enumerate(((oa_ref, wba_ref), (ob_ref, wbb_ref), (oc_ref, wbc_ref))):
            gate = _sigmoid(_dot(h, wg_refs[j][...]))
            term = gate * _dot(o_ref[rows, :], wb_ref[...])
            merged = term if merged is None else merged + term
        mix = _dot(merged.astype(bf16), wout_ref[...])
        x1 = x + mod_ref[0, 2:3, :] * mix
        x1_ref[rows, :] = x1
        h2 = _rms_mod(x1, g2_ref[...], mod_ref[0, 4:5, :], mod_ref[0, 3:4, :])
        _rows_to_slabs(h2x_ref, r0 * H2X_PITCH, h2, H2X_PITCH)
        th, tl = _split_bf16(h2)
        lt = _dot_t(wh, th) + (_dot_t(wh, tl) + _dot_t(wl, th))
        w8, code = _route(lt, tri_ref, carry_ref, n_tok)
        code_ref[:, rows] = code
        w_rows = jnp.concatenate([w8, jnp.zeros((LANES - EXPERTS_PER_GROUP, tm), f32)], axis=0)
        h2x_ref[pl.ds(r0 * H2X_PITCH + SLAB_ROWS, tm, stride=H2X_PITCH), :] = w_rows.T
    cnt_ref[...] = carry_ref[...]


def _s3_call(xin, mod_prev, oa, ob, oc, mod, g1, g2, wg, wba, wbb, wbc, wout, wrt, tri, *, tokens_per_batch,
             cond_row0, per_batch_cond):
    n_tok = xin[0].shape[0]
    tm = TOKEN_TILE
    cond_idx = _cond_index_map(tokens_per_batch, cond_row0, per_batch_cond)
    tok = lambda w: pl.BlockSpec((tm, w), lambda t: (t, 0))
    x_specs, x_args = _layer_input_specs(xin, mod_prev, cond_idx)
    gate_cols = lambda j: pl.BlockSpec((D_MODEL, D_MODEL), lambda t: (0, C_GATES // D_MODEL + j),
                                       pipeline_mode=pl.Buffered(1))
    return pl.pallas_call(
        functools.partial(_s3_kernel, n_x=len(x_args), n_tok=n_tok),
        grid=(n_tok // tm,),
        in_specs=x_specs + [
            tok(Q_W), tok(FOURIER_DIM), tok(Q_W),
            pl.BlockSpec((1, 6, D_MODEL), cond_idx),
            _resident((1, D_MODEL)), _resident((1, D_MODEL)),
            gate_cols(0), gate_cols(1), gate_cols(2),
            _resident((Q_W, D_MODEL)), _resident((FOURIER_DIM, D_MODEL)), _resident((Q_W, D_MODEL)),
            _resident((D_MODEL, D_MODEL)), _resident((ROUTE_ROWS, D_MODEL)), _resident((tm, tm))],
        out_specs=[tok(D_MODEL), pl.BlockSpec((tm * H2X_PITCH, LANES), lambda t: (t, 0)),
                   pl.BlockSpec((1, tm), lambda t: (0, t)),
                   pl.BlockSpec((EXPERTS_PER_GROUP, LANES), lambda t: (0, 0))],
        out_shape=[jax.ShapeDtypeStruct((n_tok, D_MODEL), f32),
                   jax.ShapeDtypeStruct((n_tok * H2X_PITCH, LANES), f32),
                   jax.ShapeDtypeStruct((1, n_tok), jnp.int32),
                   jax.ShapeDtypeStruct((EXPERTS_PER_GROUP, LANES), f32)],
        scratch_shapes=[pltpu.VMEM((EXPERTS_PER_GROUP, LANES), f32)],
        compiler_params=_cparams(("arbitrary",)),
        name="s3",
    )(*x_args, oa, ob, oc, mod, g1, g2, wg, wg, wg, wba, wbb, wbc, wout, wrt, tri)


MOE_TILE_LOG2 = 8
MOE_TILE = 1 << MOE_TILE_LOG2
ROW_UNROLL_LOG2 = 3
ROW_UNROLL = 1 << ROW_UNROLL_LOG2
GROUP_FF = EXPERTS_PER_GROUP * EXPERT_FF
PLAN_FIRST, PLAN_ITEMS, PLAN_LEN = 0, N_EXPERT_GROUPS, N_EXPERT_GROUPS + 1


def _moe_kernel(code_ref, cnt_ref, h2x_hbm, weg_ref, weu_ref, wed_ref, out_hbm,
                pos_ref, plan_ref, gbuf, obuf, dump, gsem, ssem, *, n_tok):
    tm = MOE_TILE
    k = pl.program_id(0)
    n_steps = pl.num_programs(0)
    g_rows = tm * H2X_PITCH
    o_rows = tm * MOE_PITCH

    def out_slab(buf, slot, i):
        return buf.at[pl.ds(pl.multiple_of(slot * o_rows + i * MOE_PITCH, MOE_PITCH), MOE_PITCH)]

    def token_slab(tok):
        return out_hbm.at[pl.ds(pl.multiple_of(tok * MOE_PITCH, MOE_PITCH), MOE_PITCH)]

    def for_row_chunks(n_chunks, body):
        def chunk(c, carry):
            body(c * ROW_UNROLL)
            return carry
        lax.fori_loop(0, n_chunks, chunk, 0)

    @pl.when(k == 0)
    def _plan():
        first = jnp.int32(0)
        for g in range(N_EXPERT_GROUPS):
            c = cnt_ref[g]
            n_tiles = (c + (tm - 1)) >> MOE_TILE_LOG2
            plan_ref[PLAN_FIRST + g] = first

            def pad(p, carry):
                pos_ref[p] = 0
                return carry

            lax.fori_loop(g * n_tok + c, g * n_tok + n_tiles * tm, pad, 0)
            first = first + n_tiles
        plan_ref[PLAN_ITEMS] = first

        def place(t0):
            dests = [code_ref[t0 + j] for j in range(ROW_UNROLL)]
            for j, dest in enumerate(dests):
                pos_ref[dest] = t0 + j

        for_row_chunks(n_tok >> ROW_UNROLL_LOG2, place)

    n_items = plan_ref[PLAN_ITEMS]

    def item_group(item):
        return ((item >= plan_ref[PLAN_FIRST + 1]).astype(jnp.int32)
                + (item >= plan_ref[PLAN_FIRST + 2]).astype(jnp.int32)
                + (item >= plan_ref[PLAN_FIRST + 3]).astype(jnp.int32))

    def item_pos0(item):
        g = item_group(item)
        return g * n_tok + (item - plan_ref[PLAN_FIRST + g]) * tm

    def start_gather(item, slot):
        pos0 = item_pos0(item)

        def chunk(i0):
            toks = [pos_ref[pos0 + i0 + j] for j in range(ROW_UNROLL)]
            for j, tok in enumerate(toks):
                pltpu.make_async_copy(h2x_hbm.at[pl.ds(tok * H2X_PITCH, H2X_PITCH)],
                                      gbuf.at[pl.ds(slot * g_rows + (i0 + j) * H2X_PITCH, H2X_PITCH)],
                                      gsem.at[slot]).start(priority=j % 2)
        for_row_chunks(tm >> ROW_UNROLL_LOG2, chunk)

    def wait_gather(slot):
        pltpu.make_async_copy(h2x_hbm.at[pl.ds(0, g_rows)], gbuf.at[pl.ds(slot * g_rows, g_rows)],
                              gsem.at[slot]).wait()

    def start_scatter(item, n_valid, slot):
        pos0 = item_pos0(item)

        def to_token(i0):
            dsts = [pos_ref[pos0 + i0 + j] for j in range(ROW_UNROLL)]
            for j, dst in enumerate(dsts):
                pltpu.make_async_copy(out_slab(obuf, slot, i0 + j), token_slab(dst),
                                      ssem.at[slot]).start(priority=j % 2)

        def row_to_token(i, carry):
            pltpu.make_async_copy(out_slab(obuf, slot, i), token_slab(pos_ref[pos0 + i]), ssem.at[slot]).start()
            return carry

        def row_to_dump(i, carry):
            pltpu.make_async_copy(out_slab(obuf, slot, i), out_slab(dump, slot, i), ssem.at[slot]).start()
            return carry

        full = n_valid >> ROW_UNROLL_LOG2
        for_row_chunks(full, to_token)
        lax.fori_loop(full * ROW_UNROLL, n_valid, row_to_token, 0)
        lax.fori_loop(n_valid, tm, row_to_dump, 0)

    def wait_scatter(slot):
        pltpu.make_async_copy(obuf.at[pl.ds(slot * o_rows, o_rows)], out_hbm.at[pl.ds(0, o_rows)],
                              ssem.at[slot]).wait()

    slot = k % 2

    @pl.when(k == 0)
    def _():
        start_gather(0, 0)

    @pl.when(k + 1 < n_items)
    def _():
        start_gather(k + 1, 1 - slot)

    @pl.when(k < n_items)
    def _():
        group = item_group(k)
        wait_gather(slot)
        h = _rows_from_slabs(gbuf, slot * g_rows, tm, H2X_PITCH).astype(bf16)
        w8 = gbuf[pl.ds(slot * g_rows + SLAB_ROWS, tm, stride=H2X_PITCH), :]
        cols = pl.ds(pl.multiple_of(group * GROUP_FF, GROUP_FF), GROUP_FF)
        gate = _dot(h, weg_ref[:, cols])
        a = gate * _sigmoid(gate) * _dot(h, weu_ref[:, cols])
        pieces = [a[:, j * EXPERT_FF:(j + 1) * EXPERT_FF] * w8[:, j:j + 1] for j in range(EXPERTS_PER_GROUP)]
        out = _dot(jnp.concatenate(pieces, axis=1).astype(bf16), wed_ref[group])

        @pl.when(k >= 2)
        def _():
            wait_scatter(slot)

        _rows_to_slabs(obuf, slot * o_rows, out, MOE_PITCH)
        n_valid = jnp.minimum(tm, cnt_ref[group] - (k - plan_ref[PLAN_FIRST + group]) * tm)
        start_scatter(k, n_valid, slot)

    @pl.when(k == n_steps - 1)
    def _drain():
        wait_scatter((n_items - 1) % 2)

        @pl.when(n_items >= 2)
        def _():
            wait_scatter(n_items % 2)


def _moe_call(code, cnt, h2x, weg, weu, wed):
    n_tok = h2x.shape[0] // H2X_PITCH
    tm = MOE_TILE
    n_steps = n_tok // tm + N_EXPERT_GROUPS - 1
    ff = N_EXPERT_GROUPS * GROUP_FF
    up_spec = pl.BlockSpec((D_MODEL, ff), lambda k, *_: (0, 0), pipeline_mode=pl.Buffered(1))
    down_spec = pl.BlockSpec((N_EXPERT_GROUPS, GROUP_FF, D_MODEL), lambda k, *_: (0, 0, 0),
                             pipeline_mode=pl.Buffered(1))
    grid_spec = pltpu.PrefetchScalarGridSpec(
        num_scalar_prefetch=2,
        grid=(n_steps,),
        in_specs=[pl.BlockSpec(memory_space=pl.ANY), up_spec, up_spec, down_spec],
        out_specs=pl.BlockSpec(memory_space=pl.ANY),
        scratch_shapes=[
            pltpu.SMEM((N_EXPERT_GROUPS * n_tok,), jnp.int32),
            pltpu.SMEM((PLAN_LEN,), jnp.int32),
            pltpu.VMEM((2 * tm * H2X_PITCH, LANES), f32),
            pltpu.VMEM((2 * tm * MOE_PITCH, LANES), f32),
            pltpu.VMEM((2 * tm * MOE_PITCH, LANES), f32),
            pltpu.SemaphoreType.DMA((2,)),
            pltpu.SemaphoreType.DMA((2,)),
        ],
    )
    return pl.pallas_call(
        functools.partial(_moe_kernel, n_tok=n_tok),
        grid_spec=grid_spec,
        out_shape=jax.ShapeDtypeStruct((n_tok * MOE_PITCH, LANES), f32),
        compiler_params=_cparams(("arbitrary",)),
        name="moe",
    )(code, cnt, h2x, weg, weu, wed)


def _final_kernel(x1_ref, moe_ref, mod_ref, gf_ref, o_ref):
    x2 = x1_ref[...] + mod_ref[0, 5:6, :] * _rows_from_slabs(moe_ref, 0, x1_ref.shape[0], MOE_PITCH)
    y = x2 * lax.rsqrt(jnp.mean(x2 * x2, axis=-1, keepdims=True) + NORM_EPS)
    o_ref[...] = y * gf_ref[...]


def _final_call(x1, moe, mod, gf, *, tokens_per_batch, cond_row0, per_batch_cond):
    n_tok = x1.shape[0]
    tok = pl.BlockSpec((TOKEN_TILE, D_MODEL), lambda t: (t, 0))
    slab = pl.BlockSpec((TOKEN_TILE * MOE_PITCH, LANES), lambda t: (t, 0))
    cond_idx = _cond_index_map(tokens_per_batch, cond_row0, per_batch_cond)
    return pl.pallas_call(
        _final_kernel,
        grid=(n_tok // TOKEN_TILE,),
        in_specs=[tok, slab, pl.BlockSpec((1, 6, D_MODEL), cond_idx), _resident((1, D_MODEL))],
        out_specs=tok,
        out_shape=jax.ShapeDtypeStruct((n_tok, D_MODEL), f32),
        compiler_params=_cparams(("arbitrary",)),
        name="final_norm",
    )(x1, moe, mod, gf)


def _dup_heads(w):
    a, b = w[..., :HEAD_DIM], w[..., HEAD_DIM:]
    return jnp.concatenate([a, a, b, b], axis=-1)


def _rope_tables(n):
    rows = n // GRID_W
    row = jnp.repeat(jnp.arange(rows, dtype=f32), GRID_W)
    col = jnp.tile(jnp.arange(GRID_W, dtype=f32), rows)
    inv = ROPE_THETA ** (-jnp.arange(AXIS_FREQS, dtype=f32) / AXIS_FREQS)
    ang = jnp.concatenate([row[:, None] * inv, col[:, None] * inv], axis=-1)
    cos, sin = jnp.cos(ang), jnp.sin(ang)
    cos_t = jnp.concatenate([cos, cos, cos, cos], axis=-1)
    sin_t = jnp.concatenate([-sin, sin, -sin, sin], axis=-1)
    return cos_t, sin_t


def _dft_tables(n):
    k = np.arange(n)
    m = (k[:, None] * k[None, :]) % n
    ang = 2.0 * np.pi * m / n
    return np.cos(ang), np.sin(ang)


def _fourier_tables(seq):
    cn, sn = _dft_tables(seq)
    cc, sc = _dft_tables(FOURIER_GROUP_DIM)
    eye = np.eye(N_FOURIER_GROUPS)
    bdc, bds = np.kron(eye, cc), np.kron(eye, sc)
    return tuple(jnp.asarray(t, dtype=f32).astype(bf16) for t in (bdc, bds, cn, sn))


def _dup_cache(cache, l):
    b, _, _, p, _, _ = cache.shape
    k = _dup_heads(cache[:, l, 0].reshape(b, p, N_KV * HEAD_DIM)).astype(bf16)
    v = _dup_heads(cache[:, l, 1].reshape(b, p, N_KV * HEAD_DIM)).astype(bf16)
    return k, v


def kernel(x_prompt, x_sample, cache_global_kv, cache_window_kv, c, c_ctx, w_mod, b_mod, norm1_g, w_in,
           q_norm_g, k_norm_g, sink_logit, w_branch_a, w_branch_b, w_branch_c, w_out, norm2_g,
           w_route_group, w_route_expert, w_exp_gate, w_exp_up, w_exp_down, final_norm_g):
    batch, seq, _ = x_prompt.shape
    dec_batch, dec_seq, _ = x_sample.shape

    cond = jnp.zeros((N_COND_ROWS, D_MODEL), f32).at[0].set(c_ctx).at[1:1 + dec_batch].set(c)
    mod = _mod_call(cond, w_mod, b_mod).reshape(DEPTH, N_COND_ROWS, 6, D_MODEL)

    ones_blk = jnp.asarray(np.kron(np.eye(N_HEADS), np.ones((HEAD_DIM, HEAD_DIM))), dtype=bf16)
    tri = jnp.asarray(np.triu(np.ones((TOKEN_TILE, TOKEN_TILE))), dtype=bf16)
    rope_tabs = _rope_tables(dec_seq)
    ftab_ctx = _fourier_tables(seq)
    ftab_lat = _fourier_tables(dec_seq)
    gf = final_norm_g.reshape(1, D_MODEL)

    ctx = dict(tokens_per_batch=seq, cond_row0=0, per_batch_cond=False)
    lat = dict(tokens_per_batch=dec_seq, cond_row0=1, per_batch_cond=True)
    xp = (x_prompt.reshape(batch * seq, D_MODEL),)
    xs = (x_sample.reshape(dec_batch * dec_seq, D_MODEL),)
    mod_prev = None
    gkv, wkv = [], []
    for l in range(DEPTH):
        w_ext = w_gate = w_in[l].astype(bf16)
        g1 = norm1_g[l].reshape(1, D_MODEL)
        g2 = norm2_g[l].reshape(1, D_MODEL)
        qg = jnp.tile(q_norm_g[l], N_HEADS).reshape(1, Q_W)
        kg = jnp.tile(k_norm_g[l], N_KV).reshape(1, LANES)
        wba, wbb, wbc = (w.astype(bf16) for w in (w_branch_a[l], w_branch_b[l], w_branch_c[l]))
        wout = w_out[l].astype(bf16)
        wrt = jnp.concatenate([w_route_expert[l].T, w_route_group[l].T,
                               jnp.zeros((ROUTE_ROWS - N_EXPERTS - N_EXPERT_GROUPS, D_MODEL), f32)], axis=0)
        weg, weu = w_exp_gate[l].astype(bf16), w_exp_up[l].astype(bf16)
        wed = w_exp_down[l].reshape(N_EXPERT_GROUPS, GROUP_FF, D_MODEL).astype(bf16)
        sink = sink_logit[l]

        def mix_and_moe(xin, qa, ka, va, qc, kc, vc, ub, attn, ftab, seq_len, stream):
            oa, oc = attn(qa, ka, va, qc, kc, vc)
            ob = _fourier_call(ub, *ftab, seq_len)
            x1, h2x, code, cnt = _s3_call(xin, mod_prev, oa, ob, oc, mod[l], g1, g2, w_gate, wba, wbb, wbc, wout,
                                          wrt, tri, **stream)
            counts = cnt[:, 0].astype(jnp.int32)
            moe = _moe_call(code.reshape(-1), counts, h2x, weg, weu, wed)
            return (x1, moe)

        qa, ka, va, qc, kc, vc, ub, gk, gv, wk, wv = _s1_call(
            xp, mod_prev, mod[l], g1, w_ext, ones_blk, qg, kg, None, emit_kv=True, **ctx)
        attn_ctx = lambda *qkv: _attn_ctx_call(sink, *qkv, seq)
        xp = mix_and_moe(xp, qa, ka, va, qc, kc, vc, ub, attn_ctx, ftab_ctx, seq, ctx)
        kv_shape = (batch, seq, N_KV, HEAD_DIM)
        gkv.append(jnp.stack([gk.reshape(kv_shape), gv.reshape(kv_shape)], axis=1))
        wkv.append(jnp.stack([wk.reshape(kv_shape), wv.reshape(kv_shape)], axis=1))

        qa, ka, va, qc, kc, vc, ub = _s1_call(
            xs, mod_prev, mod[l], g1, w_ext, ones_blk, qg, kg, rope_tabs, emit_kv=False, **lat)
        kxa, vxa = _dup_cache(cache_global_kv, l)
        kxc, vxc = _dup_cache(cache_window_kv, l)
        attn_lat = lambda qa, ka, va, qc, kc, vc: _attn_lat_call(
            sink, qa, ka, va, kxa, vxa, qc, kc, vc, kxc, vxc, dec_seq)
        xs = mix_and_moe(xs, qa, ka, va, qc, kc, vc, ub, attn_lat, ftab_lat, dec_seq, lat)
        mod_prev = mod[l]

    y_prompt = _final_call(*xp, mod_prev, gf, **ctx).reshape(batch, seq, D_MODEL)
    y_sample = _final_call(*xs, mod_prev, gf, **lat).reshape(dec_batch, dec_seq, D_MODEL)
    return (y_prompt, y_sample, jnp.stack(gkv, axis=1), jnp.stack(wkv, axis=1))
```

```python
import functools

import jax
import jax.numpy as jnp
import numpy as np
from jax import lax
from jax.experimental import pallas as pl
from jax.experimental.pallas import tpu as pltpu

f32 = jnp.float32
bf16 = jnp.bfloat16

D_MODEL = 1024
DEPTH = 2
GRID_W = 64
HEAD_DIM = 64
HALF = HEAD_DIM // 2
N_HEADS = 8
N_KV = 2
WINDOW = 128
N_FOURIER_GROUPS = 4
FOURIER_GROUP_DIM = 128
FOURIER_DIM = N_FOURIER_GROUPS * FOURIER_GROUP_DIM
ROPE_THETA = 10000.0
AXIS_FREQS = HEAD_DIM // 4
N_EXPERT_GROUPS = 4
EXPERTS_PER_GROUP = 8
N_EXPERTS = N_EXPERT_GROUPS * EXPERTS_PER_GROUP
EXPERT_FF = 128
NORM_EPS = 1e-6
NEG_INF = -1e30

LANES = 128
Q_W = N_HEADS * HEAD_DIM
KV_DUP_W = 2 * N_KV * HEAD_DIM
C_QA, C_KA, C_VA, C_QC, C_KC, C_VC, C_UB, C_GATES, C_END = 0, 512, 640, 768, 1280, 1408, 1536, 2048, 5120
N_COND_ROWS = 16
VMEM_LIMIT = 56 * 1024 * 1024


def _cparams(sem):
    return pltpu.CompilerParams(dimension_semantics=sem, vmem_limit_bytes=VMEM_LIMIT)


def _resident(shape):
    nd = len(shape)
    return pl.BlockSpec(shape, lambda *_: (0,) * nd, pipeline_mode=pl.Buffered(1))


def _sigmoid(x):
    return 0.5 * jnp.tanh(0.5 * x) + 0.5


def _rms_mod(x, g, scale, shift):
    y = x * lax.rsqrt(jnp.mean(x * x, axis=-1, keepdims=True) + NORM_EPS)
    return (y * g) * (1.0 + scale) + shift


def _dot(a, b):
    return jnp.dot(a, b, preferred_element_type=f32)


def _dot_t(a, b):
    return lax.dot_general(a, b, (((1,), (1,)), ((), ())), preferred_element_type=f32)


def _split_bf16(x):
    hi = x.astype(bf16)
    lo = (x - hi.astype(f32)).astype(bf16)
    return hi, lo


def _mod_kernel(cond_ref, w_ref, b_ref, o_ref):
    c = cond_ref[...]
    s = (c * _sigmoid(c)).astype(bf16)
    o_ref[0] = _dot(s, w_ref[0].astype(bf16)) + b_ref[0]


def _mod_call(cond, w_mod, b_mod):
    n_tiles = 6 * D_MODEL // 1024
    return pl.pallas_call(
        _mod_kernel,
        grid=(DEPTH, n_tiles),
        in_specs=[
            pl.BlockSpec((N_COND_ROWS, D_MODEL), lambda l, j: (0, 0)),
            pl.BlockSpec((1, D_MODEL, 1024), lambda l, j: (l, 0, j)),
            pl.BlockSpec((1, 1, 1024), lambda l, j: (l, 0, j)),
        ],
        out_specs=pl.BlockSpec((1, N_COND_ROWS, 1024), lambda l, j: (l, 0, j)),
        out_shape=jax.ShapeDtypeStruct((DEPTH, N_COND_ROWS, 6 * D_MODEL), f32),
        compiler_params=_cparams(("arbitrary", "arbitrary")),
        name="mod",
    )(cond, w_mod, b_mod.reshape(DEPTH, 1, 6 * D_MODEL))


def _head_norm(z, g, ones_blk):
    hi, lo = _split_bf16(z * z)
    ss = _dot(hi, ones_blk) + _dot(lo, ones_blk)
    return z * lax.rsqrt(ss * (1.0 / HEAD_DIM) + NORM_EPS) * g


def _rope(z, cos, sin_signed, first_half):
    outs = []
    for j in range(z.shape[1] // LANES):
        zj = z[:, j * LANES:(j + 1) * LANES]
        swapped = jnp.where(first_half, pltpu.roll(zj, LANES - HALF, 1), pltpu.roll(zj, HALF, 1))
        outs.append(zj * cos + swapped * sin_signed)
    return jnp.concatenate(outs, axis=1)


def _layer_input(x_refs, r0, n):
    rows = pl.ds(r0, n)
    if len(x_refs) == 1:
        return x_refs[0][rows, :]
    x1_ref, moe_ref, modp_ref = x_refs
    return x1_ref[rows, :] + modp_ref[0, 5:6, :] * _rows_from_slabs(moe_ref, r0 * SLAB_ROWS, n, SLAB_ROWS)


S1_SUB_TILES = 2
S3_SUB_TILES = 1


SLAB_ROWS = D_MODEL // LANES


def _rows_from_slabs(ref, start, n_tok, pitch):
    return jnp.concatenate([ref[pl.ds(start + c, n_tok, stride=pitch), :] for c in range(SLAB_ROWS)], axis=1)


def _rows_to_slabs(ref, start, x, pitch):
    for c in range(SLAB_ROWS):
        ref[pl.ds(start + c, x.shape[0], stride=pitch), :] = x[:, c * LANES:(c + 1) * LANES]


def _s1_kernel(*refs, rope, emit_kv, n_x):
    x_refs, refs = refs[:n_x], refs[n_x:]
    mod_ref, g1_ref, w_ref, ones_ref, qg_ref, kg_ref = refs[:6]
    pos = 6
    if rope:
        cos_ref, sin_ref = refs[pos:pos + 2]
        pos += 2
    qa_ref, ka_ref, va_ref, qc_ref, kc_ref, vc_ref, ub_ref = refs[pos:pos + 7]
    pos += 7
    if emit_kv:
        gkv_ref, wkv_ref = refs[pos:pos + 2]

    tm = x_refs[0].shape[0] // S1_SUB_TILES
    lane = lax.broadcasted_iota(jnp.int32, (tm, LANES), 1)
    low_lanes = lane < HEAD_DIM
    first_half = (lane & (HEAD_DIM - 1)) < HALF
    q_scale = HEAD_DIM ** -0.5

    def dup(z):
        r = pltpu.roll(z, HEAD_DIM, 1)
        return jnp.concatenate([jnp.where(low_lanes, z, r), jnp.where(low_lanes, r, z)], axis=1).astype(bf16)

    for sub in range(S1_SUB_TILES):
        rows = pl.ds(sub * tm, tm)
        x = _layer_input(x_refs, sub * tm, tm)
        h = _rms_mod(x, g1_ref[...], mod_ref[0, 1:2, :], mod_ref[0, 0:1, :]).astype(bf16)
        proj = lambda lo, hi: _dot(h, w_ref[:, lo:hi])
        if rope:
            cos = cos_ref[rows, :]
            sin_signed = sin_ref[rows, :]
            rot = lambda z: _rope(z, cos, sin_signed, first_half)
        else:
            rot = lambda z: z

        qa = rot(_head_norm(proj(C_QA, C_KA), qg_ref[...], ones_ref[...]))
        qa_ref[rows, :] = (qa * q_scale).astype(bf16)
        kva = proj(C_KA, C_QC)
        ka_n = _head_norm(kva[:, 0:LANES], kg_ref[...], ones_ref[0:LANES, 0:LANES])
        ka_ref[rows, :] = dup(rot(ka_n))
        va = kva[:, LANES:2 * LANES]
        va_ref[rows, :] = dup(va)
        qc_ref[rows, :] = (rot(proj(C_QC, C_KC)) * q_scale).astype(bf16)
        kvc = proj(C_KC, C_UB)
        kc = kvc[:, 0:LANES]
        kc_ref[rows, :] = dup(rot(kc))
        vc = kvc[:, LANES:2 * LANES]
        vc_ref[rows, :] = dup(vc)
        ub_ref[rows, :] = proj(C_UB, C_GATES).astype(bf16)
        if emit_kv:
            gkv_ref[sub, 0] = ka_n
            gkv_ref[sub, 1] = va
            wkv_ref[sub, 0] = kc
            wkv_ref[sub, 1] = vc


TOKEN_TILE = 512


def _cond_index_map(tokens_per_batch, cond_row0, per_batch_cond):
    tiles_per_batch = max(tokens_per_batch // TOKEN_TILE, 1)
    if per_batch_cond:
        assert tokens_per_batch % TOKEN_TILE == 0
        return lambda t: (cond_row0 + t // tiles_per_batch, 0, 0)
    return lambda t: (cond_row0, 0, 0)


def _layer_input_specs(xin, mod_prev, cond_idx):
    tok = pl.BlockSpec((TOKEN_TILE, D_MODEL), lambda t: (t, 0))
    if len(xin) == 1:
        return [tok], list(xin)
    slab = pl.BlockSpec((TOKEN_TILE * SLAB_ROWS, LANES), lambda t: (t, 0))
    return [tok, slab, pl.BlockSpec((1, 6, D_MODEL), cond_idx)], [xin[0], xin[1], mod_prev]


def _s1_call(xin, mod_prev, mod, g1, w_ext, ones_blk, qg, kg, rope_tabs, *, tokens_per_batch, cond_row0,
             per_batch_cond, emit_kv):
    n_tok = xin[0].shape[0]
    tm = TOKEN_TILE
    rope = rope_tabs is not None
    tiles_per_batch = max(tokens_per_batch // tm, 1)
    cond_idx = _cond_index_map(tokens_per_batch, cond_row0, per_batch_cond)

    tok = lambda w: pl.BlockSpec((tm, w), lambda t: (t, 0))
    x_specs, x_args = _layer_input_specs(xin, mod_prev, cond_idx)
    in_specs = x_specs + [
        pl.BlockSpec((1, 6, D_MODEL), cond_idx),
        _resident((1, D_MODEL)),
        _resident((D_MODEL, C_GATES)),
        _resident((Q_W, Q_W)),
        _resident((1, Q_W)),
        _resident((1, LANES)),
    ]
    args = x_args + [mod, g1, w_ext, ones_blk, qg, kg]
    if rope:
        in_specs += [pl.BlockSpec((tm, LANES), lambda t: (t % tiles_per_batch, 0))] * 2
        args += list(rope_tabs)
    widths = [Q_W, KV_DUP_W, KV_DUP_W, Q_W, KV_DUP_W, KV_DUP_W, FOURIER_DIM]
    out_specs = [tok(w) for w in widths]
    out_shape = [jax.ShapeDtypeStruct((n_tok, w), bf16) for w in widths]
    if emit_kv:
        assert tokens_per_batch * S1_SUB_TILES == tm
        n_batch = n_tok // tokens_per_batch
        out_specs += [pl.BlockSpec((S1_SUB_TILES, 2, tokens_per_batch, LANES), lambda t: (t, 0, 0, 0))] * 2
        out_shape += [jax.ShapeDtypeStruct((n_batch, 2, tokens_per_batch, LANES), f32)] * 2
    return pl.pallas_call(
        functools.partial(_s1_kernel, rope=rope, emit_kv=emit_kv, n_x=len(x_args)),
        grid=(n_tok // tm,),
        in_specs=in_specs,
        out_specs=out_specs,
        out_shape=out_shape,
        compiler_params=_cparams(("arbitrary",)),
        name="s1_rope" if rope else "s1_ctx",
    )(*args)


def _gqa_group(q2, segs, sink_pair_lo, sink_pair_hi):
    qb = q2.shape[0]
    qs = jnp.concatenate([q2[:, 0:LANES], q2[:, LANES:2 * LANES]], axis=0)
    lane = lax.broadcasted_iota(jnp.int32, qs.shape, 1)
    low = lane < HEAD_DIM
    zero = jnp.zeros_like(qs)
    row = lax.broadcasted_iota(jnp.int32, (2 * qb, 1), 0)

    def half(qh, sink_pair):
        scores = []
        for k, _, ok in segs:
            s = _dot_t(qh, k)
            if ok is not None:
                s = jnp.where(ok, s, NEG_INF)
            scores.append(s)
        m = scores[0].max(axis=-1, keepdims=True)
        for s in scores[1:]:
            m = jnp.maximum(m, s.max(axis=-1, keepdims=True))
        if sink_pair is not None:
            sink = jnp.where(row < qb, sink_pair[0], sink_pair[1])
            m = jnp.maximum(m, sink)
            denom = jnp.exp(sink - m)
        else:
            denom = jnp.zeros_like(m)
        acc = None
        for s, (_, v, _) in zip(scores, segs):
            p = jnp.exp(s - m)
            denom = denom + p.sum(axis=-1, keepdims=True)
            pv = _dot(p.astype(bf16), v)
            acc = pv if acc is None else acc + pv
        return acc / denom

    o_lo = half(jnp.where(low, qs, zero), sink_pair_lo)
    o_hi = half(jnp.where(low, zero, qs), sink_pair_hi)
    o = jnp.where(low, o_lo, o_hi)
    return jnp.concatenate([o[0:qb], o[qb:2 * qb]], axis=1)


def _sink_pairs(sink_ref, kv):
    h = 4 * kv
    return (sink_ref[h], sink_ref[h + 2]), (sink_ref[h + 1], sink_ref[h + 3])


def _attn_ctx_kernel(sink_ref, qa_ref, ka_ref, va_ref, qc_ref, kc_ref, vc_ref, oa_ref, oc_ref):
    for kv in range(N_KV):
        cq = slice(2 * LANES * kv, 2 * LANES * (kv + 1))
        ck = slice(LANES * kv, LANES * (kv + 1))
        oa = _gqa_group(qa_ref[:, cq], [(ka_ref[:, ck], va_ref[:, ck], None)], None, None)
        oa_ref[:, cq] = oa.astype(bf16)
        lo, hi = _sink_pairs(sink_ref, kv)
        oc = _gqa_group(qc_ref[:, cq], [(kc_ref[:, ck], vc_ref[:, ck], None)], lo, hi)
        oc_ref[:, cq] = oc.astype(bf16)


def _attn_ctx_call(sink, qa, ka, va, qc, kc, vc, seq):
    n_tok = qa.shape[0]
    tok = lambda w: pl.BlockSpec((seq, w), lambda b: (b, 0))
    return pl.pallas_call(
        _attn_ctx_kernel,
        grid=(n_tok // seq,),
        in_specs=[pl.BlockSpec(memory_space=pltpu.SMEM),
                  tok(Q_W), tok(KV_DUP_W), tok(KV_DUP_W), tok(Q_W), tok(KV_DUP_W), tok(KV_DUP_W)],
        out_specs=[tok(Q_W), tok(Q_W)],
        out_shape=[jax.ShapeDtypeStruct((n_tok, Q_W), bf16)] * 2,
        compiler_params=_cparams(("arbitrary",)),
        name="attn_ctx",
    )(sink, qa, ka, va, qc, kc, vc)


QB = 256
WIN_QB = 128
WIN_SPAN = WIN_QB + 2 * WINDOW


def _attn_lat_kernel(sink_ref, qa_ref, ka_ref, va_ref, kxa_ref, vxa_ref,
                     qc_ref, kc_ref, vc_ref, kxc_ref, vxc_ref, oa_ref, oc_ref, *, seq):
    i = pl.program_id(1)
    for kv in range(N_KV):
        cq = slice(2 * LANES * kv, 2 * LANES * (kv + 1))
        ck = slice(LANES * kv, LANES * (kv + 1))
        segs = [(ka_ref[:, ck], va_ref[:, ck], None), (kxa_ref[0, :, ck], vxa_ref[0, :, ck], None)]
        oa_ref[:, cq] = _gqa_group(qa_ref[:, cq], segs, None, None).astype(bf16)

        lo, hi = _sink_pairs(sink_ref, kv)
        for sub in range(QB // WIN_QB):
            blk = i * (QB // WIN_QB) + sub
            start = jnp.clip((blk - 1) * WIN_QB, 0, seq - WIN_SPAN)
            start = pl.multiple_of(start, WIN_QB)
            r = lax.broadcasted_iota(jnp.int32, (2 * WIN_QB, WIN_SPAN), 0)
            c = lax.broadcasted_iota(jnp.int32, (2 * WIN_QB, WIN_SPAN), 1)
            qpos = blk * WIN_QB + (r & (WIN_QB - 1))
            ok = jnp.abs(qpos - (start + c)) <= WINDOW
            rows = slice(sub * WIN_QB, (sub + 1) * WIN_QB)
            segs = [(kc_ref[pl.ds(start, WIN_SPAN), ck], vc_ref[pl.ds(start, WIN_SPAN), ck], ok),
                    (kxc_ref[0, :, ck], vxc_ref[0, :, ck], None)]
            oc_ref[rows, cq] = _gqa_group(qc_ref[rows, cq], segs, lo, hi).astype(bf16)


def _attn_lat_call(sink, qa, ka, va, kxa, vxa, qc, kc, vc, kxc, vxc, seq):
    n_tok = qa.shape[0]
    nq = seq // QB
    past = kxa.shape[1]
    qspec = pl.BlockSpec((QB, Q_W), lambda b, i: (b * nq + i, 0))
    kspec = pl.BlockSpec((seq, KV_DUP_W), lambda b, i: (b, 0))
    xspec = pl.BlockSpec((1, past, KV_DUP_W), lambda b, i: (b, 0, 0))
    return pl.pallas_call(
        functools.partial(_attn_lat_kernel, seq=seq),
        grid=(n_tok // seq, nq),
        in_specs=[pl.BlockSpec(memory_space=pltpu.SMEM),
                  qspec, kspec, kspec, xspec, xspec, qspec, kspec, kspec, xspec, xspec],
        out_specs=[qspec, qspec],
        out_shape=[jax.ShapeDtypeStruct((n_tok, Q_W), bf16)] * 2,
        compiler_params=_cparams(("arbitrary", "arbitrary")),
        name="attn_lat",
    )(sink, qa, ka, va, kxa, vxa, qc, kc, vc, kxc, vxc)


def _fourier_kernel(u_ref, bdc_ref, bds_ref, cn_ref, sn_ref, o_ref, *, scale):
    u = u_ref[...]
    uc = _dot(u, bdc_ref[...]).astype(bf16)
    us = _dot(u, bds_ref[...]).astype(bf16)
    o = _dot(cn_ref[...], uc) - _dot(sn_ref[...], us)
    o_ref[...] = (o * scale).astype(bf16)


def _fourier_call(ub, bdc, bds, cn, sn, seq):
    n_tok = ub.shape[0]
    tok = pl.BlockSpec((seq, FOURIER_DIM), lambda b: (b, 0))
    scale = float(1.0 / np.sqrt(seq * FOURIER_GROUP_DIM))
    return pl.pallas_call(
        functools.partial(_fourier_kernel, scale=scale),
        grid=(n_tok // seq,),
        in_specs=[tok, _resident((FOURIER_DIM, FOURIER_DIM)), _resident((FOURIER_DIM, FOURIER_DIM)),
                  _resident((seq, seq)), _resident((seq, seq))],
        out_specs=tok,
        out_shape=jax.ShapeDtypeStruct((n_tok, FOURIER_DIM), bf16),
        compiler_params=_cparams(("arbitrary",)),
        name=f"fourier_{seq}",
    )(ub, bdc, bds, cn, sn)


ROUTE_ROWS = LANES
H2X_PITCH = SLAB_ROWS + 1
MOE_PITCH = SLAB_ROWS


def _route(lt, tri_ref, carry_ref, n_tok):
    tm = lt.shape[1]
    row = lax.broadcasted_iota(jnp.int32, (EXPERTS_PER_GROUP, tm), 0).astype(f32)
    neg = jnp.float32(-jnp.inf)
    none = jnp.float32(EXPERTS_PER_GROUP)
    gl = jnp.where(row < N_EXPERT_GROUPS, lt[N_EXPERTS:N_EXPERTS + EXPERTS_PER_GROUP], neg)
    gmax = gl.max(axis=0, keepdims=True)
    gidx = jnp.where(gl == gmax, row, none).min(axis=0, keepdims=True)
    gw = 1.0 / jnp.exp(gl - gmax).sum(axis=0, keepdims=True)
    lg = lt[0:EXPERTS_PER_GROUP]
    for g in range(1, N_EXPERT_GROUPS):
        lg = jnp.where(gidx == g, lt[g * EXPERTS_PER_GROUP:(g + 1) * EXPERTS_PER_GROUP], lg)
    m1 = lg.max(axis=0, keepdims=True)
    i1 = jnp.where(lg == m1, row, none).min(axis=0, keepdims=True)
    l2 = jnp.where(row == i1, neg, lg)
    m2 = l2.max(axis=0, keepdims=True)
    i2 = jnp.where(l2 == m2, row, none).min(axis=0, keepdims=True)
    e = jnp.exp(m2 - m1)
    w8 = jnp.where(row == i1, gw / (1.0 + e), jnp.where(row == i2, gw * e / (1.0 + e), 0.0))

    onehot = (row == gidx).astype(f32)
    prefix = _dot(onehot.astype(bf16), tri_ref[0:tm, 0:tm])
    carry = carry_ref[...]
    rank = (onehot * (prefix - 1.0 + carry[:, 0:1])).sum(axis=0, keepdims=True)
    carry_ref[...] = carry + onehot.sum(axis=1, keepdims=True)
    dest = gidx.astype(jnp.int32) * n_tok + rank.astype(jnp.int32)
    return w8, dest


def _s3_kernel(*refs, n_x, n_tok):
    x_refs, refs = refs[:n_x], refs[n_x:]
    (oa_ref, ob_ref, oc_ref, mod_ref, g1_ref, g2_ref, wga_ref, wgb_ref, wgc_ref, wba_ref, wbb_ref, wbc_ref,
     wout_ref, wrt_ref, tri_ref, x1_ref, h2x_ref, code_ref, cnt_ref, carry_ref) = refs
    wg_refs = (wga_ref, wgb_ref, wgc_ref)

    @pl.when(pl.program_id(0) == 0)
    def _():
        carry_ref[...] = jnp.zeros_like(carry_ref)

    tm = x1_ref.shape[0] // S3_SUB_TILES
    wh, wl = _split_bf16(wrt_ref[...])
    for sub in range(S3_SUB_TILES):
        r0 = sub * tm
        rows = pl.ds(r0, tm)
        x = _layer_input(x_refs, r0, tm)
        h = _rms_mod(x, g1_ref[...], mod_ref[0, 1:2, :], mod_ref[0, 0:1, :]).astype(bf16)
        merged = None
        for j, (o_ref, wb_ref) in enumerate(((oa_ref, wba_ref), (ob_ref, wbb_ref), (oc_ref, wbc_ref))):
            gate = _sigmoid(_dot(h, wg_refs[j][...]))
            term = gate * _dot(o_ref[rows, :], wb_ref[...])
            merged = term if merged is None else merged + term
        mix = _dot(merged.astype(bf16), wout_ref[...])
        x1 = x + mod_ref[0, 2:3, :] * mix
        x1_ref[rows, :] = x1
        h2 = _rms_mod(x1, g2_ref[...], mod_ref[0, 4:5, :], mod_ref[0, 3:4, :])
        _rows_to_slabs(h2x_ref, r0 * H2X_PITCH, h2, H2X_PITCH)
        th, tl = _split_bf16(h2)
        lt = _dot_t(wh, th) + (_dot_t(wh, tl) + _dot_t(wl, th))
        w8, code = _route(lt, tri_ref, carry_ref, n_tok)
        code_ref[:, rows] = code
        w_rows = jnp.concatenate([w8, jnp.zeros((LANES - EXPERTS_PER_GROUP, tm), f32)], axis=0)
        h2x_ref[pl.ds(r0 * H2X_PITCH + SLAB_ROWS, tm, stride=H2X_PITCH), :] = w_rows.T
    cnt_ref[...] = carry_ref[...]


def _s3_call(xin, mod_prev, oa, ob, oc, mod, g1, g2, wg, wba, wbb, wbc, wout, wrt, tri, *, tokens_per_batch,
             cond_row0, per_batch_cond):
    n_tok = xin[0].shape[0]
    tm = TOKEN_TILE
    cond_idx = _cond_index_map(tokens_per_batch, cond_row0, per_batch_cond)
    tok = lambda w: pl.BlockSpec((tm, w), lambda t: (t, 0))
    x_specs, x_args = _layer_input_specs(xin, mod_prev, cond_idx)
    gate_cols = lambda j: pl.BlockSpec((D_MODEL, D_MODEL), lambda t: (0, C_GATES // D_MODEL + j),
                                       pipeline_mode=pl.Buffered(1))
    return pl.pallas_call(
        functools.partial(_s3_kernel, n_x=len(x_args), n_tok=n_tok),
        grid=(n_tok // tm,),
        in_specs=x_specs + [
            tok(Q_W), tok(FOURIER_DIM), tok(Q_W),
            pl.BlockSpec((1, 6, D_MODEL), cond_idx),
            _resident((1, D_MODEL)), _resident((1, D_MODEL)),
            gate_cols(0), gate_cols(1), gate_cols(2),
            _resident((Q_W, D_MODEL)), _resident((FOURIER_DIM, D_MODEL)), _resident((Q_W, D_MODEL)),
            _resident((D_MODEL, D_MODEL)), _resident((ROUTE_ROWS, D_MODEL)), _resident((tm, tm))],
        out_specs=[tok(D_MODEL), pl.BlockSpec((tm * H2X_PITCH, LANES), lambda t: (t, 0)),
                   pl.BlockSpec((1, tm), lambda t: (0, t)),
                   pl.BlockSpec((EXPERTS_PER_GROUP, LANES), lambda t: (0, 0))],
        out_shape=[jax.ShapeDtypeStruct((n_tok, D_MODEL), f32),
                   jax.ShapeDtypeStruct((n_tok * H2X_PITCH, LANES), f32),
                   jax.ShapeDtypeStruct((1, n_tok), jnp.int32),
                   jax.ShapeDtypeStruct((EXPERTS_PER_GROUP, LANES), f32)],
        scratch_shapes=[pltpu.VMEM((EXPERTS_PER_GROUP, LANES), f32)],
        compiler_params=_cparams(("arbitrary",)),
        name="s3",
    )(*x_args, oa, ob, oc, mod, g1, g2, wg, wg, wg, wba, wbb, wbc, wout, wrt, tri)


MOE_TILE_LOG2 = 8
MOE_TILE = 1 << MOE_TILE_LOG2
ROW_UNROLL_LOG2 = 3
ROW_UNROLL = 1 << ROW_UNROLL_LOG2
GROUP_FF = EXPERTS_PER_GROUP * EXPERT_FF
PLAN_FIRST, PLAN_ITEMS, PLAN_LEN = 0, N_EXPERT_GROUPS, N_EXPERT_GROUPS + 1


def _moe_kernel(code_ref, cnt_ref, h2x_hbm, weg_ref, weu_ref, wed_ref, out_hbm,
                pos_ref, plan_ref, gbuf, obuf, dump, gsem, ssem, *, n_tok):
    tm = MOE_TILE
    k = pl.program_id(0)
    n_steps = pl.num_programs(0)
    g_rows = tm * H2X_PITCH
    o_rows = tm * MOE_PITCH

    def out_slab(buf, slot, i):
        return buf.at[pl.ds(pl.multiple_of(slot * o_rows + i * MOE_PITCH, MOE_PITCH), MOE_PITCH)]

    def token_slab(tok):
        return out_hbm.at[pl.ds(pl.multiple_of(tok * MOE_PITCH, MOE_PITCH), MOE_PITCH)]

    def for_row_chunks(n_chunks, body):
        def chunk(c, carry):
            body(c * ROW_UNROLL)
            return carry
        lax.fori_loop(0, n_chunks, chunk, 0)

    @pl.when(k == 0)
    def _plan():
        first = jnp.int32(0)
        for g in range(N_EXPERT_GROUPS):
            c = cnt_ref[g]
            n_tiles = (c + (tm - 1)) >> MOE_TILE_LOG2
            plan_ref[PLAN_FIRST + g] = first

            def pad(p, carry):
                pos_ref[p] = 0
                return carry

            lax.fori_loop(g * n_tok + c, g * n_tok + n_tiles * tm, pad, 0)
            first = first + n_tiles
        plan_ref[PLAN_ITEMS] = first

        def place(t0):
            dests = [code_ref[t0 + j] for j in range(ROW_UNROLL)]
            for j, dest in enumerate(dests):
                pos_ref[dest] = t0 + j

        for_row_chunks(n_tok >> ROW_UNROLL_LOG2, place)

    n_items = plan_ref[PLAN_ITEMS]

    def item_group(item):
        return ((item >= plan_ref[PLAN_FIRST + 1]).astype(jnp.int32)
                + (item >= plan_ref[PLAN_FIRST + 2]).astype(jnp.int32)
                + (item >= plan_ref[PLAN_FIRST + 3]).astype(jnp.int32))

    def item_pos0(item):
        g = item_group(item)
        return g * n_tok + (item - plan_ref[PLAN_FIRST + g]) * tm

    def start_gather(item, slot):
        pos0 = item_pos0(item)

        def chunk(i0):
            toks = [pos_ref[pos0 + i0 + j] for j in range(ROW_UNROLL)]
            for j, tok in enumerate(toks):
                pltpu.make_async_copy(h2x_hbm.at[pl.ds(tok * H2X_PITCH, H2X_PITCH)],
                                      gbuf.at[pl.ds(slot * g_rows + (i0 + j) * H2X_PITCH, H2X_PITCH)],
                                      gsem.at[slot]).start(priority=j % 2)
        for_row_chunks(tm >> ROW_UNROLL_LOG2, chunk)

    def wait_gather(slot):
        pltpu.make_async_copy(h2x_hbm.at[pl.ds(0, g_rows)], gbuf.at[pl.ds(slot * g_rows, g_rows)],
                              gsem.at[slot]).wait()

    def start_scatter(item, n_valid, slot):
        pos0 = item_pos0(item)

        def to_token(i0):
            dsts = [pos_ref[pos0 + i0 + j] for j in range(ROW_UNROLL)]
            for j, dst in enumerate(dsts):
                pltpu.make_async_copy(out_slab(obuf, slot, i0 + j), token_slab(dst),
                                      ssem.at[slot]).start(priority=j % 2)

        def row_to_token(i, carry):
            pltpu.make_async_copy(out_slab(obuf, slot, i), token_slab(pos_ref[pos0 + i]), ssem.at[slot]).start()
            return carry

        def row_to_dump(i, carry):
            pltpu.make_async_copy(out_slab(obuf, slot, i), out_slab(dump, slot, i), ssem.at[slot]).start()
            return carry

        full = n_valid >> ROW_UNROLL_LOG2
        for_row_chunks(full, to_token)
        lax.fori_loop(full * ROW_UNROLL, n_valid, row_to_token, 0)
        lax.fori_loop(n_valid, tm, row_to_dump, 0)

    def wait_scatter(slot):
        pltpu.make_async_copy(obuf.at[pl.ds(slot * o_rows, o_rows)], out_hbm.at[pl.ds(0, o_rows)],
                              ssem.at[slot]).wait()

    slot = k % 2

    @pl.when(k == 0)
    def _():
        start_gather(0, 0)

    @pl.when(k + 1 < n_items)
    def _():
        start_gather(k + 1, 1 - slot)

    @pl.when(k < n_items)
    def _():
        group = item_group(k)
        wait_gather(slot)
        h = _rows_from_slabs(gbuf, slot * g_rows, tm, H2X_PITCH).astype(bf16)
        w8 = gbuf[pl.ds(slot * g_rows + SLAB_ROWS, tm, stride=H2X_PITCH), :]
        cols = pl.ds(pl.multiple_of(group * GROUP_FF, GROUP_FF), GROUP_FF)
        gate = _dot(h, weg_ref[:, cols])
        a = gate * _sigmoid(gate) * _dot(h, weu_ref[:, cols])
        pieces = [a[:, j * EXPERT_FF:(j + 1) * EXPERT_FF] * w8[:, j:j + 1] for j in range(EXPERTS_PER_GROUP)]
        out = _dot(jnp.concatenate(pieces, axis=1).astype(bf16), wed_ref[group])

        @pl.when(k >= 2)
        def _():
            wait_scatter(slot)

        _rows_to_slabs(obuf, slot * o_rows, out, MOE_PITCH)
        n_valid = jnp.minimum(tm, cnt_ref[group] - (k - plan_ref[PLAN_FIRST + group]) * tm)
        start_scatter(k, n_valid, slot)

    @pl.when(k == n_steps - 1)
    def _drain():
        wait_scatter((n_items - 1) % 2)

        @pl.when(n_items >= 2)
        def _():
            wait_scatter(n_items % 2)


def _moe_call(code, cnt, h2x, weg, weu, wed):
    n_tok = h2x.shape[0] // H2X_PITCH
    tm = MOE_TILE
    n_steps = n_tok // tm + N_EXPERT_GROUPS - 1
    ff = N_EXPERT_GROUPS * GROUP_FF
    up_spec = pl.BlockSpec((D_MODEL, ff), lambda k, *_: (0, 0), pipeline_mode=pl.Buffered(1))
    down_spec = pl.BlockSpec((N_EXPERT_GROUPS, GROUP_FF, D_MODEL), lambda k, *_: (0, 0, 0),
                             pipeline_mode=pl.Buffered(1))
    grid_spec = pltpu.PrefetchScalarGridSpec(
        num_scalar_prefetch=2,
        grid=(n_steps,),
        in_specs=[pl.BlockSpec(memory_space=pl.ANY), up_spec, up_spec, down_spec],
        out_specs=pl.BlockSpec(memory_space=pl.ANY),
        scratch_shapes=[
            pltpu.SMEM((N_EXPERT_GROUPS * n_tok,), jnp.int32),
            pltpu.SMEM((PLAN_LEN,), jnp.int32),
            pltpu.VMEM((2 * tm * H2X_PITCH, LANES), f32),
            pltpu.VMEM((2 * tm * MOE_PITCH, LANES), f32),
            pltpu.VMEM((2 * tm * MOE_PITCH, LANES), f32),
            pltpu.SemaphoreType.DMA((2,)),
            pltpu.SemaphoreType.DMA((2,)),
        ],
    )
    return pl.pallas_call(
        functools.partial(_moe_kernel, n_tok=n_tok),
        grid_spec=grid_spec,
        out_shape=jax.ShapeDtypeStruct((n_tok * MOE_PITCH, LANES), f32),
        compiler_params=_cparams(("arbitrary",)),
        name="moe",
    )(code, cnt, h2x, weg, weu, wed)


def _final_kernel(x1_ref, moe_ref, mod_ref, gf_ref, o_ref):
    x2 = x1_ref[...] + mod_ref[0, 5:6, :] * _rows_from_slabs(moe_ref, 0, x1_ref.shape[0], MOE_PITCH)
    y = x2 * lax.rsqrt(jnp.mean(x2 * x2, axis=-1, keepdims=True) + NORM_EPS)
    o_ref[...] = y * gf_ref[...]


def _final_call(x1, moe, mod, gf, *, tokens_per_batch, cond_row0, per_batch_cond):
    n_tok = x1.shape[0]
    tok = pl.BlockSpec((TOKEN_TILE, D_MODEL), lambda t: (t, 0))
    slab = pl.BlockSpec((TOKEN_TILE * MOE_PITCH, LANES), lambda t: (t, 0))
    cond_idx = _cond_index_map(tokens_per_batch, cond_row0, per_batch_cond)
    return pl.pallas_call(
        _final_kernel,
        grid=(n_tok // TOKEN_TILE,),
        in_specs=[tok, slab, pl.BlockSpec((1, 6, D_MODEL), cond_idx), _resident((1, D_MODEL))],
        out_specs=tok,
        out_shape=jax.ShapeDtypeStruct((n_tok, D_MODEL), f32),
        compiler_params=_cparams(("arbitrary",)),
        name="final_norm",
    )(x1, moe, mod, gf)


def _rope_tables(n):
    rows = n // GRID_W
    row = jnp.repeat(jnp.arange(rows, dtype=f32), GRID_W)
    col = jnp.tile(jnp.arange(GRID_W, dtype=f32), rows)
    inv = ROPE_THETA ** (-jnp.arange(AXIS_FREQS, dtype=f32) / AXIS_FREQS)
    ang = jnp.concatenate([row[:, None] * inv, col[:, None] * inv], axis=-1)
    cos, sin = jnp.cos(ang), jnp.sin(ang)
    cos_t = jnp.concatenate([cos, cos, cos, cos], axis=-1)
    sin_t = jnp.concatenate([-sin, sin, -sin, sin], axis=-1)
    return cos_t, sin_t


def _dft_tables(n):
    k = np.arange(n)
    m = (k[:, None] * k[None, :]) % n
    ang = 2.0 * np.pi * m / n
    return np.cos(ang), np.sin(ang)


def _fourier_tables(seq):
    cn, sn = _dft_tables(seq)
    cc, sc = _dft_tables(FOURIER_GROUP_DIM)
    eye = np.eye(N_FOURIER_GROUPS)
    bdc, bds = np.kron(eye, cc), np.kron(eye, sc)
    return tuple(jnp.asarray(t, dtype=f32).astype(bf16) for t in (bdc, bds, cn, sn))


def _dup_cache(cache, l):
    b, _, _, p, _, _ = cache.shape
    dup = lambda x: jnp.broadcast_to(x[:, :, :, None, :], (b, p, N_KV, 2, HEAD_DIM)).reshape(b, p, KV_DUP_W)
    return dup(cache[:, l, 0]).astype(bf16), dup(cache[:, l, 1]).astype(bf16)


def kernel(x_prompt, x_sample, cache_global_kv, cache_window_kv, c, c_ctx, w_mod, b_mod, norm1_g, w_in,
           q_norm_g, k_norm_g, sink_logit, w_branch_a, w_branch_b, w_branch_c, w_out, norm2_g,
           w_route_group, w_route_expert, w_exp_gate, w_exp_up, w_exp_down, final_norm_g):
    batch, seq, _ = x_prompt.shape
    dec_batch, dec_seq, _ = x_sample.shape

    cond = jnp.zeros((N_COND_ROWS, D_MODEL), f32).at[0].set(c_ctx).at[1:1 + dec_batch].set(c)
    mod = _mod_call(cond, w_mod, b_mod).reshape(DEPTH, N_COND_ROWS, 6, D_MODEL)

    ones_blk = jnp.asarray(np.kron(np.eye(N_HEADS), np.ones((HEAD_DIM, HEAD_DIM))), dtype=bf16)
    tri = jnp.asarray(np.triu(np.ones((TOKEN_TILE, TOKEN_TILE))), dtype=bf16)
    rope_tabs = _rope_tables(dec_seq)
    ftab_ctx = _fourier_tables(seq)
    ftab_lat = _fourier_tables(dec_seq)
    gf = final_norm_g.reshape(1, D_MODEL)

    ctx = dict(tokens_per_batch=seq, cond_row0=0, per_batch_cond=False)
    lat = dict(tokens_per_batch=dec_seq, cond_row0=1, per_batch_cond=True)
    xp = (x_prompt.reshape(batch * seq, D_MODEL),)
    xs = (x_sample.reshape(dec_batch * dec_seq, D_MODEL),)
    mod_prev = None
    gkv, wkv = [], []
    for l in range(DEPTH):
        w_ext = w_gate = w_in[l].astype(bf16)
        g1 = norm1_g[l].reshape(1, D_MODEL)
        g2 = norm2_g[l].reshape(1, D_MODEL)
        qg = jnp.tile(q_norm_g[l], N_HEADS).reshape(1, Q_W)
        kg = jnp.tile(k_norm_g[l], N_KV).reshape(1, LANES)
        wba, wbb, wbc = (w.astype(bf16) for w in (w_branch_a[l], w_branch_b[l], w_branch_c[l]))
        wout = w_out[l].astype(bf16)
        wrt = jnp.concatenate([w_route_expert[l].T, w_route_group[l].T,
                               jnp.zeros((ROUTE_ROWS - N_EXPERTS - N_EXPERT_GROUPS, D_MODEL), f32)], axis=0)
        weg, weu = w_exp_gate[l].astype(bf16), w_exp_up[l].astype(bf16)
        wed = w_exp_down[l].reshape(N_EXPERT_GROUPS, GROUP_FF, D_MODEL).astype(bf16)
        sink = sink_logit[l]

        def mix_and_moe(xin, qa, ka, va, qc, kc, vc, ub, attn, ftab, seq_len, stream):
            oa, oc = attn(qa, ka, va, qc, kc, vc)
            ob = _fourier_call(ub, *ftab, seq_len)
            x1, h2x, code, cnt = _s3_call(xin, mod_prev, oa, ob, oc, mod[l], g1, g2, w_gate, wba, wbb, wbc, wout,
                                          wrt, tri, **stream)
            counts = cnt[:, 0].astype(jnp.int32)
            moe = _moe_call(code.reshape(-1), counts, h2x, weg, weu, wed)
            return (x1, moe)

        qa, ka, va, qc, kc, vc, ub, gkv_l, wkv_l = _s1_call(
            xp, mod_prev, mod[l], g1, w_ext, ones_blk, qg, kg, None, emit_kv=True, **ctx)
        attn_ctx = lambda *qkv: _attn_ctx_call(sink, *qkv, seq)
        xp = mix_and_moe(xp, qa, ka, va, qc, kc, vc, ub, attn_ctx, ftab_ctx, seq, ctx)
        kv_shape = (batch, 2, seq, N_KV, HEAD_DIM)
        gkv.append(gkv_l.reshape(kv_shape))
        wkv.append(wkv_l.reshape(kv_shape))

        qa, ka, va, qc, kc, vc, ub = _s1_call(
            xs, mod_prev, mod[l], g1, w_ext, ones_blk, qg, kg, rope_tabs, emit_kv=False, **lat)
        kxa, vxa = _dup_cache(cache_global_kv, l)
        kxc, vxc = _dup_cache(cache_window_kv, l)
        attn_lat = lambda qa, ka, va, qc, kc, vc: _attn_lat_call(
            sink, qa, ka, va, kxa, vxa, qc, kc, vc, kxc, vxc, dec_seq)
        xs = mix_and_moe(xs, qa, ka, va, qc, kc, vc, ub, attn_lat, ftab_lat, dec_seq, lat)
        mod_prev = mod[l]

    y_prompt = _final_call(*xp, mod_prev, gf, **ctx).reshape(batch, seq, D_MODEL)
    y_sample = _final_call(*xs, mod_prev, gf, **lat).reshape(dec_batch, dec_seq, D_MODEL)
    return (y_prompt, y_sample, jnp.stack(gkv, axis=1), jnp.stack(wkv, axis=1))
```

```python
import functools

import jax
import jax.numpy as jnp
import numpy as np
from jax import lax
from jax.experimental import pallas as pl
from jax.experimental.pallas import tpu as pltpu

f32 = jnp.float32
bf16 = jnp.bfloat16

D_MODEL = 1024
DEPTH = 2
GRID_W = 64
HEAD_DIM = 64
HALF = HEAD_DIM // 2
N_HEADS = 8
N_KV = 2
WINDOW = 128
N_FOURIER_GROUPS = 4
FOURIER_GROUP_DIM = 128
FOURIER_DIM = N_FOURIER_GROUPS * FOURIER_GROUP_DIM
ROPE_THETA = 10000.0
AXIS_FREQS = HEAD_DIM // 4
N_EXPERT_GROUPS = 4
EXPERTS_PER_GROUP = 8
N_EXPERTS = N_EXPERT_GROUPS * EXPERTS_PER_GROUP
EXPERT_FF = 128
NORM_EPS = 1e-6
NEG_INF = -1e30

LANES = 128
Q_W = N_HEADS * HEAD_DIM
KV_DUP_W = 2 * N_KV * HEAD_DIM
C_QA, C_KA, C_VA, C_QC, C_KC, C_VC, C_UB, C_GATES, C_END = 0, 512, 640, 768, 1280, 1408, 1536, 2048, 5120
N_COND_ROWS = 16
VMEM_LIMIT = 56 * 1024 * 1024


def _cparams(sem):
    return pltpu.CompilerParams(dimension_semantics=sem, vmem_limit_bytes=VMEM_LIMIT)


def _resident(shape):
    nd = len(shape)
    return pl.BlockSpec(shape, lambda *_: (0,) * nd, pipeline_mode=pl.Buffered(1))


def _sigmoid(x):
    return 0.5 * jnp.tanh(0.5 * x) + 0.5


def _rms_mod(x, g, scale, shift):
    y = x * lax.rsqrt(jnp.mean(x * x, axis=-1, keepdims=True) + NORM_EPS)
    return (y * g) * (1.0 + scale) + shift


def _dot(a, b):
    return jnp.dot(a, b, preferred_element_type=f32)


def _dot_t(a, b):
    return lax.dot_general(a, b, (((1,), (1,)), ((), ())), preferred_element_type=f32)


def _split_bf16(x):
    hi = x.astype(bf16)
    lo = (x - hi.astype(f32)).astype(bf16)
    return hi, lo


def _mod_kernel(cond_ref, w_ref, b_ref, o_ref):
    c = cond_ref[...]
    s = (c * _sigmoid(c)).astype(bf16)
    o_ref[0] = _dot(s, w_ref[0].astype(bf16)) + b_ref[0]


def _mod_call(cond, w_mod, b_mod):
    n_tiles = 6 * D_MODEL // 1024
    return pl.pallas_call(
        _mod_kernel,
        grid=(DEPTH, n_tiles),
        in_specs=[
            pl.BlockSpec((N_COND_ROWS, D_MODEL), lambda l, j: (0, 0)),
            pl.BlockSpec((1, D_MODEL, 1024), lambda l, j: (l, 0, j)),
            pl.BlockSpec((1, 1, 1024), lambda l, j: (l, 0, j)),
        ],
        out_specs=pl.BlockSpec((1, N_COND_ROWS, 1024), lambda l, j: (l, 0, j)),
        out_shape=jax.ShapeDtypeStruct((DEPTH, N_COND_ROWS, 6 * D_MODEL), f32),
        compiler_params=_cparams(("arbitrary", "arbitrary")),
        name="mod",
    )(cond, w_mod, b_mod.reshape(DEPTH, 1, 6 * D_MODEL))


CAST_BLOCK_BYTES = 4 * 1024 * 1024


def _cast_kernel(w_ref, o_ref):
    o_ref[...] = w_ref[0].astype(bf16)


def _layer_weight_bf16(w, l):
    _, rows, cols = w.shape
    rb = min(rows, 1 << ((CAST_BLOCK_BYTES // (4 * cols)).bit_length() - 1))
    assert rows % rb == 0
    return pl.pallas_call(
        _cast_kernel,
        grid=(rows // rb,),
        in_specs=[pl.BlockSpec((1, rb, cols), lambda i: (l, i, 0))],
        out_specs=pl.BlockSpec((rb, cols), lambda i: (i, 0)),
        out_shape=jax.ShapeDtypeStruct((rows, cols), bf16),
        compiler_params=_cparams(("arbitrary",)),
        name="cast_bf16",
    )(w)


def _head_norm(z, g, ones_blk):
    hi, lo = _split_bf16(z * z)
    ss = _dot(hi, ones_blk) + _dot(lo, ones_blk)
    return z * lax.rsqrt(ss * (1.0 / HEAD_DIM) + NORM_EPS) * g


def _rope(z, cos, sin_signed, first_half):
    outs = []
    for j in range(z.shape[1] // LANES):
        zj = z[:, j * LANES:(j + 1) * LANES]
        swapped = jnp.where(first_half, pltpu.roll(zj, LANES - HALF, 1), pltpu.roll(zj, HALF, 1))
        outs.append(zj * cos + swapped * sin_signed)
    return jnp.concatenate(outs, axis=1)


def _layer_input(x_refs, r0, n):
    rows = pl.ds(r0, n)
    if len(x_refs) == 1:
        return x_refs[0][rows, :]
    x1_ref, moe_ref, modp_ref = x_refs
    return x1_ref[rows, :] + modp_ref[0, 5:6, :] * _rows_from_slabs(moe_ref, r0 * SLAB_ROWS, n, SLAB_ROWS)


S1_SUB_TILES = 2
S3_SUB_TILES = 1


SLAB_ROWS = D_MODEL // LANES


def _rows_from_slabs(ref, start, n_tok, pitch):
    return jnp.concatenate([ref[pl.ds(start + c, n_tok, stride=pitch), :] for c in range(SLAB_ROWS)], axis=1)


def _rows_to_slabs(ref, start, x, pitch):
    for c in range(SLAB_ROWS):
        ref[pl.ds(start + c, x.shape[0], stride=pitch), :] = x[:, c * LANES:(c + 1) * LANES]


def _s1_kernel(*refs, rope, emit_kv, n_x):
    x_refs, refs = refs[:n_x], refs[n_x:]
    mod_ref, g1_ref, w_ref, ones_ref, qg_ref, kg_ref = refs[:6]
    pos = 6
    if rope:
        cos_ref, sin_ref = refs[pos:pos + 2]
        pos += 2
    qa_ref, ka_ref, va_ref, qc_ref, kc_ref, vc_ref, ub_ref = refs[pos:pos + 7]
    pos += 7
    if emit_kv:
        gkv_ref, wkv_ref = refs[pos:pos + 2]

    tm = x_refs[0].shape[0] // S1_SUB_TILES
    lane = lax.broadcasted_iota(jnp.int32, (tm, LANES), 1)
    low_lanes = lane < HEAD_DIM
    first_half = (lane & (HEAD_DIM - 1)) < HALF
    q_scale = HEAD_DIM ** -0.5

    def dup(z):
        r = pltpu.roll(z, HEAD_DIM, 1)
        return jnp.concatenate([jnp.where(low_lanes, z, r), jnp.where(low_lanes, r, z)], axis=1).astype(bf16)

    for sub in range(S1_SUB_TILES):
        rows = pl.ds(sub * tm, tm)
        x = _layer_input(x_refs, sub * tm, tm)
        h = _rms_mod(x, g1_ref[...], mod_ref[0, 1:2, :], mod_ref[0, 0:1, :]).astype(bf16)
        proj = lambda lo, hi: _dot(h, w_ref[:, lo:hi])
        if rope:
            cos = cos_ref[rows, :]
            sin_signed = sin_ref[rows, :]
            rot = lambda z: _rope(z, cos, sin_signed, first_half)
        else:
            rot = lambda z: z

        qa = rot(_head_norm(proj(C_QA, C_KA), qg_ref[...], ones_ref[...]))
        qa_ref[rows, :] = (qa * q_scale).astype(bf16)
        kva = proj(C_KA, C_QC)
        ka_n = _head_norm(kva[:, 0:LANES], kg_ref[...], ones_ref[0:LANES, 0:LANES])
        ka_ref[rows, :] = dup(rot(ka_n))
        va = kva[:, LANES:2 * LANES]
        va_ref[rows, :] = dup(va)
        qc_ref[rows, :] = (rot(proj(C_QC, C_KC)) * q_scale).astype(bf16)
        kvc = proj(C_KC, C_UB)
        kc = kvc[:, 0:LANES]
        kc_ref[rows, :] = dup(rot(kc))
        vc = kvc[:, LANES:2 * LANES]
        vc_ref[rows, :] = dup(vc)
        ub_ref[rows, :] = proj(C_UB, C_GATES).astype(bf16)
        if emit_kv:
            gkv_ref[sub, 0] = ka_n
            gkv_ref[sub, 1] = va
            wkv_ref[sub, 0] = kc
            wkv_ref[sub, 1] = vc


TOKEN_TILE = 512


def _cond_index_map(tokens_per_batch, cond_row0, per_batch_cond):
    tiles_per_batch = max(tokens_per_batch // TOKEN_TILE, 1)
    if per_batch_cond:
        assert tokens_per_batch % TOKEN_TILE == 0
        return lambda t: (cond_row0 + t // tiles_per_batch, 0, 0)
    return lambda t: (cond_row0, 0, 0)


def _layer_input_specs(xin, mod_prev, cond_idx):
    tok = pl.BlockSpec((TOKEN_TILE, D_MODEL), lambda t: (t, 0))
    if len(xin) == 1:
        return [tok], list(xin)
    slab = pl.BlockSpec((TOKEN_TILE * SLAB_ROWS, LANES), lambda t: (t, 0))
    return [tok, slab, pl.BlockSpec((1, 6, D_MODEL), cond_idx)], [xin[0], xin[1], mod_prev]


def _s1_call(xin, mod_prev, mod, g1, w_ext, ones_blk, qg, kg, rope_tabs, *, tokens_per_batch, cond_row0,
             per_batch_cond, emit_kv):
    n_tok = xin[0].shape[0]
    tm = TOKEN_TILE
    rope = rope_tabs is not None
    tiles_per_batch = max(tokens_per_batch // tm, 1)
    cond_idx = _cond_index_map(tokens_per_batch, cond_row0, per_batch_cond)

    tok = lambda w: pl.BlockSpec((tm, w), lambda t: (t, 0))
    x_specs, x_args = _layer_input_specs(xin, mod_prev, cond_idx)
    in_specs = x_specs + [
        pl.BlockSpec((1, 6, D_MODEL), cond_idx),
        _resident((1, D_MODEL)),
        _resident((D_MODEL, C_GATES)),
        _resident((Q_W, Q_W)),
        _resident((1, Q_W)),
        _resident((1, LANES)),
    ]
    args = x_args + [mod, g1, w_ext, ones_blk, qg, kg]
    if rope:
        in_specs += [pl.BlockSpec((tm, LANES), lambda t: (t % tiles_per_batch, 0))] * 2
        args += list(rope_tabs)
    widths = [Q_W, KV_DUP_W, KV_DUP_W, Q_W, KV_DUP_W, KV_DUP_W, FOURIER_DIM]
    out_specs = [tok(w) for w in widths]
    out_shape = [jax.ShapeDtypeStruct((n_tok, w), bf16) for w in widths]
    if emit_kv:
        assert tokens_per_batch * S1_SUB_TILES == tm
        n_batch = n_tok // tokens_per_batch
        out_specs += [pl.BlockSpec((S1_SUB_TILES, 2, tokens_per_batch, LANES), lambda t: (t, 0, 0, 0))] * 2
        out_shape += [jax.ShapeDtypeStruct((n_batch, 2, tokens_per_batch, LANES), f32)] * 2
    return pl.pallas_call(
        functools.partial(_s1_kernel, rope=rope, emit_kv=emit_kv, n_x=len(x_args)),
        grid=(n_tok // tm,),
        in_specs=in_specs,
        out_specs=out_specs,
        out_shape=out_shape,
        compiler_params=_cparams(("arbitrary",)),
        name="s1_rope" if rope else "s1_ctx",
    )(*args)


def _gqa_group(q2, segs, sink_pair_lo, sink_pair_hi):
    qb = q2.shape[0]
    qs = jnp.concatenate([q2[:, 0:LANES], q2[:, LANES:2 * LANES]], axis=0)
    lane = lax.broadcasted_iota(jnp.int32, qs.shape, 1)
    low = lane < HEAD_DIM
    zero = jnp.zeros_like(qs)
    q4 = jnp.concatenate([jnp.where(low, qs, zero), jnp.where(low, zero, qs)], axis=0)

    scores = []
    for k, _, ok in segs:
        s = _dot_t(q4, k)
        if ok is not None:
            s = jnp.where(ok, s, NEG_INF)
        scores.append(s)
    m = scores[0].max(axis=-1, keepdims=True)
    for s in scores[1:]:
        m = jnp.maximum(m, s.max(axis=-1, keepdims=True))
    if sink_pair_lo is not None:
        row = lax.broadcasted_iota(jnp.int32, (4 * qb, 1), 0)
        sink = jnp.where(row < 2 * qb, jnp.where(row < qb, sink_pair_lo[0], sink_pair_lo[1]),
                         jnp.where(row < 3 * qb, sink_pair_hi[0], sink_pair_hi[1]))
        m = jnp.maximum(m, sink)
        denom = jnp.exp(sink - m)
    else:
        denom = jnp.zeros_like(m)
    acc = None
    for s, (_, v, _) in zip(scores, segs):
        p = jnp.exp(s - m)
        denom = denom + p.sum(axis=-1, keepdims=True)
        pv = _dot(p.astype(bf16), v)
        acc = pv if acc is None else acc + pv
    o4 = acc / denom
    o = jnp.where(low, o4[0:2 * qb], o4[2 * qb:4 * qb])
    return jnp.concatenate([o[0:qb], o[qb:2 * qb]], axis=1)


def _sink_pairs(sink_ref, kv):
    h = 4 * kv
    return (sink_ref[h], sink_ref[h + 2]), (sink_ref[h + 1], sink_ref[h + 3])


def _attn_ctx_kernel(sink_ref, qa_ref, ka_ref, va_ref, qc_ref, kc_ref, vc_ref, oa_ref, oc_ref):
    for kv in range(N_KV):
        cq = slice(2 * LANES * kv, 2 * LANES * (kv + 1))
        ck = slice(LANES * kv, LANES * (kv + 1))
        oa = _gqa_group(qa_ref[:, cq], [(ka_ref[:, ck], va_ref[:, ck], None)], None, None)
        oa_ref[:, cq] = oa.astype(bf16)
        lo, hi = _sink_pairs(sink_ref, kv)
        oc = _gqa_group(qc_ref[:, cq], [(kc_ref[:, ck], vc_ref[:, ck], None)], lo, hi)
        oc_ref[:, cq] = oc.astype(bf16)


def _attn_ctx_call(sink, qa, ka, va, qc, kc, vc, seq):
    n_tok = qa.shape[0]
    tok = lambda w: pl.BlockSpec((seq, w), lambda b: (b, 0))
    return pl.pallas_call(
        _attn_ctx_kernel,
        grid=(n_tok // seq,),
        in_specs=[pl.BlockSpec(memory_space=pltpu.SMEM),
                  tok(Q_W), tok(KV_DUP_W), tok(KV_DUP_W), tok(Q_W), tok(KV_DUP_W), tok(KV_DUP_W)],
        out_specs=[tok(Q_W), tok(Q_W)],
        out_shape=[jax.ShapeDtypeStruct((n_tok, Q_W), bf16)] * 2,
        compiler_params=_cparams(("arbitrary",)),
        name="attn_ctx",
    )(sink, qa, ka, va, qc, kc, vc)


QB = 256
WIN_QB = 128
WIN_SPAN = WIN_QB + 2 * WINDOW


def _attn_lat_kernel(sink_ref, qa_ref, ka_ref, va_ref, kxa_ref, vxa_ref,
                     qc_ref, kc_ref, vc_ref, kxc_ref, vxc_ref, oa_ref, oc_ref, *, seq):
    i = pl.program_id(1)
    for kv in range(N_KV):
        cq = slice(2 * LANES * kv, 2 * LANES * (kv + 1))
        ck = slice(LANES * kv, LANES * (kv + 1))
        segs = [(ka_ref[:, ck], va_ref[:, ck], None), (kxa_ref[0, :, ck], vxa_ref[0, :, ck], None)]
        oa_ref[:, cq] = _gqa_group(qa_ref[:, cq], segs, None, None).astype(bf16)

        lo, hi = _sink_pairs(sink_ref, kv)
        for sub in range(QB // WIN_QB):
            blk = i * (QB // WIN_QB) + sub
            start = jnp.clip((blk - 1) * WIN_QB, 0, seq - WIN_SPAN)
            start = pl.multiple_of(start, WIN_QB)
            r = lax.broadcasted_iota(jnp.int32, (4 * WIN_QB, WIN_SPAN), 0)
            c = lax.broadcasted_iota(jnp.int32, (4 * WIN_QB, WIN_SPAN), 1)
            qpos = blk * WIN_QB + (r & (WIN_QB - 1))
            ok = jnp.abs(qpos - (start + c)) <= WINDOW
            rows = slice(sub * WIN_QB, (sub + 1) * WIN_QB)
            segs = [(kc_ref[pl.ds(start, WIN_SPAN), ck], vc_ref[pl.ds(start, WIN_SPAN), ck], ok),
                    (kxc_ref[0, :, ck], vxc_ref[0, :, ck], None)]
            oc_ref[rows, cq] = _gqa_group(qc_ref[rows, cq], segs, lo, hi).astype(bf16)


def _attn_lat_call(sink, qa, ka, va, kxa, vxa, qc, kc, vc, kxc, vxc, seq):
    n_tok = qa.shape[0]
    nq = seq // QB
    past = kxa.shape[1]
    qspec = pl.BlockSpec((QB, Q_W), lambda b, i: (b * nq + i, 0))
    kspec = pl.BlockSpec((seq, KV_DUP_W), lambda b, i: (b, 0))
    xspec = pl.BlockSpec((1, past, KV_DUP_W), lambda b, i: (b, 0, 0))
    return pl.pallas_call(
        functools.partial(_attn_lat_kernel, seq=seq),
        grid=(n_tok // seq, nq),
        in_specs=[pl.BlockSpec(memory_space=pltpu.SMEM),
                  qspec, kspec, kspec, xspec, xspec, qspec, kspec, kspec, xspec, xspec],
        out_specs=[qspec, qspec],
        out_shape=[jax.ShapeDtypeStruct((n_tok, Q_W), bf16)] * 2,
        compiler_params=_cparams(("arbitrary", "arbitrary")),
        name="attn_lat",
    )(sink, qa, ka, va, kxa, vxa, qc, kc, vc, kxc, vxc)


def _fourier_kernel(u_ref, bdc_ref, bds_ref, cn_ref, sn_ref, o_ref, *, scale):
    u = u_ref[...]
    uc = _dot(u, bdc_ref[...]).astype(bf16)
    us = _dot(u, bds_ref[...]).astype(bf16)
    o = _dot(cn_ref[...], uc) - _dot(sn_ref[...], us)
    o_ref[...] = (o * scale).astype(bf16)


def _fourier_call(ub, bdc, bds, cn, sn, seq):
    n_tok = ub.shape[0]
    tok = pl.BlockSpec((seq, FOURIER_DIM), lambda b: (b, 0))
    scale = float(1.0 / np.sqrt(seq * FOURIER_GROUP_DIM))
    return pl.pallas_call(
        functools.partial(_fourier_kernel, scale=scale),
        grid=(n_tok // seq,),
        in_specs=[tok, _resident((FOURIER_DIM, FOURIER_DIM)), _resident((FOURIER_DIM, FOURIER_DIM)),
                  _resident((seq, seq)), _resident((seq, seq))],
        out_specs=tok,
        out_shape=jax.ShapeDtypeStruct((n_tok, FOURIER_DIM), bf16),
        compiler_params=_cparams(("arbitrary",)),
        name=f"fourier_{seq}",
    )(ub, bdc, bds, cn, sn)


ROUTE_ROWS = LANES
H2X_PITCH = SLAB_ROWS + 1
MOE_PITCH = SLAB_ROWS


def _route(lt, tri_ref, carry_ref, n_tok):
    tm = lt.shape[1]
    row = lax.broadcasted_iota(jnp.int32, (EXPERTS_PER_GROUP, tm), 0).astype(f32)
    neg = jnp.float32(-jnp.inf)
    none = jnp.float32(EXPERTS_PER_GROUP)
    gl = jnp.where(row < N_EXPERT_GROUPS, lt[N_EXPERTS:N_EXPERTS + EXPERTS_PER_GROUP], neg)
    gmax = gl.max(axis=0, keepdims=True)
    gidx = jnp.where(gl == gmax, row, none).min(axis=0, keepdims=True)
    gw = 1.0 / jnp.exp(gl - gmax).sum(axis=0, keepdims=True)
    lg = lt[0:EXPERTS_PER_GROUP]
    for g in range(1, N_EXPERT_GROUPS):
        lg = jnp.where(gidx == g, lt[g * EXPERTS_PER_GROUP:(g + 1) * EXPERTS_PER_GROUP], lg)
    m1 = lg.max(axis=0, keepdims=True)
    i1 = jnp.where(lg == m1, row, none).min(axis=0, keepdims=True)
    l2 = jnp.where(row == i1, neg, lg)
    m2 = l2.max(axis=0, keepdims=True)
    i2 = jnp.where(l2 == m2, row, none).min(axis=0, keepdims=True)
    e = jnp.exp(m2 - m1)
    w8 = jnp.where(row == i1, gw / (1.0 + e), jnp.where(row == i2, gw * e / (1.0 + e), 0.0))

    onehot = (row == gidx).astype(f32)
    prefix = _dot(onehot.astype(bf16), tri_ref[0:tm, 0:tm])
    carry = carry_ref[...]
    rank = (onehot * (prefix - 1.0 + carry[:, 0:1])).sum(axis=0, keepdims=True)
    carry_ref[...] = carry + onehot.sum(axis=1, keepdims=True)
    dest = gidx.astype(jnp.int32) * n_tok + rank.astype(jnp.int32)
    return w8, dest


def _s3_kernel(*refs, n_x, n_tok):
    x_refs, refs = refs[:n_x], refs[n_x:]
    (oa_ref, ob_ref, oc_ref, mod_ref, g1_ref, g2_ref, wga_ref, wgb_ref, wgc_ref, wba_ref, wbb_ref, wbc_ref,
     wout_ref, wrt_ref, tri_ref, x1_ref, h2x_ref, code_ref, cnt_ref, carry_ref) = refs
    wg_refs = (wga_ref, wgb_ref, wgc_ref)

    @pl.when(pl.program_id(0) == 0)
    def _():
        carry_ref[...] = jnp.zeros_like(carry_ref)

    tm = x1_ref.shape[0] // S3_SUB_TILES
    wh, wl = _split_bf16(wrt_ref[...])
    for sub in range(S3_SUB_TILES):
        r0 = sub * tm
        rows = pl.ds(r0, tm)
        x = _layer_input(x_refs, r0, tm)
        h = _rms_mod(x, g1_ref[...], mod_ref[0, 1:2, :], mod_ref[0, 0:1, :]).astype(bf16)
        merged = None
        for j, (o_ref, wb_ref) in enumerate(((oa_ref, wba_ref), (ob_ref, wbb_ref), (oc_ref, wbc_ref))):
            gate = _sigmoid(_dot(h, wg_refs[j][...]))
            term = gate * _dot(o_ref[rows, :], wb_ref[...])
            merged = term if merged is None else merged + term
        mix = _dot(merged.astype(bf16), wout_ref[...])
        x1 = x + mod_ref[0, 2:3, :] * mix
        x1_ref[rows, :] = x1
        h2 = _rms_mod(x1, g2_ref[...], mod_ref[0, 4:5, :], mod_ref[0, 3:4, :])
        _rows_to_slabs(h2x_ref, r0 * H2X_PITCH, h2, H2X_PITCH)
        th, tl = _split_bf16(h2)
        lt = _dot_t(wh, th) + (_dot_t(wh, tl) + _dot_t(wl, th))
        w8, code = _route(lt, tri_ref, carry_ref, n_tok)
        code_ref[:, rows] = code
        w_rows = jnp.concatenate([w8, jnp.zeros((LANES - EXPERTS_PER_GROUP, tm), f32)], axis=0)
        h2x_ref[pl.ds(r0 * H2X_PITCH + SLAB_ROWS, tm, stride=H2X_PITCH), :] = w_rows.T
    cnt_ref[...] = carry_ref[...]


def _s3_call(xin, mod_prev, oa, ob, oc, mod, g1, g2, wg, wba, wbb, wbc, wout, wrt, tri, *, tokens_per_batch,
             cond_row0, per_batch_cond):
    n_tok = xin[0].shape[0]
    tm = TOKEN_TILE
    cond_idx = _cond_index_map(tokens_per_batch, cond_row0, per_batch_cond)
    tok = lambda w: pl.BlockSpec((tm, w), lambda t: (t, 0))
    x_specs, x_args = _layer_input_specs(xin, mod_prev, cond_idx)
    gate_cols = lambda j: pl.BlockSpec((D_MODEL, D_MODEL), lambda t: (0, C_GATES // D_MODEL + j),
                                       pipeline_mode=pl.Buffered(1))
    return pl.pallas_call(
        functools.partial(_s3_kernel, n_x=len(x_args), n_tok=n_tok),
        grid=(n_tok // tm,),
        in_specs=x_specs + [
            tok(Q_W), tok(FOURIER_DIM), tok(Q_W),
            pl.BlockSpec((1, 6, D_MODEL), cond_idx),
            _resident((1, D_MODEL)), _resident((1, D_MODEL)),
            gate_cols(0), gate_cols(1), gate_cols(2),
            _resident((Q_W, D_MODEL)), _resident((FOURIER_DIM, D_MODEL)), _resident((Q_W, D_MODEL)),
            _resident((D_MODEL, D_MODEL)), _resident((ROUTE_ROWS, D_MODEL)), _resident((tm, tm))],
        out_specs=[tok(D_MODEL), pl.BlockSpec((tm * H2X_PITCH, LANES), lambda t: (t, 0)),
                   pl.BlockSpec((1, tm), lambda t: (0, t)),
                   pl.BlockSpec((EXPERTS_PER_GROUP, LANES), lambda t: (0, 0))],
        out_shape=[jax.ShapeDtypeStruct((n_tok, D_MODEL), f32),
                   jax.ShapeDtypeStruct((n_tok * H2X_PITCH, LANES), f32),
                   jax.ShapeDtypeStruct((1, n_tok), jnp.int32),
                   jax.ShapeDtypeStruct((EXPERTS_PER_GROUP, LANES), f32)],
        scratch_shapes=[pltpu.VMEM((EXPERTS_PER_GROUP, LANES), f32)],
        compiler_params=_cparams(("arbitrary",)),
        name="s3",
    )(*x_args, oa, ob, oc, mod, g1, g2, wg, wg, wg, wba, wbb, wbc, wout, wrt, tri)


MOE_TILE_LOG2 = 8
MOE_TILE = 1 << MOE_TILE_LOG2
ROW_UNROLL_LOG2 = 3
ROW_UNROLL = 1 << ROW_UNROLL_LOG2
GROUP_FF = EXPERTS_PER_GROUP * EXPERT_FF
PLAN_FIRST, PLAN_ITEMS, PLAN_LEN = 0, N_EXPERT_GROUPS, N_EXPERT_GROUPS + 1


def _moe_kernel(code_ref, cnt_ref, h2x_hbm, weg_ref, weu_ref, wed_ref, out_hbm,
                pos_ref, plan_ref, gbuf, obuf, dump, gsem, ssem, *, n_tok):
    tm = MOE_TILE
    k = pl.program_id(0)
    n_steps = pl.num_programs(0)
    g_rows = tm * H2X_PITCH
    o_rows = tm * MOE_PITCH

    def out_slab(buf, slot, i):
        return buf.at[pl.ds(pl.multiple_of(slot * o_rows + i * MOE_PITCH, MOE_PITCH), MOE_PITCH)]

    def token_slab(tok):
        return out_hbm.at[pl.ds(pl.multiple_of(tok * MOE_PITCH, MOE_PITCH), MOE_PITCH)]

    def for_row_chunks(n_chunks, body):
        def chunk(c, carry):
            body(c * ROW_UNROLL)
            return carry
        lax.fori_loop(0, n_chunks, chunk, 0)

    @pl.when(k == 0)
    def _plan():
        first = jnp.int32(0)
        for g in range(N_EXPERT_GROUPS):
            c = cnt_ref[g]
            n_tiles = (c + (tm - 1)) >> MOE_TILE_LOG2
            plan_ref[PLAN_FIRST + g] = first

            def pad(p, carry):
                pos_ref[p] = 0
                return carry

            lax.fori_loop(g * n_tok + c, g * n_tok + n_tiles * tm, pad, 0)
            first = first + n_tiles
        plan_ref[PLAN_ITEMS] = first

        def place(t0):
            dests = [code_ref[t0 + j] for j in range(ROW_UNROLL)]
            for j, dest in enumerate(dests):
                pos_ref[dest] = t0 + j

        for_row_chunks(n_tok >> ROW_UNROLL_LOG2, place)

    n_items = plan_ref[PLAN_ITEMS]

    def item_group(item):
        return ((item >= plan_ref[PLAN_FIRST + 1]).astype(jnp.int32)
                + (item >= plan_ref[PLAN_FIRST + 2]).astype(jnp.int32)
                + (item >= plan_ref[PLAN_FIRST + 3]).astype(jnp.int32))

    def item_pos0(item):
        g = item_group(item)
        return g * n_tok + (item - plan_ref[PLAN_FIRST + g]) * tm

    def start_gather(item, slot):
        pos0 = item_pos0(item)

        def chunk(i0):
            toks = [pos_ref[pos0 + i0 + j] for j in range(ROW_UNROLL)]
            for j, tok in enumerate(toks):
                pltpu.make_async_copy(h2x_hbm.at[pl.ds(tok * H2X_PITCH, H2X_PITCH)],
                                      gbuf.at[pl.ds(slot * g_rows + (i0 + j) * H2X_PITCH, H2X_PITCH)],
                                      gsem.at[slot]).start(priority=j % 2)
        for_row_chunks(tm >> ROW_UNROLL_LOG2, chunk)

    def wait_gather(slot):
        pltpu.make_async_copy(h2x_hbm.at[pl.ds(0, g_rows)], gbuf.at[pl.ds(slot * g_rows, g_rows)],
                              gsem.at[slot]).wait()

    def start_scatter(item, n_valid, slot):
        pos0 = item_pos0(item)

        def to_token(i0):
            dsts = [pos_ref[pos0 + i0 + j] for j in range(ROW_UNROLL)]
            for j, dst in enumerate(dsts):
                pltpu.make_async_copy(out_slab(obuf, slot, i0 + j), token_slab(dst),
                                      ssem.at[slot]).start(priority=j % 2)

        def row_to_token(i, carry):
            pltpu.make_async_copy(out_slab(obuf, slot, i), token_slab(pos_ref[pos0 + i]), ssem.at[slot]).start()
            return carry

        def row_to_dump(i, carry):
            pltpu.make_async_copy(out_slab(obuf, slot, i), out_slab(dump, slot, i), ssem.at[slot]).start()
            return carry

        full = n_valid >> ROW_UNROLL_LOG2
        for_row_chunks(full, to_token)
        lax.fori_loop(full * ROW_UNROLL, n_valid, row_to_token, 0)
        lax.fori_loop(n_valid, tm, row_to_dump, 0)

    def wait_scatter(slot):
        pltpu.make_async_copy(obuf.at[pl.ds(slot * o_rows, o_rows)], out_hbm.at[pl.ds(0, o_rows)],
                              ssem.at[slot]).wait()

    slot = k % 2

    @pl.when(k == 0)
    def _():
        start_gather(0, 0)

    @pl.when(k + 1 < n_items)
    def _():
        start_gather(k + 1, 1 - slot)

    @pl.when(k < n_items)
    def _():
        group = item_group(k)
        wait_gather(slot)
        h = _rows_from_slabs(gbuf, slot * g_rows, tm, H2X_PITCH).astype(bf16)
        w8 = gbuf[pl.ds(slot * g_rows + SLAB_ROWS, tm, stride=H2X_PITCH), :]
        cols = pl.ds(pl.multiple_of(group * GROUP_FF, GROUP_FF), GROUP_FF)
        gate = _dot(h, weg_ref[:, cols])
        a = gate * _sigmoid(gate) * _dot(h, weu_ref[:, cols])
        pieces = [a[:, j * EXPERT_FF:(j + 1) * EXPERT_FF] * w8[:, j:j + 1] for j in range(EXPERTS_PER_GROUP)]
        out = _dot(jnp.concatenate(pieces, axis=1).astype(bf16), wed_ref[group])

        @pl.when(k >= 2)
        def _():
            wait_scatter(slot)

        _rows_to_slabs(obuf, slot * o_rows, out, MOE_PITCH)
        n_valid = jnp.minimum(tm, cnt_ref[group] - (k - plan_ref[PLAN_FIRST + group]) * tm)
        start_scatter(k, n_valid, slot)

    @pl.when(k == n_steps - 1)
    def _drain():
        wait_scatter((n_items - 1) % 2)

        @pl.when(n_items >= 2)
        def _():
            wait_scatter(n_items % 2)


def _moe_call(code, cnt, h2x, weg, weu, wed):
    n_tok = h2x.shape[0] // H2X_PITCH
    tm = MOE_TILE
    n_steps = n_tok // tm + N_EXPERT_GROUPS - 1
    ff = N_EXPERT_GROUPS * GROUP_FF
    up_spec = pl.BlockSpec((D_MODEL, ff), lambda k, *_: (0, 0), pipeline_mode=pl.Buffered(1))
    down_spec = pl.BlockSpec((N_EXPERT_GROUPS, GROUP_FF, D_MODEL), lambda k, *_: (0, 0, 0),
                             pipeline_mode=pl.Buffered(1))
    grid_spec = pltpu.PrefetchScalarGridSpec(
        num_scalar_prefetch=2,
        grid=(n_steps,),
        in_specs=[pl.BlockSpec(memory_space=pl.ANY), up_spec, up_spec, down_spec],
        out_specs=pl.BlockSpec(memory_space=pl.ANY),
        scratch_shapes=[
            pltpu.SMEM((N_EXPERT_GROUPS * n_tok,), jnp.int32),
            pltpu.SMEM((PLAN_LEN,), jnp.int32),
            pltpu.VMEM((2 * tm * H2X_PITCH, LANES), f32),
            pltpu.VMEM((2 * tm * MOE_PITCH, LANES), f32),
            pltpu.VMEM((2 * tm * MOE_PITCH, LANES), f32),
            pltpu.SemaphoreType.DMA((2,)),
            pltpu.SemaphoreType.DMA((2,)),
        ],
    )
    return pl.pallas_call(
        functools.partial(_moe_kernel, n_tok=n_tok),
        grid_spec=grid_spec,
        out_shape=jax.ShapeDtypeStruct((n_tok * MOE_PITCH, LANES), f32),
        compiler_params=_cparams(("arbitrary",)),
        name="moe",
    )(code, cnt, h2x, weg, weu, wed)


def _final_kernel(x1_ref, moe_ref, mod_ref, gf_ref, o_ref):
    x2 = x1_ref[...] + mod_ref[0, 5:6, :] * _rows_from_slabs(moe_ref, 0, x1_ref.shape[0], MOE_PITCH)
    y = x2 * lax.rsqrt(jnp.mean(x2 * x2, axis=-1, keepdims=True) + NORM_EPS)
    o_ref[...] = y * gf_ref[...]


def _final_call(x1, moe, mod, gf, *, tokens_per_batch, cond_row0, per_batch_cond):
    n_tok = x1.shape[0]
    tok = pl.BlockSpec((TOKEN_TILE, D_MODEL), lambda t: (t, 0))
    slab = pl.BlockSpec((TOKEN_TILE * MOE_PITCH, LANES), lambda t: (t, 0))
    cond_idx = _cond_index_map(tokens_per_batch, cond_row0, per_batch_cond)
    return pl.pallas_call(
        _final_kernel,
        grid=(n_tok // TOKEN_TILE,),
        in_specs=[tok, slab, pl.BlockSpec((1, 6, D_MODEL), cond_idx), _resident((1, D_MODEL))],
        out_specs=tok,
        out_shape=jax.ShapeDtypeStruct((n_tok, D_MODEL), f32),
        compiler_params=_cparams(("arbitrary",)),
        name="final_norm",
    )(x1, moe, mod, gf)


def _rope_tables(n):
    rows = n // GRID_W
    row = jnp.repeat(jnp.arange(rows, dtype=f32), GRID_W)
    col = jnp.tile(jnp.arange(GRID_W, dtype=f32), rows)
    inv = ROPE_THETA ** (-jnp.arange(AXIS_FREQS, dtype=f32) / AXIS_FREQS)
    ang = jnp.concatenate([row[:, None] * inv, col[:, None] * inv], axis=-1)
    cos, sin = jnp.cos(ang), jnp.sin(ang)
    cos_t = jnp.concatenate([cos, cos, cos, cos], axis=-1)
    sin_t = jnp.concatenate([-sin, sin, -sin, sin], axis=-1)
    return cos_t, sin_t


def _dft_tables(n):
    k = np.arange(n)
    m = (k[:, None] * k[None, :]) % n
    ang = 2.0 * np.pi * m / n
    return np.cos(ang), np.sin(ang)


def _fourier_tables(seq):
    cn, sn = _dft_tables(seq)
    cc, sc = _dft_tables(FOURIER_GROUP_DIM)
    eye = np.eye(N_FOURIER_GROUPS)
    bdc, bds = np.kron(eye, cc), np.kron(eye, sc)
    return tuple(jnp.asarray(t, dtype=f32).astype(bf16) for t in (bdc, bds, cn, sn))


def _dup_cache(cache, l):
    b, _, _, p, _, _ = cache.shape
    dup = lambda x: jnp.broadcast_to(x[:, :, :, None, :], (b, p, N_KV, 2, HEAD_DIM)).reshape(b, p, KV_DUP_W)
    return dup(cache[:, l, 0]).astype(bf16), dup(cache[:, l, 1]).astype(bf16)


def kernel(x_prompt, x_sample, cache_global_kv, cache_window_kv, c, c_ctx, w_mod, b_mod, norm1_g, w_in,
           q_norm_g, k_norm_g, sink_logit, w_branch_a, w_branch_b, w_branch_c, w_out, norm2_g,
           w_route_group, w_route_expert, w_exp_gate, w_exp_up, w_exp_down, final_norm_g):
    batch, seq, _ = x_prompt.shape
    dec_batch, dec_seq, _ = x_sample.shape

    cond = jnp.zeros((N_COND_ROWS, D_MODEL), f32).at[0].set(c_ctx).at[1:1 + dec_batch].set(c)
    mod = _mod_call(cond, w_mod, b_mod).reshape(DEPTH, N_COND_ROWS, 6, D_MODEL)

    ones_blk = jnp.asarray(np.kron(np.eye(N_HEADS), np.ones((HEAD_DIM, HEAD_DIM))), dtype=bf16)
    tri = jnp.asarray(np.triu(np.ones((TOKEN_TILE, TOKEN_TILE))), dtype=bf16)
    rope_tabs = _rope_tables(dec_seq)
    ftab_ctx = _fourier_tables(seq)
    ftab_lat = _fourier_tables(dec_seq)
    gf = final_norm_g.reshape(1, D_MODEL)

    ctx = dict(tokens_per_batch=seq, cond_row0=0, per_batch_cond=False)
    lat = dict(tokens_per_batch=dec_seq, cond_row0=1, per_batch_cond=True)
    xp = (x_prompt.reshape(batch * seq, D_MODEL),)
    xs = (x_sample.reshape(dec_batch * dec_seq, D_MODEL),)
    mod_prev = None
    gkv, wkv = [], []
    for l in range(DEPTH):
        w_ext = w_gate = _layer_weight_bf16(w_in, l)
        g1 = norm1_g[l].reshape(1, D_MODEL)
        g2 = norm2_g[l].reshape(1, D_MODEL)
        qg = jnp.tile(q_norm_g[l], N_HEADS).reshape(1, Q_W)
        kg = jnp.tile(k_norm_g[l], N_KV).reshape(1, LANES)
        wba, wbb, wbc = (_layer_weight_bf16(w, l) for w in (w_branch_a, w_branch_b, w_branch_c))
        wout = _layer_weight_bf16(w_out, l)
        wrt = jnp.concatenate([w_route_expert[l].T, w_route_group[l].T,
                               jnp.zeros((ROUTE_ROWS - N_EXPERTS - N_EXPERT_GROUPS, D_MODEL), f32)], axis=0)
        weg, weu = _layer_weight_bf16(w_exp_gate, l), _layer_weight_bf16(w_exp_up, l)
        wed = _layer_weight_bf16(w_exp_down, l).reshape(N_EXPERT_GROUPS, GROUP_FF, D_MODEL)
        sink = sink_logit[l]

        def mix_and_moe(xin, qa, ka, va, qc, kc, vc, ub, attn, ftab, seq_len, stream):
            oa, oc = attn(qa, ka, va, qc, kc, vc)
            ob = _fourier_call(ub, *ftab, seq_len)
            x1, h2x, code, cnt = _s3_call(xin, mod_prev, oa, ob, oc, mod[l], g1, g2, w_gate, wba, wbb, wbc, wout,
                                          wrt, tri, **stream)
            counts = cnt[:, 0].astype(jnp.int32)
            moe = _moe_call(code.reshape(-1), counts, h2x, weg, weu, wed)
            return (x1, moe)

        qa, ka, va, qc, kc, vc, ub, gkv_l, wkv_l = _s1_call(
            xp, mod_prev, mod[l], g1, w_ext, ones_blk, qg, kg, None, emit_kv=True, **ctx)
        attn_ctx = lambda *qkv: _attn_ctx_call(sink, *qkv, seq)
        xp = mix_and_moe(xp, qa, ka, va, qc, kc, vc, ub, attn_ctx, ftab_ctx, seq, ctx)
        kv_shape = (batch, 2, seq, N_KV, HEAD_DIM)
        gkv.append(gkv_l.reshape(kv_shape))
        wkv.append(wkv_l.reshape(kv_shape))

        qa, ka, va, qc, kc, vc, ub = _s1_call(
            xs, mod_prev, mod[l], g1, w_ext, ones_blk, qg, kg, rope_tabs, emit_kv=False, **lat)
        kxa, vxa = _dup_cache(cache_global_kv, l)
        kxc, vxc = _dup_cache(cache_window_kv, l)
        attn_lat = lambda qa, ka, va, qc, kc, vc: _attn_lat_call(
            sink, qa, ka, va, kxa, vxa, qc, kc, vc, kxc, vxc, dec_seq)
        xs = mix_and_moe(xs, qa, ka, va, qc, kc, vc, ub, attn_lat, ftab_lat, dec_seq, lat)
        mod_prev = mod[l]

    y_prompt = _final_call(*xp, mod_prev, gf, **ctx).reshape(batch, seq, D_MODEL)
    y_sample = _final_call(*xs, mod_prev, gf, **lat).reshape(dec_batch, dec_seq, D_MODEL)
    return (y_prompt, y_sample, jnp.stack(gkv, axis=1), jnp.stack(wkv, axis=1))
```

```python
import functools

import jax
import jax.numpy as jnp
import numpy as np
from jax import lax
from jax.experimental import pallas as pl
from jax.experimental.pallas import tpu as pltpu

f32 = jnp.float32
bf16 = jnp.bfloat16

D_MODEL = 1024
DEPTH = 2
GRID_W = 64
HEAD_DIM = 64
HALF = HEAD_DIM // 2
N_HEADS = 8
N_KV = 2
WINDOW = 128
N_FOURIER_GROUPS = 4
FOURIER_GROUP_DIM = 128
FOURIER_DIM = N_FOURIER_GROUPS * FOURIER_GROUP_DIM
ROPE_THETA = 10000.0
AXIS_FREQS = HEAD_DIM // 4
N_EXPERT_GROUPS = 4
EXPERTS_PER_GROUP = 8
N_EXPERTS = N_EXPERT_GROUPS * EXPERTS_PER_GROUP
EXPERT_FF = 128
NORM_EPS = 1e-6
NEG_INF = -1e30

LANES = 128
Q_W = N_HEADS * HEAD_DIM
KV_DUP_W = 2 * N_KV * HEAD_DIM
C_QA, C_KA, C_VA, C_QC, C_KC, C_VC, C_UB, C_GATES, C_END = 0, 512, 640, 768, 1280, 1408, 1536, 2048, 5120
N_COND_ROWS = 16
VMEM_LIMIT = 56 * 1024 * 1024


def _cparams(sem):
    return pltpu.CompilerParams(dimension_semantics=sem, vmem_limit_bytes=VMEM_LIMIT)


def _resident(shape):
    nd = len(shape)
    return pl.BlockSpec(shape, lambda *_: (0,) * nd, pipeline_mode=pl.Buffered(1))


def _sigmoid(x):
    return 0.5 * jnp.tanh(0.5 * x) + 0.5


def _rms_mod(x, g, scale, shift):
    y = x * lax.rsqrt(jnp.mean(x * x, axis=-1, keepdims=True) + NORM_EPS)
    return (y * g) * (1.0 + scale) + shift


def _dot(a, b):
    return jnp.dot(a, b, preferred_element_type=f32)


def _dot_t(a, b):
    return lax.dot_general(a, b, (((1,), (1,)), ((), ())), preferred_element_type=f32)


def _split_bf16(x):
    hi = x.astype(bf16)
    lo = (x - hi.astype(f32)).astype(bf16)
    return hi, lo


def _mod_kernel(cond_ref, w_ref, b_ref, o_ref):
    c = cond_ref[...]
    s = (c * _sigmoid(c)).astype(bf16)
    o_ref[0] = _dot(s, w_ref[0].astype(bf16)) + b_ref[0]


def _mod_call(cond, w_mod, b_mod):
    n_tiles = 6 * D_MODEL // 1024
    return pl.pallas_call(
        _mod_kernel,
        grid=(DEPTH, n_tiles),
        in_specs=[
            pl.BlockSpec((N_COND_ROWS, D_MODEL), lambda l, j: (0, 0)),
            pl.BlockSpec((1, D_MODEL, 1024), lambda l, j: (l, 0, j)),
            pl.BlockSpec((1, 1, 1024), lambda l, j: (l, 0, j)),
        ],
        out_specs=pl.BlockSpec((1, N_COND_ROWS, 1024), lambda l, j: (l, 0, j)),
        out_shape=jax.ShapeDtypeStruct((DEPTH, N_COND_ROWS, 6 * D_MODEL), f32),
        compiler_params=_cparams(("arbitrary", "arbitrary")),
        name="mod",
    )(cond, w_mod, b_mod.reshape(DEPTH, 1, 6 * D_MODEL))


CAST_BLOCK_BYTES = 4 * 1024 * 1024


def _cast_kernel(w_ref, o_ref):
    o_ref[...] = w_ref[0].astype(bf16)


def _layer_weight_bf16(w, l):
    _, rows, cols = w.shape
    rb = min(rows, 1 << ((CAST_BLOCK_BYTES // (4 * cols)).bit_length() - 1))
    assert rows % rb == 0
    return pl.pallas_call(
        _cast_kernel,
        grid=(rows // rb,),
        in_specs=[pl.BlockSpec((1, rb, cols), lambda i: (l, i, 0))],
        out_specs=pl.BlockSpec((rb, cols), lambda i: (i, 0)),
        out_shape=jax.ShapeDtypeStruct((rows, cols), bf16),
        compiler_params=_cparams(("arbitrary",)),
        name="cast_bf16",
    )(w)


def _head_norm(z, g, ones_blk):
    hi, lo = _split_bf16(z * z)
    ss = _dot(hi, ones_blk) + _dot(lo, ones_blk)
    return z * lax.rsqrt(ss * (1.0 / HEAD_DIM) + NORM_EPS) * g


def _rope(z, cos, sin_signed, first_half):
    outs = []
    for j in range(z.shape[1] // LANES):
        zj = z[:, j * LANES:(j + 1) * LANES]
        swapped = jnp.where(first_half, pltpu.roll(zj, LANES - HALF, 1), pltpu.roll(zj, HALF, 1))
        outs.append(zj * cos + swapped * sin_signed)
    return jnp.concatenate(outs, axis=1)


def _layer_input(x_refs, r0, n):
    rows = pl.ds(r0, n)
    if len(x_refs) == 1:
        return x_refs[0][rows, :]
    x1_ref, moe_ref, modp_ref = x_refs
    return x1_ref[rows, :] + modp_ref[0, 5:6, :] * _rows_from_slabs(moe_ref, r0 * SLAB_ROWS, n, SLAB_ROWS)


S1_SUB_TILES = 2
S3_SUB_TILES = 1


SLAB_ROWS = D_MODEL // LANES


def _rows_from_slabs(ref, start, n_tok, pitch):
    return jnp.concatenate([ref[pl.ds(start + c, n_tok, stride=pitch), :] for c in range(SLAB_ROWS)], axis=1)


def _rows_to_slabs(ref, start, x, pitch):
    for c in range(SLAB_ROWS):
        ref[pl.ds(start + c, x.shape[0], stride=pitch), :] = x[:, c * LANES:(c + 1) * LANES]


def _s1_kernel(*refs, rope, emit_kv, n_x):
    x_refs, refs = refs[:n_x], refs[n_x:]
    mod_ref, g1_ref, w_ref, ones_ref, qg_ref, kg_ref = refs[:6]
    pos = 6
    if rope:
        cos_ref, sin_ref = refs[pos:pos + 2]
        pos += 2
    qa_ref, ka_ref, va_ref, qc_ref, kc_ref, vc_ref, ub_ref = refs[pos:pos + 7]
    pos += 7
    if emit_kv:
        gkv_ref, wkv_ref = refs[pos:pos + 2]

    tm = x_refs[0].shape[0] // S1_SUB_TILES
    lane = lax.broadcasted_iota(jnp.int32, (tm, LANES), 1)
    low_lanes = lane < HEAD_DIM
    first_half = (lane & (HEAD_DIM - 1)) < HALF
    q_scale = HEAD_DIM ** -0.5

    def dup(z):
        r = pltpu.roll(z, HEAD_DIM, 1)
        return jnp.concatenate([jnp.where(low_lanes, z, r), jnp.where(low_lanes, r, z)], axis=1).astype(bf16)

    for sub in range(S1_SUB_TILES):
        rows = pl.ds(sub * tm, tm)
        x = _layer_input(x_refs, sub * tm, tm)
        h = _rms_mod(x, g1_ref[...], mod_ref[0, 1:2, :], mod_ref[0, 0:1, :]).astype(bf16)
        proj = lambda lo, hi: _dot(h, w_ref[:, lo:hi])
        if rope:
            cos = cos_ref[rows, :]
            sin_signed = sin_ref[rows, :]
            rot = lambda z: _rope(z, cos, sin_signed, first_half)
        else:
            rot = lambda z: z

        qa = rot(_head_norm(proj(C_QA, C_KA), qg_ref[...], ones_ref[...]))
        qa_ref[rows, :] = (qa * q_scale).astype(bf16)
        kva = proj(C_KA, C_QC)
        ka_n = _head_norm(kva[:, 0:LANES], kg_ref[...], ones_ref[0:LANES, 0:LANES])
        ka_ref[rows, :] = dup(rot(ka_n))
        va = kva[:, LANES:2 * LANES]
        va_ref[rows, :] = dup(va)
        qc_ref[rows, :] = (rot(proj(C_QC, C_KC)) * q_scale).astype(bf16)
        kvc = proj(C_KC, C_UB)
        kc = kvc[:, 0:LANES]
        kc_ref[rows, :] = dup(rot(kc))
        vc = kvc[:, LANES:2 * LANES]
        vc_ref[rows, :] = dup(vc)
        ub_ref[rows, :] = proj(C_UB, C_GATES).astype(bf16)
        if emit_kv:
            gkv_ref[sub, 0] = ka_n
            gkv_ref[sub, 1] = va
            wkv_ref[sub, 0] = kc
            wkv_ref[sub, 1] = vc


TOKEN_TILE = 512


def _cond_index_map(tokens_per_batch, cond_row0, per_batch_cond):
    tiles_per_batch = max(tokens_per_batch // TOKEN_TILE, 1)
    if per_batch_cond:
        assert tokens_per_batch % TOKEN_TILE == 0
        return lambda t: (cond_row0 + t // tiles_per_batch, 0, 0)
    return lambda t: (cond_row0, 0, 0)


def _layer_input_specs(xin, mod_prev, cond_idx):
    tok = pl.BlockSpec((TOKEN_TILE, D_MODEL), lambda t: (t, 0))
    if len(xin) == 1:
        return [tok], list(xin)
    slab = pl.BlockSpec((TOKEN_TILE * SLAB_ROWS, LANES), lambda t: (t, 0))
    return [tok, slab, pl.BlockSpec((1, 6, D_MODEL), cond_idx)], [xin[0], xin[1], mod_prev]


def _s1_call(xin, mod_prev, mod, g1, w_ext, ones_blk, qg, kg, rope_tabs, *, tokens_per_batch, cond_row0,
             per_batch_cond, emit_kv):
    n_tok = xin[0].shape[0]
    tm = TOKEN_TILE
    rope = rope_tabs is not None
    tiles_per_batch = max(tokens_per_batch // tm, 1)
    cond_idx = _cond_index_map(tokens_per_batch, cond_row0, per_batch_cond)

    tok = lambda w: pl.BlockSpec((tm, w), lambda t: (t, 0))
    x_specs, x_args = _layer_input_specs(xin, mod_prev, cond_idx)
    in_specs = x_specs + [
        pl.BlockSpec((1, 6, D_MODEL), cond_idx),
        _resident((1, D_MODEL)),
        _resident((D_MODEL, C_GATES)),
        _resident((Q_W, Q_W)),
        _resident((1, Q_W)),
        _resident((1, LANES)),
    ]
    args = x_args + [mod, g1, w_ext, ones_blk, qg, kg]
    if rope:
        in_specs += [pl.BlockSpec((tm, LANES), lambda t: (t % tiles_per_batch, 0))] * 2
        args += list(rope_tabs)
    widths = [Q_W, KV_DUP_W, KV_DUP_W, Q_W, KV_DUP_W, KV_DUP_W, FOURIER_DIM]
    out_specs = [tok(w) for w in widths]
    out_shape = [jax.ShapeDtypeStruct((n_tok, w), bf16) for w in widths]
    if emit_kv:
        assert tokens_per_batch * S1_SUB_TILES == tm
        n_batch = n_tok // tokens_per_batch
        out_specs += [pl.BlockSpec((S1_SUB_TILES, 2, tokens_per_batch, LANES), lambda t: (t, 0, 0, 0))] * 2
        out_shape += [jax.ShapeDtypeStruct((n_batch, 2, tokens_per_batch, LANES), f32)] * 2
    return pl.pallas_call(
        functools.partial(_s1_kernel, rope=rope, emit_kv=emit_kv, n_x=len(x_args)),
        grid=(n_tok // tm,),
        in_specs=in_specs,
        out_specs=out_specs,
        out_shape=out_shape,
        compiler_params=_cparams(("arbitrary",)),
        name="s1_rope" if rope else "s1_ctx",
    )(*args)


def _gqa_group(q2, segs, sink_pair_lo, sink_pair_hi):
    qb = q2.shape[0]
    qs = jnp.concatenate([q2[:, 0:LANES], q2[:, LANES:2 * LANES]], axis=0)
    lane = lax.broadcasted_iota(jnp.int32, qs.shape, 1)
    low = lane < HEAD_DIM
    zero = jnp.zeros_like(qs)
    q4 = jnp.concatenate([jnp.where(low, qs, zero), jnp.where(low, zero, qs)], axis=0)

    scores = []
    for k, _, ok in segs:
        s = _dot_t(q4, k)
        if ok is not None:
            s = jnp.where(ok, s, NEG_INF)
        scores.append(s)
    m = scores[0].max(axis=-1, keepdims=True)
    for s in scores[1:]:
        m = jnp.maximum(m, s.max(axis=-1, keepdims=True))
    if sink_pair_lo is not None:
        row = lax.broadcasted_iota(jnp.int32, (4 * qb, 1), 0)
        sink = jnp.where(row < 2 * qb, jnp.where(row < qb, sink_pair_lo[0], sink_pair_lo[1]),
                         jnp.where(row < 3 * qb, sink_pair_hi[0], sink_pair_hi[1]))
        m = jnp.maximum(m, sink)
        denom = jnp.exp(sink - m)
    else:
        denom = jnp.zeros_like(m)
    acc = None
    for s, (_, v, _) in zip(scores, segs):
        p = jnp.exp(s - m)
        denom = denom + p.sum(axis=-1, keepdims=True)
        pv = _dot(p.astype(bf16), v)
        acc = pv if acc is None else acc + pv
    o4 = acc / denom
    o = jnp.where(low, o4[0:2 * qb], o4[2 * qb:4 * qb])
    return jnp.concatenate([o[0:qb], o[qb:2 * qb]], axis=1)


def _sink_pairs(sink_ref, kv):
    h = 4 * kv
    return (sink_ref[h], sink_ref[h + 2]), (sink_ref[h + 1], sink_ref[h + 3])


def _attn_ctx_kernel(sink_ref, qa_ref, ka_ref, va_ref, qc_ref, kc_ref, vc_ref, oa_ref, oc_ref):
    for kv in range(N_KV):
        cq = slice(2 * LANES * kv, 2 * LANES * (kv + 1))
        ck = slice(LANES * kv, LANES * (kv + 1))
        oa = _gqa_group(qa_ref[:, cq], [(ka_ref[:, ck], va_ref[:, ck], None)], None, None)
        oa_ref[:, cq] = oa.astype(bf16)
        lo, hi = _sink_pairs(sink_ref, kv)
        oc = _gqa_group(qc_ref[:, cq], [(kc_ref[:, ck], vc_ref[:, ck], None)], lo, hi)
        oc_ref[:, cq] = oc.astype(bf16)


def _attn_ctx_call(sink, qa, ka, va, qc, kc, vc, seq):
    n_tok = qa.shape[0]
    tok = lambda w: pl.BlockSpec((seq, w), lambda b: (b, 0))
    return pl.pallas_call(
        _attn_ctx_kernel,
        grid=(n_tok // seq,),
        in_specs=[pl.BlockSpec(memory_space=pltpu.SMEM),
                  tok(Q_W), tok(KV_DUP_W), tok(KV_DUP_W), tok(Q_W), tok(KV_DUP_W), tok(KV_DUP_W)],
        out_specs=[tok(Q_W), tok(Q_W)],
        out_shape=[jax.ShapeDtypeStruct((n_tok, Q_W), bf16)] * 2,
        compiler_params=_cparams(("arbitrary",)),
        name="attn_ctx",
    )(sink, qa, ka, va, qc, kc, vc)


QB = 256
WIN_QB = 128
WIN_SPAN = WIN_QB + 2 * WINDOW


def _attn_lat_kernel(sink_ref, qa_ref, ka_ref, va_ref, kxa_ref, vxa_ref,
                     qc_ref, kc_ref, vc_ref, kxc_ref, vxc_ref, oa_ref, oc_ref, *, seq):
    i = pl.program_id(1)
    for kv in range(N_KV):
        cq = slice(2 * LANES * kv, 2 * LANES * (kv + 1))
        ck = slice(LANES * kv, LANES * (kv + 1))
        segs = [(ka_ref[:, ck], va_ref[:, ck], None), (kxa_ref[0, :, ck], vxa_ref[0, :, ck], None)]
        oa_ref[:, cq] = _gqa_group(qa_ref[:, cq], segs, None, None).astype(bf16)

        lo, hi = _sink_pairs(sink_ref, kv)
        for sub in range(QB // WIN_QB):
            blk = i * (QB // WIN_QB) + sub
            start = jnp.clip((blk - 1) * WIN_QB, 0, seq - WIN_SPAN)
            start = pl.multiple_of(start, WIN_QB)
            r = lax.broadcasted_iota(jnp.int32, (4 * WIN_QB, WIN_SPAN), 0)
            c = lax.broadcasted_iota(jnp.int32, (4 * WIN_QB, WIN_SPAN), 1)
            qpos = blk * WIN_QB + (r & (WIN_QB - 1))
            ok = jnp.abs(qpos - (start + c)) <= WINDOW
            rows = slice(sub * WIN_QB, (sub + 1) * WIN_QB)
            segs = [(kc_ref[pl.ds(start, WIN_SPAN), ck], vc_ref[pl.ds(start, WIN_SPAN), ck], ok),
                    (kxc_ref[0, :, ck], vxc_ref[0, :, ck], None)]
            oc_ref[rows, cq] = _gqa_group(qc_ref[rows, cq], segs, lo, hi).astype(bf16)


def _attn_lat_call(sink, qa, ka, va, kxa, vxa, qc, kc, vc, kxc, vxc, seq):
    n_tok = qa.shape[0]
    nq = seq // QB
    past = kxa.shape[1]
    qspec = pl.BlockSpec((QB, Q_W), lambda b, i: (b * nq + i, 0))
    kspec = pl.BlockSpec((seq, KV_DUP_W), lambda b, i: (b, 0))
    xspec = pl.BlockSpec((1, past, KV_DUP_W), lambda b, i: (b, 0, 0))
    return pl.pallas_call(
        functools.partial(_attn_lat_kernel, seq=seq),
        grid=(n_tok // seq, nq),
        in_specs=[pl.BlockSpec(memory_space=pltpu.SMEM),
                  qspec, kspec, kspec, xspec, xspec, qspec, kspec, kspec, xspec, xspec],
        out_specs=[qspec, qspec],
        out_shape=[jax.ShapeDtypeStruct((n_tok, Q_W), bf16)] * 2,
        compiler_params=_cparams(("arbitrary", "arbitrary")),
        name="attn_lat",
    )(sink, qa, ka, va, kxa, vxa, qc, kc, vc, kxc, vxc)


def _fourier_kernel(u_ref, bdc_ref, bds_ref, cn_ref, sn_ref, o_ref, *, scale):
    u = u_ref[...]
    uc = _dot(u, bdc_ref[...]).astype(bf16)
    us = _dot(u, bds_ref[...]).astype(bf16)
    o = _dot(cn_ref[...], uc) - _dot(sn_ref[...], us)
    o_ref[...] = (o * scale).astype(bf16)


def _fourier_call(ub, bdc, bds, cn, sn, seq):
    n_tok = ub.shape[0]
    tok = pl.BlockSpec((seq, FOURIER_DIM), lambda b: (b, 0))
    scale = float(1.0 / np.sqrt(seq * FOURIER_GROUP_DIM))
    return pl.pallas_call(
        functools.partial(_fourier_kernel, scale=scale),
        grid=(n_tok // seq,),
        in_specs=[tok, _resident((FOURIER_DIM, FOURIER_DIM)), _resident((FOURIER_DIM, FOURIER_DIM)),
                  _resident((seq, seq)), _resident((seq, seq))],
        out_specs=tok,
        out_shape=jax.ShapeDtypeStruct((n_tok, FOURIER_DIM), bf16),
        compiler_params=_cparams(("arbitrary",)),
        name=f"fourier_{seq}",
    )(ub, bdc, bds, cn, sn)


ROUTE_ROWS = LANES
H2X_PITCH = SLAB_ROWS + 1
MOE_PITCH = SLAB_ROWS


def _route(lt, tri_ref, carry_ref, n_tok):
    tm = lt.shape[1]
    row = lax.broadcasted_iota(jnp.int32, (EXPERTS_PER_GROUP, tm), 0).astype(f32)
    neg = jnp.float32(-jnp.inf)
    none = jnp.float32(EXPERTS_PER_GROUP)
    gl = jnp.where(row < N_EXPERT_GROUPS, lt[N_EXPERTS:N_EXPERTS + EXPERTS_PER_GROUP], neg)
    gmax = gl.max(axis=0, keepdims=True)
    gidx = jnp.where(gl == gmax, row, none).min(axis=0, keepdims=True)
    gw = 1.0 / jnp.exp(gl - gmax).sum(axis=0, keepdims=True)
    lg = lt[0:EXPERTS_PER_GROUP]
    for g in range(1, N_EXPERT_GROUPS):
        lg = jnp.where(gidx == g, lt[g * EXPERTS_PER_GROUP:(g + 1) * EXPERTS_PER_GROUP], lg)
    m1 = lg.max(axis=0, keepdims=True)
    i1 = jnp.where(lg == m1, row, none).min(axis=0, keepdims=True)
    l2 = jnp.where(row == i1, neg, lg)
    m2 = l2.max(axis=0, keepdims=True)
    i2 = jnp.where(l2 == m2, row, none).min(axis=0, keepdims=True)
    e = jnp.exp(m2 - m1)
    w8 = jnp.where(row == i1, gw / (1.0 + e), jnp.where(row == i2, gw * e / (1.0 + e), 0.0))

    onehot = (row == gidx).astype(f32)
    prefix = _dot(onehot.astype(bf16), tri_ref[0:tm, 0:tm])
    carry = carry_ref[...]
    rank = (onehot * (prefix - 1.0 + carry[:, 0:1])).sum(axis=0, keepdims=True)
    carry_ref[...] = carry + onehot.sum(axis=1, keepdims=True)
    dest = gidx.astype(jnp.int32) * n_tok + rank.astype(jnp.int32)
    return w8, dest


def _s3_kernel(*refs, n_x, n_tok):
    x_refs, refs = refs[:n_x], refs[n_x:]
    (oa_ref, ob_ref, oc_ref, mod_ref, g1_ref, g2_ref, wga_ref, wgb_ref, wgc_ref, wba_ref, wbb_ref, wbc_ref,
     wout_ref, wrt_ref, tri_ref, x1_ref, h2x_ref, code_ref, cnt_ref, carry_ref) = refs
    wg_refs = (wga_ref, wgb_ref, wgc_ref)

    @pl.when(pl.program_id(0) == 0)
    def _():
        carry_ref[...] = jnp.zeros_like(carry_ref)

    tm = x1_ref.shape[0] // S3_SUB_TILES
    wh, wl = _split_bf16(wrt_ref[...])
    for sub in range(S3_SUB_TILES):
        r0 = sub * tm
        rows = pl.ds(r0, tm)
        x = _layer_input(x_refs, r0, tm)
        h = _rms_mod(x, g1_ref[...], mod_ref[0, 1:2, :], mod_ref[0, 0:1, :]).astype(bf16)
        merged = None
        for j, (o_ref, wb_ref) in enumerate(((oa_ref, wba_ref), (ob_ref, wbb_ref), (oc_ref, wbc_ref))):
            gate = _sigmoid(_dot(h, wg_refs[j][...]))
            term = gate * _dot(o_ref[rows, :], wb_ref[...])
            merged = term if merged is None else merged + term
        mix = _dot(merged.astype(bf16), wout_ref[...])
        x1 = x + mod_ref[0, 2:3, :] * mix
        x1_ref[rows, :] = x1
        h2 = _rms_mod(x1, g2_ref[...], mod_ref[0, 4:5, :], mod_ref[0, 3:4, :])
        _rows_to_slabs(h2x_ref, r0 * H2X_PITCH, h2, H2X_PITCH)
        th, tl = _split_bf16(h2)
        lt = _dot_t(wh, th) + (_dot_t(wh, tl) + _dot_t(wl, th))
        w8, code = _route(lt, tri_ref, carry_ref, n_tok)
        code_ref[:, rows] = code
        w_rows = jnp.concatenate([w8, jnp.zeros((LANES - EXPERTS_PER_GROUP, tm), f32)], axis=0)
        h2x_ref[pl.ds(r0 * H2X_PITCH + SLAB_ROWS, tm, stride=H2X_PITCH), :] = w_rows.T
    cnt_ref[...] = carry_ref[...]


def _s3_call(xin, mod_prev, oa, ob, oc, mod, g1, g2, wg, wba, wbb, wbc, wout, wrt, tri, *, tokens_per_batch,
             cond_row0, per_batch_cond):
    n_tok = xin[0].shape[0]
    tm = TOKEN_TILE
    cond_idx = _cond_index_map(tokens_per_batch, cond_row0, per_batch_cond)
    tok = lambda w: pl.BlockSpec((tm, w), lambda t: (t, 0))
    x_specs, x_args = _layer_input_specs(xin, mod_prev, cond_idx)
    gate_cols = lambda j: pl.BlockSpec((D_MODEL, D_MODEL), lambda t: (0, C_GATES // D_MODEL + j),
                                       pipeline_mode=pl.Buffered(1))
    return pl.pallas_call(
        functools.partial(_s3_kernel, n_x=len(x_args), n_tok=n_tok),
        grid=(n_tok // tm,),
        in_specs=x_specs + [
            tok(Q_W), tok(FOURIER_DIM), tok(Q_W),
            pl.BlockSpec((1, 6, D_MODEL), cond_idx),
            _resident((1, D_MODEL)), _resident((1, D_MODEL)),
            gate_cols(0), gate_cols(1), gate_cols(2),
            _resident((Q_W, D_MODEL)), _resident((FOURIER_DIM, D_MODEL)), _resident((Q_W, D_MODEL)),
            _resident((D_MODEL, D_MODEL)), _resident((ROUTE_ROWS, D_MODEL)), _resident((tm, tm))],
        out_specs=[tok(D_MODEL), pl.BlockSpec((tm * H2X_PITCH, LANES), lambda t: (t, 0)),
                   pl.BlockSpec((1, tm), lambda t: (0, t)),
                   pl.BlockSpec((EXPERTS_PER_GROUP, LANES), lambda t: (0, 0))],
        out_shape=[jax.ShapeDtypeStruct((n_tok, D_MODEL), f32),
                   jax.ShapeDtypeStruct((n_tok * H2X_PITCH, LANES), f32),
                   jax.ShapeDtypeStruct((1, n_tok), jnp.int32),
                   jax.ShapeDtypeStruct((EXPERTS_PER_GROUP, LANES), f32)],
        scratch_shapes=[pltpu.VMEM((EXPERTS_PER_GROUP, LANES), f32)],
        compiler_params=_cparams(("arbitrary",)),
        name="s3",
    )(*x_args, oa, ob, oc, mod, g1, g2, wg, wg, wg, wba, wbb, wbc, wout, wrt, tri)


MOE_TILE_LOG2 = 8
MOE_TILE = 1 << MOE_TILE_LOG2
ROW_UNROLL_LOG2 = 3
ROW_UNROLL = 1 << ROW_UNROLL_LOG2
GROUP_FF = EXPERTS_PER_GROUP * EXPERT_FF
PLAN_FIRST, PLAN_ITEMS, PLAN_LEN = 0, N_EXPERT_GROUPS, N_EXPERT_GROUPS + 1


def _moe_kernel(code_ref, cnt_ref, h2x_hbm, weg_ref, weu_ref, wed_ref, out_hbm,
                pos_ref, plan_ref, gbuf, obuf, dump, gsem, ssem, *, n_tok):
    tm = MOE_TILE
    k = pl.program_id(0)
    n_steps = pl.num_programs(0)

    def out_slab(buf, slot, i):
        return buf.at[slot, :, i]

    def token_slab(tok):
        return out_hbm.at[pl.ds(pl.multiple_of(tok * MOE_PITCH, MOE_PITCH), MOE_PITCH)]

    def for_row_chunks(n_chunks, body):
        def chunk(c, carry):
            body(c * ROW_UNROLL)
            return carry
        lax.fori_loop(0, n_chunks, chunk, 0)

    @pl.when(k == 0)
    def _plan():
        first = jnp.int32(0)
        for g in range(N_EXPERT_GROUPS):
            c = cnt_ref[g]
            n_tiles = (c + (tm - 1)) >> MOE_TILE_LOG2
            plan_ref[PLAN_FIRST + g] = first

            def pad(p, carry):
                pos_ref[p] = 0
                return carry

            lax.fori_loop(g * n_tok + c, g * n_tok + n_tiles * tm, pad, 0)
            first = first + n_tiles
        plan_ref[PLAN_ITEMS] = first

        def place(t0):
            dests = [code_ref[t0 + j] for j in range(ROW_UNROLL)]
            for j, dest in enumerate(dests):
                pos_ref[dest] = t0 + j

        for_row_chunks(n_tok >> ROW_UNROLL_LOG2, place)

    n_items = plan_ref[PLAN_ITEMS]

    def item_group(item):
        return ((item >= plan_ref[PLAN_FIRST + 1]).astype(jnp.int32)
                + (item >= plan_ref[PLAN_FIRST + 2]).astype(jnp.int32)
                + (item >= plan_ref[PLAN_FIRST + 3]).astype(jnp.int32))

    def item_pos0(item):
        g = item_group(item)
        return g * n_tok + (item - plan_ref[PLAN_FIRST + g]) * tm

    def start_gather(item, slot):
        pos0 = item_pos0(item)

        def chunk(i0):
            toks = [pos_ref[pos0 + i0 + j] for j in range(ROW_UNROLL)]
            for j, tok in enumerate(toks):
                pltpu.make_async_copy(h2x_hbm.at[pl.ds(tok * H2X_PITCH, H2X_PITCH)],
                                      gbuf.at[slot, :, i0 + j],
                                      gsem.at[slot]).start(priority=j % 2)
        for_row_chunks(tm >> ROW_UNROLL_LOG2, chunk)

    def wait_gather(slot):
        pltpu.make_async_copy(gbuf.at[1 - slot], gbuf.at[slot], gsem.at[slot]).wait()

    def start_scatter(item, n_valid, slot):
        pos0 = item_pos0(item)

        def to_token(i0):
            dsts = [pos_ref[pos0 + i0 + j] for j in range(ROW_UNROLL)]
            for j, dst in enumerate(dsts):
                pltpu.make_async_copy(out_slab(obuf, slot, i0 + j), token_slab(dst),
                                      ssem.at[slot]).start(priority=j % 2)

        def row_to_token(i, carry):
            pltpu.make_async_copy(out_slab(obuf, slot, i), token_slab(pos_ref[pos0 + i]), ssem.at[slot]).start()
            return carry

        def row_to_dump(i, carry):
            pltpu.make_async_copy(out_slab(obuf, slot, i), out_slab(dump, slot, i), ssem.at[slot]).start()
            return carry

        full = n_valid >> ROW_UNROLL_LOG2
        for_row_chunks(full, to_token)
        lax.fori_loop(full * ROW_UNROLL, n_valid, row_to_token, 0)
        lax.fori_loop(n_valid, tm, row_to_dump, 0)

    def wait_scatter(slot):
        pltpu.make_async_copy(obuf.at[slot], dump.at[slot], ssem.at[slot]).wait()

    slot = k % 2

    @pl.when(k == 0)
    def _():
        start_gather(0, 0)

    @pl.when(k + 1 < n_items)
    def _():
        start_gather(k + 1, 1 - slot)

    @pl.when(k < n_items)
    def _():
        group = item_group(k)
        wait_gather(slot)
        h = jnp.concatenate([gbuf[slot, c] for c in range(SLAB_ROWS)], axis=1).astype(bf16)
        w8 = gbuf[slot, SLAB_ROWS]
        cols = pl.ds(pl.multiple_of(group * GROUP_FF, GROUP_FF), GROUP_FF)
        gate = _dot(h, weg_ref[:, cols])
        a = gate * _sigmoid(gate) * _dot(h, weu_ref[:, cols])
        pieces = [a[:, j * EXPERT_FF:(j + 1) * EXPERT_FF] * w8[:, j:j + 1] for j in range(EXPERTS_PER_GROUP)]
        out = _dot(jnp.concatenate(pieces, axis=1).astype(bf16), wed_ref[group])

        @pl.when(k >= 2)
        def _():
            wait_scatter(slot)

        for c in range(SLAB_ROWS):
            obuf[slot, c] = out[:, c * LANES:(c + 1) * LANES]
        n_valid = jnp.minimum(tm, cnt_ref[group] - (k - plan_ref[PLAN_FIRST + group]) * tm)
        start_scatter(k, n_valid, slot)

    @pl.when(k == n_steps - 1)
    def _drain():
        wait_scatter((n_items - 1) % 2)

        @pl.when(n_items >= 2)
        def _():
            wait_scatter(n_items % 2)


def _moe_call(code, cnt, h2x, weg, weu, wed):
    n_tok = h2x.shape[0] // H2X_PITCH
    tm = MOE_TILE
    n_steps = n_tok // tm + N_EXPERT_GROUPS - 1
    ff = N_EXPERT_GROUPS * GROUP_FF
    up_spec = pl.BlockSpec((D_MODEL, ff), lambda k, *_: (0, 0), pipeline_mode=pl.Buffered(1))
    down_spec = pl.BlockSpec((N_EXPERT_GROUPS, GROUP_FF, D_MODEL), lambda k, *_: (0, 0, 0),
                             pipeline_mode=pl.Buffered(1))
    grid_spec = pltpu.PrefetchScalarGridSpec(
        num_scalar_prefetch=2,
        grid=(n_steps,),
        in_specs=[pl.BlockSpec(memory_space=pl.ANY), up_spec, up_spec, down_spec],
        out_specs=pl.BlockSpec(memory_space=pl.ANY),
        scratch_shapes=[
            pltpu.SMEM((N_EXPERT_GROUPS * n_tok,), jnp.int32),
            pltpu.SMEM((PLAN_LEN,), jnp.int32),
            pltpu.VMEM((2, H2X_PITCH, tm, LANES), f32),
            pltpu.VMEM((2, MOE_PITCH, tm, LANES), f32),
            pltpu.VMEM((2, MOE_PITCH, tm, LANES), f32),
            pltpu.SemaphoreType.DMA((2,)),
            pltpu.SemaphoreType.DMA((2,)),
        ],
    )
    return pl.pallas_call(
        functools.partial(_moe_kernel, n_tok=n_tok),
        grid_spec=grid_spec,
        out_shape=jax.ShapeDtypeStruct((n_tok * MOE_PITCH, LANES), f32),
        compiler_params=_cparams(("arbitrary",)),
        name="moe",
    )(code, cnt, h2x, weg, weu, wed)


def _final_kernel(x1_ref, moe_ref, mod_ref, gf_ref, o_ref):
    x2 = x1_ref[...] + mod_ref[0, 5:6, :] * _rows_from_slabs(moe_ref, 0, x1_ref.shape[0], MOE_PITCH)
    y = x2 * lax.rsqrt(jnp.mean(x2 * x2, axis=-1, keepdims=True) + NORM_EPS)
    o_ref[...] = y * gf_ref[...]


def _final_call(x1, moe, mod, gf, *, tokens_per_batch, cond_row0, per_batch_cond):
    n_tok = x1.shape[0]
    tok = pl.BlockSpec((TOKEN_TILE, D_MODEL), lambda t: (t, 0))
    slab = pl.BlockSpec((TOKEN_TILE * MOE_PITCH, LANES), lambda t: (t, 0))
    cond_idx = _cond_index_map(tokens_per_batch, cond_row0, per_batch_cond)
    return pl.pallas_call(
        _final_kernel,
        grid=(n_tok // TOKEN_TILE,),
        in_specs=[tok, slab, pl.BlockSpec((1, 6, D_MODEL), cond_idx), _resident((1, D_MODEL))],
        out_specs=tok,
        out_shape=jax.ShapeDtypeStruct((n_tok, D_MODEL), f32),
        compiler_params=_cparams(("arbitrary",)),
        name="final_norm",
    )(x1, moe, mod, gf)


def _rope_tables(n):
    rows = n // GRID_W
    row = jnp.repeat(jnp.arange(rows, dtype=f32), GRID_W)
    col = jnp.tile(jnp.arange(GRID_W, dtype=f32), rows)
    inv = ROPE_THETA ** (-jnp.arange(AXIS_FREQS, dtype=f32) / AXIS_FREQS)
    ang = jnp.concatenate([row[:, None] * inv, col[:, None] * inv], axis=-1)
    cos, sin = jnp.cos(ang), jnp.sin(ang)
    cos_t = jnp.concatenate([cos, cos, cos, cos], axis=-1)
    sin_t = jnp.concatenate([-sin, sin, -sin, sin], axis=-1)
    return cos_t, sin_t


def _dft_tables(n):
    k = np.arange(n)
    m = (k[:, None] * k[None, :]) % n
    ang = 2.0 * np.pi * m / n
    return np.cos(ang), np.sin(ang)


def _fourier_tables(seq):
    cn, sn = _dft_tables(seq)
    cc, sc = _dft_tables(FOURIER_GROUP_DIM)
    eye = np.eye(N_FOURIER_GROUPS)
    bdc, bds = np.kron(eye, cc), np.kron(eye, sc)
    return tuple(jnp.asarray(t, dtype=f32).astype(bf16) for t in (bdc, bds, cn, sn))


def _dup_cache(cache, l):
    b, _, _, p, _, _ = cache.shape
    dup = lambda x: jnp.broadcast_to(x[:, :, :, None, :], (b, p, N_KV, 2, HEAD_DIM)).reshape(b, p, KV_DUP_W)
    return dup(cache[:, l, 0]).astype(bf16), dup(cache[:, l, 1]).astype(bf16)


def kernel(x_prompt, x_sample, cache_global_kv, cache_window_kv, c, c_ctx, w_mod, b_mod, norm1_g, w_in,
           q_norm_g, k_norm_g, sink_logit, w_branch_a, w_branch_b, w_branch_c, w_out, norm2_g,
           w_route_group, w_route_expert, w_exp_gate, w_exp_up, w_exp_down, final_norm_g):
    batch, seq, _ = x_prompt.shape
    dec_batch, dec_seq, _ = x_sample.shape

    cond = jnp.zeros((N_COND_ROWS, D_MODEL), f32).at[0].set(c_ctx).at[1:1 + dec_batch].set(c)
    mod = _mod_call(cond, w_mod, b_mod).reshape(DEPTH, N_COND_ROWS, 6, D_MODEL)

    ones_blk = jnp.asarray(np.kron(np.eye(N_HEADS), np.ones((HEAD_DIM, HEAD_DIM))), dtype=bf16)
    tri = jnp.asarray(np.triu(np.ones((TOKEN_TILE, TOKEN_TILE))), dtype=bf16)
    rope_tabs = _rope_tables(dec_seq)
    ftab_ctx = _fourier_tables(seq)
    ftab_lat = _fourier_tables(dec_seq)
    gf = final_norm_g.reshape(1, D_MODEL)

    ctx = dict(tokens_per_batch=seq, cond_row0=0, per_batch_cond=False)
    lat = dict(tokens_per_batch=dec_seq, cond_row0=1, per_batch_cond=True)
    xp = (x_prompt.reshape(batch * seq, D_MODEL),)
    xs = (x_sample.reshape(dec_batch * dec_seq, D_MODEL),)
    mod_prev = None
    gkv, wkv = [], []
    for l in range(DEPTH):
        w_ext = w_gate = _layer_weight_bf16(w_in, l)
        g1 = norm1_g[l].reshape(1, D_MODEL)
        g2 = norm2_g[l].reshape(1, D_MODEL)
        qg = jnp.tile(q_norm_g[l], N_HEADS).reshape(1, Q_W)
        kg = jnp.tile(k_norm_g[l], N_KV).reshape(1, LANES)
        wba, wbb, wbc = (_layer_weight_bf16(w, l) for w in (w_branch_a, w_branch_b, w_branch_c))
        wout = _layer_weight_bf16(w_out, l)
        wrt = jnp.concatenate([w_route_expert[l].T, w_route_group[l].T,
                               jnp.zeros((ROUTE_ROWS - N_EXPERTS - N_EXPERT_GROUPS, D_MODEL), f32)], axis=0)
        weg, weu = _layer_weight_bf16(w_exp_gate, l), _layer_weight_bf16(w_exp_up, l)
        wed = _layer_weight_bf16(w_exp_down, l).reshape(N_EXPERT_GROUPS, GROUP_FF, D_MODEL)
        sink = sink_logit[l]

        def mix_and_moe(xin, qa, ka, va, qc, kc, vc, ub, attn, ftab, seq_len, stream):
            oa, oc = attn(qa, ka, va, qc, kc, vc)
            ob = _fourier_call(ub, *ftab, seq_len)
            x1, h2x, code, cnt = _s3_call(xin, mod_prev, oa, ob, oc, mod[l], g1, g2, w_gate, wba, wbb, wbc, wout,
                                          wrt, tri, **stream)
            counts = cnt[:, 0].astype(jnp.int32)
            moe = _moe_call(code.reshape(-1), counts, h2x, weg, weu, wed)
            return (x1, moe)

        qa, ka, va, qc, kc, vc, ub, gkv_l, wkv_l = _s1_call(
            xp, mod_prev, mod[l], g1, w_ext, ones_blk, qg, kg, None, emit_kv=True, **ctx)
        attn_ctx = lambda *qkv: _attn_ctx_call(sink, *qkv, seq)
        xp = mix_and_moe(xp, qa, ka, va, qc, kc, vc, ub, attn_ctx, ftab_ctx, seq, ctx)
        kv_shape = (batch, 2, seq, N_KV, HEAD_DIM)
        gkv.append(gkv_l.reshape(kv_shape))
        wkv.append(wkv_l.reshape(kv_shape))

        qa, ka, va, qc, kc, vc, ub = _s1_call(
            xs, mod_prev, mod[l], g1, w_ext, ones_blk, qg, kg, rope_tabs, emit_kv=False, **lat)
        kxa, vxa = _dup_cache(cache_global_kv, l)
        kxc, vxc = _dup_cache(cache_window_kv, l)
        attn_lat = lambda qa, ka, va, qc, kc, vc: _attn_lat_call(
            sink, qa, ka, va, kxa, vxa, qc, kc, vc, kxc, vxc, dec_seq)
        xs = mix_and_moe(xs, qa, ka, va, qc, kc, vc, ub, attn_lat, ftab_lat, dec_seq, lat)
        mod_prev = mod[l]

    y_prompt = _final_call(*xp, mod_prev, gf, **ctx).reshape(batch, seq, D_MODEL)
    y_sample = _final_call(*xs, mod_prev, gf, **lat).reshape(dec_batch, dec_seq, D_MODEL)
    return (y_prompt, y_sample, jnp.stack(gkv, axis=1), jnp.stack(wkv, axis=1))
```

```python
import functools

import jax
import jax.numpy as jnp
import numpy as np
from jax import lax
from jax.experimental import pallas as pl
from jax.experimental.pallas import tpu as pltpu

f32 = jnp.float32
bf16 = jnp.bfloat16

D_MODEL = 1024
DEPTH = 2
GRID_W = 64
HEAD_DIM = 64
HALF = HEAD_DIM // 2
N_HEADS = 8
N_KV = 2
WINDOW = 128
N_FOURIER_GROUPS = 4
FOURIER_GROUP_DIM = 128
FOURIER_DIM = N_FOURIER_GROUPS * FOURIER_GROUP_DIM
ROPE_THETA = 10000.0
AXIS_FREQS = HEAD_DIM // 4
N_EXPERT_GROUPS = 4
EXPERTS_PER_GROUP = 8
N_EXPERTS = N_EXPERT_GROUPS * EXPERTS_PER_GROUP
EXPERT_FF = 128
NORM_EPS = 1e-6
NEG_INF = -1e30

LANES = 128
Q_W = N_HEADS * HEAD_DIM
KV_DUP_W = 2 * N_KV * HEAD_DIM
C_QA, C_KA, C_VA, C_QC, C_KC, C_VC, C_UB, C_GATES, C_END = 0, 512, 640, 768, 1280, 1408, 1536, 2048, 5120
N_COND_ROWS = 16
VMEM_LIMIT = 56 * 1024 * 1024


def _cparams(sem):
    return pltpu.CompilerParams(dimension_semantics=sem, vmem_limit_bytes=VMEM_LIMIT)


def _resident(shape):
    nd = len(shape)
    return pl.BlockSpec(shape, lambda *_: (0,) * nd, pipeline_mode=pl.Buffered(1))


def _sigmoid(x):
    return 0.5 * jnp.tanh(0.5 * x) + 0.5


def _rms_mod(x, g, scale, shift):
    y = x * lax.rsqrt(jnp.mean(x * x, axis=-1, keepdims=True) + NORM_EPS)
    return (y * g) * (1.0 + scale) + shift


def _dot(a, b):
    return jnp.dot(a, b, preferred_element_type=f32)


def _dot_t(a, b):
    return lax.dot_general(a, b, (((1,), (1,)), ((), ())), preferred_element_type=f32)


def _split_bf16(x):
    hi = x.astype(bf16)
    lo = (x - hi.astype(f32)).astype(bf16)
    return hi, lo


def _mod_kernel(cond_ref, w_ref, b_ref, o_ref):
    c = cond_ref[...]
    s = (c * _sigmoid(c)).astype(bf16)
    o_ref[0] = _dot(s, w_ref[0].astype(bf16)) + b_ref[0]


def _mod_call(cond, w_mod, b_mod):
    n_tiles = 6 * D_MODEL // 1024
    return pl.pallas_call(
        _mod_kernel,
        grid=(DEPTH, n_tiles),
        in_specs=[
            pl.BlockSpec((N_COND_ROWS, D_MODEL), lambda l, j: (0, 0)),
            pl.BlockSpec((1, D_MODEL, 1024), lambda l, j: (l, 0, j)),
            pl.BlockSpec((1, 1, 1024), lambda l, j: (l, 0, j)),
        ],
        out_specs=pl.BlockSpec((1, N_COND_ROWS, 1024), lambda l, j: (l, 0, j)),
        out_shape=jax.ShapeDtypeStruct((DEPTH, N_COND_ROWS, 6 * D_MODEL), f32),
        compiler_params=_cparams(("arbitrary", "arbitrary")),
        name="mod",
    )(cond, w_mod, b_mod.reshape(DEPTH, 1, 6 * D_MODEL))


CAST_BLOCK_BYTES = 4 * 1024 * 1024


def _cast_kernel(w_ref, o_ref):
    o_ref[...] = w_ref[0].astype(bf16)


def _layer_weight_bf16(w, l):
    _, rows, cols = w.shape
    rb = min(rows, 1 << ((CAST_BLOCK_BYTES // (4 * cols)).bit_length() - 1))
    assert rows % rb == 0
    return pl.pallas_call(
        _cast_kernel,
        grid=(rows // rb,),
        in_specs=[pl.BlockSpec((1, rb, cols), lambda i: (l, i, 0))],
        out_specs=pl.BlockSpec((rb, cols), lambda i: (i, 0)),
        out_shape=jax.ShapeDtypeStruct((rows, cols), bf16),
        compiler_params=_cparams(("arbitrary",)),
        name="cast_bf16",
    )(w)


def _head_norm(z, g, ones_blk):
    hi, lo = _split_bf16(z * z)
    ss = _dot(hi, ones_blk) + _dot(lo, ones_blk)
    return z * lax.rsqrt(ss * (1.0 / HEAD_DIM) + NORM_EPS) * g


def _rope(z, cos, sin_signed, first_half):
    outs = []
    for j in range(z.shape[1] // LANES):
        zj = z[:, j * LANES:(j + 1) * LANES]
        swapped = jnp.where(first_half, pltpu.roll(zj, LANES - HALF, 1), pltpu.roll(zj, HALF, 1))
        outs.append(zj * cos + swapped * sin_signed)
    return jnp.concatenate(outs, axis=1)


def _layer_input(x_refs, r0, n):
    rows = pl.ds(r0, n)
    if len(x_refs) == 1:
        return x_refs[0][rows, :]
    x1_ref, moe_ref, modp_ref = x_refs
    return x1_ref[rows, :] + modp_ref[0, 5:6, :] * _rows_from_slabs(moe_ref, r0 * SLAB_ROWS, n, SLAB_ROWS)


S1_SUB_TILES = 2
S3_SUB_TILES = 1


SLAB_ROWS = D_MODEL // LANES


def _rows_from_slabs(ref, start, n_tok, pitch):
    return jnp.concatenate([ref[pl.ds(start + c, n_tok, stride=pitch), :] for c in range(SLAB_ROWS)], axis=1)


def _rows_to_slabs(ref, start, x, pitch):
    for c in range(SLAB_ROWS):
        ref[pl.ds(start + c, x.shape[0], stride=pitch), :] = x[:, c * LANES:(c + 1) * LANES]


def _s1_kernel(*refs, rope, emit_kv, n_x):
    x_refs, refs = refs[:n_x], refs[n_x:]
    mod_ref, g1_ref, w_ref, ones_ref, qg_ref, kg_ref = refs[:6]
    pos = 6
    if rope:
        cos_ref, sin_ref = refs[pos:pos + 2]
        pos += 2
    qa_ref, ka_ref, va_ref, qc_ref, kc_ref, vc_ref, ub_ref = refs[pos:pos + 7]
    pos += 7
    if emit_kv:
        gkv_ref, wkv_ref = refs[pos:pos + 2]

    tm = x_refs[0].shape[0] // S1_SUB_TILES
    lane = lax.broadcasted_iota(jnp.int32, (tm, LANES), 1)
    low_lanes = lane < HEAD_DIM
    first_half = (lane & (HEAD_DIM - 1)) < HALF
    q_scale = HEAD_DIM ** -0.5

    def dup(z):
        r = pltpu.roll(z, HEAD_DIM, 1)
        return jnp.concatenate([jnp.where(low_lanes, z, r), jnp.where(low_lanes, r, z)], axis=1).astype(bf16)

    for sub in range(S1_SUB_TILES):
        rows = pl.ds(sub * tm, tm)
        x = _layer_input(x_refs, sub * tm, tm)
        h = _rms_mod(x, g1_ref[...], mod_ref[0, 1:2, :], mod_ref[0, 0:1, :]).astype(bf16)
        proj = lambda lo, hi: _dot(h, w_ref[:, lo:hi])
        if rope:
            cos = cos_ref[rows, :]
            sin_signed = sin_ref[rows, :]
            rot = lambda z: _rope(z, cos, sin_signed, first_half)
        else:
            rot = lambda z: z

        qa = rot(_head_norm(proj(C_QA, C_KA), qg_ref[...], ones_ref[...]))
        qa_ref[rows, :] = (qa * q_scale).astype(bf16)
        kva = proj(C_KA, C_QC)
        ka_n = _head_norm(kva[:, 0:LANES], kg_ref[...], ones_ref[0:LANES, 0:LANES])
        ka_ref[rows, :] = dup(rot(ka_n))
        va = kva[:, LANES:2 * LANES]
        va_ref[rows, :] = dup(va)
        qc_ref[rows, :] = (rot(proj(C_QC, C_KC)) * q_scale).astype(bf16)
        kvc = proj(C_KC, C_UB)
        kc = kvc[:, 0:LANES]
        kc_ref[rows, :] = dup(rot(kc))
        vc = kvc[:, LANES:2 * LANES]
        vc_ref[rows, :] = dup(vc)
        ub_ref[rows, :] = proj(C_UB, C_GATES).astype(bf16)
        if emit_kv:
            gkv_ref[sub, 0] = ka_n
            gkv_ref[sub, 1] = va
            wkv_ref[sub, 0] = kc
            wkv_ref[sub, 1] = vc


TOKEN_TILE = 512


def _cond_index_map(tokens_per_batch, cond_row0, per_batch_cond):
    tiles_per_batch = max(tokens_per_batch // TOKEN_TILE, 1)
    if per_batch_cond:
        assert tokens_per_batch % TOKEN_TILE == 0
        return lambda t: (cond_row0 + t // tiles_per_batch, 0, 0)
    return lambda t: (cond_row0, 0, 0)


def _layer_input_specs(xin, mod_prev, cond_idx):
    tok = pl.BlockSpec((TOKEN_TILE, D_MODEL), lambda t: (t, 0))
    if len(xin) == 1:
        return [tok], list(xin)
    slab = pl.BlockSpec((TOKEN_TILE * SLAB_ROWS, LANES), lambda t: (t, 0))
    return [tok, slab, pl.BlockSpec((1, 6, D_MODEL), cond_idx)], [xin[0], xin[1], mod_prev]


def _s1_call(xin, mod_prev, mod, g1, w_ext, ones_blk, qg, kg, rope_tabs, *, tokens_per_batch, cond_row0,
             per_batch_cond, emit_kv):
    n_tok = xin[0].shape[0]
    tm = TOKEN_TILE
    rope = rope_tabs is not None
    tiles_per_batch = max(tokens_per_batch // tm, 1)
    cond_idx = _cond_index_map(tokens_per_batch, cond_row0, per_batch_cond)

    tok = lambda w: pl.BlockSpec((tm, w), lambda t: (t, 0))
    x_specs, x_args = _layer_input_specs(xin, mod_prev, cond_idx)
    in_specs = x_specs + [
        pl.BlockSpec((1, 6, D_MODEL), cond_idx),
        _resident((1, D_MODEL)),
        _resident((D_MODEL, C_GATES)),
        _resident((Q_W, Q_W)),
        _resident((1, Q_W)),
        _resident((1, LANES)),
    ]
    args = x_args + [mod, g1, w_ext, ones_blk, qg, kg]
    if rope:
        in_specs += [pl.BlockSpec((tm, LANES), lambda t: (t % tiles_per_batch, 0))] * 2
        args += list(rope_tabs)
    widths = [Q_W, KV_DUP_W, KV_DUP_W, Q_W, KV_DUP_W, KV_DUP_W, FOURIER_DIM]
    out_specs = [tok(w) for w in widths]
    out_shape = [jax.ShapeDtypeStruct((n_tok, w), bf16) for w in widths]
    if emit_kv:
        assert tokens_per_batch * S1_SUB_TILES == tm
        n_batch = n_tok // tokens_per_batch
        out_specs += [pl.BlockSpec((S1_SUB_TILES, 2, tokens_per_batch, LANES), lambda t: (t, 0, 0, 0))] * 2
        out_shape += [jax.ShapeDtypeStruct((n_batch, 2, tokens_per_batch, LANES), f32)] * 2
    return pl.pallas_call(
        functools.partial(_s1_kernel, rope=rope, emit_kv=emit_kv, n_x=len(x_args)),
        grid=(n_tok // tm,),
        in_specs=in_specs,
        out_specs=out_specs,
        out_shape=out_shape,
        compiler_params=_cparams(("arbitrary",)),
        name="s1_rope" if rope else "s1_ctx",
    )(*args)


def _gqa_group(q2, segs, sink_pair_lo, sink_pair_hi):
    qb = q2.shape[0]
    qs = jnp.concatenate([q2[:, 0:LANES], q2[:, LANES:2 * LANES]], axis=0)
    lane = lax.broadcasted_iota(jnp.int32, qs.shape, 1)
    low = lane < HEAD_DIM
    zero = jnp.zeros_like(qs)
    q4 = jnp.concatenate([jnp.where(low, qs, zero), jnp.where(low, zero, qs)], axis=0)

    scores = []
    for k, _, ok in segs:
        s = _dot_t(q4, k)
        if ok is not None:
            s = jnp.where(ok, s, NEG_INF)
        scores.append(s)
    m = scores[0].max(axis=-1, keepdims=True)
    for s in scores[1:]:
        m = jnp.maximum(m, s.max(axis=-1, keepdims=True))
    if sink_pair_lo is not None:
        row = lax.broadcasted_iota(jnp.int32, (4 * qb, 1), 0)
        sink = jnp.where(row < 2 * qb, jnp.where(row < qb, sink_pair_lo[0], sink_pair_lo[1]),
                         jnp.where(row < 3 * qb, sink_pair_hi[0], sink_pair_hi[1]))
        m = jnp.maximum(m, sink)
        denom = jnp.exp(sink - m)
    else:
        denom = jnp.zeros_like(m)
    acc = None
    for s, (_, v, _) in zip(scores, segs):
        p = jnp.exp(s - m)
        denom = denom + p.sum(axis=-1, keepdims=True)
        pv = _dot(p.astype(bf16), v)
        acc = pv if acc is None else acc + pv
    o4 = acc / denom
    o = jnp.where(low, o4[0:2 * qb], o4[2 * qb:4 * qb])
    return jnp.concatenate([o[0:qb], o[qb:2 * qb]], axis=1)


def _sink_pairs(sink_ref, kv):
    h = 4 * kv
    return (sink_ref[h], sink_ref[h + 2]), (sink_ref[h + 1], sink_ref[h + 3])


def _attn_ctx_kernel(sink_ref, qa_ref, ka_ref, va_ref, qc_ref, kc_ref, vc_ref, oa_ref, oc_ref):
    for kv in range(N_KV):
        cq = slice(2 * LANES * kv, 2 * LANES * (kv + 1))
        ck = slice(LANES * kv, LANES * (kv + 1))
        oa = _gqa_group(qa_ref[:, cq], [(ka_ref[:, ck], va_ref[:, ck], None)], None, None)
        oa_ref[:, cq] = oa.astype(bf16)
        lo, hi = _sink_pairs(sink_ref, kv)
        oc = _gqa_group(qc_ref[:, cq], [(kc_ref[:, ck], vc_ref[:, ck], None)], lo, hi)
        oc_ref[:, cq] = oc.astype(bf16)


def _attn_ctx_call(sink, qa, ka, va, qc, kc, vc, seq):
    n_tok = qa.shape[0]
    tok = lambda w: pl.BlockSpec((seq, w), lambda b: (b, 0))
    return pl.pallas_call(
        _attn_ctx_kernel,
        grid=(n_tok // seq,),
        in_specs=[pl.BlockSpec(memory_space=pltpu.SMEM),
                  tok(Q_W), tok(KV_DUP_W), tok(KV_DUP_W), tok(Q_W), tok(KV_DUP_W), tok(KV_DUP_W)],
        out_specs=[tok(Q_W), tok(Q_W)],
        out_shape=[jax.ShapeDtypeStruct((n_tok, Q_W), bf16)] * 2,
        compiler_params=_cparams(("arbitrary",)),
        name="attn_ctx",
    )(sink, qa, ka, va, qc, kc, vc)


QB = 256
WIN_QB = 128
WIN_SPAN = WIN_QB + 2 * WINDOW


def _attn_lat_kernel(sink_ref, qa_ref, ka_ref, va_ref, kxa_ref, vxa_ref,
                     qc_ref, kc_ref, vc_ref, kxc_ref, vxc_ref, oa_ref, oc_ref, *, seq):
    i = pl.program_id(1)
    for kv in range(N_KV):
        cq = slice(2 * LANES * kv, 2 * LANES * (kv + 1))
        ck = slice(LANES * kv, LANES * (kv + 1))
        segs = [(ka_ref[:, ck], va_ref[:, ck], None), (kxa_ref[0, :, ck], vxa_ref[0, :, ck], None)]
        oa_ref[:, cq] = _gqa_group(qa_ref[:, cq], segs, None, None).astype(bf16)

        lo, hi = _sink_pairs(sink_ref, kv)
        for sub in range(QB // WIN_QB):
            blk = i * (QB // WIN_QB) + sub
            start = jnp.clip((blk - 1) * WIN_QB, 0, seq - WIN_SPAN)
            start = pl.multiple_of(start, WIN_QB)
            r = lax.broadcasted_iota(jnp.int32, (4 * WIN_QB, WIN_SPAN), 0)
            c = lax.broadcasted_iota(jnp.int32, (4 * WIN_QB, WIN_SPAN), 1)
            qpos = blk * WIN_QB + (r & (WIN_QB - 1))
            ok = jnp.abs(qpos - (start + c)) <= WINDOW
            rows = slice(sub * WIN_QB, (sub + 1) * WIN_QB)
            segs = [(kc_ref[pl.ds(start, WIN_SPAN), ck], vc_ref[pl.ds(start, WIN_SPAN), ck], ok),
                    (kxc_ref[0, :, ck], vxc_ref[0, :, ck], None)]
            oc_ref[rows, cq] = _gqa_group(qc_ref[rows, cq], segs, lo, hi).astype(bf16)


def _attn_lat_call(sink, qa, ka, va, kxa, vxa, qc, kc, vc, kxc, vxc, seq):
    n_tok = qa.shape[0]
    nq = seq // QB
    past = kxa.shape[1]
    qspec = pl.BlockSpec((QB, Q_W), lambda b, i: (b * nq + i, 0))
    kspec = pl.BlockSpec((seq, KV_DUP_W), lambda b, i: (b, 0))
    xspec = pl.BlockSpec((1, past, KV_DUP_W), lambda b, i: (b, 0, 0))
    return pl.pallas_call(
        functools.partial(_attn_lat_kernel, seq=seq),
        grid=(n_tok // seq, nq),
        in_specs=[pl.BlockSpec(memory_space=pltpu.SMEM),
                  qspec, kspec, kspec, xspec, xspec, qspec, kspec, kspec, xspec, xspec],
        out_specs=[qspec, qspec],
        out_shape=[jax.ShapeDtypeStruct((n_tok, Q_W), bf16)] * 2,
        compiler_params=_cparams(("arbitrary", "arbitrary")),
        name="attn_lat",
    )(sink, qa, ka, va, kxa, vxa, qc, kc, vc, kxc, vxc)


FOURIER_ROWS = 1024


def _fourier_kernel(u_ref, bdc_ref, bds_ref, cn_ref, sn_ref, o_ref, *, scale, seq):
    u = u_ref[...]
    uc = _dot(u, bdc_ref[...]).astype(bf16)
    us = _dot(u, bds_ref[...]).astype(bf16)
    for b in range(u.shape[0] // seq):
        rows = slice(b * seq, (b + 1) * seq)
        o = _dot(cn_ref[...], uc[rows]) - _dot(sn_ref[...], us[rows])
        o_ref[rows, :] = (o * scale).astype(bf16)


def _fourier_call(ub, bdc, bds, cn, sn, seq):
    n_tok = ub.shape[0]
    rows = max(seq, FOURIER_ROWS)
    tok = pl.BlockSpec((rows, FOURIER_DIM), lambda b: (b, 0))
    scale = float(1.0 / np.sqrt(seq * FOURIER_GROUP_DIM))
    return pl.pallas_call(
        functools.partial(_fourier_kernel, scale=scale, seq=seq),
        grid=(n_tok // rows,),
        in_specs=[tok, _resident((FOURIER_DIM, FOURIER_DIM)), _resident((FOURIER_DIM, FOURIER_DIM)),
                  _resident((seq, seq)), _resident((seq, seq))],
        out_specs=tok,
        out_shape=jax.ShapeDtypeStruct((n_tok, FOURIER_DIM), bf16),
        compiler_params=_cparams(("arbitrary",)),
        name=f"fourier_{seq}",
    )(ub, bdc, bds, cn, sn)


ROUTE_ROWS = LANES
H2X_PITCH = SLAB_ROWS + 1
MOE_PITCH = SLAB_ROWS


def _route(lt, tri_ref, carry_ref, n_tok):
    tm = lt.shape[1]
    row = lax.broadcasted_iota(jnp.int32, (EXPERTS_PER_GROUP, tm), 0).astype(f32)
    neg = jnp.float32(-jnp.inf)
    none = jnp.float32(EXPERTS_PER_GROUP)
    gl = jnp.where(row < N_EXPERT_GROUPS, lt[N_EXPERTS:N_EXPERTS + EXPERTS_PER_GROUP], neg)
    gmax = gl.max(axis=0, keepdims=True)
    gidx = jnp.where(gl == gmax, row, none).min(axis=0, keepdims=True)
    gw = 1.0 / jnp.exp(gl - gmax).sum(axis=0, keepdims=True)
    lg = lt[0:EXPERTS_PER_GROUP]
    for g in range(1, N_EXPERT_GROUPS):
        lg = jnp.where(gidx == g, lt[g * EXPERTS_PER_GROUP:(g + 1) * EXPERTS_PER_GROUP], lg)
    m1 = lg.max(axis=0, keepdims=True)
    i1 = jnp.where(lg == m1, row, none).min(axis=0, keepdims=True)
    l2 = jnp.where(row == i1, neg, lg)
    m2 = l2.max(axis=0, keepdims=True)
    i2 = jnp.where(l2 == m2, row, none).min(axis=0, keepdims=True)
    e = jnp.exp(m2 - m1)
    w8 = jnp.where(row == i1, gw / (1.0 + e), jnp.where(row == i2, gw * e / (1.0 + e), 0.0))

    onehot = (row == gidx).astype(f32)
    prefix = _dot(onehot.astype(bf16), tri_ref[0:tm, 0:tm])
    carry = carry_ref[...]
    rank = (onehot * (prefix - 1.0 + carry[:, 0:1])).sum(axis=0, keepdims=True)
    carry_ref[...] = carry + onehot.sum(axis=1, keepdims=True)
    dest = gidx.astype(jnp.int32) * n_tok + rank.astype(jnp.int32)
    return w8, dest


def _s3_kernel(*refs, n_x, n_tok):
    x_refs, refs = refs[:n_x], refs[n_x:]
    (oa_ref, ob_ref, oc_ref, mod_ref, g1_ref, g2_ref, wga_ref, wgb_ref, wgc_ref, wba_ref, wbb_ref, wbc_ref,
     wout_ref, wrt_ref, tri_ref, x1_ref, h2x_ref, code_ref, cnt_ref, carry_ref) = refs
    wg_refs = (wga_ref, wgb_ref, wgc_ref)

    @pl.when(pl.program_id(0) == 0)
    def _():
        carry_ref[...] = jnp.zeros_like(carry_ref)

    tm = x1_ref.shape[0] // S3_SUB_TILES
    wh, wl = _split_bf16(wrt_ref[...])
    for sub in range(S3_SUB_TILES):
        r0 = sub * tm
        rows = pl.ds(r0, tm)
        x = _layer_input(x_refs, r0, tm)
        h = _rms_mod(x, g1_ref[...], mod_ref[0, 1:2, :], mod_ref[0, 0:1, :]).astype(bf16)
        merged = None
        for j, (o_ref, wb_ref) in enumerate(((oa_ref, wba_ref), (ob_ref, wbb_ref), (oc_ref, wbc_ref))):
            gate = _sigmoid(_dot(h, wg_refs[j][...]))
            term = gate * _dot(o_ref[rows, :], wb_ref[...])
            merged = term if merged is None else merged + term
        mix = _dot(merged.astype(bf16), wout_ref[...])
        x1 = x + mod_ref[0, 2:3, :] * mix
        x1_ref[rows, :] = x1
        h2 = _rms_mod(x1, g2_ref[...], mod_ref[0, 4:5, :], mod_ref[0, 3:4, :])
        _rows_to_slabs(h2x_ref, r0 * H2X_PITCH, h2, H2X_PITCH)
        th, tl = _split_bf16(h2)
        lt = _dot_t(wh, th) + (_dot_t(wh, tl) + _dot_t(wl, th))
        w8, code = _route(lt, tri_ref, carry_ref, n_tok)
        code_ref[:, rows] = code
        w_rows = jnp.concatenate([w8, jnp.zeros((LANES - EXPERTS_PER_GROUP, tm), f32)], axis=0)
        h2x_ref[pl.ds(r0 * H2X_PITCH + SLAB_ROWS, tm, stride=H2X_PITCH), :] = w_rows.T
    cnt_ref[...] = carry_ref[...]


def _s3_call(xin, mod_prev, oa, ob, oc, mod, g1, g2, wg, wba, wbb, wbc, wout, wrt, tri, *, tokens_per_batch,
             cond_row0, per_batch_cond):
    n_tok = xin[0].shape[0]
    tm = TOKEN_TILE
    cond_idx = _cond_index_map(tokens_per_batch, cond_row0, per_batch_cond)
    tok = lambda w: pl.BlockSpec((tm, w), lambda t: (t, 0))
    x_specs, x_args = _layer_input_specs(xin, mod_prev, cond_idx)
    gate_cols = lambda j: pl.BlockSpec((D_MODEL, D_MODEL), lambda t: (0, C_GATES // D_MODEL + j),
                                       pipeline_mode=pl.Buffered(1))
    return pl.pallas_call(
        functools.partial(_s3_kernel, n_x=len(x_args), n_tok=n_tok),
        grid=(n_tok // tm,),
        in_specs=x_specs + [
            tok(Q_W), tok(FOURIER_DIM), tok(Q_W),
            pl.BlockSpec((1, 6, D_MODEL), cond_idx),
            _resident((1, D_MODEL)), _resident((1, D_MODEL)),
            gate_cols(0), gate_cols(1), gate_cols(2),
            _resident((Q_W, D_MODEL)), _resident((FOURIER_DIM, D_MODEL)), _resident((Q_W, D_MODEL)),
            _resident((D_MODEL, D_MODEL)), _resident((ROUTE_ROWS, D_MODEL)), _resident((tm, tm))],
        out_specs=[tok(D_MODEL), pl.BlockSpec((tm * H2X_PITCH, LANES), lambda t: (t, 0)),
                   pl.BlockSpec((1, tm), lambda t: (0, t)),
                   pl.BlockSpec((EXPERTS_PER_GROUP, LANES), lambda t: (0, 0))],
        out_shape=[jax.ShapeDtypeStruct((n_tok, D_MODEL), f32),
                   jax.ShapeDtypeStruct((n_tok * H2X_PITCH, LANES), f32),
                   jax.ShapeDtypeStruct((1, n_tok), jnp.int32),
                   jax.ShapeDtypeStruct((EXPERTS_PER_GROUP, LANES), f32)],
        scratch_shapes=[pltpu.VMEM((EXPERTS_PER_GROUP, LANES), f32)],
        compiler_params=_cparams(("arbitrary",)),
        name="s3",
    )(*x_args, oa, ob, oc, mod, g1, g2, wg, wg, wg, wba, wbb, wbc, wout, wrt, tri)


MOE_TILE_LOG2 = 8
MOE_TILE = 1 << MOE_TILE_LOG2
ROW_UNROLL_LOG2 = 3
ROW_UNROLL = 1 << ROW_UNROLL_LOG2
GROUP_FF = EXPERTS_PER_GROUP * EXPERT_FF
PLAN_FIRST, PLAN_ITEMS, PLAN_LEN = 0, N_EXPERT_GROUPS, N_EXPERT_GROUPS + 1


def _moe_kernel(code_ref, cnt_ref, step_group_ref, h2x_hbm, weg_ref, weu_ref, wed_ref, out_hbm,
                pos_ref, plan_ref, gbuf, obuf, dump, gsem, ssem, *, n_tok):
    tm = MOE_TILE
    k = pl.program_id(0)
    n_steps = pl.num_programs(0)

    def out_slab(buf, slot, i):
        return buf.at[slot, :, i]

    def token_slab(tok):
        return out_hbm.at[pl.ds(pl.multiple_of(tok * MOE_PITCH, MOE_PITCH), MOE_PITCH)]

    def for_row_chunks(n_chunks, body):
        def chunk(c, carry):
            body(c * ROW_UNROLL)
            return carry
        lax.fori_loop(0, n_chunks, chunk, 0)

    @pl.when(k == 0)
    def _plan():
        first = jnp.int32(0)
        for g in range(N_EXPERT_GROUPS):
            c = cnt_ref[g]
            n_tiles = (c + (tm - 1)) >> MOE_TILE_LOG2
            plan_ref[PLAN_FIRST + g] = first

            def pad(p, carry):
                pos_ref[p] = 0
                return carry

            lax.fori_loop(g * n_tok + c, g * n_tok + n_tiles * tm, pad, 0)
            first = first + n_tiles
        plan_ref[PLAN_ITEMS] = first

        def place(t0):
            dests = [code_ref[t0 + j] for j in range(ROW_UNROLL)]
            for j, dest in enumerate(dests):
                pos_ref[dest] = t0 + j

        for_row_chunks(n_tok >> ROW_UNROLL_LOG2, place)

    n_items = plan_ref[PLAN_ITEMS]

    def item_group(item):
        return ((item >= plan_ref[PLAN_FIRST + 1]).astype(jnp.int32)
                + (item >= plan_ref[PLAN_FIRST + 2]).astype(jnp.int32)
                + (item >= plan_ref[PLAN_FIRST + 3]).astype(jnp.int32))

    def item_pos0(item):
        g = item_group(item)
        return g * n_tok + (item - plan_ref[PLAN_FIRST + g]) * tm

    def start_gather(item, slot):
        pos0 = item_pos0(item)

        def chunk(i0):
            toks = [pos_ref[pos0 + i0 + j] for j in range(ROW_UNROLL)]
            for j, tok in enumerate(toks):
                pltpu.make_async_copy(h2x_hbm.at[pl.ds(tok * H2X_PITCH, H2X_PITCH)],
                                      gbuf.at[slot, :, i0 + j],
                                      gsem.at[slot]).start(priority=j % 2)
        for_row_chunks(tm >> ROW_UNROLL_LOG2, chunk)

    def wait_gather(slot):
        pltpu.make_async_copy(gbuf.at[1 - slot], gbuf.at[slot], gsem.at[slot]).wait()

    def start_scatter(item, n_valid, slot):
        pos0 = item_pos0(item)

        def to_token(i0):
            dsts = [pos_ref[pos0 + i0 + j] for j in range(ROW_UNROLL)]
            for j, dst in enumerate(dsts):
                pltpu.make_async_copy(out_slab(obuf, slot, i0 + j), token_slab(dst),
                                      ssem.at[slot]).start(priority=j % 2)

        def row_to_token(i, carry):
            pltpu.make_async_copy(out_slab(obuf, slot, i), token_slab(pos_ref[pos0 + i]), ssem.at[slot]).start()
            return carry

        def row_to_dump(i, carry):
            pltpu.make_async_copy(out_slab(obuf, slot, i), out_slab(dump, slot, i), ssem.at[slot]).start()
            return carry

        full = n_valid >> ROW_UNROLL_LOG2
        for_row_chunks(full, to_token)
        lax.fori_loop(full * ROW_UNROLL, n_valid, row_to_token, 0)
        lax.fori_loop(n_valid, tm, row_to_dump, 0)

    def wait_scatter(slot):
        pltpu.make_async_copy(obuf.at[slot], dump.at[slot], ssem.at[slot]).wait()

    slot = k % 2

    @pl.when(k == 0)
    def _():
        start_gather(0, 0)

    @pl.when(k + 1 < n_items)
    def _():
        start_gather(k + 1, 1 - slot)

    @pl.when(k < n_items)
    def _():
        group = item_group(k)
        wait_gather(slot)
        h = jnp.concatenate([gbuf[slot, c] for c in range(SLAB_ROWS)], axis=1).astype(bf16)
        w8 = gbuf[slot, SLAB_ROWS]
        gate = _dot(h, weg_ref[...])
        a = gate * _sigmoid(gate) * _dot(h, weu_ref[...])
        pieces = [a[:, j * EXPERT_FF:(j + 1) * EXPERT_FF] * w8[:, j:j + 1] for j in range(EXPERTS_PER_GROUP)]
        out = _dot(jnp.concatenate(pieces, axis=1).astype(bf16), wed_ref[0])

        @pl.when(k >= 2)
        def _():
            wait_scatter(slot)

        for c in range(SLAB_ROWS):
            obuf[slot, c] = out[:, c * LANES:(c + 1) * LANES]
        n_valid = jnp.minimum(tm, cnt_ref[group] - (k - plan_ref[PLAN_FIRST + group]) * tm)
        start_scatter(k, n_valid, slot)

    @pl.when(k == n_steps - 1)
    def _drain():
        wait_scatter((n_items - 1) % 2)

        @pl.when(n_items >= 2)
        def _():
            wait_scatter(n_items % 2)


def _moe_call(code, cnt, h2x, weg, weu, wed):
    n_tok = h2x.shape[0] // H2X_PITCH
    tm = MOE_TILE
    n_steps = n_tok // tm + N_EXPERT_GROUPS - 1
    n_tiles = (cnt[:N_EXPERT_GROUPS] + (tm - 1)) // tm
    first = jnp.cumsum(n_tiles) - n_tiles
    steps = jnp.arange(n_steps, dtype=jnp.int32)
    step_group = jnp.sum(steps[:, None] >= first[None, 1:], axis=1).astype(jnp.int32)
    step_group = jnp.minimum(step_group, step_group[jnp.maximum(jnp.sum(n_tiles) - 1, 0)])
    up_spec = pl.BlockSpec((D_MODEL, GROUP_FF), lambda k, code, cnt, sg: (0, sg[k]))
    down_spec = pl.BlockSpec((1, GROUP_FF, D_MODEL), lambda k, code, cnt, sg: (sg[k], 0, 0))
    grid_spec = pltpu.PrefetchScalarGridSpec(
        num_scalar_prefetch=3,
        grid=(n_steps,),
        in_specs=[pl.BlockSpec(memory_space=pl.ANY), up_spec, up_spec, down_spec],
        out_specs=pl.BlockSpec(memory_space=pl.ANY),
        scratch_shapes=[
            pltpu.SMEM((N_EXPERT_GROUPS * n_tok,), jnp.int32),
            pltpu.SMEM((PLAN_LEN,), jnp.int32),
            pltpu.VMEM((2, H2X_PITCH, tm, LANES), f32),
            pltpu.VMEM((2, MOE_PITCH, tm, LANES), f32),
            pltpu.VMEM((2, MOE_PITCH, tm, LANES), f32),
            pltpu.SemaphoreType.DMA((2,)),
            pltpu.SemaphoreType.DMA((2,)),
        ],
    )
    return pl.pallas_call(
        functools.partial(_moe_kernel, n_tok=n_tok),
        grid_spec=grid_spec,
        out_shape=jax.ShapeDtypeStruct((n_tok * MOE_PITCH, LANES), f32),
        compiler_params=_cparams(("arbitrary",)),
        name="moe",
    )(code, cnt, step_group, h2x, weg, weu, wed)


def _final_kernel(x1_ref, moe_ref, mod_ref, gf_ref, o_ref):
    x2 = x1_ref[...] + mod_ref[0, 5:6, :] * _rows_from_slabs(moe_ref, 0, x1_ref.shape[0], MOE_PITCH)
    y = x2 * lax.rsqrt(jnp.mean(x2 * x2, axis=-1, keepdims=True) + NORM_EPS)
    o_ref[...] = y * gf_ref[...]


def _final_call(x1, moe, mod, gf, *, tokens_per_batch, cond_row0, per_batch_cond):
    n_tok = x1.shape[0]
    tok = pl.BlockSpec((TOKEN_TILE, D_MODEL), lambda t: (t, 0))
    slab = pl.BlockSpec((TOKEN_TILE * MOE_PITCH, LANES), lambda t: (t, 0))
    cond_idx = _cond_index_map(tokens_per_batch, cond_row0, per_batch_cond)
    return pl.pallas_call(
        _final_kernel,
        grid=(n_tok // TOKEN_TILE,),
        in_specs=[tok, slab, pl.BlockSpec((1, 6, D_MODEL), cond_idx), _resident((1, D_MODEL))],
        out_specs=tok,
        out_shape=jax.ShapeDtypeStruct((n_tok, D_MODEL), f32),
        compiler_params=_cparams(("arbitrary",)),
        name="final_norm",
    )(x1, moe, mod, gf)


def _rope_tables(n):
    rows = n // GRID_W
    row = jnp.repeat(jnp.arange(rows, dtype=f32), GRID_W)
    col = jnp.tile(jnp.arange(GRID_W, dtype=f32), rows)
    inv = ROPE_THETA ** (-jnp.arange(AXIS_FREQS, dtype=f32) / AXIS_FREQS)
    ang = jnp.concatenate([row[:, None] * inv, col[:, None] * inv], axis=-1)
    cos, sin = jnp.cos(ang), jnp.sin(ang)
    cos_t = jnp.concatenate([cos, cos, cos, cos], axis=-1)
    sin_t = jnp.concatenate([-sin, sin, -sin, sin], axis=-1)
    return cos_t, sin_t


def _dft_tables(n):
    k = np.arange(n)
    m = (k[:, None] * k[None, :]) % n
    ang = 2.0 * np.pi * m / n
    return np.cos(ang), np.sin(ang)


def _fourier_tables(seq):
    cn, sn = _dft_tables(seq)
    cc, sc = _dft_tables(FOURIER_GROUP_DIM)
    eye = np.eye(N_FOURIER_GROUPS)
    bdc, bds = np.kron(eye, cc), np.kron(eye, sc)
    return tuple(jnp.asarray(t, dtype=f32).astype(bf16) for t in (bdc, bds, cn, sn))


def _dup_cache(cache, l):
    b, _, _, p, _, _ = cache.shape
    dup = lambda x: jnp.broadcast_to(x[:, :, :, None, :], (b, p, N_KV, 2, HEAD_DIM)).reshape(b, p, KV_DUP_W)
    return dup(cache[:, l, 0]).astype(bf16), dup(cache[:, l, 1]).astype(bf16)


def kernel(x_prompt, x_sample, cache_global_kv, cache_window_kv, c, c_ctx, w_mod, b_mod, norm1_g, w_in,
           q_norm_g, k_norm_g, sink_logit, w_branch_a, w_branch_b, w_branch_c, w_out, norm2_g,
           w_route_group, w_route_expert, w_exp_gate, w_exp_up, w_exp_down, final_norm_g):
    batch, seq, _ = x_prompt.shape
    dec_batch, dec_seq, _ = x_sample.shape

    cond = jnp.zeros((N_COND_ROWS, D_MODEL), f32).at[0].set(c_ctx).at[1:1 + dec_batch].set(c)
    mod = _mod_call(cond, w_mod, b_mod).reshape(DEPTH, N_COND_ROWS, 6, D_MODEL)

    ones_blk = jnp.asarray(np.kron(np.eye(N_HEADS), np.ones((HEAD_DIM, HEAD_DIM))), dtype=bf16)
    tri = jnp.asarray(np.triu(np.ones((TOKEN_TILE, TOKEN_TILE))), dtype=bf16)
    rope_tabs = _rope_tables(dec_seq)
    ftab_ctx = _fourier_tables(seq)
    ftab_lat = _fourier_tables(dec_seq)
    gf = final_norm_g.reshape(1, D_MODEL)

    ctx = dict(tokens_per_batch=seq, cond_row0=0, per_batch_cond=False)
    lat = dict(tokens_per_batch=dec_seq, cond_row0=1, per_batch_cond=True)
    xp = (x_prompt.reshape(batch * seq, D_MODEL),)
    xs = (x_sample.reshape(dec_batch * dec_seq, D_MODEL),)
    mod_prev = None
    gkv, wkv = [], []
    for l in range(DEPTH):
        w_ext = w_gate = _layer_weight_bf16(w_in, l)
        g1 = norm1_g[l].reshape(1, D_MODEL)
        g2 = norm2_g[l].reshape(1, D_MODEL)
        qg = jnp.tile(q_norm_g[l], N_HEADS).reshape(1, Q_W)
        kg = jnp.tile(k_norm_g[l], N_KV).reshape(1, LANES)
        wba, wbb, wbc = (_layer_weight_bf16(w, l) for w in (w_branch_a, w_branch_b, w_branch_c))
        wout = _layer_weight_bf16(w_out, l)
        wrt = jnp.concatenate([w_route_expert[l].T, w_route_group[l].T,
                               jnp.zeros((ROUTE_ROWS - N_EXPERTS - N_EXPERT_GROUPS, D_MODEL), f32)], axis=0)
        weg, weu = _layer_weight_bf16(w_exp_gate, l), _layer_weight_bf16(w_exp_up, l)
        wed = _layer_weight_bf16(w_exp_down, l).reshape(N_EXPERT_GROUPS, GROUP_FF, D_MODEL)
        sink = sink_logit[l]

        def mix_and_moe(xin, qa, ka, va, qc, kc, vc, ub, attn, ftab, seq_len, stream):
            oa, oc = attn(qa, ka, va, qc, kc, vc)
            ob = _fourier_call(ub, *ftab, seq_len)
            x1, h2x, code, cnt = _s3_call(xin, mod_prev, oa, ob, oc, mod[l], g1, g2, w_gate, wba, wbb, wbc, wout,
                                          wrt, tri, **stream)
            counts = cnt[:, 0].astype(jnp.int32)
            moe = _moe_call(code.reshape(-1), counts, h2x, weg, weu, wed)
            return (x1, moe)

        qa, ka, va, qc, kc, vc, ub, gkv_l, wkv_l = _s1_call(
            xp, mod_prev, mod[l], g1, w_ext, ones_blk, qg, kg, None, emit_kv=True, **ctx)
        attn_ctx = lambda *qkv: _attn_ctx_call(sink, *qkv, seq)
        xp = mix_and_moe(xp, qa, ka, va, qc, kc, vc, ub, attn_ctx, ftab_ctx, seq, ctx)
        kv_shape = (batch, 2, seq, N_KV, HEAD_DIM)
        gkv.append(gkv_l.reshape(kv_shape))
        wkv.append(wkv_l.reshape(kv_shape))

        qa, ka, va, qc, kc, vc, ub = _s1_call(
            xs, mod_prev, mod[l], g1, w_ext, ones_blk, qg, kg, rope_tabs, emit_kv=False, **lat)
        kxa, vxa = _dup_cache(cache_global_kv, l)
        kxc, vxc = _dup_cache(cache_window_kv, l)
        attn_lat = lambda qa, ka, va, qc, kc, vc: _attn_lat_call(
            sink, qa, ka, va, kxa, vxa, qc, kc, vc, kxc, vxc, dec_seq)
        xs = mix_and_moe(xs, qa, ka, va, qc, kc, vc, ub, attn_lat, ftab_lat, dec_seq, lat)
        mod_prev = mod[l]

    y_prompt = _final_call(*xp, mod_prev, gf, **ctx).reshape(batch, seq, D_MODEL)
    y_sample = _final_call(*xs, mod_prev, gf, **lat).reshape(dec_batch, dec_seq, D_MODEL)
    return (y_prompt, y_sample, jnp.stack(gkv, axis=1), jnp.stack(wkv, axis=1))
```

```python
import functools

import jax
import jax.numpy as jnp
import numpy as np
from jax import lax
from jax.experimental import pallas as pl
from jax.experimental.pallas import tpu as pltpu

f32 = jnp.float32
bf16 = jnp.bfloat16

D_MODEL = 1024
DEPTH = 2
GRID_W = 64
HEAD_DIM = 64
HALF = HEAD_DIM // 2
N_HEADS = 8
N_KV = 2
WINDOW = 128
N_FOURIER_GROUPS = 4
FOURIER_GROUP_DIM = 128
FOURIER_DIM = N_FOURIER_GROUPS * FOURIER_GROUP_DIM
ROPE_THETA = 10000.0
AXIS_FREQS = HEAD_DIM // 4
N_EXPERT_GROUPS = 4
EXPERTS_PER_GROUP = 8
N_EXPERTS = N_EXPERT_GROUPS * EXPERTS_PER_GROUP
EXPERT_FF = 128
NORM_EPS = 1e-6
NEG_INF = -1e30

LANES = 128
Q_W = N_HEADS * HEAD_DIM
KV_DUP_W = 2 * N_KV * HEAD_DIM
C_QA, C_KA, C_VA, C_QC, C_KC, C_VC, C_UB, C_GATES, C_END = 0, 512, 640, 768, 1280, 1408, 1536, 2048, 5120
N_COND_ROWS = 16
VMEM_LIMIT = 56 * 1024 * 1024


def _cparams(sem):
    return pltpu.CompilerParams(dimension_semantics=sem, vmem_limit_bytes=VMEM_LIMIT)


def _resident(shape):
    nd = len(shape)
    return pl.BlockSpec(shape, lambda *_: (0,) * nd, pipeline_mode=pl.Buffered(1))


def _sigmoid(x):
    return 0.5 * jnp.tanh(0.5 * x) + 0.5


def _rms_mod(x, g, scale, shift):
    y = x * lax.rsqrt(jnp.mean(x * x, axis=-1, keepdims=True) + NORM_EPS)
    return (y * g) * (1.0 + scale) + shift


def _dot(a, b):
    return jnp.dot(a, b, preferred_element_type=f32)


def _dot_t(a, b):
    return lax.dot_general(a, b, (((1,), (1,)), ((), ())), preferred_element_type=f32)


def _split_bf16(x):
    hi = x.astype(bf16)
    lo = (x - hi.astype(f32)).astype(bf16)
    return hi, lo


def _mod_kernel(cond_ref, w_ref, b_ref, o_ref):
    c = cond_ref[...]
    s = (c * _sigmoid(c)).astype(bf16)
    o_ref[0] = _dot(s, w_ref[0].astype(bf16)) + b_ref[0]


def _mod_call(cond, w_mod, b_mod):
    n_tiles = 6 * D_MODEL // 1024
    return pl.pallas_call(
        _mod_kernel,
        grid=(DEPTH, n_tiles),
        in_specs=[
            pl.BlockSpec((N_COND_ROWS, D_MODEL), lambda l, j: (0, 0)),
            pl.BlockSpec((1, D_MODEL, 1024), lambda l, j: (l, 0, j)),
            pl.BlockSpec((1, 1, 1024), lambda l, j: (l, 0, j)),
        ],
        out_specs=pl.BlockSpec((1, N_COND_ROWS, 1024), lambda l, j: (l, 0, j)),
        out_shape=jax.ShapeDtypeStruct((DEPTH, N_COND_ROWS, 6 * D_MODEL), f32),
        compiler_params=_cparams(("arbitrary", "arbitrary")),
        name="mod",
    )(cond, w_mod, b_mod.reshape(DEPTH, 1, 6 * D_MODEL))


CAST_BLOCK_BYTES = 4 * 1024 * 1024


def _cast_kernel(w_ref, o_ref):
    o_ref[...] = w_ref[0].astype(bf16)


def _layer_weight_bf16(w, l):
    _, rows, cols = w.shape
    rb = min(rows, 1 << ((CAST_BLOCK_BYTES // (4 * cols)).bit_length() - 1))
    assert rows % rb == 0
    return pl.pallas_call(
        _cast_kernel,
        grid=(rows // rb,),
        in_specs=[pl.BlockSpec((1, rb, cols), lambda i: (l, i, 0))],
        out_specs=pl.BlockSpec((rb, cols), lambda i: (i, 0)),
        out_shape=jax.ShapeDtypeStruct((rows, cols), bf16),
        compiler_params=_cparams(("arbitrary",)),
        name="cast_bf16",
    )(w)


def _head_norm(z, g, ones_blk):
    hi, lo = _split_bf16(z * z)
    ss = _dot(hi, ones_blk) + _dot(lo, ones_blk)
    return z * lax.rsqrt(ss * (1.0 / HEAD_DIM) + NORM_EPS) * g


def _rope(z, cos, sin_signed, first_half):
    outs = []
    for j in range(z.shape[1] // LANES):
        zj = z[:, j * LANES:(j + 1) * LANES]
        swapped = jnp.where(first_half, pltpu.roll(zj, LANES - HALF, 1), pltpu.roll(zj, HALF, 1))
        outs.append(zj * cos + swapped * sin_signed)
    return jnp.concatenate(outs, axis=1)


def _layer_input(x_refs, r0, n):
    rows = pl.ds(r0, n)
    if len(x_refs) == 1:
        return x_refs[0][rows, :]
    x1_ref, moe_ref, modp_ref = x_refs
    return x1_ref[rows, :] + modp_ref[0, 5:6, :] * _rows_from_slabs(moe_ref, r0 * SLAB_ROWS, n, SLAB_ROWS)


S1_SUB_TILES = 2
S3_SUB_TILES = 1


SLAB_ROWS = D_MODEL // LANES


def _rows_from_slabs(ref, start, n_tok, pitch):
    return jnp.concatenate([ref[pl.ds(start + c, n_tok, stride=pitch), :] for c in range(SLAB_ROWS)], axis=1)


def _rows_to_slabs(ref, start, x, pitch):
    for c in range(SLAB_ROWS):
        ref[pl.ds(start + c, x.shape[0], stride=pitch), :] = x[:, c * LANES:(c + 1) * LANES]


def _s1_kernel(*refs, rope, emit_kv, n_x):
    x_refs, refs = refs[:n_x], refs[n_x:]
    mod_ref, g1_ref, w_ref, ones_ref, qg_ref, kg_ref = refs[:6]
    pos = 6
    if rope:
        cos_ref, sin_ref = refs[pos:pos + 2]
        pos += 2
    qa_ref, ka_ref, va_ref, qc_ref, kc_ref, vc_ref, ub_ref = refs[pos:pos + 7]
    pos += 7
    if emit_kv:
        gkv_ref, wkv_ref = refs[pos:pos + 2]

    tm = x_refs[0].shape[0] // S1_SUB_TILES
    lane = lax.broadcasted_iota(jnp.int32, (tm, LANES), 1)
    low_lanes = lane < HEAD_DIM
    first_half = (lane & (HEAD_DIM - 1)) < HALF
    q_scale = HEAD_DIM ** -0.5

    def dup(z):
        r = pltpu.roll(z, HEAD_DIM, 1)
        return jnp.concatenate([jnp.where(low_lanes, z, r), jnp.where(low_lanes, r, z)], axis=1).astype(bf16)

    for sub in range(S1_SUB_TILES):
        rows = pl.ds(sub * tm, tm)
        x = _layer_input(x_refs, sub * tm, tm)
        h = _rms_mod(x, g1_ref[...], mod_ref[0, 1:2, :], mod_ref[0, 0:1, :]).astype(bf16)
        proj = lambda lo, hi: _dot(h, w_ref[:, lo:hi])
        if rope:
            cos = cos_ref[rows, :]
            sin_signed = sin_ref[rows, :]
            rot = lambda z: _rope(z, cos, sin_signed, first_half)
        else:
            rot = lambda z: z

        qa = rot(_head_norm(proj(C_QA, C_KA), qg_ref[...], ones_ref[...]))
        qa_ref[rows, :] = (qa * q_scale).astype(bf16)
        kva = proj(C_KA, C_QC)
        ka_n = _head_norm(kva[:, 0:LANES], kg_ref[...], ones_ref[0:LANES, 0:LANES])
        ka_ref[rows, :] = dup(rot(ka_n))
        va = kva[:, LANES:2 * LANES]
        va_ref[rows, :] = dup(va)
        qc_ref[rows, :] = (rot(proj(C_QC, C_KC)) * q_scale).astype(bf16)
        kvc = proj(C_KC, C_UB)
        kc = kvc[:, 0:LANES]
        kc_ref[rows, :] = dup(rot(kc))
        vc = kvc[:, LANES:2 * LANES]
        vc_ref[rows, :] = dup(vc)
        ub_ref[rows, :] = proj(C_UB, C_GATES).astype(bf16)
        if emit_kv:
            gkv_ref[sub, 0] = ka_n
            gkv_ref[sub, 1] = va
            wkv_ref[sub, 0] = kc
            wkv_ref[sub, 1] = vc


TOKEN_TILE = 512


def _cond_index_map(tokens_per_batch, cond_row0, per_batch_cond):
    tiles_per_batch = max(tokens_per_batch // TOKEN_TILE, 1)
    if per_batch_cond:
        assert tokens_per_batch % TOKEN_TILE == 0
        return lambda t: (cond_row0 + t // tiles_per_batch, 0, 0)
    return lambda t: (cond_row0, 0, 0)


def _layer_input_specs(xin, mod_prev, cond_idx):
    tok = pl.BlockSpec((TOKEN_TILE, D_MODEL), lambda t: (t, 0))
    if len(xin) == 1:
        return [tok], list(xin)
    slab = pl.BlockSpec((TOKEN_TILE * SLAB_ROWS, LANES), lambda t: (t, 0))
    return [tok, slab, pl.BlockSpec((1, 6, D_MODEL), cond_idx)], [xin[0], xin[1], mod_prev]


def _s1_call(xin, mod_prev, mod, g1, w_ext, ones_blk, qg, kg, rope_tabs, *, tokens_per_batch, cond_row0,
             per_batch_cond, emit_kv):
    n_tok = xin[0].shape[0]
    tm = TOKEN_TILE
    rope = rope_tabs is not None
    tiles_per_batch = max(tokens_per_batch // tm, 1)
    cond_idx = _cond_index_map(tokens_per_batch, cond_row0, per_batch_cond)

    tok = lambda w: pl.BlockSpec((tm, w), lambda t: (t, 0))
    x_specs, x_args = _layer_input_specs(xin, mod_prev, cond_idx)
    in_specs = x_specs + [
        pl.BlockSpec((1, 6, D_MODEL), cond_idx),
        _resident((1, D_MODEL)),
        _resident((D_MODEL, C_GATES)),
        _resident((Q_W, Q_W)),
        _resident((1, Q_W)),
        _resident((1, LANES)),
    ]
    args = x_args + [mod, g1, w_ext, ones_blk, qg, kg]
    if rope:
        in_specs += [pl.BlockSpec((tm, LANES), lambda t: (t % tiles_per_batch, 0))] * 2
        args += list(rope_tabs)
    widths = [Q_W, KV_DUP_W, KV_DUP_W, Q_W, KV_DUP_W, KV_DUP_W, FOURIER_DIM]
    out_specs = [tok(w) for w in widths]
    out_shape = [jax.ShapeDtypeStruct((n_tok, w), bf16) for w in widths]
    if emit_kv:
        assert tokens_per_batch * S1_SUB_TILES == tm
        n_batch = n_tok // tokens_per_batch
        out_specs += [pl.BlockSpec((S1_SUB_TILES, 2, tokens_per_batch, LANES), lambda t: (t, 0, 0, 0))] * 2
        out_shape += [jax.ShapeDtypeStruct((n_batch, 2, tokens_per_batch, LANES), f32)] * 2
    return pl.pallas_call(
        functools.partial(_s1_kernel, rope=rope, emit_kv=emit_kv, n_x=len(x_args)),
        grid=(n_tok // tm,),
        in_specs=in_specs,
        out_specs=out_specs,
        out_shape=out_shape,
        compiler_params=_cparams(("arbitrary",)),
        name="s1_rope" if rope else "s1_ctx",
    )(*args)


def _gqa_group(q2, segs, sink_pair_lo, sink_pair_hi):
    qb = q2.shape[0]
    qs = jnp.concatenate([q2[:, 0:LANES], q2[:, LANES:2 * LANES]], axis=0)
    lane = lax.broadcasted_iota(jnp.int32, qs.shape, 1)
    low = lane < HEAD_DIM
    zero = jnp.zeros_like(qs)
    q4 = jnp.concatenate([jnp.where(low, qs, zero), jnp.where(low, zero, qs)], axis=0)

    scores = []
    for k, _, ok in segs:
        s = _dot_t(q4, k)
        if ok is not None:
            s = jnp.where(ok, s, NEG_INF)
        scores.append(s)
    m = scores[0].max(axis=-1, keepdims=True)
    for s in scores[1:]:
        m = jnp.maximum(m, s.max(axis=-1, keepdims=True))
    if sink_pair_lo is not None:
        row = lax.broadcasted_iota(jnp.int32, (4 * qb, 1), 0)
        sink = jnp.where(row < 2 * qb, jnp.where(row < qb, sink_pair_lo[0], sink_pair_lo[1]),
                         jnp.where(row < 3 * qb, sink_pair_hi[0], sink_pair_hi[1]))
        m = jnp.maximum(m, sink)
        denom = jnp.exp(sink - m)
    else:
        denom = jnp.zeros_like(m)
    acc = None
    for s, (_, v, _) in zip(scores, segs):
        p = jnp.exp(s - m)
        denom = denom + p.sum(axis=-1, keepdims=True)
        pv = _dot(p.astype(bf16), v)
        acc = pv if acc is None else acc + pv
    o4 = acc / denom
    o = jnp.where(low, o4[0:2 * qb], o4[2 * qb:4 * qb])
    return jnp.concatenate([o[0:qb], o[qb:2 * qb]], axis=1)


def _sink_pairs(sink_ref, kv):
    h = 4 * kv
    return (sink_ref[h], sink_ref[h + 2]), (sink_ref[h + 1], sink_ref[h + 3])


ATTN_CTX_BATCH = 2


def _attn_ctx_kernel(sink_ref, qa_ref, ka_ref, va_ref, qc_ref, kc_ref, vc_ref, oa_ref, oc_ref, *, seq):
    for b in range(ATTN_CTX_BATCH):
        rows = slice(b * seq, (b + 1) * seq)
        for kv in range(N_KV):
            cq = slice(2 * LANES * kv, 2 * LANES * (kv + 1))
            ck = slice(LANES * kv, LANES * (kv + 1))
            oa = _gqa_group(qa_ref[rows, cq], [(ka_ref[rows, ck], va_ref[rows, ck], None)], None, None)
            oa_ref[rows, cq] = oa.astype(bf16)
            lo, hi = _sink_pairs(sink_ref, kv)
            oc = _gqa_group(qc_ref[rows, cq], [(kc_ref[rows, ck], vc_ref[rows, ck], None)], lo, hi)
            oc_ref[rows, cq] = oc.astype(bf16)


def _attn_ctx_call(sink, qa, ka, va, qc, kc, vc, seq):
    n_tok = qa.shape[0]
    rows = ATTN_CTX_BATCH * seq
    tok = lambda w: pl.BlockSpec((rows, w), lambda b: (b, 0))
    return pl.pallas_call(
        functools.partial(_attn_ctx_kernel, seq=seq),
        grid=(n_tok // rows,),
        in_specs=[pl.BlockSpec(memory_space=pltpu.SMEM),
                  tok(Q_W), tok(KV_DUP_W), tok(KV_DUP_W), tok(Q_W), tok(KV_DUP_W), tok(KV_DUP_W)],
        out_specs=[tok(Q_W), tok(Q_W)],
        out_shape=[jax.ShapeDtypeStruct((n_tok, Q_W), bf16)] * 2,
        compiler_params=_cparams(("arbitrary",)),
        name="attn_ctx",
    )(sink, qa, ka, va, qc, kc, vc)


QB = 256
WIN_QB = 128
WIN_SPAN = WIN_QB + 2 * WINDOW


def _attn_lat_kernel(sink_ref, qa_ref, ka_ref, va_ref, kxa_ref, vxa_ref,
                     qc_ref, kc_ref, vc_ref, kxc_ref, vxc_ref, oa_ref, oc_ref, *, seq):
    i = pl.program_id(1)
    for kv in range(N_KV):
        cq = slice(2 * LANES * kv, 2 * LANES * (kv + 1))
        ck = slice(LANES * kv, LANES * (kv + 1))
        segs = [(ka_ref[:, ck], va_ref[:, ck], None), (kxa_ref[0, :, ck], vxa_ref[0, :, ck], None)]
        oa_ref[:, cq] = _gqa_group(qa_ref[:, cq], segs, None, None).astype(bf16)

        lo, hi = _sink_pairs(sink_ref, kv)
        for sub in range(QB // WIN_QB):
            blk = i * (QB // WIN_QB) + sub
            start = jnp.clip((blk - 1) * WIN_QB, 0, seq - WIN_SPAN)
            start = pl.multiple_of(start, WIN_QB)
            r = lax.broadcasted_iota(jnp.int32, (4 * WIN_QB, WIN_SPAN), 0)
            c = lax.broadcasted_iota(jnp.int32, (4 * WIN_QB, WIN_SPAN), 1)
            qpos = blk * WIN_QB + (r & (WIN_QB - 1))
            ok = jnp.abs(qpos - (start + c)) <= WINDOW
            rows = slice(sub * WIN_QB, (sub + 1) * WIN_QB)
            segs = [(kc_ref[pl.ds(start, WIN_SPAN), ck], vc_ref[pl.ds(start, WIN_SPAN), ck], ok),
                    (kxc_ref[0, :, ck], vxc_ref[0, :, ck], None)]
            oc_ref[rows, cq] = _gqa_group(qc_ref[rows, cq], segs, lo, hi).astype(bf16)


def _attn_lat_call(sink, qa, ka, va, kxa, vxa, qc, kc, vc, kxc, vxc, seq):
    n_tok = qa.shape[0]
    nq = seq // QB
    past = kxa.shape[1]
    qspec = pl.BlockSpec((QB, Q_W), lambda b, i: (b * nq + i, 0))
    kspec = pl.BlockSpec((seq, KV_DUP_W), lambda b, i: (b, 0))
    xspec = pl.BlockSpec((1, past, KV_DUP_W), lambda b, i: (b, 0, 0))
    return pl.pallas_call(
        functools.partial(_attn_lat_kernel, seq=seq),
        grid=(n_tok // seq, nq),
        in_specs=[pl.BlockSpec(memory_space=pltpu.SMEM),
                  qspec, kspec, kspec, xspec, xspec, qspec, kspec, kspec, xspec, xspec],
        out_specs=[qspec, qspec],
        out_shape=[jax.ShapeDtypeStruct((n_tok, Q_W), bf16)] * 2,
        compiler_params=_cparams(("arbitrary", "arbitrary")),
        name="attn_lat",
    )(sink, qa, ka, va, kxa, vxa, qc, kc, vc, kxc, vxc)


FOURIER_ROWS = 1024


def _fourier_kernel(u_ref, bdc_ref, bds_ref, cn_ref, sn_ref, o_ref, *, scale, seq):
    u = u_ref[...]
    uc = _dot(u, bdc_ref[...]).astype(bf16)
    us = _dot(u, bds_ref[...]).astype(bf16)
    for b in range(u.shape[0] // seq):
        rows = slice(b * seq, (b + 1) * seq)
        o = _dot(cn_ref[...], uc[rows]) - _dot(sn_ref[...], us[rows])
        o_ref[rows, :] = (o * scale).astype(bf16)


def _fourier_call(ub, bdc, bds, cn, sn, seq):
    n_tok = ub.shape[0]
    rows = max(seq, FOURIER_ROWS)
    tok = pl.BlockSpec((rows, FOURIER_DIM), lambda b: (b, 0))
    scale = float(1.0 / np.sqrt(seq * FOURIER_GROUP_DIM))
    return pl.pallas_call(
        functools.partial(_fourier_kernel, scale=scale, seq=seq),
        grid=(n_tok // rows,),
        in_specs=[tok, _resident((FOURIER_DIM, FOURIER_DIM)), _resident((FOURIER_DIM, FOURIER_DIM)),
                  _resident((seq, seq)), _resident((seq, seq))],
        out_specs=tok,
        out_shape=jax.ShapeDtypeStruct((n_tok, FOURIER_DIM), bf16),
        compiler_params=_cparams(("arbitrary",)),
        name=f"fourier_{seq}",
    )(ub, bdc, bds, cn, sn)


ROUTE_ROWS = LANES
H2X_PITCH = SLAB_ROWS + 1
MOE_PITCH = SLAB_ROWS


def _route(lt, tri_ref, carry_ref, n_tok):
    tm = lt.shape[1]
    row = lax.broadcasted_iota(jnp.int32, (EXPERTS_PER_GROUP, tm), 0).astype(f32)
    neg = jnp.float32(-jnp.inf)
    none = jnp.float32(EXPERTS_PER_GROUP)
    gl = jnp.where(row < N_EXPERT_GROUPS, lt[N_EXPERTS:N_EXPERTS + EXPERTS_PER_GROUP], neg)
    gmax = gl.max(axis=0, keepdims=True)
    gidx = jnp.where(gl == gmax, row, none).min(axis=0, keepdims=True)
    gw = 1.0 / jnp.exp(gl - gmax).sum(axis=0, keepdims=True)
    lg = lt[0:EXPERTS_PER_GROUP]
    for g in range(1, N_EXPERT_GROUPS):
        lg = jnp.where(gidx == g, lt[g * EXPERTS_PER_GROUP:(g + 1) * EXPERTS_PER_GROUP], lg)
    m1 = lg.max(axis=0, keepdims=True)
    i1 = jnp.where(lg == m1, row, none).min(axis=0, keepdims=True)
    l2 = jnp.where(row == i1, neg, lg)
    m2 = l2.max(axis=0, keepdims=True)
    i2 = jnp.where(l2 == m2, row, none).min(axis=0, keepdims=True)
    e = jnp.exp(m2 - m1)
    w8 = jnp.where(row == i1, gw / (1.0 + e), jnp.where(row == i2, gw * e / (1.0 + e), 0.0))

    onehot = (row == gidx).astype(f32)
    prefix = _dot(onehot.astype(bf16), tri_ref[0:tm, 0:tm])
    carry = carry_ref[...]
    rank = (onehot * (prefix - 1.0 + carry[:, 0:1])).sum(axis=0, keepdims=True)
    carry_ref[...] = carry + onehot.sum(axis=1, keepdims=True)
    dest = gidx.astype(jnp.int32) * n_tok + rank.astype(jnp.int32)
    return w8, dest


def _s3_kernel(*refs, n_x, n_tok):
    x_refs, refs = refs[:n_x], refs[n_x:]
    (oa_ref, ob_ref, oc_ref, mod_ref, g1_ref, g2_ref, wga_ref, wgb_ref, wgc_ref, wba_ref, wbb_ref, wbc_ref,
     wout_ref, wrt_ref, tri_ref, x1_ref, h2x_ref, code_ref, cnt_ref, carry_ref) = refs
    wg_refs = (wga_ref, wgb_ref, wgc_ref)

    @pl.when(pl.program_id(0) == 0)
    def _():
        carry_ref[...] = jnp.zeros_like(carry_ref)

    tm = x1_ref.shape[0] // S3_SUB_TILES
    wh, wl = _split_bf16(wrt_ref[...])
    for sub in range(S3_SUB_TILES):
        r0 = sub * tm
        rows = pl.ds(r0, tm)
        x = _layer_input(x_refs, r0, tm)
        h = _rms_mod(x, g1_ref[...], mod_ref[0, 1:2, :], mod_ref[0, 0:1, :]).astype(bf16)
        merged = None
        for j, (o_ref, wb_ref) in enumerate(((oa_ref, wba_ref), (ob_ref, wbb_ref), (oc_ref, wbc_ref))):
            gate = _sigmoid(_dot(h, wg_refs[j][...]))
            term = gate * _dot(o_ref[rows, :], wb_ref[...])
            merged = term if merged is None else merged + term
        mix = _dot(merged.astype(bf16), wout_ref[...])
        x1 = x + mod_ref[0, 2:3, :] * mix
        x1_ref[rows, :] = x1
        h2 = _rms_mod(x1, g2_ref[...], mod_ref[0, 4:5, :], mod_ref[0, 3:4, :])
        _rows_to_slabs(h2x_ref, r0 * H2X_PITCH, h2, H2X_PITCH)
        th, tl = _split_bf16(h2)
        lt = _dot_t(wh, th) + (_dot_t(wh, tl) + _dot_t(wl, th))
        w8, code = _route(lt, tri_ref, carry_ref, n_tok)
        code_ref[:, rows] = code
        w_rows = jnp.concatenate([w8, jnp.zeros((LANES - EXPERTS_PER_GROUP, tm), f32)], axis=0)
        h2x_ref[pl.ds(r0 * H2X_PITCH + SLAB_ROWS, tm, stride=H2X_PITCH), :] = w_rows.T
    cnt_ref[...] = carry_ref[...]


def _s3_call(xin, mod_prev, oa, ob, oc, mod, g1, g2, wg, wba, wbb, wbc, wout, wrt, tri, *, tokens_per_batch,
             cond_row0, per_batch_cond):
    n_tok = xin[0].shape[0]
    tm = TOKEN_TILE
    cond_idx = _cond_index_map(tokens_per_batch, cond_row0, per_batch_cond)
    tok = lambda w: pl.BlockSpec((tm, w), lambda t: (t, 0))
    x_specs, x_args = _layer_input_specs(xin, mod_prev, cond_idx)
    gate_cols = lambda j: pl.BlockSpec((D_MODEL, D_MODEL), lambda t: (0, C_GATES // D_MODEL + j),
                                       pipeline_mode=pl.Buffered(1))
    return pl.pallas_call(
        functools.partial(_s3_kernel, n_x=len(x_args), n_tok=n_tok),
        grid=(n_tok // tm,),
        in_specs=x_specs + [
            tok(Q_W), tok(FOURIER_DIM), tok(Q_W),
            pl.BlockSpec((1, 6, D_MODEL), cond_idx),
            _resident((1, D_MODEL)), _resident((1, D_MODEL)),
            gate_cols(0), gate_cols(1), gate_cols(2),
            _resident((Q_W, D_MODEL)), _resident((FOURIER_DIM, D_MODEL)), _resident((Q_W, D_MODEL)),
            _resident((D_MODEL, D_MODEL)), _resident((ROUTE_ROWS, D_MODEL)), _resident((tm, tm))],
        out_specs=[tok(D_MODEL), pl.BlockSpec((tm * H2X_PITCH, LANES), lambda t: (t, 0)),
                   pl.BlockSpec((1, tm), lambda t: (0, t)),
                   pl.BlockSpec((EXPERTS_PER_GROUP, LANES), lambda t: (0, 0))],
        out_shape=[jax.ShapeDtypeStruct((n_tok, D_MODEL), f32),
                   jax.ShapeDtypeStruct((n_tok * H2X_PITCH, LANES), f32),
                   jax.ShapeDtypeStruct((1, n_tok), jnp.int32),
                   jax.ShapeDtypeStruct((EXPERTS_PER_GROUP, LANES), f32)],
        scratch_shapes=[pltpu.VMEM((EXPERTS_PER_GROUP, LANES), f32)],
        compiler_params=_cparams(("arbitrary",)),
        name="s3",
    )(*x_args, oa, ob, oc, mod, g1, g2, wg, wg, wg, wba, wbb, wbc, wout, wrt, tri)


MOE_TILE_LOG2 = 8
MOE_TILE = 1 << MOE_TILE_LOG2
ROW_UNROLL_LOG2 = 3
ROW_UNROLL = 1 << ROW_UNROLL_LOG2
GROUP_FF = EXPERTS_PER_GROUP * EXPERT_FF
PLAN_FIRST, PLAN_ITEMS, PLAN_LEN = 0, N_EXPERT_GROUPS, N_EXPERT_GROUPS + 1


def _moe_kernel(code_ref, cnt_ref, h2x_hbm, weg_ref, weu_ref, wed_ref, out_hbm,
                pos_ref, plan_ref, gbuf, obuf, dump, gsem, ssem, *, n_tok):
    tm = MOE_TILE
    k = pl.program_id(0)
    n_steps = pl.num_programs(0)

    def out_slab(buf, slot, i):
        return buf.at[slot, :, i]

    def token_slab(tok):
        return out_hbm.at[pl.ds(pl.multiple_of(tok * MOE_PITCH, MOE_PITCH), MOE_PITCH)]

    def for_row_chunks(n_chunks, body):
        def chunk(c, carry):
            body(c * ROW_UNROLL)
            return carry
        lax.fori_loop(0, n_chunks, chunk, 0)

    @pl.when(k == 0)
    def _plan():
        first = jnp.int32(0)
        for g in range(N_EXPERT_GROUPS):
            c = cnt_ref[g]
            n_tiles = (c + (tm - 1)) >> MOE_TILE_LOG2
            plan_ref[PLAN_FIRST + g] = first

            def pad(p, carry):
                pos_ref[p] = 0
                return carry

            lax.fori_loop(g * n_tok + c, g * n_tok + n_tiles * tm, pad, 0)
            first = first + n_tiles
        plan_ref[PLAN_ITEMS] = first

        def place(t0):
            dests = [code_ref[t0 + j] for j in range(ROW_UNROLL)]
            for j, dest in enumerate(dests):
                pos_ref[dest] = t0 + j

        for_row_chunks(n_tok >> ROW_UNROLL_LOG2, place)

    n_items = plan_ref[PLAN_ITEMS]

    def item_group(item):
        return ((item >= plan_ref[PLAN_FIRST + 1]).astype(jnp.int32)
                + (item >= plan_ref[PLAN_FIRST + 2]).astype(jnp.int32)
                + (item >= plan_ref[PLAN_FIRST + 3]).astype(jnp.int32))

    def item_pos0(item):
        g = item_group(item)
        return g * n_tok + (item - plan_ref[PLAN_FIRST + g]) * tm

    def start_gather(item, slot):
        pos0 = item_pos0(item)

        def chunk(i0):
            toks = [pos_ref[pos0 + i0 + j] for j in range(ROW_UNROLL)]
            for j, tok in enumerate(toks):
                pltpu.make_async_copy(h2x_hbm.at[pl.ds(tok * H2X_PITCH, H2X_PITCH)],
                                      gbuf.at[slot, :, i0 + j],
                                      gsem.at[slot]).start(priority=j % 2)
        for_row_chunks(tm >> ROW_UNROLL_LOG2, chunk)

    def wait_gather(slot):
        pltpu.make_async_copy(gbuf.at[1 - slot], gbuf.at[slot], gsem.at[slot]).wait()

    def start_scatter(item, n_valid, slot):
        pos0 = item_pos0(item)

        def to_token(i0):
            dsts = [pos_ref[pos0 + i0 + j] for j in range(ROW_UNROLL)]
            for j, dst in enumerate(dsts):
                pltpu.make_async_copy(out_slab(obuf, slot, i0 + j), token_slab(dst),
                                      ssem.at[slot]).start(priority=j % 2)

        def row_to_token(i, carry):
            pltpu.make_async_copy(out_slab(obuf, slot, i), token_slab(pos_ref[pos0 + i]), ssem.at[slot]).start()
            return carry

        def row_to_dump(i, carry):
            pltpu.make_async_copy(out_slab(obuf, slot, i), out_slab(dump, slot, i), ssem.at[slot]).start()
            return carry

        full = n_valid >> ROW_UNROLL_LOG2
        for_row_chunks(full, to_token)
        lax.fori_loop(full * ROW_UNROLL, n_valid, row_to_token, 0)
        lax.fori_loop(n_valid, tm, row_to_dump, 0)

    def wait_scatter(slot):
        pltpu.make_async_copy(obuf.at[slot], dump.at[slot], ssem.at[slot]).wait()

    slot = k % 2

    @pl.when(k == 0)
    def _():
        start_gather(0, 0)

    @pl.when(k + 1 < n_items)
    def _():
        start_gather(k + 1, 1 - slot)

    @pl.when(k < n_items)
    def _():
        group = item_group(k)
        wait_gather(slot)
        h = jnp.concatenate([gbuf[slot, c] for c in range(SLAB_ROWS)], axis=1).astype(bf16)
        w8 = gbuf[slot, SLAB_ROWS]
        cols = pl.ds(pl.multiple_of(group * GROUP_FF, GROUP_FF), GROUP_FF)
        gate = _dot(h, weg_ref[:, cols])
        a = gate * _sigmoid(gate) * _dot(h, weu_ref[:, cols])
        pieces = [a[:, j * EXPERT_FF:(j + 1) * EXPERT_FF] * w8[:, j:j + 1] for j in range(EXPERTS_PER_GROUP)]
        out = _dot(jnp.concatenate(pieces, axis=1).astype(bf16), wed_ref[group])

        @pl.when(k >= 2)
        def _():
            wait_scatter(slot)

        for c in range(SLAB_ROWS):
            obuf[slot, c] = out[:, c * LANES:(c + 1) * LANES]
        n_valid = jnp.minimum(tm, cnt_ref[group] - (k - plan_ref[PLAN_FIRST + group]) * tm)
        start_scatter(k, n_valid, slot)

    @pl.when(k == n_steps - 1)
    def _drain():
        wait_scatter((n_items - 1) % 2)

        @pl.when(n_items >= 2)
        def _():
            wait_scatter(n_items % 2)


def _moe_call(code, cnt, h2x, weg, weu, wed):
    n_tok = h2x.shape[0] // H2X_PITCH
    tm = MOE_TILE
    n_steps = n_tok // tm + N_EXPERT_GROUPS - 1
    ff = N_EXPERT_GROUPS * GROUP_FF
    up_spec = pl.BlockSpec((D_MODEL, ff), lambda k, *_: (0, 0), pipeline_mode=pl.Buffered(1))
    down_spec = pl.BlockSpec((N_EXPERT_GROUPS, GROUP_FF, D_MODEL), lambda k, *_: (0, 0, 0),
                             pipeline_mode=pl.Buffered(1))
    grid_spec = pltpu.PrefetchScalarGridSpec(
        num_scalar_prefetch=2,
        grid=(n_steps,),
        in_specs=[pl.BlockSpec(memory_space=pl.ANY), up_spec, up_spec, down_spec],
        out_specs=pl.BlockSpec(memory_space=pl.ANY),
        scratch_shapes=[
            pltpu.SMEM((N_EXPERT_GROUPS * n_tok,), jnp.int32),
            pltpu.SMEM((PLAN_LEN,), jnp.int32),
            pltpu.VMEM((2, H2X_PITCH, tm, LANES), f32),
            pltpu.VMEM((2, MOE_PITCH, tm, LANES), f32),
            pltpu.VMEM((2, MOE_PITCH, tm, LANES), f32),
            pltpu.SemaphoreType.DMA((2,)),
            pltpu.SemaphoreType.DMA((2,)),
        ],
    )
    return pl.pallas_call(
        functools.partial(_moe_kernel, n_tok=n_tok),
        grid_spec=grid_spec,
        out_shape=jax.ShapeDtypeStruct((n_tok * MOE_PITCH, LANES), f32),
        compiler_params=_cparams(("arbitrary",)),
        name="moe",
    )(code, cnt, h2x, weg, weu, wed)


def _final_kernel(x1_ref, moe_ref, mod_ref, gf_ref, o_ref):
    x2 = x1_ref[...] + mod_ref[0, 5:6, :] * _rows_from_slabs(moe_ref, 0, x1_ref.shape[0], MOE_PITCH)
    y = x2 * lax.rsqrt(jnp.mean(x2 * x2, axis=-1, keepdims=True) + NORM_EPS)
    o_ref[...] = y * gf_ref[...]


def _final_call(x1, moe, mod, gf, *, tokens_per_batch, cond_row0, per_batch_cond):
    n_tok = x1.shape[0]
    tok = pl.BlockSpec((TOKEN_TILE, D_MODEL), lambda t: (t, 0))
    slab = pl.BlockSpec((TOKEN_TILE * MOE_PITCH, LANES), lambda t: (t, 0))
    cond_idx = _cond_index_map(tokens_per_batch, cond_row0, per_batch_cond)
    return pl.pallas_call(
        _final_kernel,
        grid=(n_tok // TOKEN_TILE,),
        in_specs=[tok, slab, pl.BlockSpec((1, 6, D_MODEL), cond_idx), _resident((1, D_MODEL))],
        out_specs=tok,
        out_shape=jax.ShapeDtypeStruct((n_tok, D_MODEL), f32),
        compiler_params=_cparams(("arbitrary",)),
        name="final_norm",
    )(x1, moe, mod, gf)


def _rope_tables(n):
    rows = n // GRID_W
    row = jnp.repeat(jnp.arange(rows, dtype=f32), GRID_W)
    col = jnp.tile(jnp.arange(GRID_W, dtype=f32), rows)
    inv = ROPE_THETA ** (-jnp.arange(AXIS_FREQS, dtype=f32) / AXIS_FREQS)
    ang = jnp.concatenate([row[:, None] * inv, col[:, None] * inv], axis=-1)
    cos, sin = jnp.cos(ang), jnp.sin(ang)
    cos_t = jnp.concatenate([cos, cos, cos, cos], axis=-1)
    sin_t = jnp.concatenate([-sin, sin, -sin, sin], axis=-1)
    return cos_t, sin_t


def _dft_tables(n):
    k = np.arange(n)
    m = (k[:, None] * k[None, :]) % n
    ang = 2.0 * np.pi * m / n
    return np.cos(ang), np.sin(ang)


def _fourier_tables(seq):
    cn, sn = _dft_tables(seq)
    cc, sc = _dft_tables(FOURIER_GROUP_DIM)
    eye = np.eye(N_FOURIER_GROUPS)
    bdc, bds = np.kron(eye, cc), np.kron(eye, sc)
    return tuple(jnp.asarray(t, dtype=f32).astype(bf16) for t in (bdc, bds, cn, sn))


def _dup_cache(cache, l):
    b, _, _, p, _, _ = cache.shape
    dup = lambda x: jnp.broadcast_to(x[:, :, :, None, :], (b, p, N_KV, 2, HEAD_DIM)).reshape(b, p, KV_DUP_W)
    return dup(cache[:, l, 0]).astype(bf16), dup(cache[:, l, 1]).astype(bf16)


def kernel(x_prompt, x_sample, cache_global_kv, cache_window_kv, c, c_ctx, w_mod, b_mod, norm1_g, w_in,
           q_norm_g, k_norm_g, sink_logit, w_branch_a, w_branch_b, w_branch_c, w_out, norm2_g,
           w_route_group, w_route_expert, w_exp_gate, w_exp_up, w_exp_down, final_norm_g):
    batch, seq, _ = x_prompt.shape
    dec_batch, dec_seq, _ = x_sample.shape

    cond = jnp.zeros((N_COND_ROWS, D_MODEL), f32).at[0].set(c_ctx).at[1:1 + dec_batch].set(c)
    mod = _mod_call(cond, w_mod, b_mod).reshape(DEPTH, N_COND_ROWS, 6, D_MODEL)

    ones_blk = jnp.asarray(np.kron(np.eye(N_HEADS), np.ones((HEAD_DIM, HEAD_DIM))), dtype=bf16)
    tri = jnp.asarray(np.triu(np.ones((TOKEN_TILE, TOKEN_TILE))), dtype=bf16)
    rope_tabs = _rope_tables(dec_seq)
    ftab_ctx = _fourier_tables(seq)
    ftab_lat = _fourier_tables(dec_seq)
    gf = final_norm_g.reshape(1, D_MODEL)

    ctx = dict(tokens_per_batch=seq, cond_row0=0, per_batch_cond=False)
    lat = dict(tokens_per_batch=dec_seq, cond_row0=1, per_batch_cond=True)
    xp = (x_prompt.reshape(batch * seq, D_MODEL),)
    xs = (x_sample.reshape(dec_batch * dec_seq, D_MODEL),)
    mod_prev = None
    gkv, wkv = [], []
    for l in range(DEPTH):
        w_ext = w_gate = _layer_weight_bf16(w_in, l)
        g1 = norm1_g[l].reshape(1, D_MODEL)
        g2 = norm2_g[l].reshape(1, D_MODEL)
        qg = jnp.tile(q_norm_g[l], N_HEADS).reshape(1, Q_W)
        kg = jnp.tile(k_norm_g[l], N_KV).reshape(1, LANES)
        wba, wbb, wbc = (_layer_weight_bf16(w, l) for w in (w_branch_a, w_branch_b, w_branch_c))
        wout = _layer_weight_bf16(w_out, l)
        wrt = jnp.concatenate([w_route_expert[l].T, w_route_group[l].T,
                               jnp.zeros((ROUTE_ROWS - N_EXPERTS - N_EXPERT_GROUPS, D_MODEL), f32)], axis=0)
        weg, weu = _layer_weight_bf16(w_exp_gate, l), _layer_weight_bf16(w_exp_up, l)
        wed = _layer_weight_bf16(w_exp_down, l).reshape(N_EXPERT_GROUPS, GROUP_FF, D_MODEL)
        sink = sink_logit[l]

        def mix_and_moe(xin, qa, ka, va, qc, kc, vc, ub, attn, ftab, seq_len, stream):
            oa, oc = attn(qa, ka, va, qc, kc, vc)
            ob = _fourier_call(ub, *ftab, seq_len)
            x1, h2x, code, cnt = _s3_call(xin, mod_prev, oa, ob, oc, mod[l], g1, g2, w_gate, wba, wbb, wbc, wout,
                                          wrt, tri, **stream)
            counts = cnt[:, 0].astype(jnp.int32)
            moe = _moe_call(code.reshape(-1), counts, h2x, weg, weu, wed)
            return (x1, moe)

        qa, ka, va, qc, kc, vc, ub, gkv_l, wkv_l = _s1_call(
            xp, mod_prev, mod[l], g1, w_ext, ones_blk, qg, kg, None, emit_kv=True, **ctx)
        attn_ctx = lambda *qkv: _attn_ctx_call(sink, *qkv, seq)
        xp = mix_and_moe(xp, qa, ka, va, qc, kc, vc, ub, attn_ctx, ftab_ctx, seq, ctx)
        kv_shape = (batch, 2, seq, N_KV, HEAD_DIM)
        gkv.append(gkv_l.reshape(kv_shape))
        wkv.append(wkv_l.reshape(kv_shape))

        qa, ka, va, qc, kc, vc, ub = _s1_call(
            xs, mod_prev, mod[l], g1, w_ext, ones_blk, qg, kg, rope_tabs, emit_kv=False, **lat)
        kxa, vxa = _dup_cache(cache_global_kv, l)
        kxc, vxc = _dup_cache(cache_window_kv, l)
        attn_lat = lambda qa, ka, va, qc, kc, vc: _attn_lat_call(
            sink, qa, ka, va, kxa, vxa, qc, kc, vc, kxc, vxc, dec_seq)
        xs = mix_and_moe(xs, qa, ka, va, qc, kc, vc, ub, attn_lat, ftab_lat, dec_seq, lat)
        mod_prev = mod[l]

    y_prompt = _final_call(*xp, mod_prev, gf, **ctx).reshape(batch, seq, D_MODEL)
    y_sample = _final_call(*xs, mod_prev, gf, **lat).reshape(dec_batch, dec_seq, D_MODEL)
    return (y_prompt, y_sample, jnp.stack(gkv, axis=1), jnp.stack(wkv, axis=1))
```

```python
import functools

import jax
import jax.numpy as jnp
import numpy as np
from jax import lax
from jax.experimental import pallas as pl
from jax.experimental.pallas import tpu as pltpu

f32 = jnp.float32
bf16 = jnp.bfloat16

D_MODEL = 1024
DEPTH = 2
GRID_W = 64
HEAD_DIM = 64
HALF = HEAD_DIM // 2
N_HEADS = 8
N_KV = 2
WINDOW = 128
N_FOURIER_GROUPS = 4
FOURIER_GROUP_DIM = 128
FOURIER_DIM = N_FOURIER_GROUPS * FOURIER_GROUP_DIM
ROPE_THETA = 10000.0
AXIS_FREQS = HEAD_DIM // 4
N_EXPERT_GROUPS = 4
EXPERTS_PER_GROUP = 8
N_EXPERTS = N_EXPERT_GROUPS * EXPERTS_PER_GROUP
EXPERT_FF = 128
NORM_EPS = 1e-6
NEG_INF = -1e30
LOG2_E = 1.4426950408889634

LANES = 128
Q_W = N_HEADS * HEAD_DIM
KV_DUP_W = 2 * N_KV * HEAD_DIM
C_QA, C_KA, C_VA, C_QC, C_KC, C_VC, C_UB, C_GATES, C_END = 0, 512, 640, 768, 1280, 1408, 1536, 2048, 5120
N_COND_ROWS = 16
VMEM_LIMIT = 56 * 1024 * 1024


def _cparams(sem):
    return pltpu.CompilerParams(dimension_semantics=sem, vmem_limit_bytes=VMEM_LIMIT)


def _resident(shape):
    nd = len(shape)
    return pl.BlockSpec(shape, lambda *_: (0,) * nd, pipeline_mode=pl.Buffered(1))


def _sigmoid(x):
    return 0.5 * jnp.tanh(0.5 * x) + 0.5


def _rms_mod(x, g, scale, shift):
    y = x * lax.rsqrt(jnp.mean(x * x, axis=-1, keepdims=True) + NORM_EPS)
    return (y * g) * (1.0 + scale) + shift


def _dot(a, b):
    return jnp.dot(a, b, preferred_element_type=f32)


def _dot_t(a, b):
    return lax.dot_general(a, b, (((1,), (1,)), ((), ())), preferred_element_type=f32)


def _split_bf16(x):
    hi = x.astype(bf16)
    lo = (x - hi.astype(f32)).astype(bf16)
    return hi, lo


def _mod_kernel(cond_ref, w_ref, b_ref, o_ref):
    c = cond_ref[...]
    s = (c * _sigmoid(c)).astype(bf16)
    o_ref[0] = _dot(s, w_ref[0].astype(bf16)) + b_ref[0]


def _mod_call(cond, w_mod, b_mod):
    n_tiles = 6 * D_MODEL // 1024
    return pl.pallas_call(
        _mod_kernel,
        grid=(DEPTH, n_tiles),
        in_specs=[
            pl.BlockSpec((N_COND_ROWS, D_MODEL), lambda l, j: (0, 0)),
            pl.BlockSpec((1, D_MODEL, 1024), lambda l, j: (l, 0, j)),
            pl.BlockSpec((1, 1, 1024), lambda l, j: (l, 0, j)),
        ],
        out_specs=pl.BlockSpec((1, N_COND_ROWS, 1024), lambda l, j: (l, 0, j)),
        out_shape=jax.ShapeDtypeStruct((DEPTH, N_COND_ROWS, 6 * D_MODEL), f32),
        compiler_params=_cparams(("arbitrary", "arbitrary")),
        name="mod",
    )(cond, w_mod, b_mod.reshape(DEPTH, 1, 6 * D_MODEL))


CAST_BLOCK_BYTES = 4 * 1024 * 1024


def _cast_kernel(w_ref, o_ref):
    o_ref[...] = w_ref[0].astype(bf16)


def _layer_weight_bf16(w, l):
    _, rows, cols = w.shape
    rb = min(rows, 1 << ((CAST_BLOCK_BYTES // (4 * cols)).bit_length() - 1))
    assert rows % rb == 0
    return pl.pallas_call(
        _cast_kernel,
        grid=(rows // rb,),
        in_specs=[pl.BlockSpec((1, rb, cols), lambda i: (l, i, 0))],
        out_specs=pl.BlockSpec((rb, cols), lambda i: (i, 0)),
        out_shape=jax.ShapeDtypeStruct((rows, cols), bf16),
        compiler_params=_cparams(("arbitrary",)),
        name="cast_bf16",
    )(w)


def _head_norm(z, g, ones_blk):
    hi, lo = _split_bf16(z * z)
    ss = _dot(hi, ones_blk) + _dot(lo, ones_blk)
    return z * lax.rsqrt(ss * (1.0 / HEAD_DIM) + NORM_EPS) * g


def _rope(z, cos, sin_signed, first_half):
    outs = []
    for j in range(z.shape[1] // LANES):
        zj = z[:, j * LANES:(j + 1) * LANES]
        swapped = jnp.where(first_half, pltpu.roll(zj, LANES - HALF, 1), pltpu.roll(zj, HALF, 1))
        outs.append(zj * cos + swapped * sin_signed)
    return jnp.concatenate(outs, axis=1)


def _layer_input(x_refs, r0, n):
    rows = pl.ds(r0, n)
    if len(x_refs) == 1:
        return x_refs[0][rows, :]
    x1_ref, moe_ref, modp_ref = x_refs
    return x1_ref[rows, :] + modp_ref[0, 5:6, :] * _rows_from_slabs(moe_ref, r0 * SLAB_ROWS, n, SLAB_ROWS)


S1_SUB_TILES = 2
S3_SUB_TILES = 1


SLAB_ROWS = D_MODEL // LANES


def _rows_from_slabs(ref, start, n_tok, pitch):
    return jnp.concatenate([ref[pl.ds(start + c, n_tok, stride=pitch), :] for c in range(SLAB_ROWS)], axis=1)


def _rows_to_slabs(ref, start, x, pitch):
    for c in range(SLAB_ROWS):
        ref[pl.ds(start + c, x.shape[0], stride=pitch), :] = x[:, c * LANES:(c + 1) * LANES]


def _s1_kernel(*refs, rope, emit_kv, n_x):
    x_refs, refs = refs[:n_x], refs[n_x:]
    mod_ref, g1_ref, w_ref, ones_ref, qg_ref, kg_ref = refs[:6]
    pos = 6
    if rope:
        cos_ref, sin_ref = refs[pos:pos + 2]
        pos += 2
    qa_ref, ka_ref, va_ref, qc_ref, kc_ref, vc_ref, ub_ref = refs[pos:pos + 7]
    pos += 7
    if emit_kv:
        gkv_ref, wkv_ref = refs[pos:pos + 2]

    tm = x_refs[0].shape[0] // S1_SUB_TILES
    lane = lax.broadcasted_iota(jnp.int32, (tm, LANES), 1)
    low_lanes = lane < HEAD_DIM
    first_half = (lane & (HEAD_DIM - 1)) < HALF
    q_scale = HEAD_DIM ** -0.5 * LOG2_E

    def dup(z):
        r = pltpu.roll(z, HEAD_DIM, 1)
        return jnp.concatenate([jnp.where(low_lanes, z, r), jnp.where(low_lanes, r, z)], axis=1).astype(bf16)

    for sub in range(S1_SUB_TILES):
        rows = pl.ds(sub * tm, tm)
        x = _layer_input(x_refs, sub * tm, tm)
        h = _rms_mod(x, g1_ref[...], mod_ref[0, 1:2, :], mod_ref[0, 0:1, :]).astype(bf16)
        proj = lambda lo, hi: _dot(h, w_ref[:, lo:hi])
        if rope:
            cos = cos_ref[rows, :]
            sin_signed = sin_ref[rows, :]
            rot = lambda z: _rope(z, cos, sin_signed, first_half)
        else:
            rot = lambda z: z

        qa = rot(_head_norm(proj(C_QA, C_KA), qg_ref[...], ones_ref[...]))
        qa_ref[rows, :] = (qa * q_scale).astype(bf16)
        kva = proj(C_KA, C_QC)
        ka_n = _head_norm(kva[:, 0:LANES], kg_ref[...], ones_ref[0:LANES, 0:LANES])
        ka_ref[rows, :] = dup(rot(ka_n))
        va = kva[:, LANES:2 * LANES]
        va_ref[rows, :] = dup(va)
        qc_ref[rows, :] = (rot(proj(C_QC, C_KC)) * q_scale).astype(bf16)
        kvc = proj(C_KC, C_UB)
        kc = kvc[:, 0:LANES]
        kc_ref[rows, :] = dup(rot(kc))
        vc = kvc[:, LANES:2 * LANES]
        vc_ref[rows, :] = dup(vc)
        ub_ref[rows, :] = proj(C_UB, C_GATES).astype(bf16)
        if emit_kv:
            gkv_ref[sub, 0] = ka_n
            gkv_ref[sub, 1] = va
            wkv_ref[sub, 0] = kc
            wkv_ref[sub, 1] = vc


TOKEN_TILE = 512


def _cond_index_map(tokens_per_batch, cond_row0, per_batch_cond):
    tiles_per_batch = max(tokens_per_batch // TOKEN_TILE, 1)
    if per_batch_cond:
        assert tokens_per_batch % TOKEN_TILE == 0
        return lambda t: (cond_row0 + t // tiles_per_batch, 0, 0)
    return lambda t: (cond_row0, 0, 0)


def _layer_input_specs(xin, mod_prev, cond_idx):
    tok = pl.BlockSpec((TOKEN_TILE, D_MODEL), lambda t: (t, 0))
    if len(xin) == 1:
        return [tok], list(xin)
    slab = pl.BlockSpec((TOKEN_TILE * SLAB_ROWS, LANES), lambda t: (t, 0))
    return [tok, slab, pl.BlockSpec((1, 6, D_MODEL), cond_idx)], [xin[0], xin[1], mod_prev]


def _s1_call(xin, mod_prev, mod, g1, w_ext, ones_blk, qg, kg, rope_tabs, *, tokens_per_batch, cond_row0,
             per_batch_cond, emit_kv):
    n_tok = xin[0].shape[0]
    tm = TOKEN_TILE
    rope = rope_tabs is not None
    tiles_per_batch = max(tokens_per_batch // tm, 1)
    cond_idx = _cond_index_map(tokens_per_batch, cond_row0, per_batch_cond)

    tok = lambda w: pl.BlockSpec((tm, w), lambda t: (t, 0))
    x_specs, x_args = _layer_input_specs(xin, mod_prev, cond_idx)
    in_specs = x_specs + [
        pl.BlockSpec((1, 6, D_MODEL), cond_idx),
        _resident((1, D_MODEL)),
        _resident((D_MODEL, C_GATES)),
        _resident((Q_W, Q_W)),
        _resident((1, Q_W)),
        _resident((1, LANES)),
    ]
    args = x_args + [mod, g1, w_ext, ones_blk, qg, kg]
    if rope:
        in_specs += [pl.BlockSpec((tm, LANES), lambda t: (t % tiles_per_batch, 0))] * 2
        args += list(rope_tabs)
    widths = [Q_W, KV_DUP_W, KV_DUP_W, Q_W, KV_DUP_W, KV_DUP_W, FOURIER_DIM]
    out_specs = [tok(w) for w in widths]
    out_shape = [jax.ShapeDtypeStruct((n_tok, w), bf16) for w in widths]
    if emit_kv:
        assert tokens_per_batch * S1_SUB_TILES == tm
        n_batch = n_tok // tokens_per_batch
        out_specs += [pl.BlockSpec((S1_SUB_TILES, 2, tokens_per_batch, LANES), lambda t: (t, 0, 0, 0))] * 2
        out_shape += [jax.ShapeDtypeStruct((n_batch, 2, tokens_per_batch, LANES), f32)] * 2
    return pl.pallas_call(
        functools.partial(_s1_kernel, rope=rope, emit_kv=emit_kv, n_x=len(x_args)),
        grid=(n_tok // tm,),
        in_specs=in_specs,
        out_specs=out_specs,
        out_shape=out_shape,
        compiler_params=_cparams(("arbitrary",)),
        name="s1_rope" if rope else "s1_ctx",
    )(*args)


def _gqa_group(q2, segs, sink_pair_lo, sink_pair_hi):
    qb = q2.shape[0]
    qs = jnp.concatenate([q2[:, 0:LANES], q2[:, LANES:2 * LANES]], axis=0)
    lane = lax.broadcasted_iota(jnp.int32, qs.shape, 1)
    low = lane < HEAD_DIM
    zero = jnp.zeros_like(qs)
    q4 = jnp.concatenate([jnp.where(low, qs, zero), jnp.where(low, zero, qs)], axis=0)

    scores = []
    for k, _, ok in segs:
        s = _dot_t(q4, k)
        if ok is not None:
            s = jnp.where(ok, s, NEG_INF)
        scores.append(s)
    m = scores[0].max(axis=-1, keepdims=True)
    for s in scores[1:]:
        m = jnp.maximum(m, s.max(axis=-1, keepdims=True))
    if sink_pair_lo is not None:
        row = lax.broadcasted_iota(jnp.int32, (4 * qb, 1), 0)
        sink = jnp.where(row < 2 * qb, jnp.where(row < qb, sink_pair_lo[0], sink_pair_lo[1]),
                         jnp.where(row < 3 * qb, sink_pair_hi[0], sink_pair_hi[1]))
        m = jnp.maximum(m, sink)
        denom = jnp.exp2(sink - m)
    else:
        denom = jnp.zeros_like(m)
    acc = None
    for s, (_, v, _) in zip(scores, segs):
        p = jnp.exp2(s - m)
        denom = denom + p.sum(axis=-1, keepdims=True)
        pv = _dot(p.astype(bf16), v)
        acc = pv if acc is None else acc + pv
    o4 = acc / denom
    o = jnp.where(low, o4[0:2 * qb], o4[2 * qb:4 * qb])
    return jnp.concatenate([o[0:qb], o[qb:2 * qb]], axis=1)


def _sink_pairs(sink_ref, kv):
    h = 4 * kv
    s = [sink_ref[h + j] * LOG2_E for j in range(4)]
    return (s[0], s[2]), (s[1], s[3])


ATTN_CTX_BATCH = 2


def _attn_ctx_kernel(sink_ref, qa_ref, ka_ref, va_ref, qc_ref, kc_ref, vc_ref, oa_ref, oc_ref, *, seq):
    for b in range(ATTN_CTX_BATCH):
        rows = slice(b * seq, (b + 1) * seq)
        for kv in range(N_KV):
            cq = slice(2 * LANES * kv, 2 * LANES * (kv + 1))
            ck = slice(LANES * kv, LANES * (kv + 1))
            oa = _gqa_group(qa_ref[rows, cq], [(ka_ref[rows, ck], va_ref[rows, ck], None)], None, None)
            oa_ref[rows, cq] = oa.astype(bf16)
            lo, hi = _sink_pairs(sink_ref, kv)
            oc = _gqa_group(qc_ref[rows, cq], [(kc_ref[rows, ck], vc_ref[rows, ck], None)], lo, hi)
            oc_ref[rows, cq] = oc.astype(bf16)


def _attn_ctx_call(sink, qa, ka, va, qc, kc, vc, seq):
    n_tok = qa.shape[0]
    rows = ATTN_CTX_BATCH * seq
    tok = lambda w: pl.BlockSpec((rows, w), lambda b: (b, 0))
    return pl.pallas_call(
        functools.partial(_attn_ctx_kernel, seq=seq),
        grid=(n_tok // rows,),
        in_specs=[pl.BlockSpec(memory_space=pltpu.SMEM),
                  tok(Q_W), tok(KV_DUP_W), tok(KV_DUP_W), tok(Q_W), tok(KV_DUP_W), tok(KV_DUP_W)],
        out_specs=[tok(Q_W), tok(Q_W)],
        out_shape=[jax.ShapeDtypeStruct((n_tok, Q_W), bf16)] * 2,
        compiler_params=_cparams(("arbitrary",)),
        name="attn_ctx",
    )(sink, qa, ka, va, qc, kc, vc)


QB = 256
WIN_QB = 128
WIN_SPAN = WIN_QB + 2 * WINDOW


def _attn_lat_kernel(sink_ref, qa_ref, ka_ref, va_ref, kxa_ref, vxa_ref,
                     qc_ref, kc_ref, vc_ref, kxc_ref, vxc_ref, oa_ref, oc_ref, *, seq):
    i = pl.program_id(1)
    for kv in range(N_KV):
        cq = slice(2 * LANES * kv, 2 * LANES * (kv + 1))
        ck = slice(LANES * kv, LANES * (kv + 1))
        segs = [(ka_ref[:, ck], va_ref[:, ck], None), (kxa_ref[0, :, ck], vxa_ref[0, :, ck], None)]
        oa_ref[:, cq] = _gqa_group(qa_ref[:, cq], segs, None, None).astype(bf16)

        lo, hi = _sink_pairs(sink_ref, kv)
        for sub in range(QB // WIN_QB):
            blk = i * (QB // WIN_QB) + sub
            start = jnp.clip(blk * WIN_QB - WINDOW, 0, seq - WIN_SPAN)
            start = pl.multiple_of(start, WINDOW)
            r = lax.broadcasted_iota(jnp.int32, (4 * WIN_QB, WIN_SPAN), 0)
            c = lax.broadcasted_iota(jnp.int32, (4 * WIN_QB, WIN_SPAN), 1)
            qpos = blk * WIN_QB + (r & (WIN_QB - 1))
            ok = jnp.abs(qpos - (start + c)) <= WINDOW
            rows = slice(sub * WIN_QB, (sub + 1) * WIN_QB)
            segs = [(kc_ref[pl.ds(start, WIN_SPAN), ck], vc_ref[pl.ds(start, WIN_SPAN), ck], ok),
                    (kxc_ref[0, :, ck], vxc_ref[0, :, ck], None)]
            oc_ref[rows, cq] = _gqa_group(qc_ref[rows, cq], segs, lo, hi).astype(bf16)


def _attn_lat_call(sink, qa, ka, va, kxa, vxa, qc, kc, vc, kxc, vxc, seq):
    n_tok = qa.shape[0]
    nq = seq // QB
    past = kxa.shape[1]
    qspec = pl.BlockSpec((QB, Q_W), lambda b, i: (b * nq + i, 0))
    kspec = pl.BlockSpec((seq, KV_DUP_W), lambda b, i: (b, 0))
    xspec = pl.BlockSpec((1, past, KV_DUP_W), lambda b, i: (b, 0, 0))
    return pl.pallas_call(
        functools.partial(_attn_lat_kernel, seq=seq),
        grid=(n_tok // seq, nq),
        in_specs=[pl.BlockSpec(memory_space=pltpu.SMEM),
                  qspec, kspec, kspec, xspec, xspec, qspec, kspec, kspec, xspec, xspec],
        out_specs=[qspec, qspec],
        out_shape=[jax.ShapeDtypeStruct((n_tok, Q_W), bf16)] * 2,
        compiler_params=_cparams(("arbitrary", "arbitrary")),
        name="attn_lat",
    )(sink, qa, ka, va, kxa, vxa, qc, kc, vc, kxc, vxc)


FOURIER_ROWS = 1024


def _fourier_kernel(u_ref, bdc_ref, bds_ref, cn_ref, sn_ref, o_ref, *, scale, seq):
    u = u_ref[...]
    uc = _dot(u, bdc_ref[...]).astype(bf16)
    us = _dot(u, bds_ref[...]).astype(bf16)
    for b in range(u.shape[0] // seq):
        rows = slice(b * seq, (b + 1) * seq)
        o = _dot(cn_ref[...], uc[rows]) - _dot(sn_ref[...], us[rows])
        o_ref[rows, :] = (o * scale).astype(bf16)


def _fourier_call(ub, bdc, bds, cn, sn, seq):
    n_tok = ub.shape[0]
    rows = max(seq, FOURIER_ROWS)
    tok = pl.BlockSpec((rows, FOURIER_DIM), lambda b: (b, 0))
    scale = float(1.0 / np.sqrt(seq * FOURIER_GROUP_DIM))
    return pl.pallas_call(
        functools.partial(_fourier_kernel, scale=scale, seq=seq),
        grid=(n_tok // rows,),
        in_specs=[tok, _resident((FOURIER_DIM, FOURIER_DIM)), _resident((FOURIER_DIM, FOURIER_DIM)),
                  _resident((seq, seq)), _resident((seq, seq))],
        out_specs=tok,
        out_shape=jax.ShapeDtypeStruct((n_tok, FOURIER_DIM), bf16),
        compiler_params=_cparams(("arbitrary",)),
        name=f"fourier_{seq}",
    )(ub, bdc, bds, cn, sn)


ROUTE_ROWS = LANES
H2X_PITCH = SLAB_ROWS + 1
MOE_PITCH = SLAB_ROWS


def _route(lt, tri_ref, carry_ref, n_tok):
    tm = lt.shape[1]
    row = lax.broadcasted_iota(jnp.int32, (EXPERTS_PER_GROUP, tm), 0).astype(f32)
    neg = jnp.float32(-jnp.inf)
    none = jnp.float32(EXPERTS_PER_GROUP)
    gl = jnp.where(row < N_EXPERT_GROUPS, lt[N_EXPERTS:N_EXPERTS + EXPERTS_PER_GROUP], neg)
    gmax = gl.max(axis=0, keepdims=True)
    gidx = jnp.where(gl == gmax, row, none).min(axis=0, keepdims=True)
    gw = 1.0 / jnp.exp(gl - gmax).sum(axis=0, keepdims=True)
    lg = lt[0:EXPERTS_PER_GROUP]
    for g in range(1, N_EXPERT_GROUPS):
        lg = jnp.where(gidx == g, lt[g * EXPERTS_PER_GROUP:(g + 1) * EXPERTS_PER_GROUP], lg)
    m1 = lg.max(axis=0, keepdims=True)
    i1 = jnp.where(lg == m1, row, none).min(axis=0, keepdims=True)
    l2 = jnp.where(row == i1, neg, lg)
    m2 = l2.max(axis=0, keepdims=True)
    i2 = jnp.where(l2 == m2, row, none).min(axis=0, keepdims=True)
    e = jnp.exp(m2 - m1)
    w8 = jnp.where(row == i1, gw / (1.0 + e), jnp.where(row == i2, gw * e / (1.0 + e), 0.0))

    onehot = (row == gidx).astype(f32)
    prefix = _dot(onehot.astype(bf16), tri_ref[0:tm, 0:tm])
    carry = carry_ref[...]
    rank = (onehot * (prefix - 1.0 + carry[:, 0:1])).sum(axis=0, keepdims=True)
    carry_ref[...] = carry + onehot.sum(axis=1, keepdims=True)
    dest = gidx.astype(jnp.int32) * n_tok + rank.astype(jnp.int32)
    return w8, dest


def _s3_kernel(*refs, n_x, n_tok):
    x_refs, refs = refs[:n_x], refs[n_x:]
    (oa_ref, ob_ref, oc_ref, mod_ref, g1_ref, g2_ref, wga_ref, wgb_ref, wgc_ref, wba_ref, wbb_ref, wbc_ref,
     wout_ref, wrt_ref, tri_ref, x1_ref, h2x_ref, code_ref, cnt_ref, carry_ref) = refs
    wg_refs = (wga_ref, wgb_ref, wgc_ref)

    @pl.when(pl.program_id(0) == 0)
    def _():
        carry_ref[...] = jnp.zeros_like(carry_ref)

    tm = x1_ref.shape[0] // S3_SUB_TILES
    wh, wl = _split_bf16(wrt_ref[...])
    for sub in range(S3_SUB_TILES):
        r0 = sub * tm
        rows = pl.ds(r0, tm)
        x = _layer_input(x_refs, r0, tm)
        h = _rms_mod(x, g1_ref[...], mod_ref[0, 1:2, :], mod_ref[0, 0:1, :]).astype(bf16)
        merged = None
        for j, (o_ref, wb_ref) in enumerate(((oa_ref, wba_ref), (ob_ref, wbb_ref), (oc_ref, wbc_ref))):
            gate = _sigmoid(_dot(h, wg_refs[j][...]))
            term = gate * _dot(o_ref[rows, :], wb_ref[...])
            merged = term if merged is None else merged + term
        mix = _dot(merged.astype(bf16), wout_ref[...])
        x1 = x + mod_ref[0, 2:3, :] * mix
        x1_ref[rows, :] = x1
        h2 = _rms_mod(x1, g2_ref[...], mod_ref[0, 4:5, :], mod_ref[0, 3:4, :])
        _rows_to_slabs(h2x_ref, r0 * H2X_PITCH, h2, H2X_PITCH)
        th, tl = _split_bf16(h2)
        lt = _dot_t(wh, th) + (_dot_t(wh, tl) + _dot_t(wl, th))
        w8, code = _route(lt, tri_ref, carry_ref, n_tok)
        code_ref[:, rows] = code
        w_rows = jnp.concatenate([w8, jnp.zeros((LANES - EXPERTS_PER_GROUP, tm), f32)], axis=0)
        h2x_ref[pl.ds(r0 * H2X_PITCH + SLAB_ROWS, tm, stride=H2X_PITCH), :] = w_rows.T
    cnt_ref[...] = carry_ref[...]


def _s3_call(xin, mod_prev, oa, ob, oc, mod, g1, g2, wg, wba, wbb, wbc, wout, wrt, tri, *, tokens_per_batch,
             cond_row0, per_batch_cond):
    n_tok = xin[0].shape[0]
    tm = TOKEN_TILE
    cond_idx = _cond_index_map(tokens_per_batch, cond_row0, per_batch_cond)
    tok = lambda w: pl.BlockSpec((tm, w), lambda t: (t, 0))
    x_specs, x_args = _layer_input_specs(xin, mod_prev, cond_idx)
    gate_cols = lambda j: pl.BlockSpec((D_MODEL, D_MODEL), lambda t: (0, C_GATES // D_MODEL + j),
                                       pipeline_mode=pl.Buffered(1))
    return pl.pallas_call(
        functools.partial(_s3_kernel, n_x=len(x_args), n_tok=n_tok),
        grid=(n_tok // tm,),
        in_specs=x_specs + [
            tok(Q_W), tok(FOURIER_DIM), tok(Q_W),
            pl.BlockSpec((1, 6, D_MODEL), cond_idx),
            _resident((1, D_MODEL)), _resident((1, D_MODEL)),
            gate_cols(0), gate_cols(1), gate_cols(2),
            _resident((Q_W, D_MODEL)), _resident((FOURIER_DIM, D_MODEL)), _resident((Q_W, D_MODEL)),
            _resident((D_MODEL, D_MODEL)), _resident((ROUTE_ROWS, D_MODEL)), _resident((tm, tm))],
        out_specs=[tok(D_MODEL), pl.BlockSpec((tm * H2X_PITCH, LANES), lambda t: (t, 0)),
                   pl.BlockSpec((1, tm), lambda t: (0, t)),
                   pl.BlockSpec((EXPERTS_PER_GROUP, LANES), lambda t: (0, 0))],
        out_shape=[jax.ShapeDtypeStruct((n_tok, D_MODEL), f32),
                   jax.ShapeDtypeStruct((n_tok * H2X_PITCH, LANES), f32),
                   jax.ShapeDtypeStruct((1, n_tok), jnp.int32),
                   jax.ShapeDtypeStruct((EXPERTS_PER_GROUP, LANES), f32)],
        scratch_shapes=[pltpu.VMEM((EXPERTS_PER_GROUP, LANES), f32)],
        compiler_params=_cparams(("arbitrary",)),
        name="s3",
    )(*x_args, oa, ob, oc, mod, g1, g2, wg, wg, wg, wba, wbb, wbc, wout, wrt, tri)


MOE_TILE_LOG2 = 8
MOE_TILE = 1 << MOE_TILE_LOG2
ROW_UNROLL_LOG2 = 3
ROW_UNROLL = 1 << ROW_UNROLL_LOG2
GROUP_FF = EXPERTS_PER_GROUP * EXPERT_FF
PLAN_FIRST, PLAN_ITEMS, PLAN_LEN = 0, N_EXPERT_GROUPS, N_EXPERT_GROUPS + 1


def _moe_kernel(code_ref, cnt_ref, h2x_hbm, weg_ref, weu_ref, wed_ref, out_hbm,
                pos_ref, plan_ref, gbuf, obuf, dump, gsem, ssem, *, n_tok):
    tm = MOE_TILE
    k = pl.program_id(0)
    n_steps = pl.num_programs(0)

    def out_slab(buf, slot, i):
        return buf.at[slot, :, i]

    def token_slab(tok):
        return out_hbm.at[pl.ds(pl.multiple_of(tok * MOE_PITCH, MOE_PITCH), MOE_PITCH)]

    def for_row_chunks(n_chunks, body):
        def chunk(c, carry):
            body(c * ROW_UNROLL)
            return carry
        lax.fori_loop(0, n_chunks, chunk, 0)

    @pl.when(k == 0)
    def _plan():
        first = jnp.int32(0)
        for g in range(N_EXPERT_GROUPS):
            c = cnt_ref[g]
            n_tiles = (c + (tm - 1)) >> MOE_TILE_LOG2
            plan_ref[PLAN_FIRST + g] = first

            def pad(p, carry):
                pos_ref[p] = 0
                return carry

            lax.fori_loop(g * n_tok + c, g * n_tok + n_tiles * tm, pad, 0)
            first = first + n_tiles
        plan_ref[PLAN_ITEMS] = first

        def place(t0):
            dests = [code_ref[t0 + j] for j in range(ROW_UNROLL)]
            for j, dest in enumerate(dests):
                pos_ref[dest] = t0 + j

        for_row_chunks(n_tok >> ROW_UNROLL_LOG2, place)

    n_items = plan_ref[PLAN_ITEMS]

    def item_group(item):
        return ((item >= plan_ref[PLAN_FIRST + 1]).astype(jnp.int32)
                + (item >= plan_ref[PLAN_FIRST + 2]).astype(jnp.int32)
                + (item >= plan_ref[PLAN_FIRST + 3]).astype(jnp.int32))

    def item_pos0(item):
        g = item_group(item)
        return g * n_tok + (item - plan_ref[PLAN_FIRST + g]) * tm

    def start_gather(item, slot):
        pos0 = item_pos0(item)

        def chunk(i0):
            toks = [pos_ref[pos0 + i0 + j] for j in range(ROW_UNROLL)]
            for j, tok in enumerate(toks):
                pltpu.make_async_copy(h2x_hbm.at[pl.ds(tok * H2X_PITCH, H2X_PITCH)],
                                      gbuf.at[slot, :, i0 + j],
                                      gsem.at[slot]).start(priority=j % 2)
        for_row_chunks(tm >> ROW_UNROLL_LOG2, chunk)

    def wait_gather(slot):
        pltpu.make_async_copy(gbuf.at[1 - slot], gbuf.at[slot], gsem.at[slot]).wait()

    def start_scatter(item, n_valid, slot):
        pos0 = item_pos0(item)

        def to_token(i0):
            dsts = [pos_ref[pos0 + i0 + j] for j in range(ROW_UNROLL)]
            for j, dst in enumerate(dsts):
                pltpu.make_async_copy(out_slab(obuf, slot, i0 + j), token_slab(dst),
                                      ssem.at[slot]).start(priority=j % 2)

        def row_to_token(i, carry):
            pltpu.make_async_copy(out_slab(obuf, slot, i), token_slab(pos_ref[pos0 + i]), ssem.at[slot]).start()
            return carry

        def row_to_dump(i, carry):
            pltpu.make_async_copy(out_slab(obuf, slot, i), out_slab(dump, slot, i), ssem.at[slot]).start()
            return carry

        full = n_valid >> ROW_UNROLL_LOG2
        for_row_chunks(full, to_token)
        lax.fori_loop(full * ROW_UNROLL, n_valid, row_to_token, 0)
        lax.fori_loop(n_valid, tm, row_to_dump, 0)

    def wait_scatter(slot):
        pltpu.make_async_copy(obuf.at[slot], dump.at[slot], ssem.at[slot]).wait()

    slot = k % 2

    @pl.when(k == 0)
    def _():
        start_gather(0, 0)

    @pl.when(k + 1 < n_items)
    def _():
        start_gather(k + 1, 1 - slot)

    @pl.when(k < n_items)
    def _():
        group = item_group(k)
        wait_gather(slot)
        h = jnp.concatenate([gbuf[slot, c] for c in range(SLAB_ROWS)], axis=1).astype(bf16)
        w8 = gbuf[slot, SLAB_ROWS]
        cols = pl.ds(pl.multiple_of(group * GROUP_FF, GROUP_FF), GROUP_FF)
        gate = _dot(h, weg_ref[:, cols])
        a = gate * _sigmoid(gate) * _dot(h, weu_ref[:, cols])
        pieces = [a[:, j * EXPERT_FF:(j + 1) * EXPERT_FF] * w8[:, j:j + 1] for j in range(EXPERTS_PER_GROUP)]
        out = _dot(jnp.concatenate(pieces, axis=1).astype(bf16), wed_ref[group])

        @pl.when(k >= 2)
        def _():
            wait_scatter(slot)

        for c in range(SLAB_ROWS):
            obuf[slot, c] = out[:, c * LANES:(c + 1) * LANES]
        n_valid = jnp.minimum(tm, cnt_ref[group] - (k - plan_ref[PLAN_FIRST + group]) * tm)
        start_scatter(k, n_valid, slot)

    @pl.when(k == n_steps - 1)
    def _drain():
        wait_scatter((n_items - 1) % 2)

        @pl.when(n_items >= 2)
        def _():
            wait_scatter(n_items % 2)


def _moe_call(code, cnt, h2x, weg, weu, wed):
    n_tok = h2x.shape[0] // H2X_PITCH
    tm = MOE_TILE
    n_steps = n_tok // tm + N_EXPERT_GROUPS - 1
    ff = N_EXPERT_GROUPS * GROUP_FF
    up_spec = pl.BlockSpec((D_MODEL, ff), lambda k, *_: (0, 0), pipeline_mode=pl.Buffered(1))
    down_spec = pl.BlockSpec((N_EXPERT_GROUPS, GROUP_FF, D_MODEL), lambda k, *_: (0, 0, 0),
                             pipeline_mode=pl.Buffered(1))
    grid_spec = pltpu.PrefetchScalarGridSpec(
        num_scalar_prefetch=2,
        grid=(n_steps,),
        in_specs=[pl.BlockSpec(memory_space=pl.ANY), up_spec, up_spec, down_spec],
        out_specs=pl.BlockSpec(memory_space=pl.ANY),
        scratch_shapes=[
            pltpu.SMEM((N_EXPERT_GROUPS * n_tok,), jnp.int32),
            pltpu.SMEM((PLAN_LEN,), jnp.int32),
            pltpu.VMEM((2, H2X_PITCH, tm, LANES), f32),
            pltpu.VMEM((2, MOE_PITCH, tm, LANES), f32),
            pltpu.VMEM((2, MOE_PITCH, tm, LANES), f32),
            pltpu.SemaphoreType.DMA((2,)),
            pltpu.SemaphoreType.DMA((2,)),
        ],
    )
    return pl.pallas_call(
        functools.partial(_moe_kernel, n_tok=n_tok),
        grid_spec=grid_spec,
        out_shape=jax.ShapeDtypeStruct((n_tok * MOE_PITCH, LANES), f32),
        compiler_params=_cparams(("arbitrary",)),
        name="moe",
    )(code, cnt, h2x, weg, weu, wed)


def _final_kernel(x1_ref, moe_ref, mod_ref, gf_ref, o_ref):
    x2 = x1_ref[...] + mod_ref[0, 5:6, :] * _rows_from_slabs(moe_ref, 0, x1_ref.shape[0], MOE_PITCH)
    y = x2 * lax.rsqrt(jnp.mean(x2 * x2, axis=-1, keepdims=True) + NORM_EPS)
    o_ref[...] = y * gf_ref[...]


def _final_call(x1, moe, mod, gf, *, tokens_per_batch, cond_row0, per_batch_cond):
    n_tok = x1.shape[0]
    tok = pl.BlockSpec((TOKEN_TILE, D_MODEL), lambda t: (t, 0))
    slab = pl.BlockSpec((TOKEN_TILE * MOE_PITCH, LANES), lambda t: (t, 0))
    cond_idx = _cond_index_map(tokens_per_batch, cond_row0, per_batch_cond)
    return pl.pallas_call(
        _final_kernel,
        grid=(n_tok // TOKEN_TILE,),
        in_specs=[tok, slab, pl.BlockSpec((1, 6, D_MODEL), cond_idx), _resident((1, D_MODEL))],
        out_specs=tok,
        out_shape=jax.ShapeDtypeStruct((n_tok, D_MODEL), f32),
        compiler_params=_cparams(("arbitrary",)),
        name="final_norm",
    )(x1, moe, mod, gf)


def _rope_tables(n):
    rows = n // GRID_W
    row = jnp.repeat(jnp.arange(rows, dtype=f32), GRID_W)
    col = jnp.tile(jnp.arange(GRID_W, dtype=f32), rows)
    inv = ROPE_THETA ** (-jnp.arange(AXIS_FREQS, dtype=f32) / AXIS_FREQS)
    ang = jnp.concatenate([row[:, None] * inv, col[:, None] * inv], axis=-1)
    cos, sin = jnp.cos(ang), jnp.sin(ang)
    cos_t = jnp.concatenate([cos, cos, cos, cos], axis=-1)
    sin_t = jnp.concatenate([-sin, sin, -sin, sin], axis=-1)
    return cos_t, sin_t


def _dft_tables(n):
    k = np.arange(n)
    m = (k[:, None] * k[None, :]) % n
    ang = 2.0 * np.pi * m / n
    return np.cos(ang), np.sin(ang)


def _fourier_tables(seq):
    cn, sn = _dft_tables(seq)
    cc, sc = _dft_tables(FOURIER_GROUP_DIM)
    eye = np.eye(N_FOURIER_GROUPS)
    bdc, bds = np.kron(eye, cc), np.kron(eye, sc)
    return tuple(jnp.asarray(t, dtype=f32).astype(bf16) for t in (bdc, bds, cn, sn))


def _dup_cache(cache, l):
    b, _, _, p, _, _ = cache.shape
    dup = lambda x: jnp.broadcast_to(x[:, :, :, None, :], (b, p, N_KV, 2, HEAD_DIM)).reshape(b, p, KV_DUP_W)
    return dup(cache[:, l, 0]).astype(bf16), dup(cache[:, l, 1]).astype(bf16)


def kernel(x_prompt, x_sample, cache_global_kv, cache_window_kv, c, c_ctx, w_mod, b_mod, norm1_g, w_in,
           q_norm_g, k_norm_g, sink_logit, w_branch_a, w_branch_b, w_branch_c, w_out, norm2_g,
           w_route_group, w_route_expert, w_exp_gate, w_exp_up, w_exp_down, final_norm_g):
    batch, seq, _ = x_prompt.shape
    dec_batch, dec_seq, _ = x_sample.shape

    cond = jnp.zeros((N_COND_ROWS, D_MODEL), f32).at[0].set(c_ctx).at[1:1 + dec_batch].set(c)
    mod = _mod_call(cond, w_mod, b_mod).reshape(DEPTH, N_COND_ROWS, 6, D_MODEL)

    ones_blk = jnp.asarray(np.kron(np.eye(N_HEADS), np.ones((HEAD_DIM, HEAD_DIM))), dtype=bf16)
    tri = jnp.asarray(np.triu(np.ones((TOKEN_TILE, TOKEN_TILE))), dtype=bf16)
    rope_tabs = _rope_tables(dec_seq)
    ftab_ctx = _fourier_tables(seq)
    ftab_lat = _fourier_tables(dec_seq)
    gf = final_norm_g.reshape(1, D_MODEL)

    ctx = dict(tokens_per_batch=seq, cond_row0=0, per_batch_cond=False)
    lat = dict(tokens_per_batch=dec_seq, cond_row0=1, per_batch_cond=True)
    xp = (x_prompt.reshape(batch * seq, D_MODEL),)
    xs = (x_sample.reshape(dec_batch * dec_seq, D_MODEL),)
    mod_prev = None
    gkv, wkv = [], []
    for l in range(DEPTH):
        w_ext = w_gate = _layer_weight_bf16(w_in, l)
        g1 = norm1_g[l].reshape(1, D_MODEL)
        g2 = norm2_g[l].reshape(1, D_MODEL)
        qg = jnp.tile(q_norm_g[l], N_HEADS).reshape(1, Q_W)
        kg = jnp.tile(k_norm_g[l], N_KV).reshape(1, LANES)
        wba, wbb, wbc = (_layer_weight_bf16(w, l) for w in (w_branch_a, w_branch_b, w_branch_c))
        wout = _layer_weight_bf16(w_out, l)
        wrt = jnp.concatenate([w_route_expert[l].T, w_route_group[l].T,
                               jnp.zeros((ROUTE_ROWS - N_EXPERTS - N_EXPERT_GROUPS, D_MODEL), f32)], axis=0)
        weg, weu = _layer_weight_bf16(w_exp_gate, l), _layer_weight_bf16(w_exp_up, l)
        wed = _layer_weight_bf16(w_exp_down, l).reshape(N_EXPERT_GROUPS, GROUP_FF, D_MODEL)
        sink = sink_logit[l]

        def mix_and_moe(xin, qa, ka, va, qc, kc, vc, ub, attn, ftab, seq_len, stream):
            oa, oc = attn(qa, ka, va, qc, kc, vc)
            ob = _fourier_call(ub, *ftab, seq_len)
            x1, h2x, code, cnt = _s3_call(xin, mod_prev, oa, ob, oc, mod[l], g1, g2, w_gate, wba, wbb, wbc, wout,
                                          wrt, tri, **stream)
            counts = cnt[:, 0].astype(jnp.int32)
            moe = _moe_call(code.reshape(-1), counts, h2x, weg, weu, wed)
            return (x1, moe)

        qa, ka, va, qc, kc, vc, ub, gkv_l, wkv_l = _s1_call(
            xp, mod_prev, mod[l], g1, w_ext, ones_blk, qg, kg, None, emit_kv=True, **ctx)
        attn_ctx = lambda *qkv: _attn_ctx_call(sink, *qkv, seq)
        xp = mix_and_moe(xp, qa, ka, va, qc, kc, vc, ub, attn_ctx, ftab_ctx, seq, ctx)
        kv_shape = (batch, 2, seq, N_KV, HEAD_DIM)
        gkv.append(gkv_l.reshape(kv_shape))
        wkv.append(wkv_l.reshape(kv_shape))

        qa, ka, va, qc, kc, vc, ub = _s1_call(
            xs, mod_prev, mod[l], g1, w_ext, ones_blk, qg, kg, rope_tabs, emit_kv=False, **lat)
        kxa, vxa = _dup_cache(cache_global_kv, l)
        kxc, vxc = _dup_cache(cache_window_kv, l)
        attn_lat = lambda qa, ka, va, qc, kc, vc: _attn_lat_call(
            sink, qa, ka, va, kxa, vxa, qc, kc, vc, kxc, vxc, dec_seq)
        xs = mix_and_moe(xs, qa, ka, va, qc, kc, vc, ub, attn_lat, ftab_lat, dec_seq, lat)
        mod_prev = mod[l]

    y_prompt = _final_call(*xp, mod_prev, gf, **ctx).reshape(batch, seq, D_MODEL)
    y_sample = _final_call(*xs, mod_prev, gf, **lat).reshape(dec_batch, dec_seq, D_MODEL)
    return (y_prompt, y_sample, jnp.stack(gkv, axis=1), jnp.stack(wkv, axis=1))
```

```python
import functools

import jax
import jax.numpy as jnp
import numpy as np
from jax import lax
from jax.experimental import pallas as pl
from jax.experimental.pallas import tpu as pltpu

f32 = jnp.float32
bf16 = jnp.bfloat16

D_MODEL = 1024
DEPTH = 2
GRID_W = 64
HEAD_DIM = 64
HALF = HEAD_DIM // 2
N_HEADS = 8
N_KV = 2
WINDOW = 128
N_FOURIER_GROUPS = 4
FOURIER_GROUP_DIM = 128
FOURIER_DIM = N_FOURIER_GROUPS * FOURIER_GROUP_DIM
ROPE_THETA = 10000.0
AXIS_FREQS = HEAD_DIM // 4
N_EXPERT_GROUPS = 4
EXPERTS_PER_GROUP = 8
N_EXPERTS = N_EXPERT_GROUPS * EXPERTS_PER_GROUP
EXPERT_FF = 128
NORM_EPS = 1e-6
NEG_INF = -1e30
LOG2_E = 1.4426950408889634

LANES = 128
Q_W = N_HEADS * HEAD_DIM
KV_DUP_W = 2 * N_KV * HEAD_DIM
C_QA, C_KA, C_VA, C_QC, C_KC, C_VC, C_UB, C_GATES, C_END = 0, 512, 640, 768, 1280, 1408, 1536, 2048, 5120
N_COND_ROWS = 16
VMEM_LIMIT = 56 * 1024 * 1024


def _cparams(sem):
    return pltpu.CompilerParams(dimension_semantics=sem, vmem_limit_bytes=VMEM_LIMIT)


def _resident(shape):
    nd = len(shape)
    return pl.BlockSpec(shape, lambda *_: (0,) * nd, pipeline_mode=pl.Buffered(1))


def _sigmoid(x):
    return 0.5 * jnp.tanh(0.5 * x) + 0.5


def _rms_mod(x, g, scale, shift):
    y = x * lax.rsqrt(jnp.mean(x * x, axis=-1, keepdims=True) + NORM_EPS)
    return (y * g) * (1.0 + scale) + shift


def _dot(a, b):
    return jnp.dot(a, b, preferred_element_type=f32)


def _dot_t(a, b):
    return lax.dot_general(a, b, (((1,), (1,)), ((), ())), preferred_element_type=f32)


def _split_bf16(x):
    hi = x.astype(bf16)
    lo = (x - hi.astype(f32)).astype(bf16)
    return hi, lo


def _mod_kernel(cond_ref, w_ref, b_ref, o_ref):
    c = cond_ref[...]
    s = (c * _sigmoid(c)).astype(bf16)
    o_ref[0] = _dot(s, w_ref[0].astype(bf16)) + b_ref[0]


def _mod_call(cond, w_mod, b_mod):
    n_tiles = 6 * D_MODEL // 1024
    return pl.pallas_call(
        _mod_kernel,
        grid=(DEPTH, n_tiles),
        in_specs=[
            pl.BlockSpec((N_COND_ROWS, D_MODEL), lambda l, j: (0, 0)),
            pl.BlockSpec((1, D_MODEL, 1024), lambda l, j: (l, 0, j)),
            pl.BlockSpec((1, 1, 1024), lambda l, j: (l, 0, j)),
        ],
        out_specs=pl.BlockSpec((1, N_COND_ROWS, 1024), lambda l, j: (l, 0, j)),
        out_shape=jax.ShapeDtypeStruct((DEPTH, N_COND_ROWS, 6 * D_MODEL), f32),
        compiler_params=_cparams(("arbitrary", "arbitrary")),
        name="mod",
    )(cond, w_mod, b_mod.reshape(DEPTH, 1, 6 * D_MODEL))


CAST_BLOCK_BYTES = 4 * 1024 * 1024


def _cast_kernel(w_ref, o_ref):
    o_ref[...] = w_ref[0].astype(bf16)


def _layer_weight_bf16(w, l):
    _, rows, cols = w.shape
    rb = min(rows, 1 << ((CAST_BLOCK_BYTES // (4 * cols)).bit_length() - 1))
    assert rows % rb == 0
    return pl.pallas_call(
        _cast_kernel,
        grid=(rows // rb,),
        in_specs=[pl.BlockSpec((1, rb, cols), lambda i: (l, i, 0))],
        out_specs=pl.BlockSpec((rb, cols), lambda i: (i, 0)),
        out_shape=jax.ShapeDtypeStruct((rows, cols), bf16),
        compiler_params=_cparams(("arbitrary",)),
        name="cast_bf16",
    )(w)


def _head_norm(z, g, ones_blk):
    hi, lo = _split_bf16(z * z)
    ss = _dot(hi, ones_blk) + _dot(lo, ones_blk)
    return z * lax.rsqrt(ss * (1.0 / HEAD_DIM) + NORM_EPS) * g


def _rope(z, cos, sin_signed, first_half):
    outs = []
    for j in range(z.shape[1] // LANES):
        zj = z[:, j * LANES:(j + 1) * LANES]
        swapped = jnp.where(first_half, pltpu.roll(zj, LANES - HALF, 1), pltpu.roll(zj, HALF, 1))
        outs.append(zj * cos + swapped * sin_signed)
    return jnp.concatenate(outs, axis=1)


def _layer_input(x_refs, r0, n):
    rows = pl.ds(r0, n)
    if len(x_refs) == 1:
        return x_refs[0][rows, :]
    x1_ref, moe_ref, modp_ref = x_refs
    return x1_ref[rows, :] + modp_ref[0, 5:6, :] * _rows_from_slabs(moe_ref, r0 * SLAB_ROWS, n, SLAB_ROWS)


S1_SUB_TILES = 2
S3_SUB_TILES = 1


SLAB_ROWS = D_MODEL // LANES


def _rows_from_slabs(ref, start, n_tok, pitch):
    return jnp.concatenate([ref[pl.ds(start + c, n_tok, stride=pitch), :] for c in range(SLAB_ROWS)], axis=1)


def _rows_to_slabs(ref, start, x, pitch):
    for c in range(SLAB_ROWS):
        ref[pl.ds(start + c, x.shape[0], stride=pitch), :] = x[:, c * LANES:(c + 1) * LANES]


def _s1_kernel(*refs, rope, emit_kv, n_x):
    x_refs, refs = refs[:n_x], refs[n_x:]
    mod_ref, g1_ref, w_ref, ones_ref, qg_ref, kg_ref = refs[:6]
    pos = 6
    if rope:
        cos_ref, sin_ref = refs[pos:pos + 2]
        pos += 2
    qa_ref, ka_ref, va_ref, qc_ref, kc_ref, vc_ref, ub_ref = refs[pos:pos + 7]
    pos += 7
    if emit_kv:
        gkv_ref, wkv_ref = refs[pos:pos + 2]

    tm = x_refs[0].shape[0] // S1_SUB_TILES
    lane = lax.broadcasted_iota(jnp.int32, (tm, LANES), 1)
    low_lanes = lane < HEAD_DIM
    first_half = (lane & (HEAD_DIM - 1)) < HALF
    q_scale = HEAD_DIM ** -0.5 * LOG2_E

    def dup(z):
        r = pltpu.roll(z, HEAD_DIM, 1)
        return jnp.concatenate([jnp.where(low_lanes, z, r), jnp.where(low_lanes, r, z)], axis=1).astype(bf16)

    def with_ones(z):
        r = pltpu.roll(z, HEAD_DIM, 1)
        return jnp.concatenate([jnp.where(low_lanes, z, 1.0), jnp.where(low_lanes, r, 1.0)], axis=1).astype(bf16)

    value_layout = with_ones if rope else dup

    for sub in range(S1_SUB_TILES):
        rows = pl.ds(sub * tm, tm)
        x = _layer_input(x_refs, sub * tm, tm)
        h = _rms_mod(x, g1_ref[...], mod_ref[0, 1:2, :], mod_ref[0, 0:1, :]).astype(bf16)
        proj = lambda lo, hi: _dot(h, w_ref[:, lo:hi])
        if rope:
            cos = cos_ref[rows, :]
            sin_signed = sin_ref[rows, :]
            rot = lambda z: _rope(z, cos, sin_signed, first_half)
        else:
            rot = lambda z: z

        qa = rot(_head_norm(proj(C_QA, C_KA), qg_ref[...], ones_ref[...]))
        qa_ref[rows, :] = (qa * q_scale).astype(bf16)
        kva = proj(C_KA, C_QC)
        ka_n = _head_norm(kva[:, 0:LANES], kg_ref[...], ones_ref[0:LANES, 0:LANES])
        ka_ref[rows, :] = dup(rot(ka_n))
        va = kva[:, LANES:2 * LANES]
        va_ref[rows, :] = value_layout(va)
        qc_ref[rows, :] = (rot(proj(C_QC, C_KC)) * q_scale).astype(bf16)
        kvc = proj(C_KC, C_UB)
        kc = kvc[:, 0:LANES]
        kc_ref[rows, :] = dup(rot(kc))
        vc = kvc[:, LANES:2 * LANES]
        vc_ref[rows, :] = value_layout(vc)
        ub_ref[rows, :] = proj(C_UB, C_GATES).astype(bf16)
        if emit_kv:
            gkv_ref[sub, 0] = ka_n
            gkv_ref[sub, 1] = va
            wkv_ref[sub, 0] = kc
            wkv_ref[sub, 1] = vc


TOKEN_TILE = 512


def _cond_index_map(tokens_per_batch, cond_row0, per_batch_cond):
    tiles_per_batch = max(tokens_per_batch // TOKEN_TILE, 1)
    if per_batch_cond:
        assert tokens_per_batch % TOKEN_TILE == 0
        return lambda t: (cond_row0 + t // tiles_per_batch, 0, 0)
    return lambda t: (cond_row0, 0, 0)


def _layer_input_specs(xin, mod_prev, cond_idx):
    tok = pl.BlockSpec((TOKEN_TILE, D_MODEL), lambda t: (t, 0))
    if len(xin) == 1:
        return [tok], list(xin)
    slab = pl.BlockSpec((TOKEN_TILE * SLAB_ROWS, LANES), lambda t: (t, 0))
    return [tok, slab, pl.BlockSpec((1, 6, D_MODEL), cond_idx)], [xin[0], xin[1], mod_prev]


def _s1_call(xin, mod_prev, mod, g1, w_ext, ones_blk, qg, kg, rope_tabs, *, tokens_per_batch, cond_row0,
             per_batch_cond, emit_kv):
    n_tok = xin[0].shape[0]
    tm = TOKEN_TILE
    rope = rope_tabs is not None
    tiles_per_batch = max(tokens_per_batch // tm, 1)
    cond_idx = _cond_index_map(tokens_per_batch, cond_row0, per_batch_cond)

    tok = lambda w: pl.BlockSpec((tm, w), lambda t: (t, 0))
    x_specs, x_args = _layer_input_specs(xin, mod_prev, cond_idx)
    in_specs = x_specs + [
        pl.BlockSpec((1, 6, D_MODEL), cond_idx),
        _resident((1, D_MODEL)),
        _resident((D_MODEL, C_GATES)),
        _resident((Q_W, Q_W)),
        _resident((1, Q_W)),
        _resident((1, LANES)),
    ]
    args = x_args + [mod, g1, w_ext, ones_blk, qg, kg]
    if rope:
        in_specs += [pl.BlockSpec((tm, LANES), lambda t: (t % tiles_per_batch, 0))] * 2
        args += list(rope_tabs)
    widths = [Q_W, KV_DUP_W, KV_DUP_W, Q_W, KV_DUP_W, KV_DUP_W, FOURIER_DIM]
    out_specs = [tok(w) for w in widths]
    out_shape = [jax.ShapeDtypeStruct((n_tok, w), bf16) for w in widths]
    if emit_kv:
        assert tokens_per_batch * S1_SUB_TILES == tm
        n_batch = n_tok // tokens_per_batch
        out_specs += [pl.BlockSpec((S1_SUB_TILES, 2, tokens_per_batch, LANES), lambda t: (t, 0, 0, 0))] * 2
        out_shape += [jax.ShapeDtypeStruct((n_batch, 2, tokens_per_batch, LANES), f32)] * 2
    return pl.pallas_call(
        functools.partial(_s1_kernel, rope=rope, emit_kv=emit_kv, n_x=len(x_args)),
        grid=(n_tok // tm,),
        in_specs=in_specs,
        out_specs=out_specs,
        out_shape=out_shape,
        compiler_params=_cparams(("arbitrary",)),
        name="s1_rope" if rope else "s1_ctx",
    )(*args)


def _gqa_group(q2, segs, sink_pair_lo, sink_pair_hi, value_ones):
    qb = q2.shape[0]
    qs = jnp.concatenate([q2[:, 0:LANES], q2[:, LANES:2 * LANES]], axis=0)
    lane = lax.broadcasted_iota(jnp.int32, qs.shape, 1)
    low = lane < HEAD_DIM
    zero = jnp.zeros_like(qs)
    q4 = jnp.concatenate([jnp.where(low, qs, zero), jnp.where(low, zero, qs)], axis=0)

    scores = []
    for k, _, ok in segs:
        s = _dot_t(q4, k)
        if ok is not None:
            s = jnp.where(ok, s, NEG_INF)
        scores.append(s)
    m = scores[0].max(axis=-1, keepdims=True)
    for s in scores[1:]:
        m = jnp.maximum(m, s.max(axis=-1, keepdims=True))
    if sink_pair_lo is not None:
        row = lax.broadcasted_iota(jnp.int32, (4 * qb, 1), 0)
        sink = jnp.where(row < 2 * qb, jnp.where(row < qb, sink_pair_lo[0], sink_pair_lo[1]),
                         jnp.where(row < 3 * qb, sink_pair_hi[0], sink_pair_hi[1]))
        m = jnp.maximum(m, sink)
    if not value_ones:
        denom = jnp.exp2(sink - m) if sink_pair_lo is not None else jnp.zeros_like(m)
        acc = None
        for s, (_, v, _) in zip(scores, segs):
            p = jnp.exp2(s - m)
            denom = denom + p.sum(axis=-1, keepdims=True)
            pv = _dot(p.astype(bf16), v)
            acc = pv if acc is None else acc + pv
        o4 = acc / denom
        o = jnp.where(low, o4[0:2 * qb], o4[2 * qb:4 * qb])
        return jnp.concatenate([o[0:qb], o[qb:2 * qb]], axis=1)

    acc = None
    for s, (_, v, _) in zip(scores, segs):
        pv = _dot(jnp.exp2(s - m).astype(bf16), v)
        acc = pv if acc is None else acc + pv
    if sink_pair_lo is not None:
        low4 = lax.broadcasted_iota(jnp.int32, acc.shape, 1) < HEAD_DIM
        acc = acc + jnp.where(low4, 0.0, jnp.exp2(sink - m))
    swapped = pltpu.roll(acc, HEAD_DIM, 1)
    o = jnp.where(low, acc[0:2 * qb] / swapped[0:2 * qb], swapped[2 * qb:4 * qb] / acc[2 * qb:4 * qb])
    return jnp.concatenate([o[0:qb], o[qb:2 * qb]], axis=1)


def _sink_pairs(sink_ref, kv):
    h = 4 * kv
    s = [sink_ref[h + j] * LOG2_E for j in range(4)]
    return (s[0], s[2]), (s[1], s[3])


ATTN_CTX_BATCH = 2


def _attn_ctx_kernel(sink_ref, qa_ref, ka_ref, va_ref, qc_ref, kc_ref, vc_ref, oa_ref, oc_ref, *, seq):
    for b in range(ATTN_CTX_BATCH):
        rows = slice(b * seq, (b + 1) * seq)
        for kv in range(N_KV):
            cq = slice(2 * LANES * kv, 2 * LANES * (kv + 1))
            ck = slice(LANES * kv, LANES * (kv + 1))
            oa = _gqa_group(qa_ref[rows, cq], [(ka_ref[rows, ck], va_ref[rows, ck], None)], None, None, False)
            oa_ref[rows, cq] = oa.astype(bf16)
            lo, hi = _sink_pairs(sink_ref, kv)
            oc = _gqa_group(qc_ref[rows, cq], [(kc_ref[rows, ck], vc_ref[rows, ck], None)], lo, hi, False)
            oc_ref[rows, cq] = oc.astype(bf16)


def _attn_ctx_call(sink, qa, ka, va, qc, kc, vc, seq):
    n_tok = qa.shape[0]
    rows = ATTN_CTX_BATCH * seq
    tok = lambda w: pl.BlockSpec((rows, w), lambda b: (b, 0))
    return pl.pallas_call(
        functools.partial(_attn_ctx_kernel, seq=seq),
        grid=(n_tok // rows,),
        in_specs=[pl.BlockSpec(memory_space=pltpu.SMEM),
                  tok(Q_W), tok(KV_DUP_W), tok(KV_DUP_W), tok(Q_W), tok(KV_DUP_W), tok(KV_DUP_W)],
        out_specs=[tok(Q_W), tok(Q_W)],
        out_shape=[jax.ShapeDtypeStruct((n_tok, Q_W), bf16)] * 2,
        compiler_params=_cparams(("arbitrary",)),
        name="attn_ctx",
    )(sink, qa, ka, va, qc, kc, vc)


QB = 256
WIN_QB = 128
WIN_SPAN = WIN_QB + 2 * WINDOW


def _attn_lat_kernel(sink_ref, qa_ref, ka_ref, va_ref, kxa_ref, vxa_ref,
                     qc_ref, kc_ref, vc_ref, kxc_ref, vxc_ref, oa_ref, oc_ref, *, seq):
    i = pl.program_id(1)
    for kv in range(N_KV):
        cq = slice(2 * LANES * kv, 2 * LANES * (kv + 1))
        ck = slice(LANES * kv, LANES * (kv + 1))
        segs = [(ka_ref[:, ck], va_ref[:, ck], None), (kxa_ref[0, :, ck], vxa_ref[0, :, ck], None)]
        oa_ref[:, cq] = _gqa_group(qa_ref[:, cq], segs, None, None, True).astype(bf16)

        lo, hi = _sink_pairs(sink_ref, kv)
        for sub in range(QB // WIN_QB):
            blk = i * (QB // WIN_QB) + sub
            start = jnp.clip(blk * WIN_QB - WINDOW, 0, seq - WIN_SPAN)
            start = pl.multiple_of(start, WINDOW)
            r = lax.broadcasted_iota(jnp.int32, (4 * WIN_QB, WIN_SPAN), 0)
            c = lax.broadcasted_iota(jnp.int32, (4 * WIN_QB, WIN_SPAN), 1)
            qpos = blk * WIN_QB + (r & (WIN_QB - 1))
            ok = jnp.abs(qpos - (start + c)) <= WINDOW
            rows = slice(sub * WIN_QB, (sub + 1) * WIN_QB)
            segs = [(kc_ref[pl.ds(start, WIN_SPAN), ck], vc_ref[pl.ds(start, WIN_SPAN), ck], ok),
                    (kxc_ref[0, :, ck], vxc_ref[0, :, ck], None)]
            oc_ref[rows, cq] = _gqa_group(qc_ref[rows, cq], segs, lo, hi, True).astype(bf16)


def _attn_lat_call(sink, qa, ka, va, kxa, vxa, qc, kc, vc, kxc, vxc, seq):
    n_tok = qa.shape[0]
    nq = seq // QB
    past = kxa.shape[1]
    qspec = pl.BlockSpec((QB, Q_W), lambda b, i: (b * nq + i, 0))
    kspec = pl.BlockSpec((seq, KV_DUP_W), lambda b, i: (b, 0))
    xspec = pl.BlockSpec((1, past, KV_DUP_W), lambda b, i: (b, 0, 0))
    return pl.pallas_call(
        functools.partial(_attn_lat_kernel, seq=seq),
        grid=(n_tok // seq, nq),
        in_specs=[pl.BlockSpec(memory_space=pltpu.SMEM),
                  qspec, kspec, kspec, xspec, xspec, qspec, kspec, kspec, xspec, xspec],
        out_specs=[qspec, qspec],
        out_shape=[jax.ShapeDtypeStruct((n_tok, Q_W), bf16)] * 2,
        compiler_params=_cparams(("arbitrary", "arbitrary")),
        name="attn_lat",
    )(sink, qa, ka, va, kxa, vxa, qc, kc, vc, kxc, vxc)


FOURIER_ROWS = 1024


def _fourier_kernel(u_ref, bdc_ref, bds_ref, cn_ref, sn_ref, o_ref, *, scale, seq):
    u = u_ref[...]
    uc = _dot(u, bdc_ref[...]).astype(bf16)
    us = _dot(u, bds_ref[...]).astype(bf16)
    for b in range(u.shape[0] // seq):
        rows = slice(b * seq, (b + 1) * seq)
        o = _dot(cn_ref[...], uc[rows]) - _dot(sn_ref[...], us[rows])
        o_ref[rows, :] = (o * scale).astype(bf16)


def _fourier_call(ub, bdc, bds, cn, sn, seq):
    n_tok = ub.shape[0]
    rows = max(seq, FOURIER_ROWS)
    tok = pl.BlockSpec((rows, FOURIER_DIM), lambda b: (b, 0))
    scale = float(1.0 / np.sqrt(seq * FOURIER_GROUP_DIM))
    return pl.pallas_call(
        functools.partial(_fourier_kernel, scale=scale, seq=seq),
        grid=(n_tok // rows,),
        in_specs=[tok, _resident((FOURIER_DIM, FOURIER_DIM)), _resident((FOURIER_DIM, FOURIER_DIM)),
                  _resident((seq, seq)), _resident((seq, seq))],
        out_specs=tok,
        out_shape=jax.ShapeDtypeStruct((n_tok, FOURIER_DIM), bf16),
        compiler_params=_cparams(("arbitrary",)),
        name=f"fourier_{seq}",
    )(ub, bdc, bds, cn, sn)


ROUTE_ROWS = LANES
H2X_PITCH = SLAB_ROWS + 1
MOE_PITCH = SLAB_ROWS


def _route(lt, tri_ref, carry_ref, n_tok):
    tm = lt.shape[1]
    row = lax.broadcasted_iota(jnp.int32, (EXPERTS_PER_GROUP, tm), 0).astype(f32)
    neg = jnp.float32(-jnp.inf)
    none = jnp.float32(EXPERTS_PER_GROUP)
    gl = jnp.where(row < N_EXPERT_GROUPS, lt[N_EXPERTS:N_EXPERTS + EXPERTS_PER_GROUP], neg)
    gmax = gl.max(axis=0, keepdims=True)
    gidx = jnp.where(gl == gmax, row, none).min(axis=0, keepdims=True)
    gw = 1.0 / jnp.exp(gl - gmax).sum(axis=0, keepdims=True)
    lg = lt[0:EXPERTS_PER_GROUP]
    for g in range(1, N_EXPERT_GROUPS):
        lg = jnp.where(gidx == g, lt[g * EXPERTS_PER_GROUP:(g + 1) * EXPERTS_PER_GROUP], lg)
    m1 = lg.max(axis=0, keepdims=True)
    i1 = jnp.where(lg == m1, row, none).min(axis=0, keepdims=True)
    l2 = jnp.where(row == i1, neg, lg)
    m2 = l2.max(axis=0, keepdims=True)
    i2 = jnp.where(l2 == m2, row, none).min(axis=0, keepdims=True)
    e = jnp.exp(m2 - m1)
    w8 = jnp.where(row == i1, gw / (1.0 + e), jnp.where(row == i2, gw * e / (1.0 + e), 0.0))

    onehot = (row == gidx).astype(f32)
    prefix = _dot(onehot.astype(bf16), tri_ref[0:tm, 0:tm])
    carry = carry_ref[...]
    rank = (onehot * (prefix - 1.0 + carry[:, 0:1])).sum(axis=0, keepdims=True)
    carry_ref[...] = carry + onehot.sum(axis=1, keepdims=True)
    dest = gidx.astype(jnp.int32) * n_tok + rank.astype(jnp.int32)
    return w8, dest


def _s3_kernel(*refs, n_x, n_tok):
    x_refs, refs = refs[:n_x], refs[n_x:]
    (oa_ref, ob_ref, oc_ref, mod_ref, g1_ref, g2_ref, wga_ref, wgb_ref, wgc_ref, wba_ref, wbb_ref, wbc_ref,
     wout_ref, wrt_ref, tri_ref, x1_ref, h2x_ref, code_ref, cnt_ref, carry_ref) = refs
    wg_refs = (wga_ref, wgb_ref, wgc_ref)

    @pl.when(pl.program_id(0) == 0)
    def _():
        carry_ref[...] = jnp.zeros_like(carry_ref)

    tm = x1_ref.shape[0] // S3_SUB_TILES
    wh, wl = _split_bf16(wrt_ref[...])
    for sub in range(S3_SUB_TILES):
        r0 = sub * tm
        rows = pl.ds(r0, tm)
        x = _layer_input(x_refs, r0, tm)
        h = _rms_mod(x, g1_ref[...], mod_ref[0, 1:2, :], mod_ref[0, 0:1, :]).astype(bf16)
        merged = None
        for j, (o_ref, wb_ref) in enumerate(((oa_ref, wba_ref), (ob_ref, wbb_ref), (oc_ref, wbc_ref))):
            gate = _sigmoid(_dot(h, wg_refs[j][...]))
            term = gate * _dot(o_ref[rows, :], wb_ref[...])
            merged = term if merged is None else merged + term
        mix = _dot(merged.astype(bf16), wout_ref[...])
        x1 = x + mod_ref[0, 2:3, :] * mix
        x1_ref[rows, :] = x1
        h2 = _rms_mod(x1, g2_ref[...], mod_ref[0, 4:5, :], mod_ref[0, 3:4, :])
        _rows_to_slabs(h2x_ref, r0 * H2X_PITCH, h2, H2X_PITCH)
        th, tl = _split_bf16(h2)
        lt = _dot_t(wh, th) + (_dot_t(wh, tl) + _dot_t(wl, th))
        w8, code = _route(lt, tri_ref, carry_ref, n_tok)
        code_ref[:, rows] = code
        w_rows = jnp.concatenate([w8, jnp.zeros((LANES - EXPERTS_PER_GROUP, tm), f32)], axis=0)
        h2x_ref[pl.ds(r0 * H2X_PITCH + SLAB_ROWS, tm, stride=H2X_PITCH), :] = w_rows.T
    cnt_ref[...] = carry_ref[...]


def _s3_call(xin, mod_prev, oa, ob, oc, mod, g1, g2, wg, wba, wbb, wbc, wout, wrt, tri, *, tokens_per_batch,
             cond_row0, per_batch_cond):
    n_tok = xin[0].shape[0]
    tm = TOKEN_TILE
    cond_idx = _cond_index_map(tokens_per_batch, cond_row0, per_batch_cond)
    tok = lambda w: pl.BlockSpec((tm, w), lambda t: (t, 0))
    x_specs, x_args = _layer_input_specs(xin, mod_prev, cond_idx)
    gate_cols = lambda j: pl.BlockSpec((D_MODEL, D_MODEL), lambda t: (0, C_GATES // D_MODEL + j),
                                       pipeline_mode=pl.Buffered(1))
    return pl.pallas_call(
        functools.partial(_s3_kernel, n_x=len(x_args), n_tok=n_tok),
        grid=(n_tok // tm,),
        in_specs=x_specs + [
            tok(Q_W), tok(FOURIER_DIM), tok(Q_W),
            pl.BlockSpec((1, 6, D_MODEL), cond_idx),
            _resident((1, D_MODEL)), _resident((1, D_MODEL)),
            gate_cols(0), gate_cols(1), gate_cols(2),
            _resident((Q_W, D_MODEL)), _resident((FOURIER_DIM, D_MODEL)), _resident((Q_W, D_MODEL)),
            _resident((D_MODEL, D_MODEL)), _resident((ROUTE_ROWS, D_MODEL)), _resident((tm, tm))],
        out_specs=[tok(D_MODEL), pl.BlockSpec((tm * H2X_PITCH, LANES), lambda t: (t, 0)),
                   pl.BlockSpec((1, tm), lambda t: (0, t)),
                   pl.BlockSpec((EXPERTS_PER_GROUP, LANES), lambda t: (0, 0))],
        out_shape=[jax.ShapeDtypeStruct((n_tok, D_MODEL), f32),
                   jax.ShapeDtypeStruct((n_tok * H2X_PITCH, LANES), f32),
                   jax.ShapeDtypeStruct((1, n_tok), jnp.int32),
                   jax.ShapeDtypeStruct((EXPERTS_PER_GROUP, LANES), f32)],
        scratch_shapes=[pltpu.VMEM((EXPERTS_PER_GROUP, LANES), f32)],
        compiler_params=_cparams(("arbitrary",)),
        name="s3",
    )(*x_args, oa, ob, oc, mod, g1, g2, wg, wg, wg, wba, wbb, wbc, wout, wrt, tri)


MOE_TILE_LOG2 = 8
MOE_TILE = 1 << MOE_TILE_LOG2
ROW_UNROLL_LOG2 = 3
ROW_UNROLL = 1 << ROW_UNROLL_LOG2
GROUP_FF = EXPERTS_PER_GROUP * EXPERT_FF
PLAN_FIRST, PLAN_ITEMS, PLAN_LEN = 0, N_EXPERT_GROUPS, N_EXPERT_GROUPS + 1


def _moe_kernel(code_ref, cnt_ref, h2x_hbm, weg_ref, weu_ref, wed_ref, out_hbm,
                pos_ref, plan_ref, gbuf, obuf, dump, gsem, ssem, *, n_tok):
    tm = MOE_TILE
    k = pl.program_id(0)
    n_steps = pl.num_programs(0)

    def out_slab(buf, slot, i):
        return buf.at[slot, :, i]

    def token_slab(tok):
        return out_hbm.at[pl.ds(pl.multiple_of(tok * MOE_PITCH, MOE_PITCH), MOE_PITCH)]

    def for_row_chunks(n_chunks, body):
        def chunk(c, carry):
            body(c * ROW_UNROLL)
            return carry
        lax.fori_loop(0, n_chunks, chunk, 0)

    @pl.when(k == 0)
    def _plan():
        first = jnp.int32(0)
        for g in range(N_EXPERT_GROUPS):
            c = cnt_ref[g]
            n_tiles = (c + (tm - 1)) >> MOE_TILE_LOG2
            plan_ref[PLAN_FIRST + g] = first

            def pad(p, carry):
                pos_ref[p] = 0
                return carry

            lax.fori_loop(g * n_tok + c, g * n_tok + n_tiles * tm, pad, 0)
            first = first + n_tiles
        plan_ref[PLAN_ITEMS] = first

        def place(t0):
            dests = [code_ref[t0 + j] for j in range(ROW_UNROLL)]
            for j, dest in enumerate(dests):
                pos_ref[dest] = t0 + j

        for_row_chunks(n_tok >> ROW_UNROLL_LOG2, place)

    n_items = plan_ref[PLAN_ITEMS]

    def item_group(item):
        return ((item >= plan_ref[PLAN_FIRST + 1]).astype(jnp.int32)
                + (item >= plan_ref[PLAN_FIRST + 2]).astype(jnp.int32)
                + (item >= plan_ref[PLAN_FIRST + 3]).astype(jnp.int32))

    def item_pos0(item):
        g = item_group(item)
        return g * n_tok + (item - plan_ref[PLAN_FIRST + g]) * tm

    def start_gather(item, slot):
        pos0 = item_pos0(item)

        def chunk(i0):
            toks = [pos_ref[pos0 + i0 + j] for j in range(ROW_UNROLL)]
            for j, tok in enumerate(toks):
                pltpu.make_async_copy(h2x_hbm.at[pl.ds(tok * H2X_PITCH, H2X_PITCH)],
                                      gbuf.at[slot, :, i0 + j],
                                      gsem.at[slot]).start(priority=j % 2)
        for_row_chunks(tm >> ROW_UNROLL_LOG2, chunk)

    def wait_gather(slot):
        pltpu.make_async_copy(gbuf.at[1 - slot], gbuf.at[slot], gsem.at[slot]).wait()

    def start_scatter(item, n_valid, slot):
        pos0 = item_pos0(item)

        def to_token(i0):
            dsts = [pos_ref[pos0 + i0 + j] for j in range(ROW_UNROLL)]
            for j, dst in enumerate(dsts):
                pltpu.make_async_copy(out_slab(obuf, slot, i0 + j), token_slab(dst),
                                      ssem.at[slot]).start(priority=j % 2)

        def row_to_token(i, carry):
            pltpu.make_async_copy(out_slab(obuf, slot, i), token_slab(pos_ref[pos0 + i]), ssem.at[slot]).start()
            return carry

        def row_to_dump(i, carry):
            pltpu.make_async_copy(out_slab(obuf, slot, i), out_slab(dump, slot, i), ssem.at[slot]).start()
            return carry

        full = n_valid >> ROW_UNROLL_LOG2
        for_row_chunks(full, to_token)
        lax.fori_loop(full * ROW_UNROLL, n_valid, row_to_token, 0)
        lax.fori_loop(n_valid, tm, row_to_dump, 0)

    def wait_scatter(slot):
        pltpu.make_async_copy(obuf.at[slot], dump.at[slot], ssem.at[slot]).wait()

    slot = k % 2

    @pl.when(k == 0)
    def _():
        start_gather(0, 0)

    @pl.when(k + 1 < n_items)
    def _():
        start_gather(k + 1, 1 - slot)

    @pl.when(k < n_items)
    def _():
        group = item_group(k)
        wait_gather(slot)
        h = jnp.concatenate([gbuf[slot, c] for c in range(SLAB_ROWS)], axis=1).astype(bf16)
        w8 = gbuf[slot, SLAB_ROWS]
        cols = pl.ds(pl.multiple_of(group * GROUP_FF, GROUP_FF), GROUP_FF)
        gate = _dot(h, weg_ref[:, cols])
        a = gate * _sigmoid(gate) * _dot(h, weu_ref[:, cols])
        pieces = [a[:, j * EXPERT_FF:(j + 1) * EXPERT_FF] * w8[:, j:j + 1] for j in range(EXPERTS_PER_GROUP)]
        out = _dot(jnp.concatenate(pieces, axis=1).astype(bf16), wed_ref[group])

        @pl.when(k >= 2)
        def _():
            wait_scatter(slot)

        for c in range(SLAB_ROWS):
            obuf[slot, c] = out[:, c * LANES:(c + 1) * LANES]
        n_valid = jnp.minimum(tm, cnt_ref[group] - (k - plan_ref[PLAN_FIRST + group]) * tm)
        start_scatter(k, n_valid, slot)

    @pl.when(k == n_steps - 1)
    def _drain():
        wait_scatter((n_items - 1) % 2)

        @pl.when(n_items >= 2)
        def _():
            wait_scatter(n_items % 2)


def _moe_call(code, cnt, h2x, weg, weu, wed):
    n_tok = h2x.shape[0] // H2X_PITCH
    tm = MOE_TILE
    n_steps = n_tok // tm + N_EXPERT_GROUPS - 1
    ff = N_EXPERT_GROUPS * GROUP_FF
    up_spec = pl.BlockSpec((D_MODEL, ff), lambda k, *_: (0, 0), pipeline_mode=pl.Buffered(1))
    down_spec = pl.BlockSpec((N_EXPERT_GROUPS, GROUP_FF, D_MODEL), lambda k, *_: (0, 0, 0),
                             pipeline_mode=pl.Buffered(1))
    grid_spec = pltpu.PrefetchScalarGridSpec(
        num_scalar_prefetch=2,
        grid=(n_steps,),
        in_specs=[pl.BlockSpec(memory_space=pl.ANY), up_spec, up_spec, down_spec],
        out_specs=pl.BlockSpec(memory_space=pl.ANY),
        scratch_shapes=[
            pltpu.SMEM((N_EXPERT_GROUPS * n_tok,), jnp.int32),
            pltpu.SMEM((PLAN_LEN,), jnp.int32),
            pltpu.VMEM((2, H2X_PITCH, tm, LANES), f32),
            pltpu.VMEM((2, MOE_PITCH, tm, LANES), f32),
            pltpu.VMEM((2, MOE_PITCH, tm, LANES), f32),
            pltpu.SemaphoreType.DMA((2,)),
            pltpu.SemaphoreType.DMA((2,)),
        ],
    )
    return pl.pallas_call(
        functools.partial(_moe_kernel, n_tok=n_tok),
        grid_spec=grid_spec,
        out_shape=jax.ShapeDtypeStruct((n_tok * MOE_PITCH, LANES), f32),
        compiler_params=_cparams(("arbitrary",)),
        name="moe",
    )(code, cnt, h2x, weg, weu, wed)


def _final_kernel(x1_ref, moe_ref, mod_ref, gf_ref, o_ref):
    x2 = x1_ref[...] + mod_ref[0, 5:6, :] * _rows_from_slabs(moe_ref, 0, x1_ref.shape[0], MOE_PITCH)
    y = x2 * lax.rsqrt(jnp.mean(x2 * x2, axis=-1, keepdims=True) + NORM_EPS)
    o_ref[...] = y * gf_ref[...]


def _final_call(x1, moe, mod, gf, *, tokens_per_batch, cond_row0, per_batch_cond):
    n_tok = x1.shape[0]
    tok = pl.BlockSpec((TOKEN_TILE, D_MODEL), lambda t: (t, 0))
    slab = pl.BlockSpec((TOKEN_TILE * MOE_PITCH, LANES), lambda t: (t, 0))
    cond_idx = _cond_index_map(tokens_per_batch, cond_row0, per_batch_cond)
    return pl.pallas_call(
        _final_kernel,
        grid=(n_tok // TOKEN_TILE,),
        in_specs=[tok, slab, pl.BlockSpec((1, 6, D_MODEL), cond_idx), _resident((1, D_MODEL))],
        out_specs=tok,
        out_shape=jax.ShapeDtypeStruct((n_tok, D_MODEL), f32),
        compiler_params=_cparams(("arbitrary",)),
        name="final_norm",
    )(x1, moe, mod, gf)


def _rope_tables(n):
    rows = n // GRID_W
    row = jnp.repeat(jnp.arange(rows, dtype=f32), GRID_W)
    col = jnp.tile(jnp.arange(GRID_W, dtype=f32), rows)
    inv = ROPE_THETA ** (-jnp.arange(AXIS_FREQS, dtype=f32) / AXIS_FREQS)
    ang = jnp.concatenate([row[:, None] * inv, col[:, None] * inv], axis=-1)
    cos, sin = jnp.cos(ang), jnp.sin(ang)
    cos_t = jnp.concatenate([cos, cos, cos, cos], axis=-1)
    sin_t = jnp.concatenate([-sin, sin, -sin, sin], axis=-1)
    return cos_t, sin_t


def _dft_tables(n):
    k = np.arange(n)
    m = (k[:, None] * k[None, :]) % n
    ang = 2.0 * np.pi * m / n
    return np.cos(ang), np.sin(ang)


def _fourier_tables(seq):
    cn, sn = _dft_tables(seq)
    cc, sc = _dft_tables(FOURIER_GROUP_DIM)
    eye = np.eye(N_FOURIER_GROUPS)
    bdc, bds = np.kron(eye, cc), np.kron(eye, sc)
    return tuple(jnp.asarray(t, dtype=f32).astype(bf16) for t in (bdc, bds, cn, sn))


def _dup_cache(cache, l):
    b, _, _, p, _, _ = cache.shape
    k, v = cache[:, l, 0], cache[:, l, 1]
    k2 = jnp.broadcast_to(k[:, :, :, None, :], (b, p, N_KV, 2, HEAD_DIM)).reshape(b, p, KV_DUP_W)
    v1 = jnp.stack([v, jnp.ones_like(v)], axis=3).reshape(b, p, KV_DUP_W)
    return k2.astype(bf16), v1.astype(bf16)


def kernel(x_prompt, x_sample, cache_global_kv, cache_window_kv, c, c_ctx, w_mod, b_mod, norm1_g, w_in,
           q_norm_g, k_norm_g, sink_logit, w_branch_a, w_branch_b, w_branch_c, w_out, norm2_g,
           w_route_group, w_route_expert, w_exp_gate, w_exp_up, w_exp_down, final_norm_g):
    batch, seq, _ = x_prompt.shape
    dec_batch, dec_seq, _ = x_sample.shape

    cond = jnp.zeros((N_COND_ROWS, D_MODEL), f32).at[0].set(c_ctx).at[1:1 + dec_batch].set(c)
    mod = _mod_call(cond, w_mod, b_mod).reshape(DEPTH, N_COND_ROWS, 6, D_MODEL)

    ones_blk = jnp.asarray(np.kron(np.eye(N_HEADS), np.ones((HEAD_DIM, HEAD_DIM))), dtype=bf16)
    tri = jnp.asarray(np.triu(np.ones((TOKEN_TILE, TOKEN_TILE))), dtype=bf16)
    rope_tabs = _rope_tables(dec_seq)
    ftab_ctx = _fourier_tables(seq)
    ftab_lat = _fourier_tables(dec_seq)
    gf = final_norm_g.reshape(1, D_MODEL)

    ctx = dict(tokens_per_batch=seq, cond_row0=0, per_batch_cond=False)
    lat = dict(tokens_per_batch=dec_seq, cond_row0=1, per_batch_cond=True)
    xp = (x_prompt.reshape(batch * seq, D_MODEL),)
    xs = (x_sample.reshape(dec_batch * dec_seq, D_MODEL),)
    mod_prev = None
    gkv, wkv = [], []
    for l in range(DEPTH):
        w_ext = w_gate = _layer_weight_bf16(w_in, l)
        g1 = norm1_g[l].reshape(1, D_MODEL)
        g2 = norm2_g[l].reshape(1, D_MODEL)
        qg = jnp.tile(q_norm_g[l], N_HEADS).reshape(1, Q_W)
        kg = jnp.tile(k_norm_g[l], N_KV).reshape(1, LANES)
        wba, wbb, wbc = (_layer_weight_bf16(w, l) for w in (w_branch_a, w_branch_b, w_branch_c))
        wout = _layer_weight_bf16(w_out, l)
        wrt = jnp.concatenate([w_route_expert[l].T, w_route_group[l].T,
                               jnp.zeros((ROUTE_ROWS - N_EXPERTS - N_EXPERT_GROUPS, D_MODEL), f32)], axis=0)
        weg, weu = _layer_weight_bf16(w_exp_gate, l), _layer_weight_bf16(w_exp_up, l)
        wed = _layer_weight_bf16(w_exp_down, l).reshape(N_EXPERT_GROUPS, GROUP_FF, D_MODEL)
        sink = sink_logit[l]

        def mix_and_moe(xin, qa, ka, va, qc, kc, vc, ub, attn, ftab, seq_len, stream):
            oa, oc = attn(qa, ka, va, qc, kc, vc)
            ob = _fourier_call(ub, *ftab, seq_len)
            x1, h2x, code, cnt = _s3_call(xin, mod_prev, oa, ob, oc, mod[l], g1, g2, w_gate, wba, wbb, wbc, wout,
                                          wrt, tri, **stream)
            counts = cnt[:, 0].astype(jnp.int32)
            moe = _moe_call(code.reshape(-1), counts, h2x, weg, weu, wed)
            return (x1, moe)

        qa, ka, va, qc, kc, vc, ub, gkv_l, wkv_l = _s1_call(
            xp, mod_prev, mod[l], g1, w_ext, ones_blk, qg, kg, None, emit_kv=True, **ctx)
        attn_ctx = lambda *qkv: _attn_ctx_call(sink, *qkv, seq)
        xp = mix_and_moe(xp, qa, ka, va, qc, kc, vc, ub, attn_ctx, ftab_ctx, seq, ctx)
        kv_shape = (batch, 2, seq, N_KV, HEAD_DIM)
        gkv.append(gkv_l.reshape(kv_shape))
        wkv.append(wkv_l.reshape(kv_shape))

        qa, ka, va, qc, kc, vc, ub = _s1_call(
            xs, mod_prev, mod[l], g1, w_ext, ones_blk, qg, kg, rope_tabs, emit_kv=False, **lat)
        kxa, vxa = _dup_cache(cache_global_kv, l)
        kxc, vxc = _dup_cache(cache_window_kv, l)
        attn_lat = lambda qa, ka, va, qc, kc, vc: _attn_lat_call(
            sink, qa, ka, va, kxa, vxa, qc, kc, vc, kxc, vxc, dec_seq)
        xs = mix_and_moe(xs, qa, ka, va, qc, kc, vc, ub, attn_lat, ftab_lat, dec_seq, lat)
        mod_prev = mod[l]

    y_prompt = _final_call(*xp, mod_prev, gf, **ctx).reshape(batch, seq, D_MODEL)
    y_sample = _final_call(*xs, mod_prev, gf, **lat).reshape(dec_batch, dec_seq, D_MODEL)
    return (y_prompt, y_sample, jnp.stack(gkv, axis=1), jnp.stack(wkv, axis=1))
```

```python
import functools

import jax
import jax.numpy as jnp
import numpy as np
from jax import lax
from jax.experimental import pallas as pl
from jax.experimental.pallas import tpu as pltpu

f32 = jnp.float32
bf16 = jnp.bfloat16

D_MODEL = 1024
DEPTH = 2
GRID_W = 64
HEAD_DIM = 64
HALF = HEAD_DIM // 2
N_HEADS = 8
N_KV = 2
WINDOW = 128
N_FOURIER_GROUPS = 4
FOURIER_GROUP_DIM = 128
FOURIER_DIM = N_FOURIER_GROUPS * FOURIER_GROUP_DIM
ROPE_THETA = 10000.0
AXIS_FREQS = HEAD_DIM // 4
N_EXPERT_GROUPS = 4
EXPERTS_PER_GROUP = 8
N_EXPERTS = N_EXPERT_GROUPS * EXPERTS_PER_GROUP
EXPERT_FF = 128
NORM_EPS = 1e-6
NEG_INF = -1e30
LOG2_E = 1.4426950408889634

LANES = 128
Q_W = N_HEADS * HEAD_DIM
KV_DUP_W = 2 * N_KV * HEAD_DIM
C_QA, C_KA, C_VA, C_QC, C_KC, C_VC, C_UB, C_GATES, C_END = 0, 512, 640, 768, 1280, 1408, 1536, 2048, 5120
N_COND_ROWS = 16
VMEM_LIMIT = 56 * 1024 * 1024


def _cparams(sem):
    return pltpu.CompilerParams(dimension_semantics=sem, vmem_limit_bytes=VMEM_LIMIT)


def _resident(shape):
    nd = len(shape)
    return pl.BlockSpec(shape, lambda *_: (0,) * nd, pipeline_mode=pl.Buffered(1))


def _sigmoid(x):
    return 0.5 * jnp.tanh(0.5 * x) + 0.5


def _rms_mod(x, g, scale, shift):
    y = x * lax.rsqrt(jnp.mean(x * x, axis=-1, keepdims=True) + NORM_EPS)
    return (y * g) * (1.0 + scale) + shift


def _dot(a, b):
    return jnp.dot(a, b, preferred_element_type=f32)


def _dot_t(a, b):
    return lax.dot_general(a, b, (((1,), (1,)), ((), ())), preferred_element_type=f32)


def _split_bf16(x):
    hi = x.astype(bf16)
    lo = (x - hi.astype(f32)).astype(bf16)
    return hi, lo


def _mod_kernel(cond_ref, w_ref, b_ref, o_ref):
    c = cond_ref[...]
    s = (c * _sigmoid(c)).astype(bf16)
    o_ref[0] = _dot(s, w_ref[0].astype(bf16)) + b_ref[0]


def _mod_call(cond, w_mod, b_mod):
    n_tiles = 6 * D_MODEL // 1024
    return pl.pallas_call(
        _mod_kernel,
        grid=(DEPTH, n_tiles),
        in_specs=[
            pl.BlockSpec((N_COND_ROWS, D_MODEL), lambda l, j: (0, 0)),
            pl.BlockSpec((1, D_MODEL, 1024), lambda l, j: (l, 0, j)),
            pl.BlockSpec((1, 1, 1024), lambda l, j: (l, 0, j)),
        ],
        out_specs=pl.BlockSpec((1, N_COND_ROWS, 1024), lambda l, j: (l, 0, j)),
        out_shape=jax.ShapeDtypeStruct((DEPTH, N_COND_ROWS, 6 * D_MODEL), f32),
        compiler_params=_cparams(("arbitrary", "arbitrary")),
        name="mod",
    )(cond, w_mod, b_mod.reshape(DEPTH, 1, 6 * D_MODEL))


CAST_BLOCK_BYTES = 4 * 1024 * 1024


def _cast_kernel(w_ref, o_ref):
    o_ref[...] = w_ref[0].astype(bf16)


def _layer_weight_bf16(w, l):
    _, rows, cols = w.shape
    rb = min(rows, 1 << ((CAST_BLOCK_BYTES // (4 * cols)).bit_length() - 1))
    assert rows % rb == 0
    return pl.pallas_call(
        _cast_kernel,
        grid=(rows // rb,),
        in_specs=[pl.BlockSpec((1, rb, cols), lambda i: (l, i, 0))],
        out_specs=pl.BlockSpec((rb, cols), lambda i: (i, 0)),
        out_shape=jax.ShapeDtypeStruct((rows, cols), bf16),
        compiler_params=_cparams(("arbitrary",)),
        name="cast_bf16",
    )(w)


def _head_norm(z, g, ones_blk):
    hi, lo = _split_bf16(z * z)
    ss = _dot(hi, ones_blk) + _dot(lo, ones_blk)
    return z * lax.rsqrt(ss * (1.0 / HEAD_DIM) + NORM_EPS) * g


def _rope(z, cos, sin_signed, first_half):
    outs = []
    for j in range(z.shape[1] // LANES):
        zj = z[:, j * LANES:(j + 1) * LANES]
        swapped = jnp.where(first_half, pltpu.roll(zj, LANES - HALF, 1), pltpu.roll(zj, HALF, 1))
        outs.append(zj * cos + swapped * sin_signed)
    return jnp.concatenate(outs, axis=1)


def _layer_input(x_refs, r0, n):
    rows = pl.ds(r0, n)
    if len(x_refs) == 1:
        return x_refs[0][rows, :]
    x1_ref, moe_ref, modp_ref = x_refs
    return x1_ref[rows, :] + modp_ref[0, 5:6, :] * _rows_from_slabs(moe_ref, r0 * SLAB_ROWS, n, SLAB_ROWS)


S1_SUB_TILES = 2
S3_SUB_TILES = 1


SLAB_ROWS = D_MODEL // LANES


def _rows_from_slabs(ref, start, n_tok, pitch):
    return jnp.concatenate([ref[pl.ds(start + c, n_tok, stride=pitch), :] for c in range(SLAB_ROWS)], axis=1)


def _rows_to_slabs(ref, start, x, pitch):
    for c in range(SLAB_ROWS):
        ref[pl.ds(start + c, x.shape[0], stride=pitch), :] = x[:, c * LANES:(c + 1) * LANES]


def _s1_kernel(*refs, rope, emit_kv, n_x):
    x_refs, refs = refs[:n_x], refs[n_x:]
    mod_ref, g1_ref, w_ref, ones_ref, qg_ref, kg_ref = refs[:6]
    pos = 6
    if rope:
        cos_ref, sin_ref = refs[pos:pos + 2]
        pos += 2
    qa_ref, ka_ref, va_ref, qc_ref, kc_ref, vc_ref, ub_ref = refs[pos:pos + 7]
    pos += 7
    if emit_kv:
        gkv_ref, wkv_ref = refs[pos:pos + 2]

    tm = x_refs[0].shape[0] // S1_SUB_TILES
    lane = lax.broadcasted_iota(jnp.int32, (tm, LANES), 1)
    low_lanes = lane < HEAD_DIM
    first_half = (lane & (HEAD_DIM - 1)) < HALF
    q_scale = HEAD_DIM ** -0.5 * LOG2_E

    def dup(z):
        r = pltpu.roll(z, HEAD_DIM, 1)
        return jnp.concatenate([jnp.where(low_lanes, z, r), jnp.where(low_lanes, r, z)], axis=1).astype(bf16)

    def with_ones(z):
        r = pltpu.roll(z, HEAD_DIM, 1)
        return jnp.concatenate([jnp.where(low_lanes, z, 1.0), jnp.where(low_lanes, r, 1.0)], axis=1).astype(bf16)

    value_layout = with_ones if rope else dup

    for sub in range(S1_SUB_TILES):
        rows = pl.ds(sub * tm, tm)
        x = _layer_input(x_refs, sub * tm, tm)
        h = _rms_mod(x, g1_ref[...], mod_ref[0, 1:2, :], mod_ref[0, 0:1, :]).astype(bf16)
        proj = lambda lo, hi: _dot(h, w_ref[:, lo:hi])
        if rope:
            cos = cos_ref[rows, :]
            sin_signed = sin_ref[rows, :]
            rot = lambda z: _rope(z, cos, sin_signed, first_half)
        else:
            rot = lambda z: z

        qa = rot(_head_norm(proj(C_QA, C_KA), qg_ref[...], ones_ref[...]))
        qa_ref[rows, :] = (qa * q_scale).astype(bf16)
        kva = proj(C_KA, C_QC)
        ka_n = _head_norm(kva[:, 0:LANES], kg_ref[...], ones_ref[0:LANES, 0:LANES])
        ka_ref[rows, :] = dup(rot(ka_n))
        va = kva[:, LANES:2 * LANES]
        va_ref[rows, :] = value_layout(va)
        qc_ref[rows, :] = (rot(proj(C_QC, C_KC)) * q_scale).astype(bf16)
        kvc = proj(C_KC, C_UB)
        kc = kvc[:, 0:LANES]
        kc_ref[rows, :] = dup(rot(kc))
        vc = kvc[:, LANES:2 * LANES]
        vc_ref[rows, :] = value_layout(vc)
        ub_ref[rows, :] = proj(C_UB, C_GATES).astype(bf16)
        if emit_kv:
            gkv_ref[sub, 0] = ka_n
            gkv_ref[sub, 1] = va
            wkv_ref[sub, 0] = kc
            wkv_ref[sub, 1] = vc


TOKEN_TILE = 512


def _cond_index_map(tokens_per_batch, cond_row0, per_batch_cond):
    tiles_per_batch = max(tokens_per_batch // TOKEN_TILE, 1)
    if per_batch_cond:
        assert tokens_per_batch % TOKEN_TILE == 0
        return lambda t: (cond_row0 + t // tiles_per_batch, 0, 0)
    return lambda t: (cond_row0, 0, 0)


def _layer_input_specs(xin, mod_prev, cond_idx):
    tok = pl.BlockSpec((TOKEN_TILE, D_MODEL), lambda t: (t, 0))
    if len(xin) == 1:
        return [tok], list(xin)
    slab = pl.BlockSpec((TOKEN_TILE * SLAB_ROWS, LANES), lambda t: (t, 0))
    return [tok, slab, pl.BlockSpec((1, 6, D_MODEL), cond_idx)], [xin[0], xin[1], mod_prev]


def _s1_call(xin, mod_prev, mod, g1, w_ext, ones_blk, qg, kg, rope_tabs, *, tokens_per_batch, cond_row0,
             per_batch_cond, emit_kv):
    n_tok = xin[0].shape[0]
    tm = TOKEN_TILE
    rope = rope_tabs is not None
    tiles_per_batch = max(tokens_per_batch // tm, 1)
    cond_idx = _cond_index_map(tokens_per_batch, cond_row0, per_batch_cond)

    tok = lambda w: pl.BlockSpec((tm, w), lambda t: (t, 0))
    x_specs, x_args = _layer_input_specs(xin, mod_prev, cond_idx)
    in_specs = x_specs + [
        pl.BlockSpec((1, 6, D_MODEL), cond_idx),
        _resident((1, D_MODEL)),
        _resident((D_MODEL, C_GATES)),
        _resident((Q_W, Q_W)),
        _resident((1, Q_W)),
        _resident((1, LANES)),
    ]
    args = x_args + [mod, g1, w_ext, ones_blk, qg, kg]
    if rope:
        in_specs += [pl.BlockSpec((tm, LANES), lambda t: (t % tiles_per_batch, 0))] * 2
        args += list(rope_tabs)
    widths = [Q_W, KV_DUP_W, KV_DUP_W, Q_W, KV_DUP_W, KV_DUP_W, FOURIER_DIM]
    out_specs = [tok(w) for w in widths]
    out_shape = [jax.ShapeDtypeStruct((n_tok, w), bf16) for w in widths]
    if emit_kv:
        assert tokens_per_batch * S1_SUB_TILES == tm
        n_batch = n_tok // tokens_per_batch
        out_specs += [pl.BlockSpec((S1_SUB_TILES, 2, tokens_per_batch, LANES), lambda t: (t, 0, 0, 0))] * 2
        out_shape += [jax.ShapeDtypeStruct((n_batch, 2, tokens_per_batch, LANES), f32)] * 2
    return pl.pallas_call(
        functools.partial(_s1_kernel, rope=rope, emit_kv=emit_kv, n_x=len(x_args)),
        grid=(n_tok // tm,),
        in_specs=in_specs,
        out_specs=out_specs,
        out_shape=out_shape,
        compiler_params=_cparams(("arbitrary",)),
        name="s1_rope" if rope else "s1_ctx",
    )(*args)


def _gqa_group(q2, segs, sink_pair_lo, sink_pair_hi, value_ones):
    qb = q2.shape[0]
    qs = jnp.concatenate([q2[:, 0:LANES], q2[:, LANES:2 * LANES]], axis=0)
    lane = lax.broadcasted_iota(jnp.int32, qs.shape, 1)
    low = lane < HEAD_DIM
    zero = jnp.zeros_like(qs)
    q4 = jnp.concatenate([jnp.where(low, qs, zero), jnp.where(low, zero, qs)], axis=0)

    scores = []
    for k, _, ok in segs:
        s = _dot_t(q4, k)
        if ok is not None:
            s = jnp.where(ok, s, NEG_INF)
        scores.append(s)
    m = scores[0].max(axis=-1, keepdims=True)
    for s in scores[1:]:
        m = jnp.maximum(m, s.max(axis=-1, keepdims=True))
    if sink_pair_lo is not None:
        row = lax.broadcasted_iota(jnp.int32, (4 * qb, 1), 0)
        sink = jnp.where(row < 2 * qb, jnp.where(row < qb, sink_pair_lo[0], sink_pair_lo[1]),
                         jnp.where(row < 3 * qb, sink_pair_hi[0], sink_pair_hi[1]))
        m = jnp.maximum(m, sink)
    if not value_ones:
        denom = jnp.exp2(sink - m) if sink_pair_lo is not None else jnp.zeros_like(m)
        acc = None
        for s, (_, v, _) in zip(scores, segs):
            p = jnp.exp2(s - m)
            denom = denom + p.sum(axis=-1, keepdims=True)
            pv = _dot(p.astype(bf16), v)
            acc = pv if acc is None else acc + pv
        o4 = acc / denom
        o = jnp.where(low, o4[0:2 * qb], o4[2 * qb:4 * qb])
        return jnp.concatenate([o[0:qb], o[qb:2 * qb]], axis=1)

    acc = None
    for s, (_, v, _) in zip(scores, segs):
        pv = _dot(jnp.exp2(s - m).astype(bf16), v)
        acc = pv if acc is None else acc + pv
    if sink_pair_lo is not None:
        low4 = lax.broadcasted_iota(jnp.int32, acc.shape, 1) < HEAD_DIM
        acc = acc + jnp.where(low4, 0.0, jnp.exp2(sink - m))
    swapped = pltpu.roll(acc, HEAD_DIM, 1)
    o = jnp.where(low, acc[0:2 * qb] / swapped[0:2 * qb], swapped[2 * qb:4 * qb] / acc[2 * qb:4 * qb])
    return jnp.concatenate([o[0:qb], o[qb:2 * qb]], axis=1)


def _sink_pairs(sink_ref, kv):
    h = 4 * kv
    s = [sink_ref[h + j] * LOG2_E for j in range(4)]
    return (s[0], s[2]), (s[1], s[3])


ATTN_CTX_BATCH = 4


def _attn_ctx_kernel(sink_ref, qa_ref, ka_ref, va_ref, qc_ref, kc_ref, vc_ref, oa_ref, oc_ref, *, seq):
    for b in range(ATTN_CTX_BATCH):
        rows = slice(b * seq, (b + 1) * seq)
        for kv in range(N_KV):
            cq = slice(2 * LANES * kv, 2 * LANES * (kv + 1))
            ck = slice(LANES * kv, LANES * (kv + 1))
            oa = _gqa_group(qa_ref[rows, cq], [(ka_ref[rows, ck], va_ref[rows, ck], None)], None, None, False)
            oa_ref[rows, cq] = oa.astype(bf16)
            lo, hi = _sink_pairs(sink_ref, kv)
            oc = _gqa_group(qc_ref[rows, cq], [(kc_ref[rows, ck], vc_ref[rows, ck], None)], lo, hi, False)
            oc_ref[rows, cq] = oc.astype(bf16)


def _attn_ctx_call(sink, qa, ka, va, qc, kc, vc, seq):
    n_tok = qa.shape[0]
    rows = ATTN_CTX_BATCH * seq
    tok = lambda w: pl.BlockSpec((rows, w), lambda b: (b, 0))
    return pl.pallas_call(
        functools.partial(_attn_ctx_kernel, seq=seq),
        grid=(n_tok // rows,),
        in_specs=[pl.BlockSpec(memory_space=pltpu.SMEM),
                  tok(Q_W), tok(KV_DUP_W), tok(KV_DUP_W), tok(Q_W), tok(KV_DUP_W), tok(KV_DUP_W)],
        out_specs=[tok(Q_W), tok(Q_W)],
        out_shape=[jax.ShapeDtypeStruct((n_tok, Q_W), bf16)] * 2,
        compiler_params=_cparams(("arbitrary",)),
        name="attn_ctx",
    )(sink, qa, ka, va, qc, kc, vc)


QB = 256
WIN_QB = 128
WIN_SPAN = WIN_QB + 2 * WINDOW


def _attn_lat_kernel(sink_ref, qa_ref, ka_ref, va_ref, kxa_ref, vxa_ref,
                     qc_ref, kc_ref, vc_ref, kxc_ref, vxc_ref, oa_ref, oc_ref, *, seq):
    i = pl.program_id(1)
    for kv in range(N_KV):
        cq = slice(2 * LANES * kv, 2 * LANES * (kv + 1))
        ck = slice(LANES * kv, LANES * (kv + 1))
        segs = [(ka_ref[:, ck], va_ref[:, ck], None), (kxa_ref[0, :, ck], vxa_ref[0, :, ck], None)]
        oa_ref[:, cq] = _gqa_group(qa_ref[:, cq], segs, None, None, True).astype(bf16)

        lo, hi = _sink_pairs(sink_ref, kv)
        for sub in range(QB // WIN_QB):
            blk = i * (QB // WIN_QB) + sub
            start = jnp.clip(blk * WIN_QB - WINDOW, 0, seq - WIN_SPAN)
            start = pl.multiple_of(start, WINDOW)
            r = lax.broadcasted_iota(jnp.int32, (4 * WIN_QB, WIN_SPAN), 0)
            c = lax.broadcasted_iota(jnp.int32, (4 * WIN_QB, WIN_SPAN), 1)
            qpos = blk * WIN_QB + (r & (WIN_QB - 1))
            ok = jnp.abs(qpos - (start + c)) <= WINDOW
            rows = slice(sub * WIN_QB, (sub + 1) * WIN_QB)
            segs = [(kc_ref[pl.ds(start, WIN_SPAN), ck], vc_ref[pl.ds(start, WIN_SPAN), ck], ok),
                    (kxc_ref[0, :, ck], vxc_ref[0, :, ck], None)]
            oc_ref[rows, cq] = _gqa_group(qc_ref[rows, cq], segs, lo, hi, True).astype(bf16)


def _attn_lat_call(sink, qa, ka, va, kxa, vxa, qc, kc, vc, kxc, vxc, seq):
    n_tok = qa.shape[0]
    nq = seq // QB
    past = kxa.shape[1]
    qspec = pl.BlockSpec((QB, Q_W), lambda b, i: (b * nq + i, 0))
    kspec = pl.BlockSpec((seq, KV_DUP_W), lambda b, i: (b, 0))
    xspec = pl.BlockSpec((1, past, KV_DUP_W), lambda b, i: (b, 0, 0))
    return pl.pallas_call(
        functools.partial(_attn_lat_kernel, seq=seq),
        grid=(n_tok // seq, nq),
        in_specs=[pl.BlockSpec(memory_space=pltpu.SMEM),
                  qspec, kspec, kspec, xspec, xspec, qspec, kspec, kspec, xspec, xspec],
        out_specs=[qspec, qspec],
        out_shape=[jax.ShapeDtypeStruct((n_tok, Q_W), bf16)] * 2,
        compiler_params=_cparams(("arbitrary", "arbitrary")),
        name="attn_lat",
    )(sink, qa, ka, va, kxa, vxa, qc, kc, vc, kxc, vxc)


FOURIER_ROWS = 2048


def _fourier_kernel(u_ref, bdc_ref, bds_ref, cn_ref, sn_ref, o_ref, *, scale, seq):
    u = u_ref[...]
    uc = _dot(u, bdc_ref[...]).astype(bf16)
    us = _dot(u, bds_ref[...]).astype(bf16)
    for b in range(u.shape[0] // seq):
        rows = slice(b * seq, (b + 1) * seq)
        o = _dot(cn_ref[...], uc[rows]) - _dot(sn_ref[...], us[rows])
        o_ref[rows, :] = (o * scale).astype(bf16)


def _fourier_call(ub, bdc, bds, cn, sn, seq):
    n_tok = ub.shape[0]
    rows = max(seq, FOURIER_ROWS)
    tok = pl.BlockSpec((rows, FOURIER_DIM), lambda b: (b, 0))
    scale = float(1.0 / np.sqrt(seq * FOURIER_GROUP_DIM))
    return pl.pallas_call(
        functools.partial(_fourier_kernel, scale=scale, seq=seq),
        grid=(n_tok // rows,),
        in_specs=[tok, _resident((FOURIER_DIM, FOURIER_DIM)), _resident((FOURIER_DIM, FOURIER_DIM)),
                  _resident((seq, seq)), _resident((seq, seq))],
        out_specs=tok,
        out_shape=jax.ShapeDtypeStruct((n_tok, FOURIER_DIM), bf16),
        compiler_params=_cparams(("arbitrary",)),
        name=f"fourier_{seq}",
    )(ub, bdc, bds, cn, sn)


ROUTE_ROWS = LANES
H2X_PITCH = SLAB_ROWS + 1
MOE_PITCH = SLAB_ROWS


def _route(lt, tri_ref, carry_ref, n_tok):
    tm = lt.shape[1]
    row = lax.broadcasted_iota(jnp.int32, (EXPERTS_PER_GROUP, tm), 0).astype(f32)
    neg = jnp.float32(-jnp.inf)
    none = jnp.float32(EXPERTS_PER_GROUP)
    gl = jnp.where(row < N_EXPERT_GROUPS, lt[N_EXPERTS:N_EXPERTS + EXPERTS_PER_GROUP], neg)
    gmax = gl.max(axis=0, keepdims=True)
    gidx = jnp.where(gl == gmax, row, none).min(axis=0, keepdims=True)
    gw = 1.0 / jnp.exp(gl - gmax).sum(axis=0, keepdims=True)
    lg = lt[0:EXPERTS_PER_GROUP]
    for g in range(1, N_EXPERT_GROUPS):
        lg = jnp.where(gidx == g, lt[g * EXPERTS_PER_GROUP:(g + 1) * EXPERTS_PER_GROUP], lg)
    m1 = lg.max(axis=0, keepdims=True)
    i1 = jnp.where(lg == m1, row, none).min(axis=0, keepdims=True)
    l2 = jnp.where(row == i1, neg, lg)
    m2 = l2.max(axis=0, keepdims=True)
    i2 = jnp.where(l2 == m2, row, none).min(axis=0, keepdims=True)
    e = jnp.exp(m2 - m1)
    w8 = jnp.where(row == i1, gw / (1.0 + e), jnp.where(row == i2, gw * e / (1.0 + e), 0.0))

    onehot = (row == gidx).astype(f32)
    prefix = _dot(onehot.astype(bf16), tri_ref[0:tm, 0:tm])
    carry = carry_ref[...]
    rank = (onehot * (prefix - 1.0 + carry[:, 0:1])).sum(axis=0, keepdims=True)
    carry_ref[...] = carry + onehot.sum(axis=1, keepdims=True)
    dest = gidx.astype(jnp.int32) * n_tok + rank.astype(jnp.int32)
    return w8, dest


def _s3_kernel(*refs, n_x, n_tok):
    x_refs, refs = refs[:n_x], refs[n_x:]
    (oa_ref, ob_ref, oc_ref, mod_ref, g1_ref, g2_ref, wga_ref, wgb_ref, wgc_ref, wba_ref, wbb_ref, wbc_ref,
     wout_ref, wrt_ref, tri_ref, x1_ref, h2x_ref, code_ref, cnt_ref, carry_ref) = refs
    wg_refs = (wga_ref, wgb_ref, wgc_ref)

    @pl.when(pl.program_id(0) == 0)
    def _():
        carry_ref[...] = jnp.zeros_like(carry_ref)

    tm = x1_ref.shape[0] // S3_SUB_TILES
    wh, wl = _split_bf16(wrt_ref[...])
    for sub in range(S3_SUB_TILES):
        r0 = sub * tm
        rows = pl.ds(r0, tm)
        x = _layer_input(x_refs, r0, tm)
        h = _rms_mod(x, g1_ref[...], mod_ref[0, 1:2, :], mod_ref[0, 0:1, :]).astype(bf16)
        merged = None
        for j, (o_ref, wb_ref) in enumerate(((oa_ref, wba_ref), (ob_ref, wbb_ref), (oc_ref, wbc_ref))):
            gate = _sigmoid(_dot(h, wg_refs[j][...]))
            term = gate * _dot(o_ref[rows, :], wb_ref[...])
            merged = term if merged is None else merged + term
        mix = _dot(merged.astype(bf16), wout_ref[...])
        x1 = x + mod_ref[0, 2:3, :] * mix
        x1_ref[rows, :] = x1
        h2 = _rms_mod(x1, g2_ref[...], mod_ref[0, 4:5, :], mod_ref[0, 3:4, :])
        _rows_to_slabs(h2x_ref, r0 * H2X_PITCH, h2, H2X_PITCH)
        th, tl = _split_bf16(h2)
        lt = _dot_t(wh, th) + (_dot_t(wh, tl) + _dot_t(wl, th))
        w8, code = _route(lt, tri_ref, carry_ref, n_tok)
        code_ref[:, rows] = code
        w_rows = jnp.concatenate([w8, jnp.zeros((LANES - EXPERTS_PER_GROUP, tm), f32)], axis=0)
        h2x_ref[pl.ds(r0 * H2X_PITCH + SLAB_ROWS, tm, stride=H2X_PITCH), :] = w_rows.T
    cnt_ref[...] = carry_ref[...]


def _s3_call(xin, mod_prev, oa, ob, oc, mod, g1, g2, wg, wba, wbb, wbc, wout, wrt, tri, *, tokens_per_batch,
             cond_row0, per_batch_cond):
    n_tok = xin[0].shape[0]
    tm = TOKEN_TILE
    cond_idx = _cond_index_map(tokens_per_batch, cond_row0, per_batch_cond)
    tok = lambda w: pl.BlockSpec((tm, w), lambda t: (t, 0))
    x_specs, x_args = _layer_input_specs(xin, mod_prev, cond_idx)
    gate_cols = lambda j: pl.BlockSpec((D_MODEL, D_MODEL), lambda t: (0, C_GATES // D_MODEL + j),
                                       pipeline_mode=pl.Buffered(1))
    return pl.pallas_call(
        functools.partial(_s3_kernel, n_x=len(x_args), n_tok=n_tok),
        grid=(n_tok // tm,),
        in_specs=x_specs + [
            tok(Q_W), tok(FOURIER_DIM), tok(Q_W),
            pl.BlockSpec((1, 6, D_MODEL), cond_idx),
            _resident((1, D_MODEL)), _resident((1, D_MODEL)),
            gate_cols(0), gate_cols(1), gate_cols(2),
            _resident((Q_W, D_MODEL)), _resident((FOURIER_DIM, D_MODEL)), _resident((Q_W, D_MODEL)),
            _resident((D_MODEL, D_MODEL)), _resident((ROUTE_ROWS, D_MODEL)), _resident((tm, tm))],
        out_specs=[tok(D_MODEL), pl.BlockSpec((tm * H2X_PITCH, LANES), lambda t: (t, 0)),
                   pl.BlockSpec((1, tm), lambda t: (0, t)),
                   pl.BlockSpec((EXPERTS_PER_GROUP, LANES), lambda t: (0, 0))],
        out_shape=[jax.ShapeDtypeStruct((n_tok, D_MODEL), f32),
                   jax.ShapeDtypeStruct((n_tok * H2X_PITCH, LANES), f32),
                   jax.ShapeDtypeStruct((1, n_tok), jnp.int32),
                   jax.ShapeDtypeStruct((EXPERTS_PER_GROUP, LANES), f32)],
        scratch_shapes=[pltpu.VMEM((EXPERTS_PER_GROUP, LANES), f32)],
        compiler_params=_cparams(("arbitrary",)),
        name="s3",
    )(*x_args, oa, ob, oc, mod, g1, g2, wg, wg, wg, wba, wbb, wbc, wout, wrt, tri)


MOE_TILE_LOG2 = 8
MOE_TILE = 1 << MOE_TILE_LOG2
ROW_UNROLL_LOG2 = 3
ROW_UNROLL = 1 << ROW_UNROLL_LOG2
GROUP_FF = EXPERTS_PER_GROUP * EXPERT_FF
PLAN_FIRST, PLAN_ITEMS, PLAN_LEN = 0, N_EXPERT_GROUPS, N_EXPERT_GROUPS + 1


def _moe_kernel(code_ref, cnt_ref, h2x_hbm, weg_ref, weu_ref, wed_ref, out_hbm,
                pos_ref, plan_ref, gbuf, obuf, dump, gsem, ssem, *, n_tok):
    tm = MOE_TILE
    k = pl.program_id(0)
    n_steps = pl.num_programs(0)

    def out_slab(buf, slot, i):
        return buf.at[slot, :, i]

    def token_slab(tok):
        return out_hbm.at[pl.ds(pl.multiple_of(tok * MOE_PITCH, MOE_PITCH), MOE_PITCH)]

    def for_row_chunks(n_chunks, body):
        def chunk(c, carry):
            body(c * ROW_UNROLL)
            return carry
        lax.fori_loop(0, n_chunks, chunk, 0)

    @pl.when(k == 0)
    def _plan():
        first = jnp.int32(0)
        for g in range(N_EXPERT_GROUPS):
            c = cnt_ref[g]
            n_tiles = (c + (tm - 1)) >> MOE_TILE_LOG2
            plan_ref[PLAN_FIRST + g] = first

            def pad(p, carry):
                pos_ref[p] = 0
                return carry

            lax.fori_loop(g * n_tok + c, g * n_tok + n_tiles * tm, pad, 0)
            first = first + n_tiles
        plan_ref[PLAN_ITEMS] = first

        def place(t0):
            dests = [code_ref[t0 + j] for j in range(ROW_UNROLL)]
            for j, dest in enumerate(dests):
                pos_ref[dest] = t0 + j

        for_row_chunks(n_tok >> ROW_UNROLL_LOG2, place)

    n_items = plan_ref[PLAN_ITEMS]

    def item_group(item):
        return ((item >= plan_ref[PLAN_FIRST + 1]).astype(jnp.int32)
                + (item >= plan_ref[PLAN_FIRST + 2]).astype(jnp.int32)
                + (item >= plan_ref[PLAN_FIRST + 3]).astype(jnp.int32))

    def item_pos0(item):
        g = item_group(item)
        return g * n_tok + (item - plan_ref[PLAN_FIRST + g]) * tm

    def start_gather(item, slot):
        pos0 = item_pos0(item)

        def chunk(i0):
            toks = [pos_ref[pos0 + i0 + j] for j in range(ROW_UNROLL)]
            for j, tok in enumerate(toks):
                pltpu.make_async_copy(h2x_hbm.at[pl.ds(tok * H2X_PITCH, H2X_PITCH)],
                                      gbuf.at[slot, :, i0 + j],
                                      gsem.at[slot]).start(priority=j % 2)
        for_row_chunks(tm >> ROW_UNROLL_LOG2, chunk)

    def wait_gather(slot):
        pltpu.make_async_copy(gbuf.at[1 - slot], gbuf.at[slot], gsem.at[slot]).wait()

    def start_scatter(item, n_valid, slot):
        pos0 = item_pos0(item)

        def to_token(i0):
            dsts = [pos_ref[pos0 + i0 + j] for j in range(ROW_UNROLL)]
            for j, dst in enumerate(dsts):
                pltpu.make_async_copy(out_slab(obuf, slot, i0 + j), token_slab(dst),
                                      ssem.at[slot]).start(priority=j % 2)

        def row_to_token(i, carry):
            pltpu.make_async_copy(out_slab(obuf, slot, i), token_slab(pos_ref[pos0 + i]), ssem.at[slot]).start()
            return carry

        def row_to_dump(i, carry):
            pltpu.make_async_copy(out_slab(obuf, slot, i), out_slab(dump, slot, i), ssem.at[slot]).start()
            return carry

        full = n_valid >> ROW_UNROLL_LOG2
        for_row_chunks(full, to_token)
        lax.fori_loop(full * ROW_UNROLL, n_valid, row_to_token, 0)
        lax.fori_loop(n_valid, tm, row_to_dump, 0)

    def wait_scatter(slot):
        pltpu.make_async_copy(obuf.at[slot], dump.at[slot], ssem.at[slot]).wait()

    slot = k % 2

    @pl.when(k == 0)
    def _():
        start_gather(0, 0)

    @pl.when(k + 1 < n_items)
    def _():
        start_gather(k + 1, 1 - slot)

    @pl.when(k < n_items)
    def _():
        group = item_group(k)
        wait_gather(slot)
        h = jnp.concatenate([gbuf[slot, c] for c in range(SLAB_ROWS)], axis=1).astype(bf16)
        w8 = gbuf[slot, SLAB_ROWS]
        cols = pl.ds(pl.multiple_of(group * GROUP_FF, GROUP_FF), GROUP_FF)
        gate = _dot(h, weg_ref[:, cols])
        a = gate * _sigmoid(gate) * _dot(h, weu_ref[:, cols])
        pieces = [a[:, j * EXPERT_FF:(j + 1) * EXPERT_FF] * w8[:, j:j + 1] for j in range(EXPERTS_PER_GROUP)]
        out = _dot(jnp.concatenate(pieces, axis=1).astype(bf16), wed_ref[group])

        @pl.when(k >= 2)
        def _():
            wait_scatter(slot)

        for c in range(SLAB_ROWS):
            obuf[slot, c] = out[:, c * LANES:(c + 1) * LANES]
        n_valid = jnp.minimum(tm, cnt_ref[group] - (k - plan_ref[PLAN_FIRST + group]) * tm)
        start_scatter(k, n_valid, slot)

    @pl.when(k == n_steps - 1)
    def _drain():
        wait_scatter((n_items - 1) % 2)

        @pl.when(n_items >= 2)
        def _():
            wait_scatter(n_items % 2)


def _moe_call(code, cnt, h2x, weg, weu, wed):
    n_tok = h2x.shape[0] // H2X_PITCH
    tm = MOE_TILE
    n_steps = n_tok // tm + N_EXPERT_GROUPS - 1
    ff = N_EXPERT_GROUPS * GROUP_FF
    up_spec = pl.BlockSpec((D_MODEL, ff), lambda k, *_: (0, 0), pipeline_mode=pl.Buffered(1))
    down_spec = pl.BlockSpec((N_EXPERT_GROUPS, GROUP_FF, D_MODEL), lambda k, *_: (0, 0, 0),
                             pipeline_mode=pl.Buffered(1))
    grid_spec = pltpu.PrefetchScalarGridSpec(
        num_scalar_prefetch=2,
        grid=(n_steps,),
        in_specs=[pl.BlockSpec(memory_space=pl.ANY), up_spec, up_spec, down_spec],
        out_specs=pl.BlockSpec(memory_space=pl.ANY),
        scratch_shapes=[
            pltpu.SMEM((N_EXPERT_GROUPS * n_tok,), jnp.int32),
            pltpu.SMEM((PLAN_LEN,), jnp.int32),
            pltpu.VMEM((2, H2X_PITCH, tm, LANES), f32),
            pltpu.VMEM((2, MOE_PITCH, tm, LANES), f32),
            pltpu.VMEM((2, MOE_PITCH, tm, LANES), f32),
            pltpu.SemaphoreType.DMA((2,)),
            pltpu.SemaphoreType.DMA((2,)),
        ],
    )
    return pl.pallas_call(
        functools.partial(_moe_kernel, n_tok=n_tok),
        grid_spec=grid_spec,
        out_shape=jax.ShapeDtypeStruct((n_tok * MOE_PITCH, LANES), f32),
        compiler_params=_cparams(("arbitrary",)),
        name="moe",
    )(code, cnt, h2x, weg, weu, wed)


def _final_kernel(x1_ref, moe_ref, mod_ref, gf_ref, o_ref):
    x2 = x1_ref[...] + mod_ref[0, 5:6, :] * _rows_from_slabs(moe_ref, 0, x1_ref.shape[0], MOE_PITCH)
    y = x2 * lax.rsqrt(jnp.mean(x2 * x2, axis=-1, keepdims=True) + NORM_EPS)
    o_ref[...] = y * gf_ref[...]


def _final_call(x1, moe, mod, gf, *, tokens_per_batch, cond_row0, per_batch_cond):
    n_tok = x1.shape[0]
    tok = pl.BlockSpec((TOKEN_TILE, D_MODEL), lambda t: (t, 0))
    slab = pl.BlockSpec((TOKEN_TILE * MOE_PITCH, LANES), lambda t: (t, 0))
    cond_idx = _cond_index_map(tokens_per_batch, cond_row0, per_batch_cond)
    return pl.pallas_call(
        _final_kernel,
        grid=(n_tok // TOKEN_TILE,),
        in_specs=[tok, slab, pl.BlockSpec((1, 6, D_MODEL), cond_idx), _resident((1, D_MODEL))],
        out_specs=tok,
        out_shape=jax.ShapeDtypeStruct((n_tok, D_MODEL), f32),
        compiler_params=_cparams(("arbitrary",)),
        name="final_norm",
    )(x1, moe, mod, gf)


def _rope_tables(n):
    rows = n // GRID_W
    row = jnp.repeat(jnp.arange(rows, dtype=f32), GRID_W)
    col = jnp.tile(jnp.arange(GRID_W, dtype=f32), rows)
    inv = ROPE_THETA ** (-jnp.arange(AXIS_FREQS, dtype=f32) / AXIS_FREQS)
    ang = jnp.concatenate([row[:, None] * inv, col[:, None] * inv], axis=-1)
    cos, sin = jnp.cos(ang), jnp.sin(ang)
    cos_t = jnp.concatenate([cos, cos, cos, cos], axis=-1)
    sin_t = jnp.concatenate([-sin, sin, -sin, sin], axis=-1)
    return cos_t, sin_t


def _dft_tables(n):
    k = np.arange(n)
    m = (k[:, None] * k[None, :]) % n
    ang = 2.0 * np.pi * m / n
    return np.cos(ang), np.sin(ang)


def _fourier_tables(seq):
    cn, sn = _dft_tables(seq)
    cc, sc = _dft_tables(FOURIER_GROUP_DIM)
    eye = np.eye(N_FOURIER_GROUPS)
    bdc, bds = np.kron(eye, cc), np.kron(eye, sc)
    return tuple(jnp.asarray(t, dtype=f32).astype(bf16) for t in (bdc, bds, cn, sn))


def _dup_cache(cache, l):
    b, _, _, p, _, _ = cache.shape
    k, v = cache[:, l, 0], cache[:, l, 1]
    k2 = jnp.broadcast_to(k[:, :, :, None, :], (b, p, N_KV, 2, HEAD_DIM)).reshape(b, p, KV_DUP_W)
    v1 = jnp.stack([v, jnp.ones_like(v)], axis=3).reshape(b, p, KV_DUP_W)
    return k2.astype(bf16), v1.astype(bf16)


def kernel(x_prompt, x_sample, cache_global_kv, cache_window_kv, c, c_ctx, w_mod, b_mod, norm1_g, w_in,
           q_norm_g, k_norm_g, sink_logit, w_branch_a, w_branch_b, w_branch_c, w_out, norm2_g,
           w_route_group, w_route_expert, w_exp_gate, w_exp_up, w_exp_down, final_norm_g):
    batch, seq, _ = x_prompt.shape
    dec_batch, dec_seq, _ = x_sample.shape

    cond = jnp.zeros((N_COND_ROWS, D_MODEL), f32).at[0].set(c_ctx).at[1:1 + dec_batch].set(c)
    mod = _mod_call(cond, w_mod, b_mod).reshape(DEPTH, N_COND_ROWS, 6, D_MODEL)

    ones_blk = jnp.asarray(np.kron(np.eye(N_HEADS), np.ones((HEAD_DIM, HEAD_DIM))), dtype=bf16)
    tri = jnp.asarray(np.triu(np.ones((TOKEN_TILE, TOKEN_TILE))), dtype=bf16)
    rope_tabs = _rope_tables(dec_seq)
    ftab_ctx = _fourier_tables(seq)
    ftab_lat = _fourier_tables(dec_seq)
    gf = final_norm_g.reshape(1, D_MODEL)

    ctx = dict(tokens_per_batch=seq, cond_row0=0, per_batch_cond=False)
    lat = dict(tokens_per_batch=dec_seq, cond_row0=1, per_batch_cond=True)
    xp = (x_prompt.reshape(batch * seq, D_MODEL),)
    xs = (x_sample.reshape(dec_batch * dec_seq, D_MODEL),)
    mod_prev = None
    gkv, wkv = [], []
    for l in range(DEPTH):
        w_ext = w_gate = _layer_weight_bf16(w_in, l)
        g1 = norm1_g[l].reshape(1, D_MODEL)
        g2 = norm2_g[l].reshape(1, D_MODEL)
        qg = jnp.tile(q_norm_g[l], N_HEADS).reshape(1, Q_W)
        kg = jnp.tile(k_norm_g[l], N_KV).reshape(1, LANES)
        wba, wbb, wbc = (_layer_weight_bf16(w, l) for w in (w_branch_a, w_branch_b, w_branch_c))
        wout = _layer_weight_bf16(w_out, l)
        wrt = jnp.concatenate([w_route_expert[l].T, w_route_group[l].T,
                               jnp.zeros((ROUTE_ROWS - N_EXPERTS - N_EXPERT_GROUPS, D_MODEL), f32)], axis=0)
        weg, weu = _layer_weight_bf16(w_exp_gate, l), _layer_weight_bf16(w_exp_up, l)
        wed = _layer_weight_bf16(w_exp_down, l).reshape(N_EXPERT_GROUPS, GROUP_FF, D_MODEL)
        sink = sink_logit[l]

        def mix_and_moe(xin, qa, ka, va, qc, kc, vc, ub, attn, ftab, seq_len, stream):
            oa, oc = attn(qa, ka, va, qc, kc, vc)
            ob = _fourier_call(ub, *ftab, seq_len)
            x1, h2x, code, cnt = _s3_call(xin, mod_prev, oa, ob, oc, mod[l], g1, g2, w_gate, wba, wbb, wbc, wout,
                                          wrt, tri, **stream)
            counts = cnt[:, 0].astype(jnp.int32)
            moe = _moe_call(code.reshape(-1), counts, h2x, weg, weu, wed)
            return (x1, moe)

        qa, ka, va, qc, kc, vc, ub, gkv_l, wkv_l = _s1_call(
            xp, mod_prev, mod[l], g1, w_ext, ones_blk, qg, kg, None, emit_kv=True, **ctx)
        attn_ctx = lambda *qkv: _attn_ctx_call(sink, *qkv, seq)
        xp = mix_and_moe(xp, qa, ka, va, qc, kc, vc, ub, attn_ctx, ftab_ctx, seq, ctx)
        kv_shape = (batch, 2, seq, N_KV, HEAD_DIM)
        gkv.append(gkv_l.reshape(kv_shape))
        wkv.append(wkv_l.reshape(kv_shape))

        qa, ka, va, qc, kc, vc, ub = _s1_call(
            xs, mod_prev, mod[l], g1, w_ext, ones_blk, qg, kg, rope_tabs, emit_kv=False, **lat)
        kxa, vxa = _dup_cache(cache_global_kv, l)
        kxc, vxc = _dup_cache(cache_window_kv, l)
        attn_lat = lambda qa, ka, va, qc, kc, vc: _attn_lat_call(
            sink, qa, ka, va, kxa, vxa, qc, kc, vc, kxc, vxc, dec_seq)
        xs = mix_and_moe(xs, qa, ka, va, qc, kc, vc, ub, attn_lat, ftab_lat, dec_seq, lat)
        mod_prev = mod[l]

    y_prompt = _final_call(*xp, mod_prev, gf, **ctx).reshape(batch, seq, D_MODEL)
    y_sample = _final_call(*xs, mod_prev, gf, **lat).reshape(dec_batch, dec_seq, D_MODEL)
    return (y_prompt, y_sample, jnp.stack(gkv, axis=1), jnp.stack(wkv, axis=1))
```

```python
import functools

import jax
import jax.numpy as jnp
import numpy as np
from jax import lax
from jax.experimental import pallas as pl
from jax.experimental.pallas import tpu as pltpu

f32 = jnp.float32
bf16 = jnp.bfloat16

D_MODEL = 1024
DEPTH = 2
GRID_W = 64
HEAD_DIM = 64
HALF = HEAD_DIM // 2
N_HEADS = 8
N_KV = 2
WINDOW = 128
N_FOURIER_GROUPS = 4
FOURIER_GROUP_DIM = 128
FOURIER_DIM = N_FOURIER_GROUPS * FOURIER_GROUP_DIM
ROPE_THETA = 10000.0
AXIS_FREQS = HEAD_DIM // 4
N_EXPERT_GROUPS = 4
EXPERTS_PER_GROUP = 8
N_EXPERTS = N_EXPERT_GROUPS * EXPERTS_PER_GROUP
EXPERT_FF = 128
NORM_EPS = 1e-6
NEG_INF = -1e30
LOG2_E = 1.4426950408889634

LANES = 128
Q_W = N_HEADS * HEAD_DIM
KV_DUP_W = 2 * N_KV * HEAD_DIM
C_QA, C_KA, C_VA, C_QC, C_KC, C_VC, C_UB, C_GATES, C_END = 0, 512, 640, 768, 1280, 1408, 1536, 2048, 5120
N_COND_ROWS = 16
VMEM_LIMIT = 56 * 1024 * 1024


def _cparams(sem):
    return pltpu.CompilerParams(dimension_semantics=sem, vmem_limit_bytes=VMEM_LIMIT)


def _resident(shape):
    nd = len(shape)
    return pl.BlockSpec(shape, lambda *_: (0,) * nd, pipeline_mode=pl.Buffered(1))


def _sigmoid(x):
    return 0.5 * jnp.tanh(0.5 * x) + 0.5


def _rms_mod(x, g, scale, shift):
    y = x * lax.rsqrt(jnp.mean(x * x, axis=-1, keepdims=True) + NORM_EPS)
    return (y * g) * (1.0 + scale) + shift


def _dot(a, b):
    return jnp.dot(a, b, preferred_element_type=f32)


def _dot_t(a, b):
    return lax.dot_general(a, b, (((1,), (1,)), ((), ())), preferred_element_type=f32)


def _split_bf16(x):
    hi = x.astype(bf16)
    lo = (x - hi.astype(f32)).astype(bf16)
    return hi, lo


def _mod_kernel(cond_ref, w_ref, b_ref, o_ref):
    c = cond_ref[...]
    s = (c * _sigmoid(c)).astype(bf16)
    o_ref[0] = _dot(s, w_ref[0].astype(bf16)) + b_ref[0]


def _mod_call(cond, w_mod, b_mod):
    cols = 2 * D_MODEL
    return pl.pallas_call(
        _mod_kernel,
        grid=(DEPTH, 6 * D_MODEL // cols),
        in_specs=[
            pl.BlockSpec((N_COND_ROWS, D_MODEL), lambda l, j: (0, 0)),
            pl.BlockSpec((1, D_MODEL, cols), lambda l, j: (l, 0, j)),
            pl.BlockSpec((1, 1, cols), lambda l, j: (l, 0, j)),
        ],
        out_specs=pl.BlockSpec((1, N_COND_ROWS, cols), lambda l, j: (l, 0, j)),
        out_shape=jax.ShapeDtypeStruct((DEPTH, N_COND_ROWS, 6 * D_MODEL), f32),
        compiler_params=_cparams(("arbitrary", "arbitrary")),
        name="mod",
    )(cond, w_mod, b_mod.reshape(DEPTH, 1, 6 * D_MODEL))


CAST_BLOCK_BYTES = 4 * 1024 * 1024


def _cast_kernel(w_ref, o_ref):
    o_ref[...] = w_ref[0].astype(bf16)


def _layer_weight_bf16(w, l):
    _, rows, cols = w.shape
    rb = min(rows, 1 << ((CAST_BLOCK_BYTES // (4 * cols)).bit_length() - 1))
    assert rows % rb == 0
    return pl.pallas_call(
        _cast_kernel,
        grid=(rows // rb,),
        in_specs=[pl.BlockSpec((1, rb, cols), lambda i: (l, i, 0))],
        out_specs=pl.BlockSpec((rb, cols), lambda i: (i, 0)),
        out_shape=jax.ShapeDtypeStruct((rows, cols), bf16),
        compiler_params=_cparams(("arbitrary",)),
        name="cast_bf16",
    )(w)


def _head_norm(z, g, ones_blk):
    hi, lo = _split_bf16(z * z)
    ss = _dot(hi, ones_blk) + _dot(lo, ones_blk)
    return z * lax.rsqrt(ss * (1.0 / HEAD_DIM) + NORM_EPS) * g


def _rope(z, cos, sin_signed, first_half):
    outs = []
    for j in range(z.shape[1] // LANES):
        zj = z[:, j * LANES:(j + 1) * LANES]
        swapped = jnp.where(first_half, pltpu.roll(zj, LANES - HALF, 1), pltpu.roll(zj, HALF, 1))
        outs.append(zj * cos + swapped * sin_signed)
    return jnp.concatenate(outs, axis=1)


def _layer_input(x_refs, r0, n):
    rows = pl.ds(r0, n)
    if len(x_refs) == 1:
        return x_refs[0][rows, :]
    x1_ref, moe_ref, modp_ref = x_refs
    return x1_ref[rows, :] + modp_ref[0, 5:6, :] * _rows_from_slabs(moe_ref, r0 * SLAB_ROWS, n, SLAB_ROWS)


S1_SUB_TILES = 2
S3_SUB_TILES = 1


SLAB_ROWS = D_MODEL // LANES


def _rows_from_slabs(ref, start, n_tok, pitch):
    return jnp.concatenate([ref[pl.ds(start + c, n_tok, stride=pitch), :] for c in range(SLAB_ROWS)], axis=1)


def _rows_to_slabs(ref, start, x, pitch):
    for c in range(SLAB_ROWS):
        ref[pl.ds(start + c, x.shape[0], stride=pitch), :] = x[:, c * LANES:(c + 1) * LANES]


def _s1_kernel(*refs, rope, emit_kv, n_x):
    x_refs, refs = refs[:n_x], refs[n_x:]
    mod_ref, g1_ref, w_ref, ones_ref, qg_ref, kg_ref = refs[:6]
    pos = 6
    if rope:
        cos_ref, sin_ref = refs[pos:pos + 2]
        pos += 2
    qa_ref, ka_ref, va_ref, qc_ref, kc_ref, vc_ref, ub_ref = refs[pos:pos + 7]
    pos += 7
    if emit_kv:
        gkv_ref, wkv_ref = refs[pos:pos + 2]

    tm = x_refs[0].shape[0] // S1_SUB_TILES
    lane = lax.broadcasted_iota(jnp.int32, (tm, LANES), 1)
    low_lanes = lane < HEAD_DIM
    first_half = (lane & (HEAD_DIM - 1)) < HALF
    q_scale = HEAD_DIM ** -0.5 * LOG2_E

    def dup(z):
        r = pltpu.roll(z, HEAD_DIM, 1)
        return jnp.concatenate([jnp.where(low_lanes, z, r), jnp.where(low_lanes, r, z)], axis=1).astype(bf16)

    def with_ones(z):
        r = pltpu.roll(z, HEAD_DIM, 1)
        return jnp.concatenate([jnp.where(low_lanes, z, 1.0), jnp.where(low_lanes, r, 1.0)], axis=1).astype(bf16)

    value_layout = with_ones if rope else dup

    for sub in range(S1_SUB_TILES):
        rows = pl.ds(sub * tm, tm)
        x = _layer_input(x_refs, sub * tm, tm)
        h = _rms_mod(x, g1_ref[...], mod_ref[0, 1:2, :], mod_ref[0, 0:1, :]).astype(bf16)
        proj = lambda lo, hi: _dot(h, w_ref[:, lo:hi])
        if rope:
            cos = cos_ref[rows, :]
            sin_signed = sin_ref[rows, :]
            rot = lambda z: _rope(z, cos, sin_signed, first_half)
        else:
            rot = lambda z: z

        qa = rot(_head_norm(proj(C_QA, C_KA), qg_ref[...], ones_ref[...]))
        qa_ref[rows, :] = (qa * q_scale).astype(bf16)
        kva = proj(C_KA, C_QC)
        ka_n = _head_norm(kva[:, 0:LANES], kg_ref[...], ones_ref[0:LANES, 0:LANES])
        ka_ref[rows, :] = dup(rot(ka_n))
        va = kva[:, LANES:2 * LANES]
        va_ref[rows, :] = value_layout(va)
        qc_ref[rows, :] = (rot(proj(C_QC, C_KC)) * q_scale).astype(bf16)
        kvc = proj(C_KC, C_UB)
        kc = kvc[:, 0:LANES]
        kc_ref[rows, :] = dup(rot(kc))
        vc = kvc[:, LANES:2 * LANES]
        vc_ref[rows, :] = value_layout(vc)
        ub_ref[rows, :] = proj(C_UB, C_GATES).astype(bf16)
        if emit_kv:
            gkv_ref[sub, 0] = ka_n
            gkv_ref[sub, 1] = va
            wkv_ref[sub, 0] = kc
            wkv_ref[sub, 1] = vc


TOKEN_TILE = 512


def _cond_index_map(tokens_per_batch, cond_row0, per_batch_cond, tile=TOKEN_TILE):
    tiles_per_batch = max(tokens_per_batch // tile, 1)
    if per_batch_cond:
        assert tokens_per_batch % tile == 0
        return lambda t: (cond_row0 + t // tiles_per_batch, 0, 0)
    return lambda t: (cond_row0, 0, 0)


def _layer_input_specs(xin, mod_prev, cond_idx):
    tok = pl.BlockSpec((TOKEN_TILE, D_MODEL), lambda t: (t, 0))
    if len(xin) == 1:
        return [tok], list(xin)
    slab = pl.BlockSpec((TOKEN_TILE * SLAB_ROWS, LANES), lambda t: (t, 0))
    return [tok, slab, pl.BlockSpec((1, 6, D_MODEL), cond_idx)], [xin[0], xin[1], mod_prev]


def _s1_call(xin, mod_prev, mod, g1, w_ext, ones_blk, qg, kg, rope_tabs, *, tokens_per_batch, cond_row0,
             per_batch_cond, emit_kv):
    n_tok = xin[0].shape[0]
    tm = TOKEN_TILE
    rope = rope_tabs is not None
    tiles_per_batch = max(tokens_per_batch // tm, 1)
    cond_idx = _cond_index_map(tokens_per_batch, cond_row0, per_batch_cond)

    tok = lambda w: pl.BlockSpec((tm, w), lambda t: (t, 0))
    x_specs, x_args = _layer_input_specs(xin, mod_prev, cond_idx)
    in_specs = x_specs + [
        pl.BlockSpec((1, 6, D_MODEL), cond_idx),
        _resident((1, D_MODEL)),
        _resident((D_MODEL, C_GATES)),
        _resident((Q_W, Q_W)),
        _resident((1, Q_W)),
        _resident((1, LANES)),
    ]
    args = x_args + [mod, g1, w_ext, ones_blk, qg, kg]
    if rope:
        in_specs += [pl.BlockSpec((tm, LANES), lambda t: (t % tiles_per_batch, 0))] * 2
        args += list(rope_tabs)
    widths = [Q_W, KV_DUP_W, KV_DUP_W, Q_W, KV_DUP_W, KV_DUP_W, FOURIER_DIM]
    out_specs = [tok(w) for w in widths]
    out_shape = [jax.ShapeDtypeStruct((n_tok, w), bf16) for w in widths]
    if emit_kv:
        assert tokens_per_batch * S1_SUB_TILES == tm
        n_batch = n_tok // tokens_per_batch
        out_specs += [pl.BlockSpec((S1_SUB_TILES, 2, tokens_per_batch, LANES), lambda t: (t, 0, 0, 0))] * 2
        out_shape += [jax.ShapeDtypeStruct((n_batch, 2, tokens_per_batch, LANES), f32)] * 2
    return pl.pallas_call(
        functools.partial(_s1_kernel, rope=rope, emit_kv=emit_kv, n_x=len(x_args)),
        grid=(n_tok // tm,),
        in_specs=in_specs,
        out_specs=out_specs,
        out_shape=out_shape,
        compiler_params=_cparams(("arbitrary",)),
        name="s1_rope" if rope else "s1_ctx",
    )(*args)


def _gqa_group(q2, segs, sink_pair_lo, sink_pair_hi, value_ones):
    qb = q2.shape[0]
    qs = jnp.concatenate([q2[:, 0:LANES], q2[:, LANES:2 * LANES]], axis=0)
    lane = lax.broadcasted_iota(jnp.int32, qs.shape, 1)
    low = lane < HEAD_DIM
    zero = jnp.zeros_like(qs)
    q4 = jnp.concatenate([jnp.where(low, qs, zero), jnp.where(low, zero, qs)], axis=0)

    scores = []
    for k, _, ok in segs:
        s = _dot_t(q4, k)
        if ok is not None:
            s = jnp.where(ok, s, NEG_INF)
        scores.append(s)
    m = scores[0].max(axis=-1, keepdims=True)
    for s in scores[1:]:
        m = jnp.maximum(m, s.max(axis=-1, keepdims=True))
    if sink_pair_lo is not None:
        row = lax.broadcasted_iota(jnp.int32, (4 * qb, 1), 0)
        sink = jnp.where(row < 2 * qb, jnp.where(row < qb, sink_pair_lo[0], sink_pair_lo[1]),
                         jnp.where(row < 3 * qb, sink_pair_hi[0], sink_pair_hi[1]))
        m = jnp.maximum(m, sink)
    if not value_ones:
        denom = jnp.exp2(sink - m) if sink_pair_lo is not None else jnp.zeros_like(m)
        acc = None
        for s, (_, v, _) in zip(scores, segs):
            p = jnp.exp2(s - m)
            denom = denom + p.sum(axis=-1, keepdims=True)
            pv = _dot(p.astype(bf16), v)
            acc = pv if acc is None else acc + pv
        o4 = acc / denom
        o = jnp.where(low, o4[0:2 * qb], o4[2 * qb:4 * qb])
        return jnp.concatenate([o[0:qb], o[qb:2 * qb]], axis=1)

    acc = None
    for s, (_, v, _) in zip(scores, segs):
        pv = _dot(jnp.exp2(s - m).astype(bf16), v)
        acc = pv if acc is None else acc + pv
    if sink_pair_lo is not None:
        low4 = lax.broadcasted_iota(jnp.int32, acc.shape, 1) < HEAD_DIM
        acc = acc + jnp.where(low4, 0.0, jnp.exp2(sink - m))
    swapped = pltpu.roll(acc, HEAD_DIM, 1)
    o = jnp.where(low, acc[0:2 * qb] / swapped[0:2 * qb], swapped[2 * qb:4 * qb] / acc[2 * qb:4 * qb])
    return jnp.concatenate([o[0:qb], o[qb:2 * qb]], axis=1)


def _sink_pairs(sink_ref, kv):
    h = 4 * kv
    s = [sink_ref[h + j] * LOG2_E for j in range(4)]
    return (s[0], s[2]), (s[1], s[3])


ATTN_CTX_BATCH = 4


def _attn_ctx_kernel(sink_ref, qa_ref, ka_ref, va_ref, qc_ref, kc_ref, vc_ref, oa_ref, oc_ref, *, seq):
    for b in range(ATTN_CTX_BATCH):
        rows = slice(b * seq, (b + 1) * seq)
        for kv in range(N_KV):
            cq = slice(2 * LANES * kv, 2 * LANES * (kv + 1))
            ck = slice(LANES * kv, LANES * (kv + 1))
            oa = _gqa_group(qa_ref[rows, cq], [(ka_ref[rows, ck], va_ref[rows, ck], None)], None, None, False)
            oa_ref[rows, cq] = oa.astype(bf16)
            lo, hi = _sink_pairs(sink_ref, kv)
            oc = _gqa_group(qc_ref[rows, cq], [(kc_ref[rows, ck], vc_ref[rows, ck], None)], lo, hi, False)
            oc_ref[rows, cq] = oc.astype(bf16)


def _attn_ctx_call(sink, qa, ka, va, qc, kc, vc, seq):
    n_tok = qa.shape[0]
    rows = ATTN_CTX_BATCH * seq
    tok = lambda w: pl.BlockSpec((rows, w), lambda b: (b, 0))
    return pl.pallas_call(
        functools.partial(_attn_ctx_kernel, seq=seq),
        grid=(n_tok // rows,),
        in_specs=[pl.BlockSpec(memory_space=pltpu.SMEM),
                  tok(Q_W), tok(KV_DUP_W), tok(KV_DUP_W), tok(Q_W), tok(KV_DUP_W), tok(KV_DUP_W)],
        out_specs=[tok(Q_W), tok(Q_W)],
        out_shape=[jax.ShapeDtypeStruct((n_tok, Q_W), bf16)] * 2,
        compiler_params=_cparams(("arbitrary",)),
        name="attn_ctx",
    )(sink, qa, ka, va, qc, kc, vc)


QB = 256
WIN_QB = 128
WIN_SPAN = WIN_QB + 2 * WINDOW


def _attn_lat_kernel(sink_ref, qa_ref, ka_ref, va_ref, kxa_ref, vxa_ref,
                     qc_ref, kc_ref, vc_ref, kxc_ref, vxc_ref, oa_ref, oc_ref, *, seq):
    i = pl.program_id(1)
    for kv in range(N_KV):
        cq = slice(2 * LANES * kv, 2 * LANES * (kv + 1))
        ck = slice(LANES * kv, LANES * (kv + 1))
        segs = [(ka_ref[:, ck], va_ref[:, ck], None), (kxa_ref[0, :, ck], vxa_ref[0, :, ck], None)]
        oa_ref[:, cq] = _gqa_group(qa_ref[:, cq], segs, None, None, True).astype(bf16)

        lo, hi = _sink_pairs(sink_ref, kv)
        for sub in range(QB // WIN_QB):
            blk = i * (QB // WIN_QB) + sub
            start = jnp.clip(blk * WIN_QB - WINDOW, 0, seq - WIN_SPAN)
            start = pl.multiple_of(start, WINDOW)
            r = lax.broadcasted_iota(jnp.int32, (4 * WIN_QB, WIN_SPAN), 0)
            c = lax.broadcasted_iota(jnp.int32, (4 * WIN_QB, WIN_SPAN), 1)
            qpos = blk * WIN_QB + (r & (WIN_QB - 1))
            ok = jnp.abs(qpos - (start + c)) <= WINDOW
            rows = slice(sub * WIN_QB, (sub + 1) * WIN_QB)
            segs = [(kc_ref[pl.ds(start, WIN_SPAN), ck], vc_ref[pl.ds(start, WIN_SPAN), ck], ok),
                    (kxc_ref[0, :, ck], vxc_ref[0, :, ck], None)]
            oc_ref[rows, cq] = _gqa_group(qc_ref[rows, cq], segs, lo, hi, True).astype(bf16)


def _attn_lat_call(sink, qa, ka, va, kxa, vxa, qc, kc, vc, kxc, vxc, seq):
    n_tok = qa.shape[0]
    nq = seq // QB
    past = kxa.shape[1]
    qspec = pl.BlockSpec((QB, Q_W), lambda b, i: (b * nq + i, 0))
    kspec = pl.BlockSpec((seq, KV_DUP_W), lambda b, i: (b, 0))
    xspec = pl.BlockSpec((1, past, KV_DUP_W), lambda b, i: (b, 0, 0))
    return pl.pallas_call(
        functools.partial(_attn_lat_kernel, seq=seq),
        grid=(n_tok // seq, nq),
        in_specs=[pl.BlockSpec(memory_space=pltpu.SMEM),
                  qspec, kspec, kspec, xspec, xspec, qspec, kspec, kspec, xspec, xspec],
        out_specs=[qspec, qspec],
        out_shape=[jax.ShapeDtypeStruct((n_tok, Q_W), bf16)] * 2,
        compiler_params=_cparams(("arbitrary", "arbitrary")),
        name="attn_lat",
    )(sink, qa, ka, va, kxa, vxa, qc, kc, vc, kxc, vxc)


FOURIER_ROWS = 2048


def _fourier_kernel(u_ref, bdc_ref, bds_ref, cn_ref, sn_ref, o_ref, *, scale, seq):
    u = u_ref[...]
    uc = _dot(u, bdc_ref[...]).astype(bf16)
    us = _dot(u, bds_ref[...]).astype(bf16)
    for b in range(u.shape[0] // seq):
        rows = slice(b * seq, (b + 1) * seq)
        o = _dot(cn_ref[...], uc[rows]) - _dot(sn_ref[...], us[rows])
        o_ref[rows, :] = (o * scale).astype(bf16)


def _fourier_call(ub, bdc, bds, cn, sn, seq):
    n_tok = ub.shape[0]
    rows = max(seq, FOURIER_ROWS)
    tok = pl.BlockSpec((rows, FOURIER_DIM), lambda b: (b, 0))
    scale = float(1.0 / np.sqrt(seq * FOURIER_GROUP_DIM))
    return pl.pallas_call(
        functools.partial(_fourier_kernel, scale=scale, seq=seq),
        grid=(n_tok // rows,),
        in_specs=[tok, _resident((FOURIER_DIM, FOURIER_DIM)), _resident((FOURIER_DIM, FOURIER_DIM)),
                  _resident((seq, seq)), _resident((seq, seq))],
        out_specs=tok,
        out_shape=jax.ShapeDtypeStruct((n_tok, FOURIER_DIM), bf16),
        compiler_params=_cparams(("arbitrary",)),
        name=f"fourier_{seq}",
    )(ub, bdc, bds, cn, sn)


ROUTE_ROWS = LANES
H2X_PITCH = SLAB_ROWS + 1
MOE_PITCH = SLAB_ROWS


def _route(lt, tri_ref, carry_ref, n_tok):
    tm = lt.shape[1]
    row = lax.broadcasted_iota(jnp.int32, (EXPERTS_PER_GROUP, tm), 0).astype(f32)
    neg = jnp.float32(-jnp.inf)
    none = jnp.float32(EXPERTS_PER_GROUP)
    gl = jnp.where(row < N_EXPERT_GROUPS, lt[N_EXPERTS:N_EXPERTS + EXPERTS_PER_GROUP], neg)
    gmax = gl.max(axis=0, keepdims=True)
    gidx = jnp.where(gl == gmax, row, none).min(axis=0, keepdims=True)
    gw = 1.0 / jnp.exp(gl - gmax).sum(axis=0, keepdims=True)
    lg = lt[0:EXPERTS_PER_GROUP]
    for g in range(1, N_EXPERT_GROUPS):
        lg = jnp.where(gidx == g, lt[g * EXPERTS_PER_GROUP:(g + 1) * EXPERTS_PER_GROUP], lg)
    m1 = lg.max(axis=0, keepdims=True)
    i1 = jnp.where(lg == m1, row, none).min(axis=0, keepdims=True)
    l2 = jnp.where(row == i1, neg, lg)
    m2 = l2.max(axis=0, keepdims=True)
    i2 = jnp.where(l2 == m2, row, none).min(axis=0, keepdims=True)
    e = jnp.exp(m2 - m1)
    w8 = jnp.where(row == i1, gw / (1.0 + e), jnp.where(row == i2, gw * e / (1.0 + e), 0.0))

    onehot = (row == gidx).astype(f32)
    prefix = _dot(onehot.astype(bf16), tri_ref[0:tm, 0:tm])
    carry = carry_ref[...]
    rank = (onehot * (prefix - 1.0 + carry[:, 0:1])).sum(axis=0, keepdims=True)
    carry_ref[...] = carry + onehot.sum(axis=1, keepdims=True)
    dest = gidx.astype(jnp.int32) * n_tok + rank.astype(jnp.int32)
    return w8, dest


def _s3_kernel(*refs, n_x, n_tok):
    x_refs, refs = refs[:n_x], refs[n_x:]
    (oa_ref, ob_ref, oc_ref, mod_ref, g1_ref, g2_ref, wga_ref, wgb_ref, wgc_ref, wba_ref, wbb_ref, wbc_ref,
     wout_ref, wrt_ref, tri_ref, x1_ref, h2x_ref, code_ref, cnt_ref, carry_ref) = refs
    wg_refs = (wga_ref, wgb_ref, wgc_ref)

    @pl.when(pl.program_id(0) == 0)
    def _():
        carry_ref[...] = jnp.zeros_like(carry_ref)

    tm = x1_ref.shape[0] // S3_SUB_TILES
    wh, wl = _split_bf16(wrt_ref[...])
    for sub in range(S3_SUB_TILES):
        r0 = sub * tm
        rows = pl.ds(r0, tm)
        x = _layer_input(x_refs, r0, tm)
        h = _rms_mod(x, g1_ref[...], mod_ref[0, 1:2, :], mod_ref[0, 0:1, :]).astype(bf16)
        merged = None
        for j, (o_ref, wb_ref) in enumerate(((oa_ref, wba_ref), (ob_ref, wbb_ref), (oc_ref, wbc_ref))):
            gate = _sigmoid(_dot(h, wg_refs[j][...]))
            term = gate * _dot(o_ref[rows, :], wb_ref[...])
            merged = term if merged is None else merged + term
        mix = _dot(merged.astype(bf16), wout_ref[...])
        x1 = x + mod_ref[0, 2:3, :] * mix
        x1_ref[rows, :] = x1
        h2 = _rms_mod(x1, g2_ref[...], mod_ref[0, 4:5, :], mod_ref[0, 3:4, :])
        _rows_to_slabs(h2x_ref, r0 * H2X_PITCH, h2, H2X_PITCH)
        th, tl = _split_bf16(h2)
        lt = _dot_t(wh, th) + (_dot_t(wh, tl) + _dot_t(wl, th))
        w8, code = _route(lt, tri_ref, carry_ref, n_tok)
        code_ref[:, rows] = code
        w_rows = jnp.concatenate([w8, jnp.zeros((LANES - EXPERTS_PER_GROUP, tm), f32)], axis=0)
        h2x_ref[pl.ds(r0 * H2X_PITCH + SLAB_ROWS, tm, stride=H2X_PITCH), :] = w_rows.T
    cnt_ref[...] = carry_ref[...]


def _s3_call(xin, mod_prev, oa, ob, oc, mod, g1, g2, wg, wba, wbb, wbc, wout, wrt, tri, *, tokens_per_batch,
             cond_row0, per_batch_cond):
    n_tok = xin[0].shape[0]
    tm = TOKEN_TILE
    cond_idx = _cond_index_map(tokens_per_batch, cond_row0, per_batch_cond)
    tok = lambda w: pl.BlockSpec((tm, w), lambda t: (t, 0))
    x_specs, x_args = _layer_input_specs(xin, mod_prev, cond_idx)
    gate_cols = lambda j: pl.BlockSpec((D_MODEL, D_MODEL), lambda t: (0, C_GATES // D_MODEL + j),
                                       pipeline_mode=pl.Buffered(1))
    return pl.pallas_call(
        functools.partial(_s3_kernel, n_x=len(x_args), n_tok=n_tok),
        grid=(n_tok // tm,),
        in_specs=x_specs + [
            tok(Q_W), tok(FOURIER_DIM), tok(Q_W),
            pl.BlockSpec((1, 6, D_MODEL), cond_idx),
            _resident((1, D_MODEL)), _resident((1, D_MODEL)),
            gate_cols(0), gate_cols(1), gate_cols(2),
            _resident((Q_W, D_MODEL)), _resident((FOURIER_DIM, D_MODEL)), _resident((Q_W, D_MODEL)),
            _resident((D_MODEL, D_MODEL)), _resident((ROUTE_ROWS, D_MODEL)), _resident((tm, tm))],
        out_specs=[tok(D_MODEL), pl.BlockSpec((tm * H2X_PITCH, LANES), lambda t: (t, 0)),
                   pl.BlockSpec((1, tm), lambda t: (0, t)),
                   pl.BlockSpec((EXPERTS_PER_GROUP, LANES), lambda t: (0, 0))],
        out_shape=[jax.ShapeDtypeStruct((n_tok, D_MODEL), f32),
                   jax.ShapeDtypeStruct((n_tok * H2X_PITCH, LANES), f32),
                   jax.ShapeDtypeStruct((1, n_tok), jnp.int32),
                   jax.ShapeDtypeStruct((EXPERTS_PER_GROUP, LANES), f32)],
        scratch_shapes=[pltpu.VMEM((EXPERTS_PER_GROUP, LANES), f32)],
        compiler_params=_cparams(("arbitrary",)),
        name="s3",
    )(*x_args, oa, ob, oc, mod, g1, g2, wg, wg, wg, wba, wbb, wbc, wout, wrt, tri)


MOE_TILE_LOG2 = 8
MOE_TILE = 1 << MOE_TILE_LOG2
ROW_UNROLL_LOG2 = 3
ROW_UNROLL = 1 << ROW_UNROLL_LOG2
GROUP_FF = EXPERTS_PER_GROUP * EXPERT_FF
PLAN_FIRST, PLAN_ITEMS, PLAN_LEN = 0, N_EXPERT_GROUPS, N_EXPERT_GROUPS + 1


def _moe_kernel(code_ref, cnt_ref, h2x_hbm, weg_ref, weu_ref, wed_ref, out_hbm,
                pos_ref, plan_ref, gbuf, obuf, dump, gsem, ssem, *, n_tok):
    tm = MOE_TILE
    k = pl.program_id(0)
    n_steps = pl.num_programs(0)

    def out_slab(buf, slot, i):
        return buf.at[slot, :, i]

    def token_slab(tok):
        return out_hbm.at[pl.ds(pl.multiple_of(tok * MOE_PITCH, MOE_PITCH), MOE_PITCH)]

    def for_row_chunks(n_chunks, body):
        def chunk(c, carry):
            body(c * ROW_UNROLL)
            return carry
        lax.fori_loop(0, n_chunks, chunk, 0)

    @pl.when(k == 0)
    def _plan():
        first = jnp.int32(0)
        for g in range(N_EXPERT_GROUPS):
            c = cnt_ref[g]
            n_tiles = (c + (tm - 1)) >> MOE_TILE_LOG2
            plan_ref[PLAN_FIRST + g] = first

            def pad(p, carry):
                pos_ref[p] = 0
                return carry

            lax.fori_loop(g * n_tok + c, g * n_tok + n_tiles * tm, pad, 0)
            first = first + n_tiles
        plan_ref[PLAN_ITEMS] = first

        def place(t0):
            dests = [code_ref[t0 + j] for j in range(ROW_UNROLL)]
            for j, dest in enumerate(dests):
                pos_ref[dest] = t0 + j

        for_row_chunks(n_tok >> ROW_UNROLL_LOG2, place)

    n_items = plan_ref[PLAN_ITEMS]

    def item_group(item):
        return ((item >= plan_ref[PLAN_FIRST + 1]).astype(jnp.int32)
                + (item >= plan_ref[PLAN_FIRST + 2]).astype(jnp.int32)
                + (item >= plan_ref[PLAN_FIRST + 3]).astype(jnp.int32))

    def item_pos0(item):
        g = item_group(item)
        return g * n_tok + (item - plan_ref[PLAN_FIRST + g]) * tm

    def start_gather(item, slot):
        pos0 = item_pos0(item)

        def chunk(i0):
            toks = [pos_ref[pos0 + i0 + j] for j in range(ROW_UNROLL)]
            for j, tok in enumerate(toks):
                pltpu.make_async_copy(h2x_hbm.at[pl.ds(tok * H2X_PITCH, H2X_PITCH)],
                                      gbuf.at[slot, :, i0 + j],
                                      gsem.at[slot]).start(priority=j % 2)
        for_row_chunks(tm >> ROW_UNROLL_LOG2, chunk)

    def wait_gather(slot):
        pltpu.make_async_copy(gbuf.at[1 - slot], gbuf.at[slot], gsem.at[slot]).wait()

    def start_scatter(item, n_valid, slot):
        pos0 = item_pos0(item)

        def to_token(i0):
            dsts = [pos_ref[pos0 + i0 + j] for j in range(ROW_UNROLL)]
            for j, dst in enumerate(dsts):
                pltpu.make_async_copy(out_slab(obuf, slot, i0 + j), token_slab(dst),
                                      ssem.at[slot]).start(priority=j % 2)

        def row_to_token(i, carry):
            pltpu.make_async_copy(out_slab(obuf, slot, i), token_slab(pos_ref[pos0 + i]), ssem.at[slot]).start()
            return carry

        def row_to_dump(i, carry):
            pltpu.make_async_copy(out_slab(obuf, slot, i), out_slab(dump, slot, i), ssem.at[slot]).start()
            return carry

        full = n_valid >> ROW_UNROLL_LOG2
        for_row_chunks(full, to_token)
        lax.fori_loop(full * ROW_UNROLL, n_valid, row_to_token, 0)
        lax.fori_loop(n_valid, tm, row_to_dump, 0)

    def wait_scatter(slot):
        pltpu.make_async_copy(obuf.at[slot], dump.at[slot], ssem.at[slot]).wait()

    slot = k % 2

    @pl.when(k == 0)
    def _():
        start_gather(0, 0)

    @pl.when(k + 1 < n_items)
    def _():
        start_gather(k + 1, 1 - slot)

    @pl.when(k < n_items)
    def _():
        group = item_group(k)
        wait_gather(slot)
        h = jnp.concatenate([gbuf[slot, c] for c in range(SLAB_ROWS)], axis=1).astype(bf16)
        w8 = gbuf[slot, SLAB_ROWS]
        cols = pl.ds(pl.multiple_of(group * GROUP_FF, GROUP_FF), GROUP_FF)
        gate = _dot(h, weg_ref[:, cols])
        a = gate * _sigmoid(gate) * _dot(h, weu_ref[:, cols])
        pieces = [a[:, j * EXPERT_FF:(j + 1) * EXPERT_FF] * w8[:, j:j + 1] for j in range(EXPERTS_PER_GROUP)]
        out = _dot(jnp.concatenate(pieces, axis=1).astype(bf16), wed_ref[group])

        @pl.when(k >= 2)
        def _():
            wait_scatter(slot)

        for c in range(SLAB_ROWS):
            obuf[slot, c] = out[:, c * LANES:(c + 1) * LANES]
        n_valid = jnp.minimum(tm, cnt_ref[group] - (k - plan_ref[PLAN_FIRST + group]) * tm)
        start_scatter(k, n_valid, slot)

    @pl.when(k == n_steps - 1)
    def _drain():
        wait_scatter((n_items - 1) % 2)

        @pl.when(n_items >= 2)
        def _():
            wait_scatter(n_items % 2)


def _moe_call(code, cnt, h2x, weg, weu, wed):
    n_tok = h2x.shape[0] // H2X_PITCH
    tm = MOE_TILE
    n_steps = n_tok // tm + N_EXPERT_GROUPS - 1
    ff = N_EXPERT_GROUPS * GROUP_FF
    up_spec = pl.BlockSpec((D_MODEL, ff), lambda k, *_: (0, 0), pipeline_mode=pl.Buffered(1))
    down_spec = pl.BlockSpec((N_EXPERT_GROUPS, GROUP_FF, D_MODEL), lambda k, *_: (0, 0, 0),
                             pipeline_mode=pl.Buffered(1))
    grid_spec = pltpu.PrefetchScalarGridSpec(
        num_scalar_prefetch=2,
        grid=(n_steps,),
        in_specs=[pl.BlockSpec(memory_space=pl.ANY), up_spec, up_spec, down_spec],
        out_specs=pl.BlockSpec(memory_space=pl.ANY),
        scratch_shapes=[
            pltpu.SMEM((N_EXPERT_GROUPS * n_tok,), jnp.int32),
            pltpu.SMEM((PLAN_LEN,), jnp.int32),
            pltpu.VMEM((2, H2X_PITCH, tm, LANES), f32),
            pltpu.VMEM((2, MOE_PITCH, tm, LANES), f32),
            pltpu.VMEM((2, MOE_PITCH, tm, LANES), f32),
            pltpu.SemaphoreType.DMA((2,)),
            pltpu.SemaphoreType.DMA((2,)),
        ],
    )
    return pl.pallas_call(
        functools.partial(_moe_kernel, n_tok=n_tok),
        grid_spec=grid_spec,
        out_shape=jax.ShapeDtypeStruct((n_tok * MOE_PITCH, LANES), f32),
        compiler_params=_cparams(("arbitrary",)),
        name="moe",
    )(code, cnt, h2x, weg, weu, wed)


def _final_kernel(x1_ref, moe_ref, mod_ref, gf_ref, o_ref):
    x2 = x1_ref[...] + mod_ref[0, 5:6, :] * _rows_from_slabs(moe_ref, 0, x1_ref.shape[0], MOE_PITCH)
    y = x2 * lax.rsqrt(jnp.mean(x2 * x2, axis=-1, keepdims=True) + NORM_EPS)
    o_ref[...] = y * gf_ref[...]


FINAL_TILE = 1024


def _final_call(x1, moe, mod, gf, *, tokens_per_batch, cond_row0, per_batch_cond):
    n_tok = x1.shape[0]
    tok = pl.BlockSpec((FINAL_TILE, D_MODEL), lambda t: (t, 0))
    slab = pl.BlockSpec((FINAL_TILE * MOE_PITCH, LANES), lambda t: (t, 0))
    cond_idx = _cond_index_map(tokens_per_batch, cond_row0, per_batch_cond, FINAL_TILE)
    return pl.pallas_call(
        _final_kernel,
        grid=(n_tok // FINAL_TILE,),
        in_specs=[tok, slab, pl.BlockSpec((1, 6, D_MODEL), cond_idx), _resident((1, D_MODEL))],
        out_specs=tok,
        out_shape=jax.ShapeDtypeStruct((n_tok, D_MODEL), f32),
        compiler_params=_cparams(("arbitrary",)),
        name="final_norm",
    )(x1, moe, mod, gf)


def _rope_tables(n):
    rows = n // GRID_W
    row = jnp.repeat(jnp.arange(rows, dtype=f32), GRID_W)
    col = jnp.tile(jnp.arange(GRID_W, dtype=f32), rows)
    inv = ROPE_THETA ** (-jnp.arange(AXIS_FREQS, dtype=f32) / AXIS_FREQS)
    ang = jnp.concatenate([row[:, None] * inv, col[:, None] * inv], axis=-1)
    cos, sin = jnp.cos(ang), jnp.sin(ang)
    cos_t = jnp.concatenate([cos, cos, cos, cos], axis=-1)
    sin_t = jnp.concatenate([-sin, sin, -sin, sin], axis=-1)
    return cos_t, sin_t


def _dft_tables(n):
    k = np.arange(n)
    m = (k[:, None] * k[None, :]) % n
    ang = 2.0 * np.pi * m / n
    return np.cos(ang), np.sin(ang)


def _fourier_tables(seq):
    cn, sn = _dft_tables(seq)
    cc, sc = _dft_tables(FOURIER_GROUP_DIM)
    eye = np.eye(N_FOURIER_GROUPS)
    bdc, bds = np.kron(eye, cc), np.kron(eye, sc)
    return tuple(jnp.asarray(t, dtype=f32).astype(bf16) for t in (bdc, bds, cn, sn))


def _dup_cache(cache, l):
    b, _, _, p, _, _ = cache.shape
    k, v = cache[:, l, 0], cache[:, l, 1]
    k2 = jnp.broadcast_to(k[:, :, :, None, :], (b, p, N_KV, 2, HEAD_DIM)).reshape(b, p, KV_DUP_W)
    v1 = jnp.stack([v, jnp.ones_like(v)], axis=3).reshape(b, p, KV_DUP_W)
    return k2.astype(bf16), v1.astype(bf16)


def kernel(x_prompt, x_sample, cache_global_kv, cache_window_kv, c, c_ctx, w_mod, b_mod, norm1_g, w_in,
           q_norm_g, k_norm_g, sink_logit, w_branch_a, w_branch_b, w_branch_c, w_out, norm2_g,
           w_route_group, w_route_expert, w_exp_gate, w_exp_up, w_exp_down, final_norm_g):
    batch, seq, _ = x_prompt.shape
    dec_batch, dec_seq, _ = x_sample.shape

    cond = jnp.zeros((N_COND_ROWS, D_MODEL), f32).at[0].set(c_ctx).at[1:1 + dec_batch].set(c)
    mod = _mod_call(cond, w_mod, b_mod).reshape(DEPTH, N_COND_ROWS, 6, D_MODEL)

    ones_blk = jnp.asarray(np.kron(np.eye(N_HEADS), np.ones((HEAD_DIM, HEAD_DIM))), dtype=bf16)
    tri = jnp.asarray(np.triu(np.ones((TOKEN_TILE, TOKEN_TILE))), dtype=bf16)
    rope_tabs = _rope_tables(dec_seq)
    ftab_ctx = _fourier_tables(seq)
    ftab_lat = _fourier_tables(dec_seq)
    gf = final_norm_g.reshape(1, D_MODEL)

    ctx = dict(tokens_per_batch=seq, cond_row0=0, per_batch_cond=False)
    lat = dict(tokens_per_batch=dec_seq, cond_row0=1, per_batch_cond=True)
    xp = (x_prompt.reshape(batch * seq, D_MODEL),)
    xs = (x_sample.reshape(dec_batch * dec_seq, D_MODEL),)
    mod_prev = None
    gkv, wkv = [], []
    for l in range(DEPTH):
        w_ext = w_gate = _layer_weight_bf16(w_in, l)
        g1 = norm1_g[l].reshape(1, D_MODEL)
        g2 = norm2_g[l].reshape(1, D_MODEL)
        qg = jnp.tile(q_norm_g[l], N_HEADS).reshape(1, Q_W)
        kg = jnp.tile(k_norm_g[l], N_KV).reshape(1, LANES)
        wba, wbb, wbc = (_layer_weight_bf16(w, l) for w in (w_branch_a, w_branch_b, w_branch_c))
        wout = _layer_weight_bf16(w_out, l)
        wrt = jnp.concatenate([w_route_expert[l].T, w_route_group[l].T,
                               jnp.zeros((ROUTE_ROWS - N_EXPERTS - N_EXPERT_GROUPS, D_MODEL), f32)], axis=0)
        weg, weu = _layer_weight_bf16(w_exp_gate, l), _layer_weight_bf16(w_exp_up, l)
        wed = _layer_weight_bf16(w_exp_down, l).reshape(N_EXPERT_GROUPS, GROUP_FF, D_MODEL)
        sink = sink_logit[l]

        def mix_and_moe(xin, qa, ka, va, qc, kc, vc, ub, attn, ftab, seq_len, stream):
            oa, oc = attn(qa, ka, va, qc, kc, vc)
            ob = _fourier_call(ub, *ftab, seq_len)
            x1, h2x, code, cnt = _s3_call(xin, mod_prev, oa, ob, oc, mod[l], g1, g2, w_gate, wba, wbb, wbc, wout,
                                          wrt, tri, **stream)
            counts = cnt[:, 0].astype(jnp.int32)
            moe = _moe_call(code.reshape(-1), counts, h2x, weg, weu, wed)
            return (x1, moe)

        qa, ka, va, qc, kc, vc, ub, gkv_l, wkv_l = _s1_call(
            xp, mod_prev, mod[l], g1, w_ext, ones_blk, qg, kg, None, emit_kv=True, **ctx)
        attn_ctx = lambda *qkv: _attn_ctx_call(sink, *qkv, seq)
        xp = mix_and_moe(xp, qa, ka, va, qc, kc, vc, ub, attn_ctx, ftab_ctx, seq, ctx)
        kv_shape = (batch, 2, seq, N_KV, HEAD_DIM)
        gkv.append(gkv_l.reshape(kv_shape))
        wkv.append(wkv_l.reshape(kv_shape))

        qa, ka, va, qc, kc, vc, ub = _s1_call(
            xs, mod_prev, mod[l], g1, w_ext, ones_blk, qg, kg, rope_tabs, emit_kv=False, **lat)
        kxa, vxa = _dup_cache(cache_global_kv, l)
        kxc, vxc = _dup_cache(cache_window_kv, l)
        attn_lat = lambda qa, ka, va, qc, kc, vc: _attn_lat_call(
            sink, qa, ka, va, kxa, vxa, qc, kc, vc, kxc, vxc, dec_seq)
        xs = mix_and_moe(xs, qa, ka, va, qc, kc, vc, ub, attn_lat, ftab_lat, dec_seq, lat)
        mod_prev = mod[l]

    y_prompt = _final_call(*xp, mod_prev, gf, **ctx).reshape(batch, seq, D_MODEL)
    y_sample = _final_call(*xs, mod_prev, gf, **lat).reshape(dec_batch, dec_seq, D_MODEL)
    return (y_prompt, y_sample, jnp.stack(gkv, axis=1), jnp.stack(wkv, axis=1))
```

```python
import functools

import jax
import jax.numpy as jnp
import numpy as np
from jax import lax
from jax.experimental import pallas as pl
from jax.experimental.pallas import tpu as pltpu

f32 = jnp.float32
bf16 = jnp.bfloat16

D_MODEL = 1024
DEPTH = 2
GRID_W = 64
HEAD_DIM = 64
HALF = HEAD_DIM // 2
N_HEADS = 8
N_KV = 2
WINDOW = 128
N_FOURIER_GROUPS = 4
FOURIER_GROUP_DIM = 128
FOURIER_DIM = N_FOURIER_GROUPS * FOURIER_GROUP_DIM
ROPE_THETA = 10000.0
AXIS_FREQS = HEAD_DIM // 4
N_EXPERT_GROUPS = 4
EXPERTS_PER_GROUP = 8
N_EXPERTS = N_EXPERT_GROUPS * EXPERTS_PER_GROUP
EXPERT_FF = 128
NORM_EPS = 1e-6
NEG_INF = -1e30
LOG2_E = 1.4426950408889634

LANES = 128
Q_W = N_HEADS * HEAD_DIM
KV_DUP_W = 2 * N_KV * HEAD_DIM
C_QA, C_KA, C_VA, C_QC, C_KC, C_VC, C_UB, C_GATES, C_END = 0, 512, 640, 768, 1280, 1408, 1536, 2048, 5120
N_COND_ROWS = 16
VMEM_LIMIT = 56 * 1024 * 1024


def _cparams(sem):
    return pltpu.CompilerParams(dimension_semantics=sem, vmem_limit_bytes=VMEM_LIMIT)


def _resident(shape):
    nd = len(shape)
    return pl.BlockSpec(shape, lambda *_: (0,) * nd, pipeline_mode=pl.Buffered(1))


def _sigmoid(x):
    return 0.5 * jnp.tanh(0.5 * x) + 0.5


def _rms_mod(x, g, scale, shift):
    y = x * lax.rsqrt(jnp.mean(x * x, axis=-1, keepdims=True) + NORM_EPS)
    return (y * g) * (1.0 + scale) + shift


def _dot(a, b):
    return jnp.dot(a, b, preferred_element_type=f32)


def _dot_t(a, b):
    return lax.dot_general(a, b, (((1,), (1,)), ((), ())), preferred_element_type=f32)


def _split_bf16(x):
    hi = x.astype(bf16)
    lo = (x - hi.astype(f32)).astype(bf16)
    return hi, lo


def _mod_kernel(cond_ref, w_ref, b_ref, o_ref):
    c = cond_ref[...]
    s = (c * _sigmoid(c)).astype(bf16)
    o_ref[0] = _dot(s, w_ref[0].astype(bf16)) + b_ref[0]


def _mod_call(cond, w_mod, b_mod):
    cols = 2 * D_MODEL
    return pl.pallas_call(
        _mod_kernel,
        grid=(DEPTH, 6 * D_MODEL // cols),
        in_specs=[
            pl.BlockSpec((N_COND_ROWS, D_MODEL), lambda l, j: (0, 0)),
            pl.BlockSpec((1, D_MODEL, cols), lambda l, j: (l, 0, j)),
            pl.BlockSpec((1, 1, cols), lambda l, j: (l, 0, j)),
        ],
        out_specs=pl.BlockSpec((1, N_COND_ROWS, cols), lambda l, j: (l, 0, j)),
        out_shape=jax.ShapeDtypeStruct((DEPTH, N_COND_ROWS, 6 * D_MODEL), f32),
        compiler_params=_cparams(("arbitrary", "arbitrary")),
        name="mod",
    )(cond, w_mod, b_mod.reshape(DEPTH, 1, 6 * D_MODEL))


CAST_BLOCK_BYTES = 4 * 1024 * 1024


def _cast_kernel(w_ref, o_ref):
    o_ref[...] = w_ref[0].astype(bf16)


def _layer_weight_bf16(w, l):
    _, rows, cols = w.shape
    rb = min(rows, 1 << ((CAST_BLOCK_BYTES // (4 * cols)).bit_length() - 1))
    assert rows % rb == 0
    return pl.pallas_call(
        _cast_kernel,
        grid=(rows // rb,),
        in_specs=[pl.BlockSpec((1, rb, cols), lambda i: (l, i, 0))],
        out_specs=pl.BlockSpec((rb, cols), lambda i: (i, 0)),
        out_shape=jax.ShapeDtypeStruct((rows, cols), bf16),
        compiler_params=_cparams(("arbitrary",)),
        name="cast_bf16",
    )(w)


def _head_norm(z, g, ones_blk):
    hi, lo = _split_bf16(z * z)
    ss = _dot(hi, ones_blk) + _dot(lo, ones_blk)
    return z * lax.rsqrt(ss * (1.0 / HEAD_DIM) + NORM_EPS) * g


def _rope(z, cos, sin_signed, first_half):
    outs = []
    for j in range(z.shape[1] // LANES):
        zj = z[:, j * LANES:(j + 1) * LANES]
        swapped = jnp.where(first_half, pltpu.roll(zj, LANES - HALF, 1), pltpu.roll(zj, HALF, 1))
        outs.append(zj * cos + swapped * sin_signed)
    return jnp.concatenate(outs, axis=1)


def _layer_input(x_refs, r0, n):
    rows = pl.ds(r0, n)
    if len(x_refs) == 1:
        return x_refs[0][rows, :]
    x1_ref, moe_ref, modp_ref = x_refs
    return x1_ref[rows, :] + modp_ref[0, 5:6, :] * _rows_from_slabs(moe_ref, r0 * SLAB_ROWS, n, SLAB_ROWS)


S1_SUB_TILES = 2
S3_SUB_TILES = 1


SLAB_ROWS = D_MODEL // LANES


def _rows_from_slabs(ref, start, n_tok, pitch):
    return jnp.concatenate([ref[pl.ds(start + c, n_tok, stride=pitch), :] for c in range(SLAB_ROWS)], axis=1)


def _rows_to_slabs(ref, start, x, pitch):
    for c in range(SLAB_ROWS):
        ref[pl.ds(start + c, x.shape[0], stride=pitch), :] = x[:, c * LANES:(c + 1) * LANES]


def _s1_kernel(*refs, rope, emit_kv, n_x):
    x_refs, refs = refs[:n_x], refs[n_x:]
    mod_ref, g1_ref, w_ref, ones_ref, qg_ref, kg_ref = refs[:6]
    pos = 6
    if rope:
        cos_ref, sin_ref = refs[pos:pos + 2]
        pos += 2
    qa_ref, ka_ref, va_ref, qc_ref, kc_ref, vc_ref, ub_ref = refs[pos:pos + 7]
    pos += 7
    if emit_kv:
        gkv_ref, wkv_ref = refs[pos:pos + 2]

    tm = x_refs[0].shape[0] // S1_SUB_TILES
    lane = lax.broadcasted_iota(jnp.int32, (tm, LANES), 1)
    low_lanes = lane < HEAD_DIM
    first_half = (lane & (HEAD_DIM - 1)) < HALF
    q_scale = HEAD_DIM ** -0.5 * LOG2_E

    def dup(z):
        r = pltpu.roll(z, HEAD_DIM, 1)
        return jnp.concatenate([jnp.where(low_lanes, z, r), jnp.where(low_lanes, r, z)], axis=1).astype(bf16)

    def with_ones(z):
        r = pltpu.roll(z, HEAD_DIM, 1)
        return jnp.concatenate([jnp.where(low_lanes, z, 1.0), jnp.where(low_lanes, r, 1.0)], axis=1).astype(bf16)

    value_layout = with_ones if rope else dup

    for sub in range(S1_SUB_TILES):
        rows = pl.ds(sub * tm, tm)
        x = _layer_input(x_refs, sub * tm, tm)
        h = _rms_mod(x, g1_ref[...], mod_ref[0, 1:2, :], mod_ref[0, 0:1, :]).astype(bf16)
        proj = lambda lo, hi: _dot(h, w_ref[:, lo:hi])
        if rope:
            cos = cos_ref[rows, :]
            sin_signed = sin_ref[rows, :]
            rot = lambda z: _rope(z, cos, sin_signed, first_half)
        else:
            rot = lambda z: z

        qa = rot(_head_norm(proj(C_QA, C_KA), qg_ref[...], ones_ref[...]))
        qa_ref[rows, :] = (qa * q_scale).astype(bf16)
        kva = proj(C_KA, C_QC)
        ka_n = _head_norm(kva[:, 0:LANES], kg_ref[...], ones_ref[0:LANES, 0:LANES])
        ka_ref[rows, :] = dup(rot(ka_n))
        va = kva[:, LANES:2 * LANES]
        va_ref[rows, :] = value_layout(va)
        qc_ref[rows, :] = (rot(proj(C_QC, C_KC)) * q_scale).astype(bf16)
        kvc = proj(C_KC, C_UB)
        kc = kvc[:, 0:LANES]
        kc_ref[rows, :] = dup(rot(kc))
        vc = kvc[:, LANES:2 * LANES]
        vc_ref[rows, :] = value_layout(vc)
        ub_ref[rows, :] = proj(C_UB, C_GATES).astype(bf16)
        if emit_kv:
            gkv_ref[sub, 0] = ka_n
            gkv_ref[sub, 1] = va
            wkv_ref[sub, 0] = kc
            wkv_ref[sub, 1] = vc


TOKEN_TILE = 512


def _cond_index_map(tokens_per_batch, cond_row0, per_batch_cond, tile=TOKEN_TILE):
    tiles_per_batch = max(tokens_per_batch // tile, 1)
    if per_batch_cond:
        assert tokens_per_batch % tile == 0
        return lambda t: (cond_row0 + t // tiles_per_batch, 0, 0)
    return lambda t: (cond_row0, 0, 0)


def _layer_input_specs(xin, mod_prev, cond_idx):
    tok = pl.BlockSpec((TOKEN_TILE, D_MODEL), lambda t: (t, 0))
    if len(xin) == 1:
        return [tok], list(xin)
    slab = pl.BlockSpec((TOKEN_TILE * SLAB_ROWS, LANES), lambda t: (t, 0))
    return [tok, slab, pl.BlockSpec((1, 6, D_MODEL), cond_idx)], [xin[0], xin[1], mod_prev]


def _s1_call(xin, mod_prev, mod, g1, w_ext, ones_blk, qg, kg, rope_tabs, *, tokens_per_batch, cond_row0,
             per_batch_cond, emit_kv):
    n_tok = xin[0].shape[0]
    tm = TOKEN_TILE
    rope = rope_tabs is not None
    tiles_per_batch = max(tokens_per_batch // tm, 1)
    cond_idx = _cond_index_map(tokens_per_batch, cond_row0, per_batch_cond)

    tok = lambda w: pl.BlockSpec((tm, w), lambda t: (t, 0))
    x_specs, x_args = _layer_input_specs(xin, mod_prev, cond_idx)
    in_specs = x_specs + [
        pl.BlockSpec((1, 6, D_MODEL), cond_idx),
        _resident((1, D_MODEL)),
        _resident((D_MODEL, C_GATES)),
        _resident((Q_W, Q_W)),
        _resident((1, Q_W)),
        _resident((1, LANES)),
    ]
    args = x_args + [mod, g1, w_ext, ones_blk, qg, kg]
    if rope:
        in_specs += [pl.BlockSpec((tm, LANES), lambda t: (t % tiles_per_batch, 0))] * 2
        args += list(rope_tabs)
    widths = [Q_W, KV_DUP_W, KV_DUP_W, Q_W, KV_DUP_W, KV_DUP_W, FOURIER_DIM]
    out_specs = [tok(w) for w in widths]
    out_shape = [jax.ShapeDtypeStruct((n_tok, w), bf16) for w in widths]
    if emit_kv:
        assert tokens_per_batch * S1_SUB_TILES == tm
        n_batch = n_tok // tokens_per_batch
        out_specs += [pl.BlockSpec((S1_SUB_TILES, 2, tokens_per_batch, LANES), lambda t: (t, 0, 0, 0))] * 2
        out_shape += [jax.ShapeDtypeStruct((n_batch, 2, tokens_per_batch, LANES), f32)] * 2
    return pl.pallas_call(
        functools.partial(_s1_kernel, rope=rope, emit_kv=emit_kv, n_x=len(x_args)),
        grid=(n_tok // tm,),
        in_specs=in_specs,
        out_specs=out_specs,
        out_shape=out_shape,
        compiler_params=_cparams(("arbitrary",)),
        name="s1_rope" if rope else "s1_ctx",
    )(*args)


def _gqa_group(q2, segs, sink_pair_lo, sink_pair_hi, value_ones):
    qb = q2.shape[0]
    qs = jnp.concatenate([q2[:, 0:LANES], q2[:, LANES:2 * LANES]], axis=0)
    lane = lax.broadcasted_iota(jnp.int32, qs.shape, 1)
    low = lane < HEAD_DIM
    zero = jnp.zeros_like(qs)
    q4 = jnp.concatenate([jnp.where(low, qs, zero), jnp.where(low, zero, qs)], axis=0)

    scores = []
    for k, _, ok in segs:
        s = _dot_t(q4, k)
        if ok is not None:
            s = jnp.where(ok, s, NEG_INF)
        scores.append(s)
    m = scores[0].max(axis=-1, keepdims=True)
    for s in scores[1:]:
        m = jnp.maximum(m, s.max(axis=-1, keepdims=True))
    if sink_pair_lo is not None:
        row = lax.broadcasted_iota(jnp.int32, (4 * qb, 1), 0)
        sink = jnp.where(row < 2 * qb, jnp.where(row < qb, sink_pair_lo[0], sink_pair_lo[1]),
                         jnp.where(row < 3 * qb, sink_pair_hi[0], sink_pair_hi[1]))
        m = jnp.maximum(m, sink)
    if not value_ones:
        denom = jnp.exp2(sink - m) if sink_pair_lo is not None else jnp.zeros_like(m)
        acc = None
        for s, (_, v, _) in zip(scores, segs):
            p = jnp.exp2(s - m)
            denom = denom + p.sum(axis=-1, keepdims=True)
            pv = _dot(p.astype(bf16), v)
            acc = pv if acc is None else acc + pv
        o4 = acc / denom
        o = jnp.where(low, o4[0:2 * qb], o4[2 * qb:4 * qb])
        return jnp.concatenate([o[0:qb], o[qb:2 * qb]], axis=1)

    acc = None
    for s, (_, v, _) in zip(scores, segs):
        pv = _dot(jnp.exp2(s - m).astype(bf16), v)
        acc = pv if acc is None else acc + pv
    if sink_pair_lo is not None:
        low4 = lax.broadcasted_iota(jnp.int32, acc.shape, 1) < HEAD_DIM
        acc = acc + jnp.where(low4, 0.0, jnp.exp2(sink - m))
    swapped = pltpu.roll(acc, HEAD_DIM, 1)
    o = jnp.where(low, acc[0:2 * qb] / swapped[0:2 * qb], swapped[2 * qb:4 * qb] / acc[2 * qb:4 * qb])
    return jnp.concatenate([o[0:qb], o[qb:2 * qb]], axis=1)


def _sink_pairs(sink_ref, kv):
    h = 4 * kv
    s = [sink_ref[h + j] * LOG2_E for j in range(4)]
    return (s[0], s[2]), (s[1], s[3])


ATTN_CTX_BATCH = 4


def _attn_ctx_kernel(sink_ref, qa_ref, ka_ref, va_ref, qc_ref, kc_ref, vc_ref, oa_ref, oc_ref, *, seq):
    for b in range(ATTN_CTX_BATCH):
        rows = slice(b * seq, (b + 1) * seq)
        for kv in range(N_KV):
            cq = slice(2 * LANES * kv, 2 * LANES * (kv + 1))
            ck = slice(LANES * kv, LANES * (kv + 1))
            oa = _gqa_group(qa_ref[rows, cq], [(ka_ref[rows, ck], va_ref[rows, ck], None)], None, None, False)
            oa_ref[rows, cq] = oa.astype(bf16)
            lo, hi = _sink_pairs(sink_ref, kv)
            oc = _gqa_group(qc_ref[rows, cq], [(kc_ref[rows, ck], vc_ref[rows, ck], None)], lo, hi, False)
            oc_ref[rows, cq] = oc.astype(bf16)


def _attn_ctx_call(sink, qa, ka, va, qc, kc, vc, seq):
    n_tok = qa.shape[0]
    rows = ATTN_CTX_BATCH * seq
    tok = lambda w: pl.BlockSpec((rows, w), lambda b: (b, 0))
    return pl.pallas_call(
        functools.partial(_attn_ctx_kernel, seq=seq),
        grid=(n_tok // rows,),
        in_specs=[pl.BlockSpec(memory_space=pltpu.SMEM),
                  tok(Q_W), tok(KV_DUP_W), tok(KV_DUP_W), tok(Q_W), tok(KV_DUP_W), tok(KV_DUP_W)],
        out_specs=[tok(Q_W), tok(Q_W)],
        out_shape=[jax.ShapeDtypeStruct((n_tok, Q_W), bf16)] * 2,
        compiler_params=_cparams(("arbitrary",)),
        name="attn_ctx",
    )(sink, qa, ka, va, qc, kc, vc)


QB = 256
WIN_QB = 128
WIN_SPAN = WIN_QB + 2 * WINDOW


def _attn_lat_kernel(sink_ref, qa_ref, ka_ref, va_ref, kxa_ref, vxa_ref,
                     qc_ref, kc_ref, vc_ref, kxc_ref, vxc_ref, oa_ref, oc_ref, *, seq):
    i = pl.program_id(1)
    for kv in range(N_KV):
        cq = slice(2 * LANES * kv, 2 * LANES * (kv + 1))
        ck = slice(LANES * kv, LANES * (kv + 1))
        segs = [(ka_ref[:, ck], va_ref[:, ck], None), (kxa_ref[0, :, ck], vxa_ref[0, :, ck], None)]
        oa_ref[:, cq] = _gqa_group(qa_ref[:, cq], segs, None, None, True).astype(bf16)

        lo, hi = _sink_pairs(sink_ref, kv)
        for sub in range(QB // WIN_QB):
            blk = i * (QB // WIN_QB) + sub
            start = jnp.clip(blk * WIN_QB - WINDOW, 0, seq - WIN_SPAN)
            start = pl.multiple_of(start, WINDOW)
            r = lax.broadcasted_iota(jnp.int32, (4 * WIN_QB, WIN_SPAN), 0)
            c = lax.broadcasted_iota(jnp.int32, (4 * WIN_QB, WIN_SPAN), 1)
            qpos = blk * WIN_QB + (r & (WIN_QB - 1))
            ok = jnp.abs(qpos - (start + c)) <= WINDOW
            rows = slice(sub * WIN_QB, (sub + 1) * WIN_QB)
            segs = [(kc_ref[pl.ds(start, WIN_SPAN), ck], vc_ref[pl.ds(start, WIN_SPAN), ck], ok),
                    (kxc_ref[0, :, ck], vxc_ref[0, :, ck], None)]
            oc_ref[rows, cq] = _gqa_group(qc_ref[rows, cq], segs, lo, hi, True).astype(bf16)


def _attn_lat_call(sink, qa, ka, va, kxa, vxa, qc, kc, vc, kxc, vxc, seq):
    n_tok = qa.shape[0]
    nq = seq // QB
    past = kxa.shape[1]
    qspec = pl.BlockSpec((QB, Q_W), lambda b, i: (b * nq + i, 0))
    kspec = pl.BlockSpec((seq, KV_DUP_W), lambda b, i: (b, 0))
    xspec = pl.BlockSpec((1, past, KV_DUP_W), lambda b, i: (b, 0, 0))
    return pl.pallas_call(
        functools.partial(_attn_lat_kernel, seq=seq),
        grid=(n_tok // seq, nq),
        in_specs=[pl.BlockSpec(memory_space=pltpu.SMEM),
                  qspec, kspec, kspec, xspec, xspec, qspec, kspec, kspec, xspec, xspec],
        out_specs=[qspec, qspec],
        out_shape=[jax.ShapeDtypeStruct((n_tok, Q_W), bf16)] * 2,
        compiler_params=_cparams(("arbitrary", "arbitrary")),
        name="attn_lat",
    )(sink, qa, ka, va, kxa, vxa, qc, kc, vc, kxc, vxc)


FOURIER_ROWS = 2048


def _fourier_kernel(u_ref, bdc_ref, bds_ref, cn_ref, sn_ref, o_ref, *, scale, seq):
    u = u_ref[...]
    uc = _dot(u, bdc_ref[...]).astype(bf16)
    us = _dot(u, bds_ref[...]).astype(bf16)
    for b in range(u.shape[0] // seq):
        rows = slice(b * seq, (b + 1) * seq)
        o = _dot(cn_ref[...], uc[rows]) - _dot(sn_ref[...], us[rows])
        o_ref[rows, :] = (o * scale).astype(bf16)


def _fourier_call(ub, bdc, bds, cn, sn, seq):
    n_tok = ub.shape[0]
    rows = max(seq, FOURIER_ROWS)
    tok = pl.BlockSpec((rows, FOURIER_DIM), lambda b: (b, 0))
    scale = float(1.0 / np.sqrt(seq * FOURIER_GROUP_DIM))
    return pl.pallas_call(
        functools.partial(_fourier_kernel, scale=scale, seq=seq),
        grid=(n_tok // rows,),
        in_specs=[tok, _resident((FOURIER_DIM, FOURIER_DIM)), _resident((FOURIER_DIM, FOURIER_DIM)),
                  _resident((seq, seq)), _resident((seq, seq))],
        out_specs=tok,
        out_shape=jax.ShapeDtypeStruct((n_tok, FOURIER_DIM), bf16),
        compiler_params=_cparams(("arbitrary",)),
        name=f"fourier_{seq}",
    )(ub, bdc, bds, cn, sn)


ROUTE_ROWS = LANES
H2X_PITCH = SLAB_ROWS + 1
MOE_PITCH = SLAB_ROWS


def _route(lt, tri_ref, carry_ref, n_tok):
    tm = lt.shape[1]
    row = lax.broadcasted_iota(jnp.int32, (EXPERTS_PER_GROUP, tm), 0).astype(f32)
    neg = jnp.float32(-jnp.inf)
    none = jnp.float32(EXPERTS_PER_GROUP)
    gl = jnp.where(row < N_EXPERT_GROUPS, lt[N_EXPERTS:N_EXPERTS + EXPERTS_PER_GROUP], neg)
    gmax = gl.max(axis=0, keepdims=True)
    gidx = jnp.where(gl == gmax, row, none).min(axis=0, keepdims=True)
    gw = 1.0 / jnp.exp(gl - gmax).sum(axis=0, keepdims=True)
    lg = lt[0:EXPERTS_PER_GROUP]
    for g in range(1, N_EXPERT_GROUPS):
        lg = jnp.where(gidx == g, lt[g * EXPERTS_PER_GROUP:(g + 1) * EXPERTS_PER_GROUP], lg)
    m1 = lg.max(axis=0, keepdims=True)
    i1 = jnp.where(lg == m1, row, none).min(axis=0, keepdims=True)
    l2 = jnp.where(row == i1, neg, lg)
    m2 = l2.max(axis=0, keepdims=True)
    i2 = jnp.where(l2 == m2, row, none).min(axis=0, keepdims=True)
    e = jnp.exp(m2 - m1)
    w8 = jnp.where(row == i1, gw / (1.0 + e), jnp.where(row == i2, gw * e / (1.0 + e), 0.0))

    onehot = (row == gidx).astype(f32)
    prefix = _dot(onehot.astype(bf16), tri_ref[0:tm, 0:tm])
    carry = carry_ref[...]
    rank = (onehot * (prefix - 1.0 + carry[:, 0:1])).sum(axis=0, keepdims=True)
    carry_ref[...] = carry + onehot.sum(axis=1, keepdims=True)
    dest = gidx.astype(jnp.int32) * n_tok + rank.astype(jnp.int32)
    return w8, dest


def _s3_kernel(*refs, n_x, n_tok):
    x_refs, refs = refs[:n_x], refs[n_x:]
    (oa_ref, ob_ref, oc_ref, mod_ref, g1_ref, g2_ref, wga_ref, wgb_ref, wgc_ref, wba_ref, wbb_ref, wbc_ref,
     wout_ref, wrt_ref, tri_ref, x1_ref, h2x_ref, code_ref, cnt_ref, carry_ref) = refs
    wg_refs = (wga_ref, wgb_ref, wgc_ref)

    @pl.when(pl.program_id(0) == 0)
    def _():
        carry_ref[...] = jnp.zeros_like(carry_ref)

    tm = x1_ref.shape[0] // S3_SUB_TILES
    wh, wl = _split_bf16(wrt_ref[...])
    for sub in range(S3_SUB_TILES):
        r0 = sub * tm
        rows = pl.ds(r0, tm)
        x = _layer_input(x_refs, r0, tm)
        h = _rms_mod(x, g1_ref[...], mod_ref[0, 1:2, :], mod_ref[0, 0:1, :]).astype(bf16)
        merged = None
        for j, (o_ref, wb_ref) in enumerate(((oa_ref, wba_ref), (ob_ref, wbb_ref), (oc_ref, wbc_ref))):
            gate = _sigmoid(_dot(h, wg_refs[j][...]))
            term = gate * _dot(o_ref[rows, :], wb_ref[...])
            merged = term if merged is None else merged + term
        mix = _dot(merged.astype(bf16), wout_ref[...])
        x1 = x + mod_ref[0, 2:3, :] * mix
        x1_ref[rows, :] = x1
        h2 = _rms_mod(x1, g2_ref[...], mod_ref[0, 4:5, :], mod_ref[0, 3:4, :])
        _rows_to_slabs(h2x_ref, r0 * H2X_PITCH, h2, H2X_PITCH)
        th, tl = _split_bf16(h2)
        lt = _dot_t(wh, th) + (_dot_t(wh, tl) + _dot_t(wl, th))
        w8, code = _route(lt, tri_ref, carry_ref, n_tok)
        code_ref[:, rows] = code
        w_rows = jnp.concatenate([w8, jnp.zeros((LANES - EXPERTS_PER_GROUP, tm), f32)], axis=0)
        h2x_ref[pl.ds(r0 * H2X_PITCH + SLAB_ROWS, tm, stride=H2X_PITCH), :] = w_rows.T
    cnt_ref[...] = carry_ref[...]


def _s3_call(xin, mod_prev, oa, ob, oc, mod, g1, g2, wg, wba, wbb, wbc, wout, wrt, tri, *, tokens_per_batch,
             cond_row0, per_batch_cond):
    n_tok = xin[0].shape[0]
    tm = TOKEN_TILE
    cond_idx = _cond_index_map(tokens_per_batch, cond_row0, per_batch_cond)
    tok = lambda w: pl.BlockSpec((tm, w), lambda t: (t, 0))
    x_specs, x_args = _layer_input_specs(xin, mod_prev, cond_idx)
    gate_cols = lambda j: pl.BlockSpec((D_MODEL, D_MODEL), lambda t: (0, C_GATES // D_MODEL + j),
                                       pipeline_mode=pl.Buffered(1))
    return pl.pallas_call(
        functools.partial(_s3_kernel, n_x=len(x_args), n_tok=n_tok),
        grid=(n_tok // tm,),
        in_specs=x_specs + [
            tok(Q_W), tok(FOURIER_DIM), tok(Q_W),
            pl.BlockSpec((1, 6, D_MODEL), cond_idx),
            _resident((1, D_MODEL)), _resident((1, D_MODEL)),
            gate_cols(0), gate_cols(1), gate_cols(2),
            _resident((Q_W, D_MODEL)), _resident((FOURIER_DIM, D_MODEL)), _resident((Q_W, D_MODEL)),
            _resident((D_MODEL, D_MODEL)), _resident((ROUTE_ROWS, D_MODEL)), _resident((tm, tm))],
        out_specs=[tok(D_MODEL), pl.BlockSpec((tm * H2X_PITCH, LANES), lambda t: (t, 0)),
                   pl.BlockSpec((1, tm), lambda t: (0, t)),
                   pl.BlockSpec((EXPERTS_PER_GROUP, LANES), lambda t: (0, 0))],
        out_shape=[jax.ShapeDtypeStruct((n_tok, D_MODEL), f32),
                   jax.ShapeDtypeStruct((n_tok * H2X_PITCH, LANES), f32),
                   jax.ShapeDtypeStruct((1, n_tok), jnp.int32),
                   jax.ShapeDtypeStruct((EXPERTS_PER_GROUP, LANES), f32)],
        scratch_shapes=[pltpu.VMEM((EXPERTS_PER_GROUP, LANES), f32)],
        compiler_params=_cparams(("arbitrary",)),
        name="s3",
    )(*x_args, oa, ob, oc, mod, g1, g2, wg, wg, wg, wba, wbb, wbc, wout, wrt, tri)


MOE_TILE_LOG2 = 9
MOE_TILE = 1 << MOE_TILE_LOG2
ROW_UNROLL_LOG2 = 3
ROW_UNROLL = 1 << ROW_UNROLL_LOG2
GROUP_FF = EXPERTS_PER_GROUP * EXPERT_FF
PLAN_FIRST, PLAN_ITEMS, PLAN_LEN = 0, N_EXPERT_GROUPS, N_EXPERT_GROUPS + 1


def _moe_kernel(code_ref, cnt_ref, h2x_hbm, weg_ref, weu_ref, wed_ref, out_hbm,
                pos_ref, plan_ref, gbuf, obuf, dump, gsem, ssem, *, n_tok):
    tm = MOE_TILE
    k = pl.program_id(0)
    n_steps = pl.num_programs(0)

    def out_slab(buf, slot, i):
        return buf.at[slot, :, i]

    def token_slab(tok):
        return out_hbm.at[pl.ds(pl.multiple_of(tok * MOE_PITCH, MOE_PITCH), MOE_PITCH)]

    def for_row_chunks(n_chunks, body):
        def chunk(c, carry):
            body(c * ROW_UNROLL)
            return carry
        lax.fori_loop(0, n_chunks, chunk, 0)

    @pl.when(k == 0)
    def _plan():
        first = jnp.int32(0)
        for g in range(N_EXPERT_GROUPS):
            c = cnt_ref[g]
            n_tiles = (c + (tm - 1)) >> MOE_TILE_LOG2
            plan_ref[PLAN_FIRST + g] = first

            def pad(p, carry):
                pos_ref[p] = 0
                return carry

            lax.fori_loop(g * n_tok + c, g * n_tok + n_tiles * tm, pad, 0)
            first = first + n_tiles
        plan_ref[PLAN_ITEMS] = first

        def place(t0):
            dests = [code_ref[t0 + j] for j in range(ROW_UNROLL)]
            for j, dest in enumerate(dests):
                pos_ref[dest] = t0 + j

        for_row_chunks(n_tok >> ROW_UNROLL_LOG2, place)

    n_items = plan_ref[PLAN_ITEMS]

    def item_group(item):
        return ((item >= plan_ref[PLAN_FIRST + 1]).astype(jnp.int32)
                + (item >= plan_ref[PLAN_FIRST + 2]).astype(jnp.int32)
                + (item >= plan_ref[PLAN_FIRST + 3]).astype(jnp.int32))

    def item_pos0(item):
        g = item_group(item)
        return g * n_tok + (item - plan_ref[PLAN_FIRST + g]) * tm

    def start_gather(item, slot):
        pos0 = item_pos0(item)

        def chunk(i0):
            toks = [pos_ref[pos0 + i0 + j] for j in range(ROW_UNROLL)]
            for j, tok in enumerate(toks):
                pltpu.make_async_copy(h2x_hbm.at[pl.ds(tok * H2X_PITCH, H2X_PITCH)],
                                      gbuf.at[slot, :, i0 + j],
                                      gsem.at[slot]).start(priority=j % 2)
        for_row_chunks(tm >> ROW_UNROLL_LOG2, chunk)

    def wait_gather(slot):
        pltpu.make_async_copy(gbuf.at[1 - slot], gbuf.at[slot], gsem.at[slot]).wait()

    def start_scatter(item, n_valid, slot):
        pos0 = item_pos0(item)

        def to_token(i0):
            dsts = [pos_ref[pos0 + i0 + j] for j in range(ROW_UNROLL)]
            for j, dst in enumerate(dsts):
                pltpu.make_async_copy(out_slab(obuf, slot, i0 + j), token_slab(dst),
                                      ssem.at[slot]).start(priority=j % 2)

        def row_to_token(i, carry):
            pltpu.make_async_copy(out_slab(obuf, slot, i), token_slab(pos_ref[pos0 + i]), ssem.at[slot]).start()
            return carry

        def row_to_dump(i, carry):
            pltpu.make_async_copy(out_slab(obuf, slot, i), out_slab(dump, slot, i), ssem.at[slot]).start()
            return carry

        full = n_valid >> ROW_UNROLL_LOG2
        for_row_chunks(full, to_token)
        lax.fori_loop(full * ROW_UNROLL, n_valid, row_to_token, 0)
        lax.fori_loop(n_valid, tm, row_to_dump, 0)

    def wait_scatter(slot):
        pltpu.make_async_copy(obuf.at[slot], dump.at[slot], ssem.at[slot]).wait()

    slot = k % 2

    @pl.when(k == 0)
    def _():
        start_gather(0, 0)

    @pl.when(k + 1 < n_items)
    def _():
        start_gather(k + 1, 1 - slot)

    @pl.when(k < n_items)
    def _():
        group = item_group(k)
        wait_gather(slot)
        h = jnp.concatenate([gbuf[slot, c] for c in range(SLAB_ROWS)], axis=1).astype(bf16)
        w8 = gbuf[slot, SLAB_ROWS]
        cols = pl.ds(pl.multiple_of(group * GROUP_FF, GROUP_FF), GROUP_FF)
        gate = _dot(h, weg_ref[:, cols])
        a = gate * _sigmoid(gate) * _dot(h, weu_ref[:, cols])
        pieces = [a[:, j * EXPERT_FF:(j + 1) * EXPERT_FF] * w8[:, j:j + 1] for j in range(EXPERTS_PER_GROUP)]
        out = _dot(jnp.concatenate(pieces, axis=1).astype(bf16), wed_ref[group])

        @pl.when(k >= 2)
        def _():
            wait_scatter(slot)

        for c in range(SLAB_ROWS):
            obuf[slot, c] = out[:, c * LANES:(c + 1) * LANES]
        n_valid = jnp.minimum(tm, cnt_ref[group] - (k - plan_ref[PLAN_FIRST + group]) * tm)
        start_scatter(k, n_valid, slot)

    @pl.when(k == n_steps - 1)
    def _drain():
        wait_scatter((n_items - 1) % 2)

        @pl.when(n_items >= 2)
        def _():
            wait_scatter(n_items % 2)


def _moe_call(code, cnt, h2x, weg, weu, wed):
    n_tok = h2x.shape[0] // H2X_PITCH
    tm = MOE_TILE
    n_steps = n_tok // tm + N_EXPERT_GROUPS - 1
    ff = N_EXPERT_GROUPS * GROUP_FF
    up_spec = pl.BlockSpec((D_MODEL, ff), lambda k, *_: (0, 0), pipeline_mode=pl.Buffered(1))
    down_spec = pl.BlockSpec((N_EXPERT_GROUPS, GROUP_FF, D_MODEL), lambda k, *_: (0, 0, 0),
                             pipeline_mode=pl.Buffered(1))
    grid_spec = pltpu.PrefetchScalarGridSpec(
        num_scalar_prefetch=2,
        grid=(n_steps,),
        in_specs=[pl.BlockSpec(memory_space=pl.ANY), up_spec, up_spec, down_spec],
        out_specs=pl.BlockSpec(memory_space=pl.ANY),
        scratch_shapes=[
            pltpu.SMEM((N_EXPERT_GROUPS * n_tok,), jnp.int32),
            pltpu.SMEM((PLAN_LEN,), jnp.int32),
            pltpu.VMEM((2, H2X_PITCH, tm, LANES), f32),
            pltpu.VMEM((2, MOE_PITCH, tm, LANES), f32),
            pltpu.VMEM((2, MOE_PITCH, tm, LANES), f32),
            pltpu.SemaphoreType.DMA((2,)),
            pltpu.SemaphoreType.DMA((2,)),
        ],
    )
    return pl.pallas_call(
        functools.partial(_moe_kernel, n_tok=n_tok),
        grid_spec=grid_spec,
        out_shape=jax.ShapeDtypeStruct((n_tok * MOE_PITCH, LANES), f32),
        compiler_params=_cparams(("arbitrary",)),
        name="moe",
    )(code, cnt, h2x, weg, weu, wed)


def _final_kernel(x1_ref, moe_ref, mod_ref, gf_ref, o_ref):
    x2 = x1_ref[...] + mod_ref[0, 5:6, :] * _rows_from_slabs(moe_ref, 0, x1_ref.shape[0], MOE_PITCH)
    y = x2 * lax.rsqrt(jnp.mean(x2 * x2, axis=-1, keepdims=True) + NORM_EPS)
    o_ref[...] = y * gf_ref[...]


FINAL_TILE = 1024


def _final_call(x1, moe, mod, gf, *, tokens_per_batch, cond_row0, per_batch_cond):
    n_tok = x1.shape[0]
    tok = pl.BlockSpec((FINAL_TILE, D_MODEL), lambda t: (t, 0))
    slab = pl.BlockSpec((FINAL_TILE * MOE_PITCH, LANES), lambda t: (t, 0))
    cond_idx = _cond_index_map(tokens_per_batch, cond_row0, per_batch_cond, FINAL_TILE)
    return pl.pallas_call(
        _final_kernel,
        grid=(n_tok // FINAL_TILE,),
        in_specs=[tok, slab, pl.BlockSpec((1, 6, D_MODEL), cond_idx), _resident((1, D_MODEL))],
        out_specs=tok,
        out_shape=jax.ShapeDtypeStruct((n_tok, D_MODEL), f32),
        compiler_params=_cparams(("arbitrary",)),
        name="final_norm",
    )(x1, moe, mod, gf)


def _rope_tables(n):
    rows = n // GRID_W
    row = jnp.repeat(jnp.arange(rows, dtype=f32), GRID_W)
    col = jnp.tile(jnp.arange(GRID_W, dtype=f32), rows)
    inv = ROPE_THETA ** (-jnp.arange(AXIS_FREQS, dtype=f32) / AXIS_FREQS)
    ang = jnp.concatenate([row[:, None] * inv, col[:, None] * inv], axis=-1)
    cos, sin = jnp.cos(ang), jnp.sin(ang)
    cos_t = jnp.concatenate([cos, cos, cos, cos], axis=-1)
    sin_t = jnp.concatenate([-sin, sin, -sin, sin], axis=-1)
    return cos_t, sin_t


def _dft_tables(n):
    k = np.arange(n)
    m = (k[:, None] * k[None, :]) % n
    ang = 2.0 * np.pi * m / n
    return np.cos(ang), np.sin(ang)


def _fourier_tables(seq):
    cn, sn = _dft_tables(seq)
    cc, sc = _dft_tables(FOURIER_GROUP_DIM)
    eye = np.eye(N_FOURIER_GROUPS)
    bdc, bds = np.kron(eye, cc), np.kron(eye, sc)
    return tuple(jnp.asarray(t, dtype=f32).astype(bf16) for t in (bdc, bds, cn, sn))


def _dup_cache(cache, l):
    b, _, _, p, _, _ = cache.shape
    k, v = cache[:, l, 0], cache[:, l, 1]
    k2 = jnp.broadcast_to(k[:, :, :, None, :], (b, p, N_KV, 2, HEAD_DIM)).reshape(b, p, KV_DUP_W)
    v1 = jnp.stack([v, jnp.ones_like(v)], axis=3).reshape(b, p, KV_DUP_W)
    return k2.astype(bf16), v1.astype(bf16)


def kernel(x_prompt, x_sample, cache_global_kv, cache_window_kv, c, c_ctx, w_mod, b_mod, norm1_g, w_in,
           q_norm_g, k_norm_g, sink_logit, w_branch_a, w_branch_b, w_branch_c, w_out, norm2_g,
           w_route_group, w_route_expert, w_exp_gate, w_exp_up, w_exp_down, final_norm_g):
    batch, seq, _ = x_prompt.shape
    dec_batch, dec_seq, _ = x_sample.shape

    cond = jnp.zeros((N_COND_ROWS, D_MODEL), f32).at[0].set(c_ctx).at[1:1 + dec_batch].set(c)
    mod = _mod_call(cond, w_mod, b_mod).reshape(DEPTH, N_COND_ROWS, 6, D_MODEL)

    ones_blk = jnp.asarray(np.kron(np.eye(N_HEADS), np.ones((HEAD_DIM, HEAD_DIM))), dtype=bf16)
    tri = jnp.asarray(np.triu(np.ones((TOKEN_TILE, TOKEN_TILE))), dtype=bf16)
    rope_tabs = _rope_tables(dec_seq)
    ftab_ctx = _fourier_tables(seq)
    ftab_lat = _fourier_tables(dec_seq)
    gf = final_norm_g.reshape(1, D_MODEL)

    ctx = dict(tokens_per_batch=seq, cond_row0=0, per_batch_cond=False)
    lat = dict(tokens_per_batch=dec_seq, cond_row0=1, per_batch_cond=True)
    xp = (x_prompt.reshape(batch * seq, D_MODEL),)
    xs = (x_sample.reshape(dec_batch * dec_seq, D_MODEL),)
    mod_prev = None
    gkv, wkv = [], []
    for l in range(DEPTH):
        w_ext = w_gate = _layer_weight_bf16(w_in, l)
        g1 = norm1_g[l].reshape(1, D_MODEL)
        g2 = norm2_g[l].reshape(1, D_MODEL)
        qg = jnp.tile(q_norm_g[l], N_HEADS).reshape(1, Q_W)
        kg = jnp.tile(k_norm_g[l], N_KV).reshape(1, LANES)
        wba, wbb, wbc = (_layer_weight_bf16(w, l) for w in (w_branch_a, w_branch_b, w_branch_c))
        wout = _layer_weight_bf16(w_out, l)
        wrt = jnp.concatenate([w_route_expert[l].T, w_route_group[l].T,
                               jnp.zeros((ROUTE_ROWS - N_EXPERTS - N_EXPERT_GROUPS, D_MODEL), f32)], axis=0)
        weg, weu = _layer_weight_bf16(w_exp_gate, l), _layer_weight_bf16(w_exp_up, l)
        wed = _layer_weight_bf16(w_exp_down, l).reshape(N_EXPERT_GROUPS, GROUP_FF, D_MODEL)
        sink = sink_logit[l]

        def mix_and_moe(xin, qa, ka, va, qc, kc, vc, ub, attn, ftab, seq_len, stream):
            oa, oc = attn(qa, ka, va, qc, kc, vc)
            ob = _fourier_call(ub, *ftab, seq_len)
            x1, h2x, code, cnt = _s3_call(xin, mod_prev, oa, ob, oc, mod[l], g1, g2, w_gate, wba, wbb, wbc, wout,
                                          wrt, tri, **stream)
            counts = cnt[:, 0].astype(jnp.int32)
            moe = _moe_call(code.reshape(-1), counts, h2x, weg, weu, wed)
            return (x1, moe)

        qa, ka, va, qc, kc, vc, ub, gkv_l, wkv_l = _s1_call(
            xp, mod_prev, mod[l], g1, w_ext, ones_blk, qg, kg, None, emit_kv=True, **ctx)
        attn_ctx = lambda *qkv: _attn_ctx_call(sink, *qkv, seq)
        xp = mix_and_moe(xp, qa, ka, va, qc, kc, vc, ub, attn_ctx, ftab_ctx, seq, ctx)
        kv_shape = (batch, 2, seq, N_KV, HEAD_DIM)
        gkv.append(gkv_l.reshape(kv_shape))
        wkv.append(wkv_l.reshape(kv_shape))

        qa, ka, va, qc, kc, vc, ub = _s1_call(
            xs, mod_prev, mod[l], g1, w_ext, ones_blk, qg, kg, rope_tabs, emit_kv=False, **lat)
        kxa, vxa = _dup_cache(cache_global_kv, l)
        kxc, vxc = _dup_cache(cache_window_kv, l)
        attn_lat = lambda qa, ka, va, qc, kc, vc: _attn_lat_call(
            sink, qa, ka, va, kxa, vxa, qc, kc, vc, kxc, vxc, dec_seq)
        xs = mix_and_moe(xs, qa, ka, va, qc, kc, vc, ub, attn_lat, ftab_lat, dec_seq, lat)
        mod_prev = mod[l]

    y_prompt = _final_call(*xp, mod_prev, gf, **ctx).reshape(batch, seq, D_MODEL)
    y_sample = _final_call(*xs, mod_prev, gf, **lat).reshape(dec_batch, dec_seq, D_MODEL)
    return (y_prompt, y_sample, jnp.stack(gkv, axis=1), jnp.stack(wkv, axis=1))
```
